```python
import math
import jax, jax.numpy as jnp
from jax import lax
import numpy as np

D_MODEL = 1024
BATCH = 8
SEQ = 4096
DEPTH = 1

MIX_WIDTH = D_MODEL
RET_WIDTH = MIX_WIDTH // 2
ATT_WIDTH = MIX_WIDTH - RET_WIDTH
HEAD_DIM = 64
RET_HEADS = RET_WIDTH // HEAD_DIM
ATT_HEADS = ATT_WIDTH // HEAD_DIM
RET_CHUNK = 128
ROPE_BASE = 10000.0
DILATED_PATTERNS = ((128, 1), (512, 4), (2048, 16))
N_BUCKETS = 32
MAX_DISTANCE = 1024
N_EXPERTS = 16
CAPACITY_FACTOR = 2
D_FF = 2 * D_MODEL
EPS = 1e-6
PROJ_COLS = 4 * RET_WIDTH + 3 * ATT_WIDTH

kernel_name = "hybrid_retention_dilated_ecmoe_encoder"


def _rmsnorm(x, g):
    xf = x.astype(jnp.float32)
    y = xf * lax.rsqrt(jnp.mean(xf * xf, axis=-1, keepdims=True) + EPS)
    return (y * g.astype(jnp.float32)).astype(x.dtype)


def _heads(t, n_heads):
    b, s, _ = t.shape
    return t.reshape(b, s, n_heads, HEAD_DIM).transpose(0, 2, 1, 3)


def _merge_heads(t):
    b, h, s, dh = t.shape
    return t.transpose(0, 2, 1, 3).reshape(b, s, h * dh)


def _rope(x):
    s, dh = x.shape[2], x.shape[3]
    half = dh // 2
    pos = jnp.arange(s, dtype=jnp.float32)
    inv = ROPE_BASE ** (-jnp.arange(0, dh, 2, dtype=jnp.float32) / dh)
    ang = pos[:, None] * inv[None, :]
    cos, sin = jnp.cos(ang).astype(x.dtype), jnp.sin(ang).astype(x.dtype)
    x1, x2 = x[..., :half], x[..., half:]
    return jnp.concatenate([x1 * cos - x2 * sin, x2 * cos + x1 * sin], axis=-1)


def _retention_dir(q, k, v, log_g, strict):
    b, h, s, dh = q.shape
    c = RET_CHUNK
    n = s // c
    qc = q.reshape(b, h, n, c, dh)
    kc = k.reshape(b, h, n, c, dh)
    vc = v.reshape(b, h, n, c, dh)
    lg = log_g.astype(jnp.float32)
    idx = jnp.arange(c, dtype=jnp.float32)
    diff = idx[:, None] - idx[None, :]
    keep = (diff > 0) if strict else (diff >= 0)
    dmat = jnp.where(keep[None], jnp.exp(jnp.maximum(diff, 0.0)[None] * lg[:, None, None]), 0.0)
    scores = jnp.einsum('bhncd,bhnmd->bhncm', qc, kc) * dmat[:, None]
    inner = jnp.einsum('bhncm,bhnme->bhnce', scores, vc)
    zeta = jnp.exp((c - 1 - idx)[None, :] * lg[:, None])
    xi = jnp.exp((idx + 1)[None, :] * lg[:, None])
    g_chunk = jnp.exp(c * lg)[None, :, None, None]
    kv = jnp.einsum('bhnmd,hm,bhnme->bhnde', kc, zeta, vc)

    def step(state, kv_i):
        return g_chunk * state + kv_i, state

    _, r_prev = lax.scan(step, jnp.zeros_like(kv[:, :, 0]), jnp.moveaxis(kv, 2, 0))
    r_prev = jnp.moveaxis(r_prev, 0, 2)
    cross = jnp.einsum('bhncd,bhnde->bhnce', qc, r_prev) * xi[None, :, None, :, None]
    return (inner + cross).reshape(b, h, s, dh)


def _t5_bucket(rel):
    half = N_BUCKETS // 2
    max_exact = half // 2
    bucket = jnp.where(rel > 0, half, 0)
    n = jnp.abs(rel)
    nf = jnp.maximum(n, 1).astype(jnp.float32)
    large = max_exact + (jnp.log(nf / max_exact) / math.log(MAX_DISTANCE / max_exact)
                         * (half - max_exact)).astype(jnp.int32)
    large = jnp.minimum(large, half - 1)
    return bucket + jnp.where(n < max_exact, n, large)


def _dilated_pattern(q, k, v, rel_bias, window, dilation):
    b, h, s, dh = q.shape
    side = window // (2 * dilation)
    L = s // dilation
    qb = side
    nb = -(-L // qb)
    Lp = nb * qb

    def strided(t):
        return t.reshape(b, h, L, dilation, dh).transpose(0, 1, 3, 2, 4)

    qs = jnp.pad(strided(q), ((0, 0), (0, 0), (0, 0), (0, Lp - L), (0, 0))).reshape(b, h, dilation, nb, qb, dh)

    def neighbours(t):
        tp = jnp.pad(strided(t), ((0, 0), (0, 0), (0, 0), (qb, Lp - L + qb), (0, 0)))
        tp = tp.reshape(b, h, dilation, nb + 2, qb, dh)
        return jnp.concatenate([tp[:, :, :, 0:nb], tp[:, :, :, 1:nb + 1], tp[:, :, :, 2:nb + 2]], axis=4)

    ks, vs = neighbours(k), neighbours(v)
    t = jnp.arange(qb, dtype=jnp.int32)
    u = jnp.arange(3 * qb, dtype=jnp.int32)
    off = u[None, :] - qb - t[:, None]
    bias = rel_bias[_t5_bucket(off * dilation)].transpose(2, 0, 1).astype(jnp.float32)
    kpos = jnp.arange(nb, dtype=jnp.int32)[:, None] * qb + u[None, :] - qb
    valid = (jnp.abs(off) <= side)[None] & ((kpos >= 0) & (kpos < L))[:, None, :]
    logits = jnp.einsum('bhrnqd,bhrnkd->bhrnqk', qs, ks).astype(jnp.float32) * (dh ** -0.5)
    logits = logits + bias[None, :, None, None]
    logits = jnp.where(valid[None, None, None], logits, -1e30)
    m = jnp.max(logits, axis=-1, keepdims=True)
    p = jnp.exp(logits - m)
    l = jnp.sum(p, axis=-1, keepdims=True)
    o = jnp.einsum('bhrnqk,bhrnkd->bhrnqd', p, vs.astype(jnp.float32)) / l
    lse = (m + jnp.log(l))[..., 0]
    o = o.reshape(b, h, dilation, Lp, dh)[:, :, :, :L].transpose(0, 1, 3, 2, 4).reshape(b, h, s, dh)
    lse = lse.reshape(b, h, dilation, Lp)[..., :L].transpose(0, 1, 3, 2).reshape(b, h, s)
    return o, lse


def _dilated_mixture(q, k, v, rel_bias):
    outs, lses = [], []
    for window, dilation in DILATED_PATTERNS:
        o, lse = _dilated_pattern(q, k, v, rel_bias, window, dilation)
        outs.append(o)
        lses.append(lse)
    w = jax.nn.softmax(jnp.stack(lses, axis=0), axis=0)
    return sum(w[i][..., None] * outs[i] for i in range(len(outs)))


def _expert_choice_moe(h, w_router, w_gate, w_up, w_down):
    b, s, d = h.shape
    cap = CAPACITY_FACTOR * s // N_EXPERTS
    aff = jax.nn.softmax(jnp.einsum('bsd,de->bse', h, w_router).astype(jnp.float32), axis=-1)
    gates, idx = lax.top_k(aff.transpose(0, 2, 1), cap)
    xg = jax.vmap(lambda xb, ib: xb[ib])(h, idx)
    hid = jax.nn.silu(jnp.einsum('becd,edf->becf', xg, w_gate)) * jnp.einsum('becd,edf->becf', xg, w_up)
    y = jnp.einsum('becf,efd->becd', hid, w_down) * gates[..., None].astype(hid.dtype)
    out = jax.vmap(lambda yb, ib: jnp.zeros((s, d), yb.dtype).at[ib.reshape(-1)].add(yb.reshape(-1, d)))(y, idx)
    return out.astype(h.dtype)


def setup_inputs(seed: int = 0) -> dict:
    key = jax.random.key(seed)
    ks = jax.random.split(key, 13)
    f32 = jnp.float32
    base_decay = jnp.asarray(np.log(1.0 - 2.0 ** (-5.0 - np.arange(RET_HEADS))).astype(np.float32))
    return {
        "x": jax.random.normal(ks[0], (BATCH, SEQ, D_MODEL), f32),
        "norm1_gain": 1.0 + 0.02 * jax.random.normal(ks[1], (DEPTH, D_MODEL), f32),
        "w_in": jax.random.normal(ks[2], (DEPTH, D_MODEL, PROJ_COLS), f32) * D_MODEL ** -0.5,
        "ret_log_decay": base_decay[None, None, :] * (1.0 + 0.05 * jax.random.normal(ks[3], (DEPTH, 2, RET_HEADS), f32)),
        "ret_gn_gain": 1.0 + 0.02 * jax.random.normal(ks[4], (DEPTH, RET_WIDTH), f32),
        "rel_bias": 0.5 * jax.random.normal(ks[5], (N_BUCKETS, ATT_HEADS), f32),
        "w_out": jax.random.normal(ks[6], (DEPTH, MIX_WIDTH, D_MODEL), f32) * MIX_WIDTH ** -0.5,
        "norm2_gain": 1.0 + 0.02 * jax.random.normal(ks[7], (DEPTH, D_MODEL), f32),
        "w_router": jax.random.normal(ks[8], (DEPTH, D_MODEL, N_EXPERTS), f32) * D_MODEL ** -0.5,
        "w_gate": jax.random.normal(ks[9], (DEPTH, N_EXPERTS, D_MODEL, D_FF), f32) * D_MODEL ** -0.5,
        "w_up": jax.random.normal(ks[10], (DEPTH, N_EXPERTS, D_MODEL, D_FF), f32) * D_MODEL ** -0.5,
        "w_down": jax.random.normal(ks[11], (DEPTH, N_EXPERTS, D_FF, D_MODEL), f32) * D_FF ** -0.5,
        "final_gain": 1.0 + 0.02 * jax.random.normal(ks[12], (D_MODEL,), f32),
    }


def reference(x, norm1_gain, w_in, ret_log_decay, ret_gn_gain, rel_bias, w_out,
              norm2_gain, w_router, w_gate, w_up, w_down, final_gain):
    splits = [RET_WIDTH, 2 * RET_WIDTH, 3 * RET_WIDTH, 4 * RET_WIDTH,
              4 * RET_WIDTH + ATT_WIDTH, 4 * RET_WIDTH + 2 * ATT_WIDTH]
    for layer in range(DEPTH):
        h = _rmsnorm(x, norm1_gain[layer])
        proj = jnp.einsum('bsd,dp->bsp', h, w_in[layer])
        rq, rk, rv, rg, aq, ak, av = jnp.split(proj, splits, axis=-1)

        rq = _rope(_heads(rq, RET_HEADS))
        rk = _rope(_heads(rk, RET_HEADS)) * (HEAD_DIM ** -0.5)
        rv = _heads(rv, RET_HEADS)
        fwd = _retention_dir(rq, rk, rv, ret_log_decay[layer, 0], False)
        bwd = jnp.flip(_retention_dir(jnp.flip(rq, 2), jnp.flip(rk, 2), jnp.flip(rv, 2),
                                      ret_log_decay[layer, 1], True), 2)
        r = (fwd + bwd).astype(jnp.float32)
        mu = jnp.mean(r, axis=-1, keepdims=True)
        var = jnp.mean(jnp.square(r - mu), axis=-1, keepdims=True)
        r = _merge_heads((r - mu) * lax.rsqrt(var + EPS)) * ret_gn_gain[layer].astype(jnp.float32)
        r = (jax.nn.silu(rg.astype(jnp.float32)) * r).astype(x.dtype)

        a = _dilated_mixture(_heads(aq, ATT_HEADS), _heads(ak, ATT_HEADS), _heads(av, ATT_HEADS), rel_bias)
        a = _merge_heads(a).astype(x.dtype)

        mixed = jnp.concatenate([r, a], axis=-1)
        x = x + jnp.einsum('bsm,md->bsd', mixed, w_out[layer])

        h2 = _rmsnorm(x, norm2_gain[layer])
        x = x + _expert_choice_moe(h2, w_router[layer], w_gate[layer], w_up[layer], w_down[layer])
    return _rmsnorm(x, final_gain)
```

```python
import functools
import math

import numpy as np
import jax
import jax.numpy as jnp
from jax import lax
from jax.experimental import pallas as pl
from jax.experimental.pallas import tpu as pltpu

F32 = jnp.float32
BF16 = jnp.bfloat16
I32 = jnp.int32

HEAD_DIM = 64
LANES = 128
RET_CHUNK = 128
ROPE_BASE = 10000.0
DILATED_PATTERNS = ((128, 1), (512, 4), (2048, 16))
SIDE = 64
N_BUCKETS = 32
MAX_DISTANCE = 1024
N_EXPERTS = 16
CAPACITY_FACTOR = 2
EPS = 1e-6
NEG = -1e30
VMEM_LIMIT = 56 * 1024 * 1024

_NT = (((1,), (1,)), ((), ()))
_TN = (((0,), (0,)), ((), ()))


def _params(*sem):
    return pltpu.CompilerParams(dimension_semantics=sem, vmem_limit_bytes=VMEM_LIMIT)


def _proj_kernel(x_ref, g_ref, w_ref, ret_ref, att_ref, *, n_ret, chunk):
    x = x_ref[...]
    ms = jnp.mean(x * x, axis=-1, keepdims=True)
    h = (x * lax.rsqrt(ms + EPS) * g_ref[...]).astype(BF16)
    n_cols = w_ref.shape[1]
    for c0 in range(0, n_cols, chunk):
        o = jnp.dot(h, w_ref[:, c0:c0 + chunk], preferred_element_type=F32)
        if c0 < n_ret:
            ret_ref[:, c0:c0 + chunk] = o.astype(BF16)
        else:
            att_ref[:, c0 - n_ret:c0 - n_ret + chunk] = o


def _proj(x2d, gain, w_bf16, n_ret, tm=512):
    t, d = x2d.shape
    n_cols = w_bf16.shape[1]
    n_att = n_cols - n_ret
    return pl.pallas_call(
        functools.partial(_proj_kernel, n_ret=n_ret, chunk=512),
        grid=(t // tm,),
        in_specs=[
            pl.BlockSpec((tm, d), lambda i: (i, 0)),
            pl.BlockSpec((1, d), lambda i: (0, 0)),
            pl.BlockSpec((d, n_cols), lambda i: (0, 0)),
        ],
        out_specs=[
            pl.BlockSpec((tm, n_ret), lambda i: (i, 0)),
            pl.BlockSpec((tm, n_att), lambda i: (i, 0)),
        ],
        out_shape=[
            jax.ShapeDtypeStruct((t, n_ret), BF16),
            jax.ShapeDtypeStruct((t, n_att), F32),
        ],
        compiler_params=_params("parallel"),
        name="proj",
    )(x2d, gain, w_bf16)


def _retention_kernel(decay_ref, q_ref, k_ref, v_ref, g_ref, cos_ref, sin_ref, gn_ref, out_ref,
                      qs_ref, ks_ref, rb_ref, *, seq):
    c = RET_CHUNK
    n_chunks = seq // c
    p = pl.program_id(1)
    lane = lax.broadcasted_iota(I32, (c, LANES), 1)
    row = lax.broadcasted_iota(I32, (c, LANES), 0)
    head0 = lane < HEAD_DIM
    rowf = row.astype(F32)

    lgf0, lgf1 = decay_ref[0, 2 * p], decay_ref[0, 2 * p + 1]
    lgb0, lgb1 = decay_ref[1, 2 * p], decay_ref[1, 2 * p + 1]
    lgf_lane = jnp.where(head0, lgf0, lgf1)
    lgb_lane = jnp.where(head0, lgb0, lgb1)
    lgf_row = jnp.where(row < HEAD_DIM, lgf0, lgf1)
    lgb_row = jnp.where(row < HEAD_DIM, lgb0, lgb1)
    same_head = (row < HEAD_DIM) == head0

    zeta_f = jnp.exp((c - 1 - rowf) * lgf_lane)
    zeta_b = jnp.exp(rowf * lgb_lane)
    xi_f = jnp.exp((rowf + 1.0) * lgf_lane)
    xi_b = jnp.exp((c - rowf) * lgb_lane)
    gch_f = jnp.where(same_head, jnp.exp(c * lgf_row), 0.0)
    gch_b = jnp.where(same_head, jnp.exp(c * lgb_row), 0.0)

    diff = (row - lane).astype(F32)

    def dmat(lf, lb):
        return jnp.exp(jnp.where(diff >= 0, diff * lf, -diff * lb))

    d_stack = jnp.concatenate([dmat(lgf0, lgb0), dmat(lgf1, lgb1)], axis=0)

    first_half = (lane % HEAD_DIM) < (HEAD_DIM // 2)
    rows_per_step = 4 * c

    def rope_step(i, carry):
        r0 = pl.multiple_of(i * rows_per_step, rows_per_step)
        sl = pl.ds(r0, rows_per_step)
        cos = cos_ref[sl, :]
        sin = sin_ref[sl, :]
        fh = jnp.concatenate([first_half] * 4, axis=0)

        def rot(t):
            sw = jnp.where(fh, pltpu.roll(t, LANES - HEAD_DIM // 2, 1), pltpu.roll(t, HEAD_DIM // 2, 1))
            return t * cos + sw * sin

        qs_ref[sl, :] = rot(q_ref[0, sl, :].astype(F32)).astype(BF16)
        ks_ref[sl, :] = (rot(k_ref[0, sl, :].astype(F32)) * (HEAD_DIM ** -0.5)).astype(BF16)
        return carry

    lax.fori_loop(0, seq // rows_per_step, rope_step, 0)

    def chunk_kv(n, zeta):
        sl = pl.ds(pl.multiple_of(n * c, c), c)
        kw = (ks_ref[sl, :].astype(F32) * zeta).astype(BF16)
        return lax.dot_general(kw, v_ref[0, sl, :], _TN, preferred_element_type=F32)

    def bwd_step(i, state):
        n = n_chunks - 1 - i
        rb_ref[n] = state.astype(BF16)
        return gch_b * state + jnp.where(same_head, chunk_kv(n, zeta_b), 0.0)

    lax.fori_loop(0, n_chunks, bwd_step, jnp.zeros((LANES, LANES), F32))

    gn_gain = gn_ref[...]

    def fwd_step(n, state):
        sl = pl.ds(pl.multiple_of(n * c, c), c)
        q = qs_ref[sl, :]
        k = ks_ref[sl, :]
        v = v_ref[0, sl, :]
        zero = jnp.zeros_like(q)
        q_stack = jnp.concatenate([jnp.where(head0, q, zero), jnp.where(head0, zero, q)], axis=0)
        scores = lax.dot_general(q_stack, k, _NT, preferred_element_type=F32) * d_stack
        sb = scores.astype(BF16)
        s_cat = jnp.concatenate([sb[:c], sb[c:]], axis=1)
        v_stack = jnp.concatenate([jnp.where(head0, v, zero), jnp.where(head0, zero, v)], axis=0)
        inner = jnp.dot(s_cat, v_stack, preferred_element_type=F32)
        cross_f = jnp.dot(q, state.astype(BF16), preferred_element_type=F32) * xi_f
        cross_b = jnp.dot(q, rb_ref[n], preferred_element_type=F32) * xi_b
        r = inner + cross_f + cross_b

        def half_mean(t):
            s0 = jnp.sum(jnp.where(head0, t, 0.0), axis=-1, keepdims=True)
            s1 = jnp.sum(jnp.where(head0, 0.0, t), axis=-1, keepdims=True)
            return jnp.where(head0, s0, s1) * (1.0 / HEAD_DIM)

        dlt = r - half_mean(r)
        var = half_mean(dlt * dlt)
        gate = g_ref[0, sl, :].astype(F32)
        y = dlt * lax.rsqrt(var + EPS) * gn_gain * (gate * jax.nn.sigmoid(gate))
        out_ref[0, sl, :] = y.astype(out_ref.dtype)
        return gch_f * state + jnp.where(same_head, chunk_kv(n, zeta_f), 0.0)

    lax.fori_loop(0, n_chunks, fwd_step, jnp.zeros((LANES, LANES), F32))


def _retention(ret, decay, cos_t, sin_t, gn_gain, n_pairs):
    b, s, _ = ret.shape
    blk = lambda off: pl.BlockSpec((1, s, LANES), lambda bi, pi: (bi, 0, off + pi))
    full = pl.BlockSpec((s, LANES), lambda bi, pi: (0, 0))
    return pl.pallas_call(
        functools.partial(_retention_kernel, seq=s),
        grid=(b, n_pairs),
        in_specs=[
            pl.BlockSpec(memory_space=pltpu.SMEM),
            blk(0), blk(n_pairs), blk(2 * n_pairs), blk(3 * n_pairs),
            full, full,
            pl.BlockSpec((1, LANES), lambda bi, pi: (0, pi)),
        ],
        out_specs=pl.BlockSpec((1, s, LANES), lambda bi, pi: (bi, 0, pi)),
        out_shape=jax.ShapeDtypeStruct((b, s, n_pairs * LANES), BF16),
        scratch_shapes=[
            pltpu.VMEM((s, LANES), BF16),
            pltpu.VMEM((s, LANES), BF16),
            pltpu.VMEM((s // RET_CHUNK, LANES, LANES), BF16),
        ],
        compiler_params=_params("parallel", "parallel"),
        name="retention",
    )(decay, ret, ret, ret, ret, cos_t, sin_t, gn_gain)


QBLK = 2 * SIDE
KWIN = 4 * SIDE
BASE_W = 512


def _t5_bucket_np(rel):
    half = N_BUCKETS // 2
    max_exact = half // 2
    bucket = np.where(rel > 0, half, 0)
    n = np.abs(rel)
    nf = np.maximum(n, 1).astype(np.float32)
    large = max_exact + (np.log(nf / np.float32(max_exact)) / np.float32(math.log(MAX_DISTANCE / max_exact))
                         * np.float32(half - max_exact)).astype(np.int32)
    large = np.minimum(large, half - 1)
    return (bucket + np.where(n < max_exact, n, large)).astype(np.int32)


def _bucket_rows():
    k = np.arange(BASE_W)
    off = k - SIDE
    rows = []
    for _, dilation in DILATED_PATTERNS:
        rows.append(np.where(k <= 2 * SIDE, _t5_bucket_np(off * dilation), -1))
    return np.stack(rows).astype(np.int32)


def _dilated_kernel(bias_tab_ref, bucket_ref, q_ref, k_ref, v_ref, out_ref,
                    bias_ref, qp_ref, kp_ref, vp_ref, o_ref, lse_ref, *, seq):
    p = pl.program_id(1)
    lane = lax.broadcasted_iota(I32, (QBLK, LANES), 1)
    head0 = lane < HEAD_DIM
    n_blocks = seq // QBLK

    for pi in range(len(DILATED_PATTERNS)):
        bucket = jnp.broadcast_to(bucket_ref[pi:pi + 1, :], (QBLK, BASE_W))
        for hh in range(2):
            base = jnp.full((QBLK, BASE_W), NEG, F32)
            for bk in range(N_BUCKETS):
                base = jnp.where(bucket == bk, bias_tab_ref[bk, 2 * p + hh], base)
            for var, shift in enumerate((BASE_W - SIDE, 0, SIDE)):
                tile = pltpu.roll(base, shift, 1, stride=1, stride_axis=0)
                bias_ref[(pi * 3 + var) * 2 + hh] = tile[:, :KWIN]

    for pi, (_, d) in enumerate(DILATED_PATTERNS):
        seg_len = seq // d
        blocks_per_seg = seg_len // QBLK

        for r in range(d):
            src = pl.ds(r, seg_len, stride=d) if d > 1 else pl.ds(0, seg_len)
            dst = pl.ds(r * seg_len, seg_len)
            qp_ref[dst, :] = (q_ref[0, src, :] * (HEAD_DIM ** -0.5)).astype(BF16)
            kp_ref[dst, :] = k_ref[0, src, :].astype(BF16)
            vp_ref[dst, :] = v_ref[0, src, :].astype(BF16)

        def block(t, carry, pi=pi, d=d, seg_len=seg_len, blocks_per_seg=blocks_per_seg):
            seg = t // blocks_per_seg
            u = t % blocks_per_seg
            q0 = pl.multiple_of(t * QBLK, QBLK)
            seg0 = seg * seg_len
            k0 = pl.multiple_of(jnp.clip(q0 - SIDE, seg0, seg0 + seg_len - KWIN), SIDE)
            var = jnp.where(u == 0, 0, jnp.where(u == blocks_per_seg - 1, 2, 1))
            q = qp_ref[pl.ds(q0, QBLK), :]
            kw = kp_ref[pl.ds(k0, KWIN), :]
            vw = vp_ref[pl.ds(k0, KWIN), :]
            zero = jnp.zeros_like(q)
            outs, lses = [], []
            for hh in range(2):
                qm = jnp.where(head0, q, zero) if hh == 0 else jnp.where(head0, zero, q)
                s = lax.dot_general(qm, kw, _NT, preferred_element_type=F32)
                s = s + bias_ref[(pi * 3 + var) * 2 + hh]
                m = jnp.max(s, axis=-1, keepdims=True)
                e = jnp.exp(s - m)
                l = jnp.sum(e, axis=-1, keepdims=True)
                pv = jnp.dot(e.astype(BF16), vw, preferred_element_type=F32)
                outs.append(pv * (1.0 / l))
                lses.append(jnp.broadcast_to(m + jnp.log(l), (QBLK, LANES)))
            o = jnp.where(head0, outs[0], outs[1])
            lse = jnp.where(head0, lses[0], lses[1])
            if d > 1:
                dst = pl.ds(seg + d * (u * QBLK), QBLK, stride=d)
            else:
                dst = pl.ds(q0, QBLK)
            o_ref[pi, dst, :] = o
            lse_ref[pi, dst, :] = lse
            return carry

        lax.fori_loop(0, n_blocks, block, 0)

    rows = 4 * QBLK

    def mix(i, carry):
        sl = pl.ds(pl.multiple_of(i * rows, rows), rows)
        l0, l1, l2 = lse_ref[0, sl, :], lse_ref[1, sl, :], lse_ref[2, sl, :]
        mx = jnp.maximum(jnp.maximum(l0, l1), l2)
        e0, e1, e2 = jnp.exp(l0 - mx), jnp.exp(l1 - mx), jnp.exp(l2 - mx)
        inv = 1.0 / (e0 + e1 + e2)
        acc = (e0 * inv) * o_ref[0, sl, :] + (e1 * inv) * o_ref[1, sl, :] + (e2 * inv) * o_ref[2, sl, :]
        out_ref[0, sl, :] = acc.astype(out_ref.dtype)
        return carry

    lax.fori_loop(0, seq // rows, mix, 0)


def _dilated(att, rel_bias, n_pairs):
    b, s, _ = att.shape
    n_pat = len(DILATED_PATTERNS)
    blk = lambda off: pl.BlockSpec((1, s, LANES), lambda bi, pi: (bi, 0, off + pi))
    bucket_rows = jnp.asarray(_bucket_rows())
    return pl.pallas_call(
        functools.partial(_dilated_kernel, seq=s),
        grid=(b, n_pairs),
        in_specs=[
            pl.BlockSpec(memory_space=pltpu.SMEM),
            pl.BlockSpec((n_pat, BASE_W), lambda bi, pi: (0, 0)),
            blk(0), blk(n_pairs), blk(2 * n_pairs),
        ],
        out_specs=pl.BlockSpec((1, s, LANES), lambda bi, pi: (bi, 0, pi)),
        out_shape=jax.ShapeDtypeStruct((b, s, n_pairs * LANES), BF16),
        scratch_shapes=[
            pltpu.VMEM((n_pat * 3 * 2, QBLK, KWIN), F32),
            pltpu.VMEM((s, LANES), BF16),
            pltpu.VMEM((s, LANES), BF16),
            pltpu.VMEM((s, LANES), BF16),
            pltpu.VMEM((n_pat, s, LANES), F32),
            pltpu.VMEM((n_pat, s, LANES), F32),
        ],
        compiler_params=_params("parallel", "parallel"),
        name="dilated",
    )(rel_bias, bucket_rows, att, att, att)


def _outproj_kernel(r_ref, a_ref, x_ref, w_ref, g_ref, wr_ref, x1_ref, h2_ref, aff_ref):
    n_r = r_ref.shape[1]
    x1 = (x_ref[...]
          + jnp.dot(r_ref[...], w_ref[:n_r, :], preferred_element_type=F32)
          + jnp.dot(a_ref[...], w_ref[n_r:, :], preferred_element_type=F32))
    x1_ref[...] = x1
    ms = jnp.mean(x1 * x1, axis=-1, keepdims=True)
    h2 = x1 * lax.rsqrt(ms + EPS) * g_ref[...]
    h2_ref[...] = h2.astype(BF16)
    logits = lax.dot_general(wr_ref[...], h2, _NT, preferred_element_type=F32,
                             precision=lax.Precision.HIGHEST)
    m = jnp.max(logits, axis=0, keepdims=True)
    e = jnp.exp(logits - m)
    aff_ref[0] = e / jnp.sum(e, axis=0, keepdims=True)


def _outproj(r, a, x, w_bf16, gain, w_router_t, tm=512):
    b, s, d = x.shape
    n_r, n_a = r.shape[-1], a.shape[-1]
    n_e = w_router_t.shape[0]
    spt = s // tm
    tok = lambda width: pl.BlockSpec((tm, width), lambda i: (i, 0))
    const = lambda shape: pl.BlockSpec(shape, lambda i: (0, 0))
    return pl.pallas_call(
        _outproj_kernel,
        grid=(b * spt,),
        in_specs=[tok(n_r), tok(n_a), tok(d), const((n_r + n_a, d)), const((1, d)), const((n_e, d))],
        out_specs=[tok(d), tok(d), pl.BlockSpec((1, n_e, tm), lambda i: (i // spt, 0, i % spt))],
        out_shape=[
            jax.ShapeDtypeStruct((b * s, d), F32),
            jax.ShapeDtypeStruct((b * s, d), BF16),
            jax.ShapeDtypeStruct((b, n_e, s), F32),
        ],
        compiler_params=_params("parallel"),
        name="outproj",
    )(r.reshape(b * s, n_r), a.reshape(b * s, n_a), x.reshape(b * s, d), w_bf16, gain, w_router_t)


def _select_kernel(aff_ref, posm_ref, *, cap):
    rows, s = aff_ref.shape
    bits = pltpu.bitcast(aff_ref[...], I32)

    def search(i, thr):
        cand = thr | jnp.left_shift(jnp.int32(1), 30 - i)
        cnt = jnp.sum((bits >= cand).astype(I32), axis=1, keepdims=True)
        return jnp.where(cnt >= cap, cand, thr)

    thr = lax.fori_loop(0, 31, search, jnp.zeros((rows, 1), I32))
    gt = bits > thr
    eq = bits == thr
    need = cap - jnp.sum(gt.astype(I32), axis=1, keepdims=True)

    w = LANES
    tri = (lax.broadcasted_iota(I32, (w, w), 0) <= lax.broadcasted_iota(I32, (w, w), 1)).astype(BF16)

    def excl_prefix(flags):
        carry = jnp.zeros((rows, 1), F32)
        out = []
        for c0 in range(0, s, w):
            f = flags[:, c0:c0 + w].astype(BF16)
            inc = jnp.dot(f, tri, preferred_element_type=F32)
            out.append(inc - f.astype(F32) + carry)
            carry = carry + inc[:, w - 1:w]
        return jnp.concatenate(out, axis=1).astype(I32)

    sel = gt | (eq & (excl_prefix(eq) < need))
    posm_ref[...] = jnp.where(sel, excl_prefix(sel), -1)


def _select(aff_rows, cap):
    rows, s = aff_rows.shape
    return pl.pallas_call(
        functools.partial(_select_kernel, cap=cap),
        out_shape=jax.ShapeDtypeStruct((rows, s), I32),
        compiler_params=pltpu.CompilerParams(vmem_limit_bytes=VMEM_LIMIT),
        name="select",
    )(aff_rows)


def _ffn_kernel(posm_ref, h2_ref, wg_ref, wu_ref, wd_ref, y_ref, xg_ref, acc_ref, *, tt, tf):
    cap, d = acc_ref.shape
    s = h2_ref.shape[1]
    f_total = wg_ref.shape[2]
    slot = lax.broadcasted_iota(I32, (cap, tt), 0)
    for j in range(s // tt):
        onehot = (posm_ref[0, 0, :, j * tt:(j + 1) * tt] == slot).astype(BF16)
        part = jnp.dot(onehot, h2_ref[0, j * tt:(j + 1) * tt, :], preferred_element_type=F32)
        if j == 0:
            acc_ref[...] = part
        else:
            acc_ref[...] += part
    xg_ref[...] = acc_ref[...].astype(BF16)
    for fi in range(f_total // tf):
        fs = slice(fi * tf, (fi + 1) * tf)
        gate = jnp.dot(xg_ref[...], wg_ref[0, :, fs], preferred_element_type=F32)
        up = jnp.dot(xg_ref[...], wu_ref[0, :, fs], preferred_element_type=F32)
        hid = ((gate * jax.nn.sigmoid(gate)) * up).astype(BF16)
        part = jnp.dot(hid, wd_ref[0, fs, :], preferred_element_type=F32)
        if fi == 0:
            acc_ref[...] = part
        else:
            acc_ref[...] += part
    y_ref[0, 0] = acc_ref[...].astype(y_ref.dtype)


def _ffn(posm4, h2, wg, wu, wd, cap, tt=512, tf=512):
    b, n_e = posm4.shape[:2]
    s, d = h2.shape[1:]
    f = wg.shape[2]
    return pl.pallas_call(
        functools.partial(_ffn_kernel, tt=tt, tf=tf),
        grid=(n_e, b),
        in_specs=[
            pl.BlockSpec((1, 1, 1, s), lambda e, bi: (bi, e, 0, 0)),
            pl.BlockSpec((1, s, d), lambda e, bi: (bi, 0, 0)),
            pl.BlockSpec((1, d, f), lambda e, bi: (e, 0, 0)),
            pl.BlockSpec((1, d, f), lambda e, bi: (e, 0, 0)),
            pl.BlockSpec((1, f, d), lambda e, bi: (e, 0, 0)),
        ],
        out_specs=pl.BlockSpec((1, 1, cap, d), lambda e, bi: (bi, e, 0, 0)),
        out_shape=jax.ShapeDtypeStruct((b, n_e, cap, d), BF16),
        scratch_shapes=[pltpu.VMEM((cap, d), BF16), pltpu.VMEM((cap, d), F32)],
        compiler_params=_params("parallel", "parallel"),
        name="ffn",
    )(posm4, h2, wg, wu, wd)


def _combine_kernel(pos_ref, gate_ref, y_ref, x1_ref, g_ref, out_ref):
    tt = x1_ref.shape[1]
    n_e, cap = y_ref.shape[1:3]
    slot = lax.broadcasted_iota(I32, (tt, cap), 1)
    acc = x1_ref[0]
    for e in range(n_e):
        scatter = jnp.where(pos_ref[0, :, e:e + 1] == slot, gate_ref[0, :, e:e + 1], 0.0).astype(BF16)
        acc = acc + jnp.dot(scatter, y_ref[0, e], preferred_element_type=F32)
    ms = jnp.mean(acc * acc, axis=-1, keepdims=True)
    out_ref[0] = acc * lax.rsqrt(ms + EPS) * g_ref[...]


def _combine(pos_t, gate_t, y, x1, gain, tt=512):
    b, s, d = x1.shape
    n_e, cap = y.shape[1:3]
    return pl.pallas_call(
        _combine_kernel,
        grid=(b, s // tt),
        in_specs=[
            pl.BlockSpec((1, tt, n_e), lambda bi, j: (bi, j, 0)),
            pl.BlockSpec((1, tt, n_e), lambda bi, j: (bi, j, 0)),
            pl.BlockSpec((1, n_e, cap, d), lambda bi, j: (bi, 0, 0, 0)),
            pl.BlockSpec((1, tt, d), lambda bi, j: (bi, j, 0)),
            pl.BlockSpec((1, d), lambda bi, j: (0, 0)),
        ],
        out_specs=pl.BlockSpec((1, tt, d), lambda bi, j: (bi, j, 0)),
        out_shape=jax.ShapeDtypeStruct((b, s, d), F32),
        compiler_params=_params("parallel", "parallel"),
        name="combine",
    )(pos_t, gate_t, y, x1, gain)


def _rope_tables(seq):
    half = HEAD_DIM // 2
    pos = jnp.arange(seq, dtype=F32)
    inv = ROPE_BASE ** (-jnp.arange(0, HEAD_DIM, 2, dtype=F32) / HEAD_DIM)
    ang = pos[:, None] * inv[None, :]
    cos, sin = jnp.cos(ang), jnp.sin(ang)
    reps = LANES // HEAD_DIM
    cos_t = jnp.tile(jnp.concatenate([cos, cos], axis=1), (1, reps))
    sin_t = jnp.tile(jnp.concatenate([-sin, sin], axis=1), (1, reps))
    return cos_t, sin_t


def kernel(x, norm1_gain, w_in, ret_log_decay, ret_gn_gain, rel_bias, w_out, norm2_gain, w_router,
           w_gate, w_up, w_down, final_gain):
    b, s, d = x.shape
    depth = w_in.shape[0]
    ret_width = ret_gn_gain.shape[1]
    n_ret_pairs = ret_width // LANES
    att_width = (w_in.shape[2] - 4 * ret_width) // 3
    n_att_pairs = att_width // LANES
    cap = CAPACITY_FACTOR * s // N_EXPERTS
    cos_t, sin_t = _rope_tables(s)

    assert depth == 1, "single-layer block: the final norm is fused into the combine kernel"
    layer = 0
    ret, att = _proj(x.reshape(b * s, d), norm1_gain[layer][None, :], w_in[layer].astype(BF16),
                     n_ret=4 * ret_width)
    r = _retention(ret.reshape(b, s, -1), ret_log_decay[layer], cos_t, sin_t,
                   ret_gn_gain[layer][None, :], n_ret_pairs)
    a = _dilated(att.reshape(b, s, -1), rel_bias, n_att_pairs)
    x1, h2, aff = _outproj(r, a, x, w_out[layer].astype(BF16), norm2_gain[layer][None, :],
                           w_router[layer].T)
    posm = _select(aff.reshape(b * N_EXPERTS, s), cap).reshape(b, N_EXPERTS, s)
    y = _ffn(posm.reshape(b, N_EXPERTS, 1, s), h2.reshape(b, s, d), w_gate[layer].astype(BF16),
             w_up[layer].astype(BF16), w_down[layer].astype(BF16), cap)
    return _combine(jnp.swapaxes(posm, 1, 2), jnp.swapaxes(aff, 1, 2), y, x1.reshape(b, s, d),
                    final_gain[None, :])
```

```python
import functools
import math

import numpy as np
import jax
import jax.numpy as jnp
from jax import lax
from jax.experimental import pallas as pl
from jax.experimental.pallas import tpu as pltpu

F32 = jnp.float32
BF16 = jnp.bfloat16
I32 = jnp.int32

HEAD_DIM = 64
LANES = 128
RET_CHUNK = 128
ROPE_BASE = 10000.0
DILATED_PATTERNS = ((128, 1), (512, 4), (2048, 16))
SIDE = 64
N_BUCKETS = 32
MAX_DISTANCE = 1024
N_EXPERTS = 16
CAPACITY_FACTOR = 2
MOE_TILE = 512
SLOT_WIN = 256
CNT_STRIDE = 16
EPS = 1e-6
NEG = -1e30
VMEM_LIMIT = 56 * 1024 * 1024

_NT = (((1,), (1,)), ((), ()))
_TN = (((0,), (0,)), ((), ()))


def _params(*sem):
    return pltpu.CompilerParams(dimension_semantics=sem, vmem_limit_bytes=VMEM_LIMIT)


def _proj_kernel(x_ref, g_ref, w_ref, ret_ref, att_ref, *, n_ret, chunk):
    x = x_ref[...]
    ms = jnp.mean(x * x, axis=-1, keepdims=True)
    h = (x * lax.rsqrt(ms + EPS) * g_ref[...]).astype(BF16)
    n_cols = w_ref.shape[1]
    for c0 in range(0, n_cols, chunk):
        o = jnp.dot(h, w_ref[:, c0:c0 + chunk], preferred_element_type=F32)
        if c0 < n_ret:
            ret_ref[:, c0:c0 + chunk] = o.astype(BF16)
        else:
            att_ref[:, c0 - n_ret:c0 - n_ret + chunk] = o


def _proj(x2d, gain, w_bf16, n_ret, tm=512):
    t, d = x2d.shape
    n_cols = w_bf16.shape[1]
    n_att = n_cols - n_ret
    return pl.pallas_call(
        functools.partial(_proj_kernel, n_ret=n_ret, chunk=512),
        grid=(t // tm,),
        in_specs=[
            pl.BlockSpec((tm, d), lambda i: (i, 0)),
            pl.BlockSpec((1, d), lambda i: (0, 0)),
            pl.BlockSpec((d, n_cols), lambda i: (0, 0)),
        ],
        out_specs=[
            pl.BlockSpec((tm, n_ret), lambda i: (i, 0)),
            pl.BlockSpec((tm, n_att), lambda i: (i, 0)),
        ],
        out_shape=[
            jax.ShapeDtypeStruct((t, n_ret), BF16),
            jax.ShapeDtypeStruct((t, n_att), F32),
        ],
        compiler_params=_params("parallel"),
        name="proj",
    )(x2d, gain, w_bf16)


def _retention_kernel(decay_ref, q_ref, k_ref, v_ref, g_ref, cos_ref, sin_ref, gn_ref, out_ref,
                      qs_ref, ks_ref, rb_ref, *, seq):
    c = RET_CHUNK
    n_chunks = seq // c
    p = pl.program_id(1)
    lane = lax.broadcasted_iota(I32, (c, LANES), 1)
    row = lax.broadcasted_iota(I32, (c, LANES), 0)
    head0 = lane < HEAD_DIM
    rowf = row.astype(F32)

    lgf0, lgf1 = decay_ref[0, 2 * p], decay_ref[0, 2 * p + 1]
    lgb0, lgb1 = decay_ref[1, 2 * p], decay_ref[1, 2 * p + 1]
    lgf_lane = jnp.where(head0, lgf0, lgf1)
    lgb_lane = jnp.where(head0, lgb0, lgb1)
    lgf_row = jnp.where(row < HEAD_DIM, lgf0, lgf1)
    lgb_row = jnp.where(row < HEAD_DIM, lgb0, lgb1)
    same_head = (row < HEAD_DIM) == head0

    zeta_f = jnp.exp((c - 1 - rowf) * lgf_lane)
    zeta_b = jnp.exp(rowf * lgb_lane)
    xi_f = jnp.exp((rowf + 1.0) * lgf_lane)
    xi_b = jnp.exp((c - rowf) * lgb_lane)
    gch_f = jnp.where(same_head, jnp.exp(c * lgf_row), 0.0)
    gch_b = jnp.where(same_head, jnp.exp(c * lgb_row), 0.0)

    diff = (row - lane).astype(F32)

    def dmat(lf, lb):
        return jnp.exp(jnp.where(diff >= 0, diff * lf, -diff * lb))

    d_stack = jnp.concatenate([dmat(lgf0, lgb0), dmat(lgf1, lgb1)], axis=0)

    first_half = (lane % HEAD_DIM) < (HEAD_DIM // 2)
    rows_per_step = 4 * c

    def rope_step(i, carry):
        r0 = pl.multiple_of(i * rows_per_step, rows_per_step)
        sl = pl.ds(r0, rows_per_step)
        cos = cos_ref[sl, :]
        sin = sin_ref[sl, :]
        fh = jnp.concatenate([first_half] * 4, axis=0)

        def rot(t):
            sw = jnp.where(fh, pltpu.roll(t, LANES - HEAD_DIM // 2, 1), pltpu.roll(t, HEAD_DIM // 2, 1))
            return t * cos + sw * sin

        qs_ref[sl, :] = rot(q_ref[0, sl, :].astype(F32)).astype(BF16)
        ks_ref[sl, :] = (rot(k_ref[0, sl, :].astype(F32)) * (HEAD_DIM ** -0.5)).astype(BF16)
        return carry

    lax.fori_loop(0, seq // rows_per_step, rope_step, 0)

    def chunk_kv(n, zeta):
        sl = pl.ds(pl.multiple_of(n * c, c), c)
        kw = (ks_ref[sl, :].astype(F32) * zeta).astype(BF16)
        return lax.dot_general(kw, v_ref[0, sl, :], _TN, preferred_element_type=F32)

    def bwd_step(i, state):
        n = n_chunks - 1 - i
        rb_ref[n] = state.astype(BF16)
        return gch_b * state + jnp.where(same_head, chunk_kv(n, zeta_b), 0.0)

    lax.fori_loop(0, n_chunks, bwd_step, jnp.zeros((LANES, LANES), F32), unroll=4)

    gn_gain = gn_ref[...]

    def fwd_step(n, state):
        sl = pl.ds(pl.multiple_of(n * c, c), c)
        q = qs_ref[sl, :]
        k = ks_ref[sl, :]
        v = v_ref[0, sl, :]
        zero = jnp.zeros_like(q)
        q_stack = jnp.concatenate([jnp.where(head0, q, zero), jnp.where(head0, zero, q)], axis=0)
        scores = lax.dot_general(q_stack, k, _NT, preferred_element_type=F32) * d_stack
        sb = scores.astype(BF16)
        s_cat = jnp.concatenate([sb[:c], sb[c:]], axis=1)
        v_stack = jnp.concatenate([jnp.where(head0, v, zero), jnp.where(head0, zero, v)], axis=0)
        inner = jnp.dot(s_cat, v_stack, preferred_element_type=F32)
        cross_f = jnp.dot(q, state.astype(BF16), preferred_element_type=F32) * xi_f
        cross_b = jnp.dot(q, rb_ref[n], preferred_element_type=F32) * xi_b
        r = inner + cross_f + cross_b

        def half_mean(t):
            s0 = jnp.sum(jnp.where(head0, t, 0.0), axis=-1, keepdims=True)
            s1 = jnp.sum(jnp.where(head0, 0.0, t), axis=-1, keepdims=True)
            return jnp.where(head0, s0, s1) * (1.0 / HEAD_DIM)

        dlt = r - half_mean(r)
        var = half_mean(dlt * dlt)
        gate = g_ref[0, sl, :].astype(F32)
        y = dlt * lax.rsqrt(var + EPS) * gn_gain * (gate * jax.nn.sigmoid(gate))
        out_ref[0, sl, :] = y.astype(out_ref.dtype)
        return gch_f * state + jnp.where(same_head, chunk_kv(n, zeta_f), 0.0)

    lax.fori_loop(0, n_chunks, fwd_step, jnp.zeros((LANES, LANES), F32), unroll=8)


def _retention(ret, decay, cos_t, sin_t, gn_gain, n_pairs):
    b, s, _ = ret.shape
    blk = lambda off: pl.BlockSpec((1, s, LANES), lambda bi, pi: (bi, 0, off + pi))
    full = pl.BlockSpec((s, LANES), lambda bi, pi: (0, 0))
    return pl.pallas_call(
        functools.partial(_retention_kernel, seq=s),
        grid=(b, n_pairs),
        in_specs=[
            pl.BlockSpec(memory_space=pltpu.SMEM),
            blk(0), blk(n_pairs), blk(2 * n_pairs), blk(3 * n_pairs),
            full, full,
            pl.BlockSpec((1, LANES), lambda bi, pi: (0, pi)),
        ],
        out_specs=pl.BlockSpec((1, s, LANES), lambda bi, pi: (bi, 0, pi)),
        out_shape=jax.ShapeDtypeStruct((b, s, n_pairs * LANES), BF16),
        scratch_shapes=[
            pltpu.VMEM((s, LANES), BF16),
            pltpu.VMEM((s, LANES), BF16),
            pltpu.VMEM((s // RET_CHUNK, LANES, LANES), BF16),
        ],
        compiler_params=_params("parallel", "parallel"),
        name="retention",
    )(decay, ret, ret, ret, ret, cos_t, sin_t, gn_gain)


QBLK = 2 * SIDE
KWIN = 4 * SIDE
BASE_W = 512


def _t5_bucket_np(rel):
    half = N_BUCKETS // 2
    max_exact = half // 2
    bucket = np.where(rel > 0, half, 0)
    n = np.abs(rel)
    nf = np.maximum(n, 1).astype(np.float32)
    large = max_exact + (np.log(nf / np.float32(max_exact)) / np.float32(math.log(MAX_DISTANCE / max_exact))
                         * np.float32(half - max_exact)).astype(np.int32)
    large = np.minimum(large, half - 1)
    return (bucket + np.where(n < max_exact, n, large)).astype(np.int32)


def _bucket_rows():
    k = np.arange(BASE_W)
    off = k - SIDE
    rows = []
    for _, dilation in DILATED_PATTERNS:
        rows.append(np.where(k <= 2 * SIDE, _t5_bucket_np(off * dilation), -1))
    return np.stack(rows).astype(np.int32)


def _dilated_kernel(bias_tab_ref, bucket_ref, q_ref, k_ref, v_ref, out_ref,
                    bias_ref, qp_ref, kp_ref, vp_ref, stage_ref, o_ref, lse_ref, *, seq):
    assert len(DILATED_PATTERNS) == 3 and DILATED_PATTERNS[0][1] == 1
    p = pl.program_id(1)
    lane = lax.broadcasted_iota(I32, (QBLK, LANES), 1)
    head0 = lane < HEAD_DIM
    n_blocks = seq // QBLK

    for pi in range(len(DILATED_PATTERNS)):
        bucket = jnp.broadcast_to(bucket_ref[pi:pi + 1, :], (QBLK, BASE_W))
        for hh in range(2):
            base = jnp.full((QBLK, BASE_W), NEG, F32)
            for bk in range(N_BUCKETS):
                base = jnp.where(bucket == bk, bias_tab_ref[bk, 2 * p + hh], base)
            for var, shift in enumerate((BASE_W - SIDE, 0, SIDE)):
                tile = pltpu.roll(base, shift, 1, stride=1, stride_axis=0)
                bias_ref[(pi * 3 + var) * 2 + hh] = tile[:, :KWIN]

    head0_k = lax.broadcasted_iota(I32, (KWIN, LANES), 1) < HEAD_DIM
    prev_d = 1
    for pi, (_, d) in enumerate(DILATED_PATTERNS):
        seg_len = seq // d
        blocks_per_seg = seg_len // QBLK

        step, prev_len = d // prev_d, seq // prev_d
        assert step * prev_d == d
        keep_f32 = 0 < pi < len(DILATED_PATTERNS) - 1
        for i, (src_ref, dst_ref, scale) in enumerate(
                ((q_ref, qp_ref, HEAD_DIM ** -0.5), (k_ref, kp_ref, None), (v_ref, vp_ref, None))):
            for r_prev in range(prev_d):
                for r_step in range(step):
                    src = pl.ds(r_prev * prev_len + r_step, seg_len, stride=step) if step > 1 else pl.ds(0, seq)
                    dst = pl.ds((r_prev + prev_d * r_step) * seg_len, seg_len)
                    val = src_ref[0, src, :] if prev_d == 1 else stage_ref[i, src, :]
                    if keep_f32:
                        stage_ref[i, dst, :] = val
                    dst_ref[dst, :] = (val if scale is None else val * scale).astype(BF16)
        prev_d = d

        def block(t, carry, pi=pi, d=d, seg_len=seg_len, blocks_per_seg=blocks_per_seg):
            seg = t // blocks_per_seg
            u = t % blocks_per_seg
            q0 = pl.multiple_of(t * QBLK, QBLK)
            seg0 = seg * seg_len
            k0 = pl.multiple_of(jnp.clip(q0 - SIDE, seg0, seg0 + seg_len - KWIN), SIDE)
            var = jnp.where(u == 0, 0, jnp.where(u == blocks_per_seg - 1, 2, 1))
            q = qp_ref[pl.ds(q0, QBLK), :]
            kw = kp_ref[pl.ds(k0, KWIN), :]
            vw = vp_ref[pl.ds(k0, KWIN), :]
            zero = jnp.zeros_like(q)
            one = jnp.ones_like(vw)
            pvs, ms = [], []
            for hh in range(2):
                qm = jnp.where(head0, q, zero) if hh == 0 else jnp.where(head0, zero, q)
                vm = jnp.where(head0_k, vw, one) if hh == 0 else jnp.where(head0_k, one, vw)
                s = lax.dot_general(qm, kw, _NT, preferred_element_type=F32)
                s = s + bias_ref[(pi * 3 + var) * 2 + hh]
                m = jnp.max(s, axis=-1, keepdims=True)
                e = jnp.exp(s - m)
                pvs.append(jnp.dot(e.astype(BF16), vm, preferred_element_type=F32))
                ms.append(m)
            num = jnp.where(head0, pvs[0], pvs[1])
            den = pltpu.roll(jnp.where(head0, pvs[1], pvs[0]), HEAD_DIM, 1)
            o = num * (1.0 / den)
            lse = jnp.where(head0, ms[0], ms[1]) + jnp.log(den)
            if d > 1:
                dst = pl.ds(seg + d * (u * QBLK), QBLK, stride=d)
            else:
                dst = pl.ds(q0, QBLK)
            o_ref[pi, dst, :] = o
            lse_ref[pi, dst, :] = lse
            return carry

        lax.fori_loop(0, n_blocks, block, 0, unroll=8)

    rows = 4 * QBLK

    def mix(i, carry):
        sl = pl.ds(pl.multiple_of(i * rows, rows), rows)
        l0, l1, l2 = lse_ref[0, sl, :], lse_ref[1, sl, :], lse_ref[2, sl, :]
        mx = jnp.maximum(jnp.maximum(l0, l1), l2)
        e0, e1, e2 = jnp.exp(l0 - mx), jnp.exp(l1 - mx), jnp.exp(l2 - mx)
        inv = 1.0 / (e0 + e1 + e2)
        acc = (e0 * inv) * o_ref[0, sl, :] + (e1 * inv) * o_ref[1, sl, :] + (e2 * inv) * o_ref[2, sl, :]
        out_ref[0, sl, :] = acc.astype(out_ref.dtype)
        return carry

    lax.fori_loop(0, seq // rows, mix, 0)


def _dilated(att, rel_bias, n_pairs):
    b, s, _ = att.shape
    n_pat = len(DILATED_PATTERNS)
    blk = lambda off: pl.BlockSpec((1, s, LANES), lambda bi, pi: (bi, 0, off + pi))
    bucket_rows = jnp.asarray(_bucket_rows())
    return pl.pallas_call(
        functools.partial(_dilated_kernel, seq=s),
        grid=(b, n_pairs),
        in_specs=[
            pl.BlockSpec(memory_space=pltpu.SMEM),
            pl.BlockSpec((n_pat, BASE_W), lambda bi, pi: (0, 0)),
            blk(0), blk(n_pairs), blk(2 * n_pairs),
        ],
        out_specs=pl.BlockSpec((1, s, LANES), lambda bi, pi: (bi, 0, pi)),
        out_shape=jax.ShapeDtypeStruct((b, s, n_pairs * LANES), BF16),
        scratch_shapes=[
            pltpu.VMEM((n_pat * 3 * 2, QBLK, KWIN), F32),
            pltpu.VMEM((s, LANES), BF16),
            pltpu.VMEM((s, LANES), BF16),
            pltpu.VMEM((s, LANES), BF16),
            pltpu.VMEM((3, s, LANES), F32),
            pltpu.VMEM((n_pat, s, LANES), F32),
            pltpu.VMEM((n_pat, s, LANES), F32),
        ],
        compiler_params=_params("parallel", "parallel"),
        name="dilated",
    )(rel_bias, bucket_rows, att, att, att)


def _outproj_kernel(r_ref, a_ref, x_ref, w_ref, g_ref, wr_ref, x1_ref, h2_ref, aff_ref):
    n_r = r_ref.shape[1]
    x1 = (x_ref[...]
          + jnp.dot(r_ref[...], w_ref[:n_r, :], preferred_element_type=F32)
          + jnp.dot(a_ref[...], w_ref[n_r:, :], preferred_element_type=F32))
    x1_ref[...] = x1
    ms = jnp.mean(x1 * x1, axis=-1, keepdims=True)
    h2 = x1 * lax.rsqrt(ms + EPS) * g_ref[...]
    h2_ref[...] = h2.astype(BF16)
    logits = lax.dot_general(wr_ref[...], h2, _NT, preferred_element_type=F32,
                             precision=lax.Precision.HIGHEST)
    m = jnp.max(logits, axis=0, keepdims=True)
    e = jnp.exp(logits - m)
    aff_ref[0] = e / jnp.sum(e, axis=0, keepdims=True)


def _outproj(r, a, x, w_bf16, gain, w_router_t, tm=512):
    b, s, d = x.shape
    n_r, n_a = r.shape[-1], a.shape[-1]
    n_e = w_router_t.shape[0]
    spt = s // tm
    tok = lambda width: pl.BlockSpec((tm, width), lambda i: (i, 0))
    const = lambda shape: pl.BlockSpec(shape, lambda i: (0, 0))
    return pl.pallas_call(
        _outproj_kernel,
        grid=(b * spt,),
        in_specs=[tok(n_r), tok(n_a), tok(d), const((n_r + n_a, d)), const((1, d)), const((n_e, d))],
        out_specs=[tok(d), tok(d), pl.BlockSpec((1, n_e, tm), lambda i: (i // spt, 0, i % spt))],
        out_shape=[
            jax.ShapeDtypeStruct((b * s, d), F32),
            jax.ShapeDtypeStruct((b * s, d), BF16),
            jax.ShapeDtypeStruct((b, n_e, s), F32),
        ],
        compiler_params=_params("parallel"),
        name="outproj",
    )(r.reshape(b * s, n_r), a.reshape(b * s, n_a), x.reshape(b * s, d), w_bf16, gain, w_router_t)


def _select_kernel(aff_ref, posm_ref, cnt_ref, *, cap):
    rows, s = aff_ref.shape
    bits = pltpu.bitcast(aff_ref[...], I32)

    def search(i, thr):
        cand = thr | jnp.left_shift(jnp.int32(1), 30 - i)
        cnt = jnp.sum((bits >= cand).astype(I32), axis=1, keepdims=True)
        return jnp.where(cnt >= cap, cand, thr)

    thr = lax.fori_loop(0, 31, search, jnp.zeros((rows, 1), I32))
    gt = bits > thr
    eq = bits == thr
    need = cap - jnp.sum(gt.astype(I32), axis=1, keepdims=True)

    w = LANES
    tri = (lax.broadcasted_iota(I32, (w, w), 0) <= lax.broadcasted_iota(I32, (w, w), 1)).astype(BF16)

    def excl_prefix(flags):
        carry = jnp.zeros((rows, 1), F32)
        out = []
        for c0 in range(0, s, w):
            f = flags[:, c0:c0 + w].astype(BF16)
            inc = jnp.dot(f, tri, preferred_element_type=F32)
            out.append(inc - f.astype(F32) + carry)
            carry = carry + inc[:, w - 1:w]
        return jnp.concatenate(out, axis=1).astype(I32)

    sel = gt | (eq & (excl_prefix(eq) < need))
    posm_ref[...] = jnp.where(sel, excl_prefix(sel), -1)

    tok = lax.broadcasted_iota(I32, (w, w), 0)
    edge = lax.broadcasted_iota(I32, (w, w), 1) * MOE_TILE
    cnt = jnp.zeros((rows, w), F32)
    for c0 in range(0, s, w):
        before = ((tok + c0) < edge).astype(BF16)
        cnt = cnt + jnp.dot(sel[:, c0:c0 + w].astype(BF16), before, preferred_element_type=F32)
    cnt_ref[...] = cnt.astype(I32)


def _select(aff_rows, cap):
    rows, s = aff_rows.shape
    return pl.pallas_call(
        functools.partial(_select_kernel, cap=cap),
        out_shape=[jax.ShapeDtypeStruct((rows, s), I32), jax.ShapeDtypeStruct((rows, LANES), I32)],
        compiler_params=pltpu.CompilerParams(vmem_limit_bytes=VMEM_LIMIT),
        name="select",
    )(aff_rows)


def _slot_window(cnt_ref, row, j, cap):
    lo, hi = cnt_ref[row * CNT_STRIDE + j], cnt_ref[row * CNT_STRIDE + j + 1]
    w0 = pl.multiple_of(jnp.minimum((lo // LANES) * LANES, cap - SLOT_WIN), LANES)
    return hi, w0


def _gather_kernel(cnt_ref, posm_ref, h2_ref, xg_ref):
    bi, j = pl.program_id(0), pl.program_id(1)
    n_e, cap = xg_ref.shape[1:3]
    tt = h2_ref.shape[1]

    @pl.when(j == 0)
    def _():
        xg_ref[...] = jnp.zeros_like(xg_ref)

    slot0 = lax.broadcasted_iota(I32, (SLOT_WIN, tt), 0)
    windows = [_slot_window(cnt_ref, bi * n_e + e, j, cap) for e in range(n_e)]
    for e, (hi, w0) in enumerate(windows):
        rows = jnp.dot((posm_ref[0, e:e + 1, :] == slot0 + w0).astype(BF16), h2_ref[0],
                       preferred_element_type=F32)
        xg_ref[0, e, pl.ds(w0, SLOT_WIN), :] += rows.astype(BF16)

    @pl.when(functools.reduce(jnp.logical_or, [hi > w0 + SLOT_WIN for hi, w0 in windows]))
    def _():
        for e, (hi, w0) in enumerate(windows):
            @pl.when(hi > w0 + SLOT_WIN)
            def _(e=e, w0=w0):
                for ws in range(0, cap, SLOT_WIN):
                    slot = slot0 + ws
                    onehot = ((posm_ref[0, e:e + 1, :] == slot) & (slot >= w0 + SLOT_WIN)).astype(BF16)
                    extra = jnp.dot(onehot, h2_ref[0], preferred_element_type=F32)
                    xg_ref[0, e, ws:ws + SLOT_WIN, :] += extra.astype(BF16)


def _gather(cnt, posm, h2, cap):
    b, n_e, s = posm.shape
    d = h2.shape[2]
    return pl.pallas_call(
        _gather_kernel,
        grid_spec=pltpu.PrefetchScalarGridSpec(
            num_scalar_prefetch=1,
            grid=(b, s // MOE_TILE),
            in_specs=[
                pl.BlockSpec((1, n_e, MOE_TILE), lambda bi, j, cnt: (bi, 0, j)),
                pl.BlockSpec((1, MOE_TILE, d), lambda bi, j, cnt: (bi, j, 0)),
            ],
            out_specs=pl.BlockSpec((1, n_e, cap, d), lambda bi, j, cnt: (bi, 0, 0, 0)),
        ),
        out_shape=jax.ShapeDtypeStruct((b, n_e, cap, d), BF16),
        compiler_params=_params("parallel", "arbitrary"),
        name="gather",
    )(cnt, posm, h2)


def _ffn_kernel(xg_ref, wg_ref, wu_ref, wd_ref, y_ref, acc_ref, *, tf):
    f_total = wg_ref.shape[2]
    xg = xg_ref[0, 0]
    for fi in range(f_total // tf):
        fs = slice(fi * tf, (fi + 1) * tf)
        gate = jnp.dot(xg, wg_ref[0, :, fs], preferred_element_type=F32)
        up = jnp.dot(xg, wu_ref[0, :, fs], preferred_element_type=F32)
        hid = ((gate * jax.nn.sigmoid(gate)) * up).astype(BF16)
        part = jnp.dot(hid, wd_ref[0, fs, :], preferred_element_type=F32)
        if fi == 0:
            acc_ref[...] = part
        else:
            acc_ref[...] += part
    y_ref[0, 0] = acc_ref[...].astype(y_ref.dtype)


def _ffn(xg, wg, wu, wd, tf=512):
    b, n_e, cap, d = xg.shape
    f = wg.shape[2]
    tok = pl.BlockSpec((1, 1, cap, d), lambda e, bi: (bi, e, 0, 0))
    return pl.pallas_call(
        functools.partial(_ffn_kernel, tf=tf),
        grid=(n_e, b),
        in_specs=[
            tok,
            pl.BlockSpec((1, d, f), lambda e, bi: (e, 0, 0)),
            pl.BlockSpec((1, d, f), lambda e, bi: (e, 0, 0)),
            pl.BlockSpec((1, f, d), lambda e, bi: (e, 0, 0)),
        ],
        out_specs=tok,
        out_shape=jax.ShapeDtypeStruct((b, n_e, cap, d), BF16),
        scratch_shapes=[pltpu.VMEM((cap, d), F32)],
        compiler_params=_params("parallel", "parallel"),
        name="ffn",
    )(xg, wg, wu, wd)


def _combine_kernel(cnt_ref, pos_ref, gate_ref, y_ref, x1_ref, g_ref, out_ref, acc_ref):
    bi, j = pl.program_id(0), pl.program_id(1)
    tt = x1_ref.shape[1]
    n_e, cap = y_ref.shape[1:3]
    slot0 = lax.broadcasted_iota(I32, (tt, SLOT_WIN), 1)
    windows = [_slot_window(cnt_ref, bi * n_e + e, j, cap) for e in range(n_e)]
    acc = x1_ref[0]
    for e, (hi, w0) in enumerate(windows):
        scatter = jnp.where(pos_ref[0, :, e:e + 1] == slot0 + w0, gate_ref[0, :, e:e + 1], 0.0)
        acc = acc + jnp.dot(scatter.astype(BF16), y_ref[0, e, pl.ds(w0, SLOT_WIN), :],
                            preferred_element_type=F32)
    acc_ref[...] = acc

    @pl.when(functools.reduce(jnp.logical_or, [hi > w0 + SLOT_WIN for hi, w0 in windows]))
    def _():
        for e, (hi, w0) in enumerate(windows):
            @pl.when(hi > w0 + SLOT_WIN)
            def _(e=e, w0=w0):
                for ws in range(0, cap, SLOT_WIN):
                    slot = slot0 + ws
                    extra = jnp.where((pos_ref[0, :, e:e + 1] == slot) & (slot >= w0 + SLOT_WIN),
                                      gate_ref[0, :, e:e + 1], 0.0).astype(BF16)
                    acc_ref[...] += jnp.dot(extra, y_ref[0, e, ws:ws + SLOT_WIN, :],
                                            preferred_element_type=F32)

    acc = acc_ref[...]
    ms = jnp.mean(acc * acc, axis=-1, keepdims=True)
    out_ref[0] = acc * lax.rsqrt(ms + EPS) * g_ref[...]


def _combine(cnt, pos_t, gate_t, y, x1, gain):
    b, s, d = x1.shape
    n_e, cap = y.shape[1:3]
    tt = MOE_TILE
    return pl.pallas_call(
        _combine_kernel,
        grid_spec=pltpu.PrefetchScalarGridSpec(
            num_scalar_prefetch=1,
            grid=(b, s // tt),
            in_specs=[
                pl.BlockSpec((1, tt, n_e), lambda bi, j, cnt: (bi, j, 0)),
                pl.BlockSpec((1, tt, n_e), lambda bi, j, cnt: (bi, j, 0)),
                pl.BlockSpec((1, n_e, cap, d), lambda bi, j, cnt: (bi, 0, 0, 0)),
                pl.BlockSpec((1, tt, d), lambda bi, j, cnt: (bi, j, 0)),
                pl.BlockSpec((1, d), lambda bi, j, cnt: (0, 0)),
            ],
            out_specs=pl.BlockSpec((1, tt, d), lambda bi, j, cnt: (bi, j, 0)),
            scratch_shapes=[pltpu.VMEM((tt, d), F32)],
        ),
        out_shape=jax.ShapeDtypeStruct((b, s, d), F32),
        compiler_params=_params("parallel", "parallel"),
        name="combine",
    )(cnt, pos_t, gate_t, y, x1, gain)


def _rope_tables(seq):
    half = HEAD_DIM // 2
    pos = jnp.arange(seq, dtype=F32)
    inv = ROPE_BASE ** (-jnp.arange(0, HEAD_DIM, 2, dtype=F32) / HEAD_DIM)
    ang = pos[:, None] * inv[None, :]
    cos, sin = jnp.cos(ang), jnp.sin(ang)
    reps = LANES // HEAD_DIM
    cos_t = jnp.tile(jnp.concatenate([cos, cos], axis=1), (1, reps))
    sin_t = jnp.tile(jnp.concatenate([-sin, sin], axis=1), (1, reps))
    return cos_t, sin_t


def kernel(x, norm1_gain, w_in, ret_log_decay, ret_gn_gain, rel_bias, w_out, norm2_gain, w_router,
           w_gate, w_up, w_down, final_gain):
    b, s, d = x.shape
    depth = w_in.shape[0]
    ret_width = ret_gn_gain.shape[1]
    n_ret_pairs = ret_width // LANES
    att_width = (w_in.shape[2] - 4 * ret_width) // 3
    n_att_pairs = att_width // LANES
    cap = CAPACITY_FACTOR * s // N_EXPERTS
    cos_t, sin_t = _rope_tables(s)

    assert depth == 1, "single-layer block: the final norm is fused into the combine kernel"
    layer = 0
    ret, att = _proj(x.reshape(b * s, d), norm1_gain[layer][None, :], w_in[layer].astype(BF16),
                     n_ret=4 * ret_width)
    r = _retention(ret.reshape(b, s, -1), ret_log_decay[layer], cos_t, sin_t,
                   ret_gn_gain[layer][None, :], n_ret_pairs)
    a = _dilated(att.reshape(b, s, -1), rel_bias, n_att_pairs)
    x1, h2, aff = _outproj(r, a, x, w_out[layer].astype(BF16), norm2_gain[layer][None, :],
                           w_router[layer].T)
    assert s % MOE_TILE == 0 and s // MOE_TILE < CNT_STRIDE and cap % SLOT_WIN == 0
    posm, cnt = _select(aff.reshape(b * N_EXPERTS, s), cap)
    posm = posm.reshape(b, N_EXPERTS, s)
    cnt = cnt[:, :CNT_STRIDE].reshape(-1)
    xg = _gather(cnt, posm, h2.reshape(b, s, d), cap)
    y = _ffn(xg, w_gate[layer].astype(BF16), w_up[layer].astype(BF16), w_down[layer].astype(BF16))
    return _combine(cnt, jnp.swapaxes(posm, 1, 2), jnp.swapaxes(aff, 1, 2), y, x1.reshape(b, s, d),
                    final_gain[None, :])
```

```python
import functools
import math

import numpy as np
import jax
import jax.numpy as jnp
from jax import lax
from jax.experimental import pallas as pl
from jax.experimental.pallas import tpu as pltpu

F32 = jnp.float32
BF16 = jnp.bfloat16
I32 = jnp.int32

HEAD_DIM = 64
LANES = 128
RET_CHUNK = 128
ROPE_BASE = 10000.0
DILATED_PATTERNS = ((128, 1), (512, 4), (2048, 16))
SIDE = 64
N_BUCKETS = 32
MAX_DISTANCE = 1024
N_EXPERTS = 16
CAPACITY_FACTOR = 2
MOE_TILE = 512
MXU_DEPTH = 256
SLOT_WIN = MXU_DEPTH // 2
SLOT_ALIGN = 16
CNT_STRIDE = 16
EPS = 1e-6
NEG = -1e30
VMEM_LIMIT = 56 * 1024 * 1024

_NT = (((1,), (1,)), ((), ()))
_TN = (((0,), (0,)), ((), ()))


def _params(*sem):
    return pltpu.CompilerParams(dimension_semantics=sem, vmem_limit_bytes=VMEM_LIMIT)


def _proj_kernel(x_ref, g_ref, w_ref, ret_ref, att_ref, *, n_ret, chunk):
    x = x_ref[...]
    ms = jnp.mean(x * x, axis=-1, keepdims=True)
    h = (x * lax.rsqrt(ms + EPS) * g_ref[...]).astype(BF16)
    n_cols = w_ref.shape[1]
    for c0 in range(0, n_cols, chunk):
        o = jnp.dot(h, w_ref[:, c0:c0 + chunk], preferred_element_type=F32)
        if c0 < n_ret:
            ret_ref[:, c0:c0 + chunk] = o.astype(BF16)
        else:
            att_ref[:, c0 - n_ret:c0 - n_ret + chunk] = o


def _proj(x2d, gain, w_bf16, n_ret, tm=512):
    t, d = x2d.shape
    n_cols = w_bf16.shape[1]
    n_att = n_cols - n_ret
    return pl.pallas_call(
        functools.partial(_proj_kernel, n_ret=n_ret, chunk=512),
        grid=(t // tm,),
        in_specs=[
            pl.BlockSpec((tm, d), lambda i: (i, 0)),
            pl.BlockSpec((1, d), lambda i: (0, 0)),
            pl.BlockSpec((d, n_cols), lambda i: (0, 0)),
        ],
        out_specs=[
            pl.BlockSpec((tm, n_ret), lambda i: (i, 0)),
            pl.BlockSpec((tm, n_att), lambda i: (i, 0)),
        ],
        out_shape=[
            jax.ShapeDtypeStruct((t, n_ret), BF16),
            jax.ShapeDtypeStruct((t, n_att), F32),
        ],
        compiler_params=_params("parallel"),
        name="proj",
    )(x2d, gain, w_bf16)


def _retention_kernel(decay_ref, q_ref, k_ref, v_ref, g_ref, cos_ref, sin_ref, gn_ref, out_ref,
                      qs_ref, ks_ref, rb_ref, *, seq):
    c = RET_CHUNK
    n_chunks = seq // c
    p = pl.program_id(1)
    lane = lax.broadcasted_iota(I32, (c, LANES), 1)
    row = lax.broadcasted_iota(I32, (c, LANES), 0)
    head0 = lane < HEAD_DIM
    rowf = row.astype(F32)

    lgf0, lgf1 = decay_ref[0, 2 * p], decay_ref[0, 2 * p + 1]
    lgb0, lgb1 = decay_ref[1, 2 * p], decay_ref[1, 2 * p + 1]
    lgf_lane = jnp.where(head0, lgf0, lgf1)
    lgb_lane = jnp.where(head0, lgb0, lgb1)
    lgf_row = jnp.where(row < HEAD_DIM, lgf0, lgf1)
    lgb_row = jnp.where(row < HEAD_DIM, lgb0, lgb1)
    same_head = (row < HEAD_DIM) == head0

    zeta_f = jnp.exp((c - 1 - rowf) * lgf_lane)
    zeta_b = jnp.exp(rowf * lgb_lane)
    xi_f = jnp.exp((rowf + 1.0) * lgf_lane)
    xi_b = jnp.exp((c - rowf) * lgb_lane)
    gch_f = jnp.where(same_head, jnp.exp(c * lgf_row), 0.0)
    gch_b = jnp.where(same_head, jnp.exp(c * lgb_row), 0.0)

    diff = (row - lane).astype(F32)

    def dmat(lf, lb):
        return jnp.exp(jnp.where(diff >= 0, diff * lf, -diff * lb))

    d_stack = jnp.concatenate([dmat(lgf0, lgb0), dmat(lgf1, lgb1)], axis=0)

    first_half = (lane % HEAD_DIM) < (HEAD_DIM // 2)
    rows_per_step = 4 * c

    def rope_step(i, carry):
        r0 = pl.multiple_of(i * rows_per_step, rows_per_step)
        sl = pl.ds(r0, rows_per_step)
        cos = cos_ref[sl, :]
        sin = sin_ref[sl, :]
        fh = jnp.concatenate([first_half] * 4, axis=0)

        def rot(t):
            sw = jnp.where(fh, pltpu.roll(t, LANES - HEAD_DIM // 2, 1), pltpu.roll(t, HEAD_DIM // 2, 1))
            return t * cos + sw * sin

        qs_ref[sl, :] = rot(q_ref[0, sl, :].astype(F32)).astype(BF16)
        ks_ref[sl, :] = (rot(k_ref[0, sl, :].astype(F32)) * (HEAD_DIM ** -0.5)).astype(BF16)
        return carry

    lax.fori_loop(0, seq // rows_per_step, rope_step, 0)

    def chunk_kv(n, zeta):
        sl = pl.ds(pl.multiple_of(n * c, c), c)
        kw = (ks_ref[sl, :].astype(F32) * zeta).astype(BF16)
        return lax.dot_general(kw, v_ref[0, sl, :], _TN, preferred_element_type=F32)

    def bwd_step(i, state):
        n = n_chunks - 1 - i
        rb_ref[n] = state.astype(BF16)
        return gch_b * state + jnp.where(same_head, chunk_kv(n, zeta_b), 0.0)

    lax.fori_loop(0, n_chunks, bwd_step, jnp.zeros((LANES, LANES), F32), unroll=4)

    gn_gain = gn_ref[...]

    def fwd_step(n, state):
        sl = pl.ds(pl.multiple_of(n * c, c), c)
        q = qs_ref[sl, :]
        k = ks_ref[sl, :]
        v = v_ref[0, sl, :]
        zero = jnp.zeros_like(q)
        q_stack = jnp.concatenate([jnp.where(head0, q, zero), jnp.where(head0, zero, q)], axis=0)
        scores = lax.dot_general(q_stack, k, _NT, preferred_element_type=F32) * d_stack
        sb = scores.astype(BF16)
        s_cat = jnp.concatenate([sb[:c], sb[c:]], axis=1)
        v_stack = jnp.concatenate([jnp.where(head0, v, zero), jnp.where(head0, zero, v)], axis=0)
        inner = jnp.dot(s_cat, v_stack, preferred_element_type=F32)
        cross_f = jnp.dot(q, state.astype(BF16), preferred_element_type=F32) * xi_f
        cross_b = jnp.dot(q, rb_ref[n], preferred_element_type=F32) * xi_b
        r = inner + cross_f + cross_b

        def half_mean(t):
            s0 = jnp.sum(jnp.where(head0, t, 0.0), axis=-1, keepdims=True)
            s1 = jnp.sum(jnp.where(head0, 0.0, t), axis=-1, keepdims=True)
            return jnp.where(head0, s0, s1) * (1.0 / HEAD_DIM)

        dlt = r - half_mean(r)
        var = half_mean(dlt * dlt)
        gate = g_ref[0, sl, :].astype(F32)
        y = dlt * lax.rsqrt(var + EPS) * gn_gain * (gate * jax.nn.sigmoid(gate))
        out_ref[0, sl, :] = y.astype(out_ref.dtype)
        return gch_f * state + jnp.where(same_head, chunk_kv(n, zeta_f), 0.0)

    lax.fori_loop(0, n_chunks, fwd_step, jnp.zeros((LANES, LANES), F32), unroll=8)


def _retention(ret, decay, cos_t, sin_t, gn_gain, n_pairs):
    b, s, _ = ret.shape
    blk = lambda off: pl.BlockSpec((1, s, LANES), lambda bi, pi: (bi, 0, off + pi))
    full = pl.BlockSpec((s, LANES), lambda bi, pi: (0, 0))
    return pl.pallas_call(
        functools.partial(_retention_kernel, seq=s),
        grid=(b, n_pairs),
        in_specs=[
            pl.BlockSpec(memory_space=pltpu.SMEM),
            blk(0), blk(n_pairs), blk(2 * n_pairs), blk(3 * n_pairs),
            full, full,
            pl.BlockSpec((1, LANES), lambda bi, pi: (0, pi)),
        ],
        out_specs=pl.BlockSpec((1, s, LANES), lambda bi, pi: (bi, 0, pi)),
        out_shape=jax.ShapeDtypeStruct((b, s, n_pairs * LANES), BF16),
        scratch_shapes=[
            pltpu.VMEM((s, LANES), BF16),
            pltpu.VMEM((s, LANES), BF16),
            pltpu.VMEM((s // RET_CHUNK, LANES, LANES), BF16),
        ],
        compiler_params=_params("parallel", "parallel"),
        name="retention",
    )(decay, ret, ret, ret, ret, cos_t, sin_t, gn_gain)


QBLK = 2 * SIDE
KWIN = 4 * SIDE
BASE_W = 512


def _t5_bucket_np(rel):
    half = N_BUCKETS // 2
    max_exact = half // 2
    bucket = np.where(rel > 0, half, 0)
    n = np.abs(rel)
    nf = np.maximum(n, 1).astype(np.float32)
    large = max_exact + (np.log(nf / np.float32(max_exact)) / np.float32(math.log(MAX_DISTANCE / max_exact))
                         * np.float32(half - max_exact)).astype(np.int32)
    large = np.minimum(large, half - 1)
    return (bucket + np.where(n < max_exact, n, large)).astype(np.int32)


def _bucket_rows():
    k = np.arange(BASE_W)
    off = k - SIDE
    rows = []
    for _, dilation in DILATED_PATTERNS:
        rows.append(np.where(k <= 2 * SIDE, _t5_bucket_np(off * dilation), -1))
    return np.stack(rows).astype(np.int32)


def _dilated_kernel(bias_tab_ref, bucket_ref, q_ref, k_ref, v_ref, out_ref,
                    bias_ref, qp_ref, kp_ref, vp_ref, stage_ref, o_ref, lse_ref, *, seq):
    assert len(DILATED_PATTERNS) == 3 and DILATED_PATTERNS[0][1] == 1
    p = pl.program_id(1)
    lane = lax.broadcasted_iota(I32, (QBLK, LANES), 1)
    head0 = lane < HEAD_DIM
    n_blocks = seq // QBLK

    for pi in range(len(DILATED_PATTERNS)):
        bucket = jnp.broadcast_to(bucket_ref[pi:pi + 1, :], (QBLK, BASE_W))
        for hh in range(2):
            base = jnp.full((QBLK, BASE_W), NEG, F32)
            for bk in range(N_BUCKETS):
                base = jnp.where(bucket == bk, bias_tab_ref[bk, 2 * p + hh], base)
            for var, shift in enumerate((BASE_W - SIDE, 0, SIDE)):
                tile = pltpu.roll(base, shift, 1, stride=1, stride_axis=0)
                bias_ref[(pi * 3 + var) * 2 + hh] = tile[:, :KWIN]

    head0_k = lax.broadcasted_iota(I32, (KWIN, LANES), 1) < HEAD_DIM
    prev_d = 1
    for pi, (_, d) in enumerate(DILATED_PATTERNS):
        seg_len = seq // d
        blocks_per_seg = seg_len // QBLK

        step, prev_len = d // prev_d, seq // prev_d
        assert step * prev_d == d
        keep_f32 = 0 < pi < len(DILATED_PATTERNS) - 1
        for i, (src_ref, dst_ref, scale) in enumerate(
                ((q_ref, qp_ref, HEAD_DIM ** -0.5), (k_ref, kp_ref, None), (v_ref, vp_ref, None))):
            for r_prev in range(prev_d):
                for r_step in range(step):
                    src = pl.ds(r_prev * prev_len + r_step, seg_len, stride=step) if step > 1 else pl.ds(0, seq)
                    dst = pl.ds((r_prev + prev_d * r_step) * seg_len, seg_len)
                    val = src_ref[0, src, :] if prev_d == 1 else stage_ref[i, src, :]
                    if keep_f32:
                        stage_ref[i, dst, :] = val
                    dst_ref[dst, :] = (val if scale is None else val * scale).astype(BF16)
        prev_d = d

        def block(t, carry, pi=pi, d=d, seg_len=seg_len, blocks_per_seg=blocks_per_seg):
            seg = t // blocks_per_seg
            u = t % blocks_per_seg
            q0 = pl.multiple_of(t * QBLK, QBLK)
            seg0 = seg * seg_len
            k0 = pl.multiple_of(jnp.clip(q0 - SIDE, seg0, seg0 + seg_len - KWIN), SIDE)
            var = jnp.where(u == 0, 0, jnp.where(u == blocks_per_seg - 1, 2, 1))
            q = qp_ref[pl.ds(q0, QBLK), :]
            kw = kp_ref[pl.ds(k0, KWIN), :]
            vw = vp_ref[pl.ds(k0, KWIN), :]
            zero = jnp.zeros_like(q)
            one = jnp.ones_like(vw)
            pvs, ms = [], []
            for hh in range(2):
                qm = jnp.where(head0, q, zero) if hh == 0 else jnp.where(head0, zero, q)
                vm = jnp.where(head0_k, vw, one) if hh == 0 else jnp.where(head0_k, one, vw)
                s = lax.dot_general(qm, kw, _NT, preferred_element_type=F32)
                s = s + bias_ref[(pi * 3 + var) * 2 + hh]
                m = jnp.max(s, axis=-1, keepdims=True)
                e = jnp.exp(s - m)
                pvs.append(jnp.dot(e.astype(BF16), vm, preferred_element_type=F32))
                ms.append(m)
            num = jnp.where(head0, pvs[0], pvs[1])
            den = pltpu.roll(jnp.where(head0, pvs[1], pvs[0]), HEAD_DIM, 1)
            o = num * (1.0 / den)
            lse = jnp.where(head0, ms[0], ms[1]) + jnp.log(den)
            if d > 1:
                dst = pl.ds(seg + d * (u * QBLK), QBLK, stride=d)
            else:
                dst = pl.ds(q0, QBLK)
            o_ref[pi, dst, :] = o
            lse_ref[pi, dst, :] = lse
            return carry

        lax.fori_loop(0, n_blocks, block, 0, unroll=8)

    rows = 4 * QBLK

    def mix(i, carry):
        sl = pl.ds(pl.multiple_of(i * rows, rows), rows)
        l0, l1, l2 = lse_ref[0, sl, :], lse_ref[1, sl, :], lse_ref[2, sl, :]
        mx = jnp.maximum(jnp.maximum(l0, l1), l2)
        e0, e1, e2 = jnp.exp(l0 - mx), jnp.exp(l1 - mx), jnp.exp(l2 - mx)
        inv = 1.0 / (e0 + e1 + e2)
        acc = (e0 * inv) * o_ref[0, sl, :] + (e1 * inv) * o_ref[1, sl, :] + (e2 * inv) * o_ref[2, sl, :]
        out_ref[0, sl, :] = acc.astype(out_ref.dtype)
        return carry

    lax.fori_loop(0, seq // rows, mix, 0)


def _dilated(att, rel_bias, n_pairs):
    b, s, _ = att.shape
    n_pat = len(DILATED_PATTERNS)
    blk = lambda off: pl.BlockSpec((1, s, LANES), lambda bi, pi: (bi, 0, off + pi))
    bucket_rows = jnp.asarray(_bucket_rows())
    return pl.pallas_call(
        functools.partial(_dilated_kernel, seq=s),
        grid=(b, n_pairs),
        in_specs=[
            pl.BlockSpec(memory_space=pltpu.SMEM),
            pl.BlockSpec((n_pat, BASE_W), lambda bi, pi: (0, 0)),
            blk(0), blk(n_pairs), blk(2 * n_pairs),
        ],
        out_specs=pl.BlockSpec((1, s, LANES), lambda bi, pi: (bi, 0, pi)),
        out_shape=jax.ShapeDtypeStruct((b, s, n_pairs * LANES), BF16),
        scratch_shapes=[
            pltpu.VMEM((n_pat * 3 * 2, QBLK, KWIN), F32),
            pltpu.VMEM((s, LANES), BF16),
            pltpu.VMEM((s, LANES), BF16),
            pltpu.VMEM((s, LANES), BF16),
            pltpu.VMEM((3, s, LANES), F32),
            pltpu.VMEM((n_pat, s, LANES), F32),
            pltpu.VMEM((n_pat, s, LANES), F32),
        ],
        compiler_params=_params("parallel", "parallel"),
        name="dilated",
    )(rel_bias, bucket_rows, att, att, att)


def _outproj_kernel(r_ref, a_ref, x_ref, w_ref, g_ref, wr_ref, x1_ref, h2_ref, aff_ref):
    n_r = r_ref.shape[1]
    x1 = (x_ref[...]
          + jnp.dot(r_ref[...], w_ref[:n_r, :], preferred_element_type=F32)
          + jnp.dot(a_ref[...], w_ref[n_r:, :], preferred_element_type=F32))
    x1_ref[...] = x1
    ms = jnp.mean(x1 * x1, axis=-1, keepdims=True)
    h2 = x1 * lax.rsqrt(ms + EPS) * g_ref[...]
    h2_ref[...] = h2.astype(BF16)
    logits = lax.dot_general(wr_ref[...], h2, _NT, preferred_element_type=F32,
                             precision=lax.Precision.HIGHEST)
    m = jnp.max(logits, axis=0, keepdims=True)
    e = jnp.exp(logits - m)
    aff_ref[0] = e / jnp.sum(e, axis=0, keepdims=True)


def _outproj(r, a, x, w_bf16, gain, w_router_t, tm=512):
    b, s, d = x.shape
    n_r, n_a = r.shape[-1], a.shape[-1]
    n_e = w_router_t.shape[0]
    spt = s // tm
    tok = lambda width: pl.BlockSpec((tm, width), lambda i: (i, 0))
    const = lambda shape: pl.BlockSpec(shape, lambda i: (0, 0))
    return pl.pallas_call(
        _outproj_kernel,
        grid=(b * spt,),
        in_specs=[tok(n_r), tok(n_a), tok(d), const((n_r + n_a, d)), const((1, d)), const((n_e, d))],
        out_specs=[tok(d), tok(d), pl.BlockSpec((1, n_e, tm), lambda i: (i // spt, 0, i % spt))],
        out_shape=[
            jax.ShapeDtypeStruct((b * s, d), F32),
            jax.ShapeDtypeStruct((b * s, d), BF16),
            jax.ShapeDtypeStruct((b, n_e, s), F32),
        ],
        compiler_params=_params("parallel"),
        name="outproj",
    )(r.reshape(b * s, n_r), a.reshape(b * s, n_a), x.reshape(b * s, d), w_bf16, gain, w_router_t)


def _select_kernel(aff_ref, posm_ref, cnt_ref, *, cap):
    rows, s = aff_ref.shape
    bits = pltpu.bitcast(aff_ref[...], I32)

    def search(i, thr):
        cand = thr | jnp.left_shift(jnp.int32(1), 30 - i)
        cnt = jnp.sum((bits >= cand).astype(I32), axis=1, keepdims=True)
        return jnp.where(cnt >= cap, cand, thr)

    thr = lax.fori_loop(0, 31, search, jnp.zeros((rows, 1), I32))
    gt = bits > thr
    eq = bits == thr
    need = cap - jnp.sum(gt.astype(I32), axis=1, keepdims=True)

    w = LANES
    tri = (lax.broadcasted_iota(I32, (w, w), 0) <= lax.broadcasted_iota(I32, (w, w), 1)).astype(BF16)

    def excl_prefix(flags):
        carry = jnp.zeros((rows, 1), F32)
        out = []
        for c0 in range(0, s, w):
            f = flags[:, c0:c0 + w].astype(BF16)
            inc = jnp.dot(f, tri, preferred_element_type=F32)
            out.append(inc - f.astype(F32) + carry)
            carry = carry + inc[:, w - 1:w]
        return jnp.concatenate(out, axis=1).astype(I32)

    sel = gt | (eq & (excl_prefix(eq) < need))
    posm_ref[...] = jnp.where(sel, excl_prefix(sel), -1)

    tok = lax.broadcasted_iota(I32, (w, w), 0)
    edge = lax.broadcasted_iota(I32, (w, w), 1) * MOE_TILE
    cnt = jnp.zeros((rows, w), F32)
    for c0 in range(0, s, w):
        before = ((tok + c0) < edge).astype(BF16)
        cnt = cnt + jnp.dot(sel[:, c0:c0 + w].astype(BF16), before, preferred_element_type=F32)
    cnt_ref[...] = cnt.astype(I32)


def _select(aff_rows, cap):
    rows, s = aff_rows.shape
    return pl.pallas_call(
        functools.partial(_select_kernel, cap=cap),
        out_shape=[jax.ShapeDtypeStruct((rows, s), I32), jax.ShapeDtypeStruct((rows, LANES), I32)],
        compiler_params=pltpu.CompilerParams(vmem_limit_bytes=VMEM_LIMIT),
        name="select",
    )(aff_rows)


def _slot_window(cnt_ref, row, j, cap):
    lo, hi = cnt_ref[row * CNT_STRIDE + j], cnt_ref[row * CNT_STRIDE + j + 1]
    w0 = pl.multiple_of(jnp.minimum((lo // SLOT_ALIGN) * SLOT_ALIGN, cap - SLOT_WIN), SLOT_ALIGN)
    return hi, w0


def _any_overflow(windows):
    return functools.reduce(jnp.logical_or, [hi > w0 + SLOT_WIN for hi, w0 in windows])


def _gather_kernel(cnt_ref, posm_ref, h2_ref, xg_ref):
    bi, j = pl.program_id(0), pl.program_id(1)
    n_e, cap = xg_ref.shape[1:3]
    tt = h2_ref.shape[1]

    @pl.when(j == 0)
    def _():
        xg_ref[...] = jnp.zeros_like(xg_ref)

    slot0 = lax.broadcasted_iota(I32, (SLOT_WIN, tt), 0)
    windows = [_slot_window(cnt_ref, bi * n_e + e, j, cap) for e in range(n_e)]
    for e in range(0, n_e, 2):
        onehot = jnp.concatenate(
            [(posm_ref[0, e + i:e + i + 1, :] == slot0 + windows[e + i][1]).astype(BF16) for i in range(2)], axis=0)
        rows = jnp.dot(onehot, h2_ref[0], preferred_element_type=F32).astype(BF16)
        for i in range(2):
            xg_ref[0, e + i, pl.ds(windows[e + i][1], SLOT_WIN), :] += rows[i * SLOT_WIN:(i + 1) * SLOT_WIN]

    @pl.when(_any_overflow(windows))
    def _():
        wide = lax.broadcasted_iota(I32, (MXU_DEPTH, tt), 0)
        for e, (hi, w0) in enumerate(windows):
            @pl.when(hi > w0 + SLOT_WIN)
            def _(e=e, w0=w0):
                for ws in range(0, cap, MXU_DEPTH):
                    slot = wide + ws
                    onehot = ((posm_ref[0, e:e + 1, :] == slot) & (slot >= w0 + SLOT_WIN)).astype(BF16)
                    extra = jnp.dot(onehot, h2_ref[0], preferred_element_type=F32)
                    xg_ref[0, e, ws:ws + MXU_DEPTH, :] += extra.astype(BF16)


def _gather(cnt, posm, h2, cap):
    b, n_e, s = posm.shape
    d = h2.shape[2]
    return pl.pallas_call(
        _gather_kernel,
        grid_spec=pltpu.PrefetchScalarGridSpec(
            num_scalar_prefetch=1,
            grid=(b, s // MOE_TILE),
            in_specs=[
                pl.BlockSpec((1, n_e, MOE_TILE), lambda bi, j, cnt: (bi, 0, j)),
                pl.BlockSpec((1, MOE_TILE, d), lambda bi, j, cnt: (bi, j, 0)),
            ],
            out_specs=pl.BlockSpec((1, n_e, cap, d), lambda bi, j, cnt: (bi, 0, 0, 0)),
        ),
        out_shape=jax.ShapeDtypeStruct((b, n_e, cap, d), BF16),
        compiler_params=_params("parallel", "arbitrary"),
        name="gather",
    )(cnt, posm, h2)


def _ffn_kernel(xg_ref, wg_hbm, wu_hbm, wd_hbm, y_ref, wg_buf, wu_buf, wd_buf, wg_stage, wu_stage, wd_stage,
                sem, acc_ref, *, tf):
    e, bi = pl.program_id(0), pl.program_id(1)
    n_e = pl.num_programs(0)
    rows_in, rows_dn = wg_stage.shape[0], wd_stage.shape[0]
    f_total = wg_buf.shape[2]

    def chunk_copies(expert, c):
        return (
            pltpu.make_async_copy(wg_hbm.at[expert, pl.ds(c * rows_in, rows_in), :], wg_stage, sem.at[0]),
            pltpu.make_async_copy(wu_hbm.at[expert, pl.ds(c * rows_in, rows_in), :], wu_stage, sem.at[1]),
            pltpu.make_async_copy(wd_hbm.at[expert, pl.ds(c * rows_dn, rows_dn), :], wd_stage, sem.at[2]),
        )

    def cast_chunk(slot, c):
        wg_buf[slot, pl.ds(c * rows_in, rows_in), :] = wg_stage[...].astype(BF16)
        wu_buf[slot, pl.ds(c * rows_in, rows_in), :] = wu_stage[...].astype(BF16)
        wd_buf[slot, pl.ds(c * rows_dn, rows_dn), :] = wd_stage[...].astype(BF16)

    @pl.when((e == 0) & (bi == 0))
    def _():
        def load_first(c, carry):
            copies = chunk_copies(0, c)
            for cp in copies:
                cp.start()
            for cp in copies:
                cp.wait()
            cast_chunk(0, c)
            return carry

        lax.fori_loop(0, pl.num_programs(1), load_first, 0)

    @pl.when(e + 1 < n_e)
    def _():
        for cp in chunk_copies(e + 1, bi):
            cp.start()

    slot = e % 2
    xg = xg_ref[0, 0]
    for fi in range(f_total // tf):
        fs = slice(fi * tf, (fi + 1) * tf)
        gate = jnp.dot(xg, wg_buf[slot, :, fs], preferred_element_type=F32)
        up = jnp.dot(xg, wu_buf[slot, :, fs], preferred_element_type=F32)
        hid = ((gate * jax.nn.sigmoid(gate)) * up).astype(BF16)
        part = jnp.dot(hid, wd_buf[slot, fs, :], preferred_element_type=F32)
        if fi == 0:
            acc_ref[...] = part
        else:
            acc_ref[...] += part
    y_ref[0, 0] = acc_ref[...].astype(y_ref.dtype)

    @pl.when(e + 1 < n_e)
    def _():
        for cp in chunk_copies(e + 1, bi):
            cp.wait()
        cast_chunk(1 - slot, bi)


def _ffn(xg, wg, wu, wd, tf=512):
    b, n_e, cap, d = xg.shape
    f = wg.shape[2]
    assert d % b == 0 and f % b == 0 and (d // b) % SLOT_ALIGN == 0, "one weight row chunk per batch step"
    tok = pl.BlockSpec((1, 1, cap, d), lambda e, bi: (bi, e, 0, 0))
    hbm = pl.BlockSpec(memory_space=pl.ANY)
    return pl.pallas_call(
        functools.partial(_ffn_kernel, tf=tf),
        grid=(n_e, b),
        in_specs=[tok, hbm, hbm, hbm],
        out_specs=tok,
        out_shape=jax.ShapeDtypeStruct((b, n_e, cap, d), BF16),
        scratch_shapes=[
            pltpu.VMEM((2, d, f), BF16), pltpu.VMEM((2, d, f), BF16), pltpu.VMEM((2, f, d), BF16),
            pltpu.VMEM((d // b, f), F32), pltpu.VMEM((d // b, f), F32), pltpu.VMEM((f // b, d), F32),
            pltpu.SemaphoreType.DMA((3,)),
            pltpu.VMEM((cap, d), F32),
        ],
        compiler_params=_params("arbitrary", "arbitrary"),
        name="ffn",
    )(xg, wg, wu, wd)


def _combine_kernel(cnt_ref, pos_ref, gate_ref, y_ref, x1_ref, g_ref, out_ref, acc_ref):
    bi, j = pl.program_id(0), pl.program_id(1)
    tt = x1_ref.shape[1]
    n_e, cap = y_ref.shape[1:3]
    slot0 = lax.broadcasted_iota(I32, (tt, SLOT_WIN), 1)
    windows = [_slot_window(cnt_ref, bi * n_e + e, j, cap) for e in range(n_e)]
    acc = x1_ref[0]
    for e in range(0, n_e, 2):
        scatter = jnp.concatenate(
            [jnp.where(pos_ref[0, :, e + i:e + i + 1] == slot0 + windows[e + i][1],
                       gate_ref[0, :, e + i:e + i + 1], 0.0).astype(BF16) for i in range(2)], axis=1)
        y_pair = jnp.concatenate(
            [y_ref[0, e + i, pl.ds(windows[e + i][1], SLOT_WIN), :] for i in range(2)], axis=0)
        acc = acc + jnp.dot(scatter, y_pair, preferred_element_type=F32)
    acc_ref[...] = acc

    @pl.when(_any_overflow(windows))
    def _():
        wide = lax.broadcasted_iota(I32, (tt, MXU_DEPTH), 1)
        for e, (hi, w0) in enumerate(windows):
            @pl.when(hi > w0 + SLOT_WIN)
            def _(e=e, w0=w0):
                for ws in range(0, cap, MXU_DEPTH):
                    slot = wide + ws
                    extra = jnp.where((pos_ref[0, :, e:e + 1] == slot) & (slot >= w0 + SLOT_WIN),
                                      gate_ref[0, :, e:e + 1], 0.0).astype(BF16)
                    acc_ref[...] += jnp.dot(extra, y_ref[0, e, ws:ws + MXU_DEPTH, :],
                                            preferred_element_type=F32)

    acc = acc_ref[...]
    ms = jnp.mean(acc * acc, axis=-1, keepdims=True)
    out_ref[0] = acc * lax.rsqrt(ms + EPS) * g_ref[...]


def _combine(cnt, pos_t, gate_t, y, x1, gain):
    b, s, d = x1.shape
    n_e, cap = y.shape[1:3]
    tt = MOE_TILE
    return pl.pallas_call(
        _combine_kernel,
        grid_spec=pltpu.PrefetchScalarGridSpec(
            num_scalar_prefetch=1,
            grid=(b, s // tt),
            in_specs=[
                pl.BlockSpec((1, tt, n_e), lambda bi, j, cnt: (bi, j, 0)),
                pl.BlockSpec((1, tt, n_e), lambda bi, j, cnt: (bi, j, 0)),
                pl.BlockSpec((1, n_e, cap, d), lambda bi, j, cnt: (bi, 0, 0, 0)),
                pl.BlockSpec((1, tt, d), lambda bi, j, cnt: (bi, j, 0)),
                pl.BlockSpec((1, d), lambda bi, j, cnt: (0, 0)),
            ],
            out_specs=pl.BlockSpec((1, tt, d), lambda bi, j, cnt: (bi, j, 0)),
            scratch_shapes=[pltpu.VMEM((tt, d), F32)],
        ),
        out_shape=jax.ShapeDtypeStruct((b, s, d), F32),
        compiler_params=_params("parallel", "parallel"),
        name="combine",
    )(cnt, pos_t, gate_t, y, x1, gain)


def _rope_tables(seq):
    half = HEAD_DIM // 2
    pos = jnp.arange(seq, dtype=F32)
    inv = ROPE_BASE ** (-jnp.arange(0, HEAD_DIM, 2, dtype=F32) / HEAD_DIM)
    ang = pos[:, None] * inv[None, :]
    cos, sin = jnp.cos(ang), jnp.sin(ang)
    reps = LANES // HEAD_DIM
    cos_t = jnp.tile(jnp.concatenate([cos, cos], axis=1), (1, reps))
    sin_t = jnp.tile(jnp.concatenate([-sin, sin], axis=1), (1, reps))
    return cos_t, sin_t


def kernel(x, norm1_gain, w_in, ret_log_decay, ret_gn_gain, rel_bias, w_out, norm2_gain, w_router,
           w_gate, w_up, w_down, final_gain):
    b, s, d = x.shape
    depth = w_in.shape[0]
    ret_width = ret_gn_gain.shape[1]
    n_ret_pairs = ret_width // LANES
    att_width = (w_in.shape[2] - 4 * ret_width) // 3
    n_att_pairs = att_width // LANES
    cap = CAPACITY_FACTOR * s // N_EXPERTS
    cos_t, sin_t = _rope_tables(s)

    assert depth == 1, "single-layer block: the final norm is fused into the combine kernel"
    layer = 0
    ret, att = _proj(x.reshape(b * s, d), norm1_gain[layer][None, :], w_in[layer].astype(BF16),
                     n_ret=4 * ret_width)
    r = _retention(ret.reshape(b, s, -1), ret_log_decay[layer], cos_t, sin_t,
                   ret_gn_gain[layer][None, :], n_ret_pairs)
    a = _dilated(att.reshape(b, s, -1), rel_bias, n_att_pairs)
    x1, h2, aff = _outproj(r, a, x, w_out[layer].astype(BF16), norm2_gain[layer][None, :],
                           w_router[layer].T)
    assert s % MOE_TILE == 0 and s // MOE_TILE < CNT_STRIDE and cap % MXU_DEPTH == 0 and N_EXPERTS % 2 == 0
    posm, cnt = _select(aff.reshape(b * N_EXPERTS, s), cap)
    posm = posm.reshape(b, N_EXPERTS, s)
    cnt = cnt[:, :CNT_STRIDE].reshape(-1)
    xg = _gather(cnt, posm, h2.reshape(b, s, d), cap)
    y = _ffn(xg, w_gate[layer], w_up[layer], w_down[layer])
    return _combine(cnt, jnp.swapaxes(posm, 1, 2), jnp.swapaxes(aff, 1, 2), y, x1.reshape(b, s, d),
                    final_gain[None, :])
```

```python
import functools
import math

import numpy as np
import jax
import jax.numpy as jnp
from jax import lax
from jax.experimental import pallas as pl
from jax.experimental.pallas import tpu as pltpu

F32 = jnp.float32
BF16 = jnp.bfloat16
I32 = jnp.int32

HEAD_DIM = 64
LANES = 128
RET_CHUNK = 128
ROPE_BASE = 10000.0
DILATED_PATTERNS = ((128, 1), (512, 4), (2048, 16))
SIDE = 64
N_BUCKETS = 32
MAX_DISTANCE = 1024
N_EXPERTS = 16
CAPACITY_FACTOR = 2
MOE_TILE = 512
MXU_DEPTH = 256
SLOT_WIN = MXU_DEPTH // 2
SLOT_ALIGN = 16
CNT_STRIDE = 16
EPS = 1e-6
NEG = -1e30
VMEM_LIMIT = 56 * 1024 * 1024

_NT = (((1,), (1,)), ((), ()))
_TN = (((0,), (0,)), ((), ()))


def _params(*sem):
    return pltpu.CompilerParams(dimension_semantics=sem, vmem_limit_bytes=VMEM_LIMIT)


def _proj_kernel(x_ref, g_ref, w_ref, ret_ref, att_ref, *, n_ret, chunk):
    x = x_ref[...]
    ms = jnp.mean(x * x, axis=-1, keepdims=True)
    h = (x * lax.rsqrt(ms + EPS) * g_ref[...]).astype(BF16)
    n_cols = w_ref.shape[1]
    for c0 in range(0, n_cols, chunk):
        o = jnp.dot(h, w_ref[:, c0:c0 + chunk], preferred_element_type=F32)
        if c0 < n_ret:
            ret_ref[:, c0:c0 + chunk] = o.astype(BF16)
        else:
            att_ref[:, c0 - n_ret:c0 - n_ret + chunk] = o


def _proj(x2d, gain, w_bf16, n_ret, tm=512):
    t, d = x2d.shape
    n_cols = w_bf16.shape[1]
    n_att = n_cols - n_ret
    return pl.pallas_call(
        functools.partial(_proj_kernel, n_ret=n_ret, chunk=512),
        grid=(t // tm,),
        in_specs=[
            pl.BlockSpec((tm, d), lambda i: (i, 0)),
            pl.BlockSpec((1, d), lambda i: (0, 0)),
            pl.BlockSpec((d, n_cols), lambda i: (0, 0)),
        ],
        out_specs=[
            pl.BlockSpec((tm, n_ret), lambda i: (i, 0)),
            pl.BlockSpec((tm, n_att), lambda i: (i, 0)),
        ],
        out_shape=[
            jax.ShapeDtypeStruct((t, n_ret), BF16),
            jax.ShapeDtypeStruct((t, n_att), F32),
        ],
        compiler_params=_params("parallel"),
        name="proj",
    )(x2d, gain, w_bf16)


def _retention_kernel(decay_ref, q_ref, k_ref, v_ref, g_ref, cos_ref, sin_ref, gn_ref, out_ref,
                      qs_ref, ks_ref, rb_ref, *, seq):
    c = RET_CHUNK
    n_chunks = seq // c
    p = pl.program_id(1)
    lane = lax.broadcasted_iota(I32, (c, LANES), 1)
    row = lax.broadcasted_iota(I32, (c, LANES), 0)
    head0 = lane < HEAD_DIM
    rowf = row.astype(F32)

    lgf0, lgf1 = decay_ref[0, 2 * p], decay_ref[0, 2 * p + 1]
    lgb0, lgb1 = decay_ref[1, 2 * p], decay_ref[1, 2 * p + 1]
    lgf_lane = jnp.where(head0, lgf0, lgf1)
    lgb_lane = jnp.where(head0, lgb0, lgb1)
    lgf_row = jnp.where(row < HEAD_DIM, lgf0, lgf1)
    lgb_row = jnp.where(row < HEAD_DIM, lgb0, lgb1)
    same_head = (row < HEAD_DIM) == head0

    zeta_f = jnp.exp((c - 1 - rowf) * lgf_lane)
    zeta_b = jnp.exp(rowf * lgb_lane)
    xi_f = jnp.exp((rowf + 1.0) * lgf_lane)
    xi_b = jnp.exp((c - rowf) * lgb_lane)
    gch_f = jnp.where(same_head, jnp.exp(c * lgf_row), 0.0)
    gch_b = jnp.where(same_head, jnp.exp(c * lgb_row), 0.0)

    diff = (row - lane).astype(F32)

    def dmat(lf, lb):
        return jnp.exp(jnp.where(diff >= 0, diff * lf, -diff * lb))

    d_stack = jnp.concatenate([dmat(lgf0, lgb0), dmat(lgf1, lgb1)], axis=0)

    first_half = (lane % HEAD_DIM) < (HEAD_DIM // 2)
    rows_per_step = 4 * c

    def rope_step(i, carry):
        r0 = pl.multiple_of(i * rows_per_step, rows_per_step)
        sl = pl.ds(r0, rows_per_step)
        cos = cos_ref[sl, :]
        sin = sin_ref[sl, :]
        fh = jnp.concatenate([first_half] * 4, axis=0)

        def rot(t):
            sw = jnp.where(fh, pltpu.roll(t, LANES - HEAD_DIM // 2, 1), pltpu.roll(t, HEAD_DIM // 2, 1))
            return t * cos + sw * sin

        qs_ref[sl, :] = rot(q_ref[0, sl, :].astype(F32)).astype(BF16)
        ks_ref[sl, :] = (rot(k_ref[0, sl, :].astype(F32)) * (HEAD_DIM ** -0.5)).astype(BF16)
        return carry

    lax.fori_loop(0, seq // rows_per_step, rope_step, 0)

    def chunk_kv(n, zeta):
        sl = pl.ds(pl.multiple_of(n * c, c), c)
        kw = (ks_ref[sl, :].astype(F32) * zeta).astype(BF16)
        return lax.dot_general(kw, v_ref[0, sl, :], _TN, preferred_element_type=F32)

    def bwd_step(i, state):
        n = n_chunks - 1 - i
        rb_ref[n] = state.astype(BF16)
        return gch_b * state + jnp.where(same_head, chunk_kv(n, zeta_b), 0.0)

    lax.fori_loop(0, n_chunks, bwd_step, jnp.zeros((LANES, LANES), F32), unroll=4)

    gn_gain = gn_ref[...]

    def fwd_step(n, state):
        sl = pl.ds(pl.multiple_of(n * c, c), c)
        q = qs_ref[sl, :]
        k = ks_ref[sl, :]
        v = v_ref[0, sl, :]
        zero = jnp.zeros_like(q)
        q_stack = jnp.concatenate([jnp.where(head0, q, zero), jnp.where(head0, zero, q)], axis=0)
        scores = lax.dot_general(q_stack, k, _NT, preferred_element_type=F32) * d_stack
        sb = scores.astype(BF16)
        s_cat = jnp.concatenate([sb[:c], sb[c:]], axis=1)
        v_stack = jnp.concatenate([jnp.where(head0, v, zero), jnp.where(head0, zero, v)], axis=0)
        inner = jnp.dot(s_cat, v_stack, preferred_element_type=F32)
        cross_f = jnp.dot(q, state.astype(BF16), preferred_element_type=F32) * xi_f
        cross_b = jnp.dot(q, rb_ref[n], preferred_element_type=F32) * xi_b
        r = inner + cross_f + cross_b

        def half_mean(t):
            s0 = jnp.sum(jnp.where(head0, t, 0.0), axis=-1, keepdims=True)
            s1 = jnp.sum(jnp.where(head0, 0.0, t), axis=-1, keepdims=True)
            return jnp.where(head0, s0, s1) * (1.0 / HEAD_DIM)

        dlt = r - half_mean(r)
        var = half_mean(dlt * dlt)
        gate = g_ref[0, sl, :].astype(F32)
        y = dlt * lax.rsqrt(var + EPS) * gn_gain * (gate * jax.nn.sigmoid(gate))
        out_ref[0, sl, :] = y.astype(out_ref.dtype)
        return gch_f * state + jnp.where(same_head, chunk_kv(n, zeta_f), 0.0)

    lax.fori_loop(0, n_chunks, fwd_step, jnp.zeros((LANES, LANES), F32), unroll=8)


def _retention(ret, decay, cos_t, sin_t, gn_gain, n_pairs):
    b, s, _ = ret.shape
    blk = lambda off: pl.BlockSpec((1, s, LANES), lambda bi, pi: (bi, 0, off + pi))
    full = pl.BlockSpec((s, LANES), lambda bi, pi: (0, 0))
    return pl.pallas_call(
        functools.partial(_retention_kernel, seq=s),
        grid=(b, n_pairs),
        in_specs=[
            pl.BlockSpec(memory_space=pltpu.SMEM),
            blk(0), blk(n_pairs), blk(2 * n_pairs), blk(3 * n_pairs),
            full, full,
            pl.BlockSpec((1, LANES), lambda bi, pi: (0, pi)),
        ],
        out_specs=pl.BlockSpec((1, s, LANES), lambda bi, pi: (bi, 0, pi)),
        out_shape=jax.ShapeDtypeStruct((b, s, n_pairs * LANES), BF16),
        scratch_shapes=[
            pltpu.VMEM((s, LANES), BF16),
            pltpu.VMEM((s, LANES), BF16),
            pltpu.VMEM((s // RET_CHUNK, LANES, LANES), BF16),
        ],
        compiler_params=_params("parallel", "parallel"),
        name="retention",
    )(decay, ret, ret, ret, ret, cos_t, sin_t, gn_gain)


QBLK = 2 * SIDE
KWIN = 4 * SIDE
BASE_W = 512
ATT_GROUP = 4


def _t5_bucket_np(rel):
    half = N_BUCKETS // 2
    max_exact = half // 2
    bucket = np.where(rel > 0, half, 0)
    n = np.abs(rel)
    nf = np.maximum(n, 1).astype(np.float32)
    large = max_exact + (np.log(nf / np.float32(max_exact)) / np.float32(math.log(MAX_DISTANCE / max_exact))
                         * np.float32(half - max_exact)).astype(np.int32)
    large = np.minimum(large, half - 1)
    return (bucket + np.where(n < max_exact, n, large)).astype(np.int32)


def _bucket_rows():
    k = np.arange(BASE_W)
    off = k - SIDE
    rows = []
    for _, dilation in DILATED_PATTERNS:
        rows.append(np.where(k <= 2 * SIDE, _t5_bucket_np(off * dilation), -1))
    return np.stack(rows).astype(np.int32)


def _dilated_kernel(bias_tab_ref, bucket_ref, q_ref, k_ref, v_ref, out_ref,
                    bias_ref, qp_ref, kp_ref, vp_ref, stage_ref, o_ref, lse_ref,
                    e0_ref, e1_ref, m0_ref, m1_ref, *, seq):
    assert len(DILATED_PATTERNS) == 3 and DILATED_PATTERNS[0][1] == 1
    p = pl.program_id(1)
    lane = lax.broadcasted_iota(I32, (QBLK, LANES), 1)
    head0 = lane < HEAD_DIM
    n_blocks = seq // QBLK

    for pi in range(len(DILATED_PATTERNS)):
        bucket = jnp.broadcast_to(bucket_ref[pi:pi + 1, :], (QBLK, BASE_W))
        for hh in range(2):
            base = jnp.full((QBLK, BASE_W), NEG, F32)
            for bk in range(N_BUCKETS):
                base = jnp.where(bucket == bk, bias_tab_ref[bk, 2 * p + hh], base)
            for var, shift in enumerate((BASE_W - SIDE, 0, SIDE)):
                tile = pltpu.roll(base, shift, 1, stride=1, stride_axis=0)
                bias_ref[pi * 3 + var, hh * QBLK:(hh + 1) * QBLK, :] = tile[:, :KWIN]

    head0_k = lax.broadcasted_iota(I32, (KWIN, LANES), 1) < HEAD_DIM
    prev_d = 1
    for pi, (_, d) in enumerate(DILATED_PATTERNS):
        seg_len = seq // d
        blocks_per_seg = seg_len // QBLK

        step, prev_len = d // prev_d, seq // prev_d
        assert step * prev_d == d
        keep_f32 = 0 < pi < len(DILATED_PATTERNS) - 1
        for i, (src_ref, dst_ref, scale) in enumerate(
                ((q_ref, qp_ref, HEAD_DIM ** -0.5), (k_ref, kp_ref, None), (v_ref, vp_ref, None))):
            for r_prev in range(prev_d):
                for r_step in range(step):
                    src = pl.ds(r_prev * prev_len + r_step, seg_len, stride=step) if step > 1 else pl.ds(0, seq)
                    dst = pl.ds((r_prev + prev_d * r_step) * seg_len, seg_len)
                    val = src_ref[0, src, :] if prev_d == 1 else stage_ref[i, src, :]
                    if keep_f32:
                        stage_ref[i, dst, :] = val
                    dst_ref[dst, :] = (val if scale is None else val * scale).astype(BF16)
        prev_d = d

        def placement(t, seg_len=seg_len, blocks_per_seg=blocks_per_seg):
            seg = t // blocks_per_seg
            u = t % blocks_per_seg
            q0 = pl.multiple_of(t * QBLK, QBLK)
            seg0 = seg * seg_len
            k0 = pl.multiple_of(jnp.clip(q0 - SIDE, seg0, seg0 + seg_len - KWIN), SIDE)
            var = jnp.where(u == 0, 0, jnp.where(u == blocks_per_seg - 1, 2, 1))
            return seg, u, q0, k0, var

        def logits_stage(g, e_ref, m_ref, pi=pi):
            for jb in range(ATT_GROUP):
                _, _, q0, k0, var = placement(g * ATT_GROUP + jb)
                q = qp_ref[pl.ds(q0, QBLK), :]
                kw = kp_ref[pl.ds(k0, KWIN), :]
                zero = jnp.zeros_like(q)
                q_stack = jnp.concatenate([jnp.where(head0, q, zero), jnp.where(head0, zero, q)], axis=0)
                s = lax.dot_general(q_stack, kw, _NT, preferred_element_type=F32) + bias_ref[pi * 3 + var]
                m = jnp.max(s, axis=-1, keepdims=True)
                e_ref[jb] = jnp.exp(s - m).astype(BF16)
                m_ref[jb] = jnp.broadcast_to(m, (2 * QBLK, LANES))

        def value_stage(g, e_ref, m_ref, pi=pi, d=d):
            for jb in range(ATT_GROUP):
                seg, u, q0, k0, _ = placement(g * ATT_GROUP + jb)
                vw = vp_ref[pl.ds(k0, KWIN), :]
                one = jnp.ones_like(vw)
                pv0 = jnp.dot(e_ref[jb, :QBLK, :], jnp.where(head0_k, vw, one), preferred_element_type=F32)
                pv1 = jnp.dot(e_ref[jb, QBLK:, :], jnp.where(head0_k, one, vw), preferred_element_type=F32)
                num = jnp.where(head0, pv0, pv1)
                den = pltpu.roll(jnp.where(head0, pv1, pv0), HEAD_DIM, 1)
                dst = pl.ds(seg + d * (u * QBLK), QBLK, stride=d) if d > 1 else pl.ds(q0, QBLK)
                o_ref[pi, dst, :] = num * (1.0 / den)
                lse_ref[pi, dst, :] = jnp.where(head0, m_ref[jb, :QBLK, :], m_ref[jb, QBLK:, :]) + jnp.log(den)

        n_groups = n_blocks // ATT_GROUP
        assert n_groups % 2 == 0 and n_groups >= 2
        logits_stage(0, e0_ref, m0_ref)

        def group_pair(i, carry):
            g = 2 * i
            logits_stage(g + 1, e1_ref, m1_ref)
            value_stage(g, e0_ref, m0_ref)
            logits_stage(g + 2, e0_ref, m0_ref)
            value_stage(g + 1, e1_ref, m1_ref)
            return carry

        lax.fori_loop(0, n_groups // 2 - 1, group_pair, 0)
        logits_stage(n_groups - 1, e1_ref, m1_ref)
        value_stage(n_groups - 2, e0_ref, m0_ref)
        value_stage(n_groups - 1, e1_ref, m1_ref)

    rows = 4 * QBLK

    def mix(i, carry):
        sl = pl.ds(pl.multiple_of(i * rows, rows), rows)
        l0, l1, l2 = lse_ref[0, sl, :], lse_ref[1, sl, :], lse_ref[2, sl, :]
        mx = jnp.maximum(jnp.maximum(l0, l1), l2)
        e0, e1, e2 = jnp.exp(l0 - mx), jnp.exp(l1 - mx), jnp.exp(l2 - mx)
        inv = 1.0 / (e0 + e1 + e2)
        acc = (e0 * inv) * o_ref[0, sl, :] + (e1 * inv) * o_ref[1, sl, :] + (e2 * inv) * o_ref[2, sl, :]
        out_ref[0, sl, :] = acc.astype(out_ref.dtype)
        return carry

    lax.fori_loop(0, seq // rows, mix, 0)


def _dilated(att, rel_bias, n_pairs):
    b, s, _ = att.shape
    n_pat = len(DILATED_PATTERNS)
    blk = lambda off: pl.BlockSpec((1, s, LANES), lambda bi, pi: (bi, 0, off + pi))
    bucket_rows = jnp.asarray(_bucket_rows())
    return pl.pallas_call(
        functools.partial(_dilated_kernel, seq=s),
        grid=(b, n_pairs),
        in_specs=[
            pl.BlockSpec(memory_space=pltpu.SMEM),
            pl.BlockSpec((n_pat, BASE_W), lambda bi, pi: (0, 0)),
            blk(0), blk(n_pairs), blk(2 * n_pairs),
        ],
        out_specs=pl.BlockSpec((1, s, LANES), lambda bi, pi: (bi, 0, pi)),
        out_shape=jax.ShapeDtypeStruct((b, s, n_pairs * LANES), BF16),
        scratch_shapes=[
            pltpu.VMEM((n_pat * 3, 2 * QBLK, KWIN), F32),
            pltpu.VMEM((s, LANES), BF16),
            pltpu.VMEM((s, LANES), BF16),
            pltpu.VMEM((s, LANES), BF16),
            pltpu.VMEM((3, s, LANES), F32),
            pltpu.VMEM((n_pat, s, LANES), F32),
            pltpu.VMEM((n_pat, s, LANES), F32),
            pltpu.VMEM((ATT_GROUP, 2 * QBLK, KWIN), BF16),
            pltpu.VMEM((ATT_GROUP, 2 * QBLK, KWIN), BF16),
            pltpu.VMEM((ATT_GROUP, 2 * QBLK, LANES), F32),
            pltpu.VMEM((ATT_GROUP, 2 * QBLK, LANES), F32),
        ],
        compiler_params=_params("parallel", "parallel"),
        name="dilated",
    )(rel_bias, bucket_rows, att, att, att)


def _outproj_kernel(r_ref, a_ref, x_ref, w_ref, g_ref, wr_ref, x1_ref, h2_ref, aff_ref):
    n_r = r_ref.shape[1]
    x1 = (x_ref[...]
          + jnp.dot(r_ref[...], w_ref[:n_r, :], preferred_element_type=F32)
          + jnp.dot(a_ref[...], w_ref[n_r:, :], preferred_element_type=F32))
    x1_ref[...] = x1
    ms = jnp.mean(x1 * x1, axis=-1, keepdims=True)
    h2 = x1 * lax.rsqrt(ms + EPS) * g_ref[...]
    h2_ref[...] = h2.astype(BF16)
    logits = lax.dot_general(wr_ref[...], h2, _NT, preferred_element_type=F32,
                             precision=lax.Precision.HIGHEST)
    m = jnp.max(logits, axis=0, keepdims=True)
    e = jnp.exp(logits - m)
    aff_ref[0] = e / jnp.sum(e, axis=0, keepdims=True)


def _outproj(r, a, x, w_bf16, gain, w_router_t, tm=512):
    b, s, d = x.shape
    n_r, n_a = r.shape[-1], a.shape[-1]
    n_e = w_router_t.shape[0]
    spt = s // tm
    tok = lambda width: pl.BlockSpec((tm, width), lambda i: (i, 0))
    const = lambda shape: pl.BlockSpec(shape, lambda i: (0, 0))
    return pl.pallas_call(
        _outproj_kernel,
        grid=(b * spt,),
        in_specs=[tok(n_r), tok(n_a), tok(d), const((n_r + n_a, d)), const((1, d)), const((n_e, d))],
        out_specs=[tok(d), tok(d), pl.BlockSpec((1, n_e, tm), lambda i: (i // spt, 0, i % spt))],
        out_shape=[
            jax.ShapeDtypeStruct((b * s, d), F32),
            jax.ShapeDtypeStruct((b * s, d), BF16),
            jax.ShapeDtypeStruct((b, n_e, s), F32),
        ],
        compiler_params=_params("parallel"),
        name="outproj",
    )(r.reshape(b * s, n_r), a.reshape(b * s, n_a), x.reshape(b * s, d), w_bf16, gain, w_router_t)


def _select_kernel(aff_ref, posm_ref, cnt_ref, *, cap):
    rows, s = aff_ref.shape
    bits = pltpu.bitcast(aff_ref[...], I32)

    def search(i, thr):
        cand = thr | jnp.left_shift(jnp.int32(1), 30 - i)
        cnt = jnp.sum((bits >= cand).astype(I32), axis=1, keepdims=True)
        return jnp.where(cnt >= cap, cand, thr)

    thr = lax.fori_loop(0, 31, search, jnp.zeros((rows, 1), I32))
    gt = bits > thr
    eq = bits == thr
    need = cap - jnp.sum(gt.astype(I32), axis=1, keepdims=True)

    w = LANES
    tri = (lax.broadcasted_iota(I32, (w, w), 0) <= lax.broadcasted_iota(I32, (w, w), 1)).astype(BF16)

    def excl_prefix(flags):
        carry = jnp.zeros((rows, 1), F32)
        out = []
        for c0 in range(0, s, w):
            f = flags[:, c0:c0 + w].astype(BF16)
            inc = jnp.dot(f, tri, preferred_element_type=F32)
            out.append(inc - f.astype(F32) + carry)
            carry = carry + inc[:, w - 1:w]
        return jnp.concatenate(out, axis=1).astype(I32)

    sel = gt | (eq & (excl_prefix(eq) < need))
    posm_ref[...] = jnp.where(sel, excl_prefix(sel), -1)

    tok = lax.broadcasted_iota(I32, (w, w), 0)
    edge = lax.broadcasted_iota(I32, (w, w), 1) * MOE_TILE
    cnt = jnp.zeros((rows, w), F32)
    for c0 in range(0, s, w):
        before = ((tok + c0) < edge).astype(BF16)
        cnt = cnt + jnp.dot(sel[:, c0:c0 + w].astype(BF16), before, preferred_element_type=F32)
    cnt_ref[...] = cnt.astype(I32)


def _select(aff_rows, cap):
    rows, s = aff_rows.shape
    return pl.pallas_call(
        functools.partial(_select_kernel, cap=cap),
        out_shape=[jax.ShapeDtypeStruct((rows, s), I32), jax.ShapeDtypeStruct((rows, LANES), I32)],
        compiler_params=pltpu.CompilerParams(vmem_limit_bytes=VMEM_LIMIT),
        name="select",
    )(aff_rows)


def _slot_window(cnt_ref, row, j, cap):
    lo, hi = cnt_ref[row * CNT_STRIDE + j], cnt_ref[row * CNT_STRIDE + j + 1]
    w0 = pl.multiple_of(jnp.minimum((lo // SLOT_ALIGN) * SLOT_ALIGN, cap - SLOT_WIN), SLOT_ALIGN)
    return hi, w0


def _any_overflow(windows):
    return functools.reduce(jnp.logical_or, [hi > w0 + SLOT_WIN for hi, w0 in windows])


def _gather_kernel(cnt_ref, posm_ref, h2_ref, xg_ref):
    bi, j = pl.program_id(0), pl.program_id(1)
    n_e, cap = xg_ref.shape[1:3]
    tt = h2_ref.shape[1]

    @pl.when(j == 0)
    def _():
        xg_ref[...] = jnp.zeros_like(xg_ref)

    slot0 = lax.broadcasted_iota(I32, (SLOT_WIN, tt), 0)
    windows = [_slot_window(cnt_ref, bi * n_e + e, j, cap) for e in range(n_e)]
    for e in range(0, n_e, 2):
        onehot = jnp.concatenate(
            [(posm_ref[0, e + i:e + i + 1, :] == slot0 + windows[e + i][1]).astype(BF16) for i in range(2)], axis=0)
        rows = jnp.dot(onehot, h2_ref[0], preferred_element_type=F32).astype(BF16)
        for i in range(2):
            xg_ref[0, e + i, pl.ds(windows[e + i][1], SLOT_WIN), :] += rows[i * SLOT_WIN:(i + 1) * SLOT_WIN]

    @pl.when(_any_overflow(windows))
    def _():
        wide = lax.broadcasted_iota(I32, (MXU_DEPTH, tt), 0)
        for e, (hi, w0) in enumerate(windows):
            @pl.when(hi > w0 + SLOT_WIN)
            def _(e=e, w0=w0):
                for ws in range(0, cap, MXU_DEPTH):
                    slot = wide + ws
                    onehot = ((posm_ref[0, e:e + 1, :] == slot) & (slot >= w0 + SLOT_WIN)).astype(BF16)
                    extra = jnp.dot(onehot, h2_ref[0], preferred_element_type=F32)
                    xg_ref[0, e, ws:ws + MXU_DEPTH, :] += extra.astype(BF16)


def _gather(cnt, posm, h2, cap):
    b, n_e, s = posm.shape
    d = h2.shape[2]
    return pl.pallas_call(
        _gather_kernel,
        grid_spec=pltpu.PrefetchScalarGridSpec(
            num_scalar_prefetch=1,
            grid=(b, s // MOE_TILE),
            in_specs=[
                pl.BlockSpec((1, n_e, MOE_TILE), lambda bi, j, cnt: (bi, 0, j)),
                pl.BlockSpec((1, MOE_TILE, d), lambda bi, j, cnt: (bi, j, 0)),
            ],
            out_specs=pl.BlockSpec((1, n_e, cap, d), lambda bi, j, cnt: (bi, 0, 0, 0)),
        ),
        out_shape=jax.ShapeDtypeStruct((b, n_e, cap, d), BF16),
        compiler_params=_params("parallel", "arbitrary"),
        name="gather",
    )(cnt, posm, h2)


def _ffn_kernel(xg_ref, wg_hbm, wu_hbm, wd_hbm, y_ref, wg_buf, wu_buf, wd_buf, wg_stage, wu_stage, wd_stage,
                sem, acc_ref, *, tf):
    e, bi = pl.program_id(0), pl.program_id(1)
    n_e, n_b = pl.num_programs(0), pl.num_programs(1)
    rows_in, rows_dn = wg_stage.shape[1], wd_stage.shape[1]
    f_total = wg_buf.shape[2]

    def chunk_copies(expert, c):
        st = c % 2
        return (
            pltpu.make_async_copy(wg_hbm.at[expert, pl.ds(c * rows_in, rows_in), :], wg_stage.at[st], sem.at[0, st]),
            pltpu.make_async_copy(wu_hbm.at[expert, pl.ds(c * rows_in, rows_in), :], wu_stage.at[st], sem.at[1, st]),
            pltpu.make_async_copy(wd_hbm.at[expert, pl.ds(c * rows_dn, rows_dn), :], wd_stage.at[st], sem.at[2, st]),
        )

    def finish_chunk(expert, c):
        for cp in chunk_copies(expert, c):
            cp.wait()
        slot, st = expert % 2, c % 2
        wg_buf[slot, pl.ds(c * rows_in, rows_in), :] = wg_stage[st].astype(BF16)
        wu_buf[slot, pl.ds(c * rows_in, rows_in), :] = wu_stage[st].astype(BF16)
        wd_buf[slot, pl.ds(c * rows_dn, rows_dn), :] = wd_stage[st].astype(BF16)

    @pl.when((e == 0) & (bi == 0))
    def _():
        for cp in chunk_copies(0, 0):
            cp.start()

        def load_first(c, carry):
            @pl.when(c + 1 < n_b)
            def _():
                for cp in chunk_copies(0, c + 1):
                    cp.start()

            finish_chunk(0, c)
            return carry

        lax.fori_loop(0, n_b, load_first, 0)

    pending = jnp.where(bi > 0, e + 1 < n_e, e > 0)

    @pl.when(pending)
    def _():
        finish_chunk(jnp.where(bi > 0, e + 1, e), jnp.where(bi > 0, bi - 1, n_b - 1))

    @pl.when(e + 1 < n_e)
    def _():
        for cp in chunk_copies(e + 1, bi):
            cp.start()

    slot = e % 2
    xg = xg_ref[0, 0]
    for fi in range(f_total // tf):
        fs = slice(fi * tf, (fi + 1) * tf)
        gate = jnp.dot(xg, wg_buf[slot, :, fs], preferred_element_type=F32)
        up = jnp.dot(xg, wu_buf[slot, :, fs], preferred_element_type=F32)
        hid = ((gate * jax.nn.sigmoid(gate)) * up).astype(BF16)
        part = jnp.dot(hid, wd_buf[slot, fs, :], preferred_element_type=F32)
        if fi == 0:
            acc_ref[...] = part
        else:
            acc_ref[...] += part
    y_ref[0, 0] = acc_ref[...].astype(y_ref.dtype)


def _ffn(xg, wg, wu, wd, tf=512):
    b, n_e, cap, d = xg.shape
    f = wg.shape[2]
    assert d % b == 0 and f % b == 0 and (d // b) % SLOT_ALIGN == 0, "one weight row chunk per batch step"
    tok = pl.BlockSpec((1, 1, cap, d), lambda e, bi: (bi, e, 0, 0))
    hbm = pl.BlockSpec(memory_space=pl.ANY)
    return pl.pallas_call(
        functools.partial(_ffn_kernel, tf=tf),
        grid=(n_e, b),
        in_specs=[tok, hbm, hbm, hbm],
        out_specs=tok,
        out_shape=jax.ShapeDtypeStruct((b, n_e, cap, d), BF16),
        scratch_shapes=[
            pltpu.VMEM((2, d, f), BF16), pltpu.VMEM((2, d, f), BF16), pltpu.VMEM((2, f, d), BF16),
            pltpu.VMEM((2, d // b, f), F32), pltpu.VMEM((2, d // b, f), F32), pltpu.VMEM((2, f // b, d), F32),
            pltpu.SemaphoreType.DMA((3, 2)),
            pltpu.VMEM((cap, d), F32),
        ],
        compiler_params=_params("arbitrary", "arbitrary"),
        name="ffn",
    )(xg, wg, wu, wd)


def _combine_kernel(cnt_ref, pos_ref, gate_ref, y_ref, x1_ref, g_ref, out_ref, acc_ref):
    bi, j = pl.program_id(0), pl.program_id(1)
    tt = x1_ref.shape[1]
    n_e, cap = y_ref.shape[1:3]
    slot0 = lax.broadcasted_iota(I32, (tt, SLOT_WIN), 1)
    windows = [_slot_window(cnt_ref, bi * n_e + e, j, cap) for e in range(n_e)]
    acc = x1_ref[0]
    for e in range(0, n_e, 2):
        scatter = jnp.concatenate(
            [jnp.where(pos_ref[0, :, e + i:e + i + 1] == slot0 + windows[e + i][1],
                       gate_ref[0, :, e + i:e + i + 1], 0.0).astype(BF16) for i in range(2)], axis=1)
        y_pair = jnp.concatenate(
            [y_ref[0, e + i, pl.ds(windows[e + i][1], SLOT_WIN), :] for i in range(2)], axis=0)
        acc = acc + jnp.dot(scatter, y_pair, preferred_element_type=F32)
    acc_ref[...] = acc

    @pl.when(_any_overflow(windows))
    def _():
        wide = lax.broadcasted_iota(I32, (tt, MXU_DEPTH), 1)
        for e, (hi, w0) in enumerate(windows):
            @pl.when(hi > w0 + SLOT_WIN)
            def _(e=e, w0=w0):
                for ws in range(0, cap, MXU_DEPTH):
                    slot = wide + ws
                    extra = jnp.where((pos_ref[0, :, e:e + 1] == slot) & (slot >= w0 + SLOT_WIN),
                                      gate_ref[0, :, e:e + 1], 0.0).astype(BF16)
                    acc_ref[...] += jnp.dot(extra, y_ref[0, e, ws:ws + MXU_DEPTH, :],
                                            preferred_element_type=F32)

    acc = acc_ref[...]
    ms = jnp.mean(acc * acc, axis=-1, keepdims=True)
    out_ref[0] = acc * lax.rsqrt(ms + EPS) * g_ref[...]


def _combine(cnt, pos_t, gate_t, y, x1, gain):
    b, s, d = x1.shape
    n_e, cap = y.shape[1:3]
    tt = MOE_TILE
    return pl.pallas_call(
        _combine_kernel,
        grid_spec=pltpu.PrefetchScalarGridSpec(
            num_scalar_prefetch=1,
            grid=(b, s // tt),
            in_specs=[
                pl.BlockSpec((1, tt, n_e), lambda bi, j, cnt: (bi, j, 0)),
                pl.BlockSpec((1, tt, n_e), lambda bi, j, cnt: (bi, j, 0)),
                pl.BlockSpec((1, n_e, cap, d), lambda bi, j, cnt: (bi, 0, 0, 0)),
                pl.BlockSpec((1, tt, d), lambda bi, j, cnt: (bi, j, 0)),
                pl.BlockSpec((1, d), lambda bi, j, cnt: (0, 0)),
            ],
            out_specs=pl.BlockSpec((1, tt, d), lambda bi, j, cnt: (bi, j, 0)),
            scratch_shapes=[pltpu.VMEM((tt, d), F32)],
        ),
        out_shape=jax.ShapeDtypeStruct((b, s, d), F32),
        compiler_params=_params("parallel", "parallel"),
        name="combine",
    )(cnt, pos_t, gate_t, y, x1, gain)


def _rope_tables(seq):
    half = HEAD_DIM // 2
    pos = jnp.arange(seq, dtype=F32)
    inv = ROPE_BASE ** (-jnp.arange(0, HEAD_DIM, 2, dtype=F32) / HEAD_DIM)
    ang = pos[:, None] * inv[None, :]
    cos, sin = jnp.cos(ang), jnp.sin(ang)
    reps = LANES // HEAD_DIM
    cos_t = jnp.tile(jnp.concatenate([cos, cos], axis=1), (1, reps))
    sin_t = jnp.tile(jnp.concatenate([-sin, sin], axis=1), (1, reps))
    return cos_t, sin_t


def kernel(x, norm1_gain, w_in, ret_log_decay, ret_gn_gain, rel_bias, w_out, norm2_gain, w_router,
           w_gate, w_up, w_down, final_gain):
    b, s, d = x.shape
    depth = w_in.shape[0]
    ret_width = ret_gn_gain.shape[1]
    n_ret_pairs = ret_width // LANES
    att_width = (w_in.shape[2] - 4 * ret_width) // 3
    n_att_pairs = att_width // LANES
    cap = CAPACITY_FACTOR * s // N_EXPERTS
    cos_t, sin_t = _rope_tables(s)

    assert depth == 1, "single-layer block: the final norm is fused into the combine kernel"
    layer = 0
    ret, att = _proj(x.reshape(b * s, d), norm1_gain[layer][None, :], w_in[layer].astype(BF16),
                     n_ret=4 * ret_width)
    r = _retention(ret.reshape(b, s, -1), ret_log_decay[layer], cos_t, sin_t,
                   ret_gn_gain[layer][None, :], n_ret_pairs)
    a = _dilated(att.reshape(b, s, -1), rel_bias, n_att_pairs)
    x1, h2, aff = _outproj(r, a, x, w_out[layer].astype(BF16), norm2_gain[layer][None, :],
                           w_router[layer].T)
    assert s % MOE_TILE == 0 and s // MOE_TILE < CNT_STRIDE and cap % MXU_DEPTH == 0 and N_EXPERTS % 2 == 0
    posm, cnt = _select(aff.reshape(b * N_EXPERTS, s), cap)
    posm = posm.reshape(b, N_EXPERTS, s)
    cnt = cnt[:, :CNT_STRIDE].reshape(-1)
    xg = _gather(cnt, posm, h2.reshape(b, s, d), cap)
    y = _ffn(xg, w_gate[layer], w_up[layer], w_down[layer])
    return _combine(cnt, jnp.swapaxes(posm, 1, 2), jnp.swapaxes(aff, 1, 2), y, x1.reshape(b, s, d),
                    final_gain[None, :])
```

```python
import functools
import math

import numpy as np
import jax
import jax.numpy as jnp
from jax import lax
from jax.experimental import pallas as pl
from jax.experimental.pallas import tpu as pltpu

F32 = jnp.float32
BF16 = jnp.bfloat16
I32 = jnp.int32

HEAD_DIM = 64
LANES = 128
RET_CHUNK = 128
RET_GROUP = 4
ROPE_BASE = 10000.0
DILATED_PATTERNS = ((128, 1), (512, 4), (2048, 16))
SIDE = 64
N_BUCKETS = 32
MAX_DISTANCE = 1024
N_EXPERTS = 16
CAPACITY_FACTOR = 2
MOE_TILE = 512
MXU_DEPTH = 256
SLOT_WIN = MXU_DEPTH // 2
SLOT_ALIGN = 16
CNT_STRIDE = 16
EPS = 1e-6
NEG = -1e30
VMEM_LIMIT = 56 * 1024 * 1024

_NT = (((1,), (1,)), ((), ()))
_TN = (((0,), (0,)), ((), ()))


def _params(*sem):
    return pltpu.CompilerParams(dimension_semantics=sem, vmem_limit_bytes=VMEM_LIMIT)


def _proj_kernel(x_ref, g_ref, w_ref, cos_ref, sin_ref, ret_ref, att_ref, *, n_ret, ret_width, chunk):
    x = x_ref[...]
    ms = jnp.mean(x * x, axis=-1, keepdims=True)
    h = (x * lax.rsqrt(ms + EPS) * g_ref[...]).astype(BF16)
    n_cols = w_ref.shape[1]
    cos, sin = cos_ref[...], sin_ref[...]
    first_half = (lax.broadcasted_iota(I32, cos.shape, 1) % HEAD_DIM) < (HEAD_DIM // 2)
    for c0 in range(0, n_cols, chunk):
        o = jnp.dot(h, w_ref[:, c0:c0 + chunk], preferred_element_type=F32)
        if c0 >= n_ret:
            att_ref[:, c0 - n_ret:c0 - n_ret + chunk] = o
        elif c0 >= 2 * ret_width:
            ret_ref[:, c0:c0 + chunk] = o.astype(BF16)
        else:
            scale = 1.0 if c0 < ret_width else HEAD_DIM ** -0.5
            for l0 in range(0, chunk, LANES):
                t = o[:, l0:l0 + LANES]
                swapped = jnp.where(first_half, pltpu.roll(t, LANES - HEAD_DIM // 2, 1),
                                    pltpu.roll(t, HEAD_DIM // 2, 1))
                ret_ref[:, c0 + l0:c0 + l0 + LANES] = ((t * cos + swapped * sin) * scale).astype(BF16)


def _proj(x2d, gain, w_bf16, cos_t, sin_t, n_ret, ret_width, tm=512):
    t, d = x2d.shape
    n_cols = w_bf16.shape[1]
    n_att = n_cols - n_ret
    seq_tiles = cos_t.shape[0] // tm
    assert ret_width % 512 == 0 and n_ret % 512 == 0 and n_cols % 512 == 0
    return pl.pallas_call(
        functools.partial(_proj_kernel, n_ret=n_ret, ret_width=ret_width, chunk=512),
        grid=(t // tm,),
        in_specs=[
            pl.BlockSpec((tm, d), lambda i: (i, 0)),
            pl.BlockSpec((1, d), lambda i: (0, 0)),
            pl.BlockSpec((d, n_cols), lambda i: (0, 0)),
            pl.BlockSpec((tm, LANES), lambda i: (i % seq_tiles, 0)),
            pl.BlockSpec((tm, LANES), lambda i: (i % seq_tiles, 0)),
        ],
        out_specs=[
            pl.BlockSpec((tm, n_ret), lambda i: (i, 0)),
            pl.BlockSpec((tm, n_att), lambda i: (i, 0)),
        ],
        out_shape=[
            jax.ShapeDtypeStruct((t, n_ret), BF16),
            jax.ShapeDtypeStruct((t, n_att), F32),
        ],
        compiler_params=_params("parallel"),
        name="proj",
    )(x2d, gain, w_bf16, cos_t, sin_t)


def _retention_kernel(decay_ref, q_ref, k_ref, v_ref, g_ref, gn_ref, out_ref,
                      rf_ref, rb_ref, kvf_ref, kvb_ref, p0_ref, p1_ref, *, seq):
    c = RET_CHUNK
    n_chunks = seq // c
    p = pl.program_id(1)
    lane = lax.broadcasted_iota(I32, (c, LANES), 1)
    row = lax.broadcasted_iota(I32, (c, LANES), 0)
    head0 = lane < HEAD_DIM
    rowf = row.astype(F32)

    lgf0, lgf1 = decay_ref[0, 2 * p], decay_ref[0, 2 * p + 1]
    lgb0, lgb1 = decay_ref[1, 2 * p], decay_ref[1, 2 * p + 1]
    lgf_lane = jnp.where(head0, lgf0, lgf1)
    lgb_lane = jnp.where(head0, lgb0, lgb1)
    lgf_row = jnp.where(row < HEAD_DIM, lgf0, lgf1)
    lgb_row = jnp.where(row < HEAD_DIM, lgb0, lgb1)
    same_head = (row < HEAD_DIM) == head0

    zeta_f = jnp.exp((c - 1 - rowf) * lgf_lane)
    zeta_b = jnp.exp(rowf * lgb_lane)
    xi_f = jnp.exp((rowf + 1.0) * lgf_lane)
    xi_b = jnp.exp((c - rowf) * lgb_lane)
    gch_f = jnp.where(same_head, jnp.exp(c * lgf_row), 0.0)
    gch_b = jnp.where(same_head, jnp.exp(c * lgb_row), 0.0)

    diff = (row - lane).astype(F32)

    def dmat(lf, lb):
        return jnp.exp(jnp.where(diff >= 0, diff * lf, -diff * lb))

    d_stack = jnp.concatenate([dmat(lgf0, lgb0), dmat(lgf1, lgb1)], axis=0)

    def kv_step(n, carry):
        sl = pl.ds(pl.multiple_of(n * c, c), c)
        k = k_ref[0, sl, :].astype(F32)
        v = v_ref[0, sl, :]
        kvf = lax.dot_general((k * zeta_f).astype(BF16), v, _TN, preferred_element_type=F32)
        kvb = lax.dot_general((k * zeta_b).astype(BF16), v, _TN, preferred_element_type=F32)
        kvf_ref[n] = jnp.where(same_head, kvf, 0.0)
        kvb_ref[n] = jnp.where(same_head, kvb, 0.0)
        return carry

    lax.fori_loop(0, n_chunks, kv_step, 0, unroll=8)

    def scan_step(i, states):
        sf, sb = states
        nb = n_chunks - 1 - i
        rf_ref[i] = sf.astype(BF16)
        rb_ref[nb] = sb.astype(BF16)
        return gch_f * sf + kvf_ref[i], gch_b * sb + kvb_ref[nb]

    zero_state = jnp.zeros((LANES, LANES), F32)
    lax.fori_loop(0, n_chunks, scan_step, (zero_state, zero_state), unroll=4)

    gn_gain = gn_ref[...]

    def score_stage(g, p_ref):
        for jc in range(RET_GROUP):
            sl = pl.ds(pl.multiple_of((g * RET_GROUP + jc) * c, c), c)
            q = q_ref[0, sl, :]
            zero = jnp.zeros_like(q)
            q_stack = jnp.concatenate([jnp.where(head0, q, zero), jnp.where(head0, zero, q)], axis=0)
            scores = lax.dot_general(q_stack, k_ref[0, sl, :], _NT, preferred_element_type=F32) * d_stack
            sb = scores.astype(BF16)
            p_ref[jc] = jnp.concatenate([sb[:c], sb[c:]], axis=1)

    def half_mean(t):
        s0 = jnp.sum(jnp.where(head0, t, 0.0), axis=-1, keepdims=True)
        s1 = jnp.sum(jnp.where(head0, 0.0, t), axis=-1, keepdims=True)
        return jnp.where(head0, s0, s1) * (1.0 / HEAD_DIM)

    def value_stage(g, p_ref):
        for jc in range(RET_GROUP):
            n = g * RET_GROUP + jc
            sl = pl.ds(pl.multiple_of(n * c, c), c)
            q = q_ref[0, sl, :]
            v = v_ref[0, sl, :]
            zero = jnp.zeros_like(v)
            v_stack = jnp.concatenate([jnp.where(head0, v, zero), jnp.where(head0, zero, v)], axis=0)
            inner = jnp.dot(p_ref[jc], v_stack, preferred_element_type=F32)
            cross_f = jnp.dot(q, rf_ref[n], preferred_element_type=F32) * xi_f
            cross_b = jnp.dot(q, rb_ref[n], preferred_element_type=F32) * xi_b
            r = inner + cross_f + cross_b
            dlt = r - half_mean(r)
            var = half_mean(dlt * dlt)
            gate = g_ref[0, sl, :].astype(F32)
            y = dlt * lax.rsqrt(var + EPS) * gn_gain * (gate * jax.nn.sigmoid(gate))
            out_ref[0, sl, :] = y.astype(out_ref.dtype)

    n_groups = n_chunks // RET_GROUP
    assert n_groups % 2 == 0 and n_groups >= 2
    score_stage(0, p0_ref)

    def group_pair(i, carry):
        g = 2 * i
        score_stage(g + 1, p1_ref)
        value_stage(g, p0_ref)
        score_stage(g + 2, p0_ref)
        value_stage(g + 1, p1_ref)
        return carry

    lax.fori_loop(0, n_groups // 2 - 1, group_pair, 0)
    score_stage(n_groups - 1, p1_ref)
    value_stage(n_groups - 2, p0_ref)
    value_stage(n_groups - 1, p1_ref)


def _retention(ret, decay, gn_gain, n_pairs):
    b, s, _ = ret.shape
    blk = lambda off: pl.BlockSpec((1, s, LANES), lambda bi, pi: (bi, 0, off + pi))
    return pl.pallas_call(
        functools.partial(_retention_kernel, seq=s),
        grid=(b, n_pairs),
        in_specs=[
            pl.BlockSpec(memory_space=pltpu.SMEM),
            blk(0), blk(n_pairs), blk(2 * n_pairs), blk(3 * n_pairs),
            pl.BlockSpec((1, LANES), lambda bi, pi: (0, pi)),
        ],
        out_specs=pl.BlockSpec((1, s, LANES), lambda bi, pi: (bi, 0, pi)),
        out_shape=jax.ShapeDtypeStruct((b, s, n_pairs * LANES), BF16),
        scratch_shapes=[
            pltpu.VMEM((s // RET_CHUNK, LANES, LANES), BF16),
            pltpu.VMEM((s // RET_CHUNK, LANES, LANES), BF16),
            pltpu.VMEM((s // RET_CHUNK, LANES, LANES), F32),
            pltpu.VMEM((s // RET_CHUNK, LANES, LANES), F32),
            pltpu.VMEM((RET_GROUP, RET_CHUNK, 2 * RET_CHUNK), BF16),
            pltpu.VMEM((RET_GROUP, RET_CHUNK, 2 * RET_CHUNK), BF16),
        ],
        compiler_params=_params("parallel", "parallel"),
        name="retention",
    )(decay, ret, ret, ret, ret, gn_gain)


QBLK = 2 * SIDE
KWIN = 4 * SIDE
BASE_W = 512
ATT_GROUP = 4


def _t5_bucket_np(rel):
    half = N_BUCKETS // 2
    max_exact = half // 2
    bucket = np.where(rel > 0, half, 0)
    n = np.abs(rel)
    nf = np.maximum(n, 1).astype(np.float32)
    large = max_exact + (np.log(nf / np.float32(max_exact)) / np.float32(math.log(MAX_DISTANCE / max_exact))
                         * np.float32(half - max_exact)).astype(np.int32)
    large = np.minimum(large, half - 1)
    return (bucket + np.where(n < max_exact, n, large)).astype(np.int32)


def _bucket_rows():
    k = np.arange(BASE_W)
    off = k - SIDE
    rows = []
    for _, dilation in DILATED_PATTERNS:
        rows.append(np.where(k <= 2 * SIDE, _t5_bucket_np(off * dilation), -1))
    return np.stack(rows).astype(np.int32)


def _dilated_kernel(bias_tab_ref, bucket_ref, q_ref, k_ref, v_ref, out_ref,
                    bias_ref, qp_ref, kp_ref, vp_ref, stage_ref, o_ref, lse_ref,
                    e0_ref, e1_ref, m0_ref, m1_ref, *, seq):
    assert len(DILATED_PATTERNS) == 3 and DILATED_PATTERNS[0][1] == 1
    p = pl.program_id(1)
    lane = lax.broadcasted_iota(I32, (QBLK, LANES), 1)
    head0 = lane < HEAD_DIM
    n_blocks = seq // QBLK

    for pi in range(len(DILATED_PATTERNS)):
        bucket = jnp.broadcast_to(bucket_ref[pi:pi + 1, :], (QBLK, BASE_W))
        for hh in range(2):
            base = jnp.full((QBLK, BASE_W), NEG, F32)
            for bk in range(N_BUCKETS):
                base = jnp.where(bucket == bk, bias_tab_ref[bk, 2 * p + hh], base)
            for var, shift in enumerate((BASE_W - SIDE, 0, SIDE)):
                tile = pltpu.roll(base, shift, 1, stride=1, stride_axis=0)
                bias_ref[pi * 3 + var, hh * QBLK:(hh + 1) * QBLK, :] = tile[:, :KWIN]

    head0_k = lax.broadcasted_iota(I32, (KWIN, LANES), 1) < HEAD_DIM
    prev_d = 1
    for pi, (_, d) in enumerate(DILATED_PATTERNS):
        seg_len = seq // d
        blocks_per_seg = seg_len // QBLK

        step, prev_len = d // prev_d, seq // prev_d
        assert step * prev_d == d
        keep_f32 = 0 < pi < len(DILATED_PATTERNS) - 1
        for i, (src_ref, dst_ref, scale) in enumerate(
                ((q_ref, qp_ref, HEAD_DIM ** -0.5), (k_ref, kp_ref, None), (v_ref, vp_ref, None))):
            for r_prev in range(prev_d):
                for r_step in range(step):
                    src = pl.ds(r_prev * prev_len + r_step, seg_len, stride=step) if step > 1 else pl.ds(0, seq)
                    dst = pl.ds((r_prev + prev_d * r_step) * seg_len, seg_len)
                    val = src_ref[0, src, :] if prev_d == 1 else stage_ref[i, src, :]
                    if keep_f32:
                        stage_ref[i, dst, :] = val
                    dst_ref[dst, :] = (val if scale is None else val * scale).astype(BF16)
        prev_d = d

        def placement(t, seg_len=seg_len, blocks_per_seg=blocks_per_seg):
            seg = t // blocks_per_seg
            u = t % blocks_per_seg
            q0 = pl.multiple_of(t * QBLK, QBLK)
            seg0 = seg * seg_len
            k0 = pl.multiple_of(jnp.clip(q0 - SIDE, seg0, seg0 + seg_len - KWIN), SIDE)
            var = jnp.where(u == 0, 0, jnp.where(u == blocks_per_seg - 1, 2, 1))
            return seg, u, q0, k0, var

        def logits_stage(g, e_ref, m_ref, pi=pi):
            for jb in range(ATT_GROUP):
                _, _, q0, k0, var = placement(g * ATT_GROUP + jb)
                q = qp_ref[pl.ds(q0, QBLK), :]
                kw = kp_ref[pl.ds(k0, KWIN), :]
                zero = jnp.zeros_like(q)
                q_stack = jnp.concatenate([jnp.where(head0, q, zero), jnp.where(head0, zero, q)], axis=0)
                s = lax.dot_general(q_stack, kw, _NT, preferred_element_type=F32) + bias_ref[pi * 3 + var]
                m = jnp.max(s, axis=-1, keepdims=True)
                e_ref[jb] = jnp.exp(s - m).astype(BF16)
                m_ref[jb] = jnp.broadcast_to(m, (2 * QBLK, LANES))

        def value_stage(g, e_ref, m_ref, pi=pi, d=d):
            for jb in range(ATT_GROUP):
                seg, u, q0, k0, _ = placement(g * ATT_GROUP + jb)
                vw = vp_ref[pl.ds(k0, KWIN), :]
                one = jnp.ones_like(vw)
                pv0 = jnp.dot(e_ref[jb, :QBLK, :], jnp.where(head0_k, vw, one), preferred_element_type=F32)
                pv1 = jnp.dot(e_ref[jb, QBLK:, :], jnp.where(head0_k, one, vw), preferred_element_type=F32)
                num = jnp.where(head0, pv0, pv1)
                den = pltpu.roll(jnp.where(head0, pv1, pv0), HEAD_DIM, 1)
                dst = pl.ds(seg + d * (u * QBLK), QBLK, stride=d) if d > 1 else pl.ds(q0, QBLK)
                o_ref[pi, dst, :] = num * (1.0 / den)
                lse_ref[pi, dst, :] = jnp.where(head0, m_ref[jb, :QBLK, :], m_ref[jb, QBLK:, :]) + jnp.log(den)

        n_groups = n_blocks // ATT_GROUP
        assert n_groups % 2 == 0 and n_groups >= 2
        logits_stage(0, e0_ref, m0_ref)

        def group_pair(i, carry):
            g = 2 * i
            logits_stage(g + 1, e1_ref, m1_ref)
            value_stage(g, e0_ref, m0_ref)
            logits_stage(g + 2, e0_ref, m0_ref)
            value_stage(g + 1, e1_ref, m1_ref)
            return carry

        lax.fori_loop(0, n_groups // 2 - 1, group_pair, 0)
        logits_stage(n_groups - 1, e1_ref, m1_ref)
        value_stage(n_groups - 2, e0_ref, m0_ref)
        value_stage(n_groups - 1, e1_ref, m1_ref)

    rows = 4 * QBLK

    def mix(i, carry):
        sl = pl.ds(pl.multiple_of(i * rows, rows), rows)
        l0, l1, l2 = lse_ref[0, sl, :], lse_ref[1, sl, :], lse_ref[2, sl, :]
        mx = jnp.maximum(jnp.maximum(l0, l1), l2)
        e0, e1, e2 = jnp.exp(l0 - mx), jnp.exp(l1 - mx), jnp.exp(l2 - mx)
        inv = 1.0 / (e0 + e1 + e2)
        acc = (e0 * inv) * o_ref[0, sl, :] + (e1 * inv) * o_ref[1, sl, :] + (e2 * inv) * o_ref[2, sl, :]
        out_ref[0, sl, :] = acc.astype(out_ref.dtype)
        return carry

    lax.fori_loop(0, seq // rows, mix, 0)


def _dilated(att, rel_bias, n_pairs):
    b, s, _ = att.shape
    n_pat = len(DILATED_PATTERNS)
    blk = lambda off: pl.BlockSpec((1, s, LANES), lambda bi, pi: (bi, 0, off + pi))
    bucket_rows = jnp.asarray(_bucket_rows())
    return pl.pallas_call(
        functools.partial(_dilated_kernel, seq=s),
        grid=(b, n_pairs),
        in_specs=[
            pl.BlockSpec(memory_space=pltpu.SMEM),
            pl.BlockSpec((n_pat, BASE_W), lambda bi, pi: (0, 0)),
            blk(0), blk(n_pairs), blk(2 * n_pairs),
        ],
        out_specs=pl.BlockSpec((1, s, LANES), lambda bi, pi: (bi, 0, pi)),
        out_shape=jax.ShapeDtypeStruct((b, s, n_pairs * LANES), BF16),
        scratch_shapes=[
            pltpu.VMEM((n_pat * 3, 2 * QBLK, KWIN), F32),
            pltpu.VMEM((s, LANES), BF16),
            pltpu.VMEM((s, LANES), BF16),
            pltpu.VMEM((s, LANES), BF16),
            pltpu.VMEM((3, s, LANES), F32),
            pltpu.VMEM((n_pat, s, LANES), F32),
            pltpu.VMEM((n_pat, s, LANES), F32),
            pltpu.VMEM((ATT_GROUP, 2 * QBLK, KWIN), BF16),
            pltpu.VMEM((ATT_GROUP, 2 * QBLK, KWIN), BF16),
            pltpu.VMEM((ATT_GROUP, 2 * QBLK, LANES), F32),
            pltpu.VMEM((ATT_GROUP, 2 * QBLK, LANES), F32),
        ],
        compiler_params=_params("parallel", "parallel"),
        name="dilated",
    )(rel_bias, bucket_rows, att, att, att)


def _outproj_kernel(r_ref, a_ref, x_ref, w_ref, g_ref, wr_ref, x1_ref, h2_ref, aff_ref):
    n_r = r_ref.shape[1]
    x1 = (x_ref[...]
          + jnp.dot(r_ref[...], w_ref[:n_r, :], preferred_element_type=F32)
          + jnp.dot(a_ref[...], w_ref[n_r:, :], preferred_element_type=F32))
    x1_ref[...] = x1
    ms = jnp.mean(x1 * x1, axis=-1, keepdims=True)
    h2 = x1 * lax.rsqrt(ms + EPS) * g_ref[...]
    h_hi = h2.astype(BF16)
    h2_ref[...] = h_hi
    n_e = aff_ref.shape[1]
    tm = h2.shape[0]
    h_lo = (h2 - h_hi.astype(F32)).astype(BF16)
    wr = wr_ref[...]
    w_hi = wr.astype(BF16)
    w_lo = (wr - w_hi.astype(F32)).astype(BF16)
    w_parts = jnp.where(lax.broadcasted_iota(I32, wr.shape, 1) < n_e, w_hi, w_lo)
    prod = jnp.dot(jnp.concatenate([h_hi, h_lo], axis=0), w_parts, preferred_element_type=F32)
    logits = prod[:tm] + (pltpu.roll(prod[:tm], LANES - n_e, 1) + prod[tm:])
    expert_lane = lax.broadcasted_iota(I32, logits.shape, 1) < n_e
    logits = jnp.where(expert_lane, logits, NEG)
    m = jnp.max(logits, axis=1, keepdims=True)
    e = jnp.exp(logits - m)
    aff = e / jnp.sum(e, axis=1, keepdims=True)
    aff_ref[0] = aff.T[:n_e, :]


def _outproj(r, a, x, w_bf16, gain, w_router, tm=512):
    b, s, d = x.shape
    n_r, n_a = r.shape[-1], a.shape[-1]
    n_e = w_router.shape[1]
    assert 2 * n_e <= LANES
    w_router = jnp.pad(jnp.concatenate([w_router, w_router], axis=1), ((0, 0), (0, LANES - 2 * n_e)))
    spt = s // tm
    tok = lambda width: pl.BlockSpec((tm, width), lambda i: (i, 0))
    const = lambda shape: pl.BlockSpec(shape, lambda i: (0, 0))
    return pl.pallas_call(
        _outproj_kernel,
        grid=(b * spt,),
        in_specs=[tok(n_r), tok(n_a), tok(d), const((n_r + n_a, d)), const((1, d)), const((d, LANES))],
        out_specs=[tok(d), tok(d), pl.BlockSpec((1, n_e, tm), lambda i: (i // spt, 0, i % spt))],
        out_shape=[
            jax.ShapeDtypeStruct((b * s, d), F32),
            jax.ShapeDtypeStruct((b * s, d), BF16),
            jax.ShapeDtypeStruct((b, n_e, s), F32),
        ],
        compiler_params=_params("parallel"),
        name="outproj",
    )(r.reshape(b * s, n_r), a.reshape(b * s, n_a), x.reshape(b * s, d), w_bf16, gain, w_router)


def _select_kernel(aff_ref, posm_ref, cnt_ref, *, cap):
    rows, s = aff_ref.shape
    bits = pltpu.bitcast(aff_ref[...], I32)

    def search(i, thr):
        cand = thr | jnp.left_shift(jnp.int32(1), 30 - i)
        cnt = jnp.sum((bits >= cand).astype(I32), axis=1, keepdims=True)
        return jnp.where(cnt >= cap, cand, thr)

    thr = lax.fori_loop(0, 31, search, jnp.zeros((rows, 1), I32))
    gt = bits > thr
    eq = bits == thr
    need = cap - jnp.sum(gt.astype(I32), axis=1, keepdims=True)

    w = LANES
    tri = (lax.broadcasted_iota(I32, (w, w), 0) <= lax.broadcasted_iota(I32, (w, w), 1)).astype(BF16)

    def excl_prefix(flags):
        carry = jnp.zeros((rows, 1), F32)
        out = []
        for c0 in range(0, s, w):
            f = flags[:, c0:c0 + w].astype(BF16)
            inc = jnp.dot(f, tri, preferred_element_type=F32)
            out.append(inc - f.astype(F32) + carry)
            carry = carry + inc[:, w - 1:w]
        return jnp.concatenate(out, axis=1).astype(I32)

    sel = gt | (eq & (excl_prefix(eq) < need))
    posm_ref[...] = jnp.where(sel, excl_prefix(sel), -1)

    tok = lax.broadcasted_iota(I32, (w, w), 0)
    edge = lax.broadcasted_iota(I32, (w, w), 1) * MOE_TILE
    cnt = jnp.zeros((rows, w), F32)
    for c0 in range(0, s, w):
        before = ((tok + c0) < edge).astype(BF16)
        cnt = cnt + jnp.dot(sel[:, c0:c0 + w].astype(BF16), before, preferred_element_type=F32)
    cnt_ref[...] = cnt.astype(I32)


def _select(aff_rows, cap):
    rows, s = aff_rows.shape
    return pl.pallas_call(
        functools.partial(_select_kernel, cap=cap),
        out_shape=[jax.ShapeDtypeStruct((rows, s), I32), jax.ShapeDtypeStruct((rows, LANES), I32)],
        compiler_params=pltpu.CompilerParams(vmem_limit_bytes=VMEM_LIMIT),
        name="select",
    )(aff_rows)


def _slot_window(cnt_ref, row, j, cap):
    lo, hi = cnt_ref[row * CNT_STRIDE + j], cnt_ref[row * CNT_STRIDE + j + 1]
    w0 = pl.multiple_of(jnp.minimum((lo // SLOT_ALIGN) * SLOT_ALIGN, cap - SLOT_WIN), SLOT_ALIGN)
    return hi, w0


def _any_overflow(windows):
    return functools.reduce(jnp.logical_or, [hi > w0 + SLOT_WIN for hi, w0 in windows])


def _gather_kernel(cnt_ref, posm_ref, h2_ref, xg_ref):
    bi, j = pl.program_id(0), pl.program_id(1)
    n_e, cap = xg_ref.shape[1:3]
    tt = h2_ref.shape[1]

    @pl.when(j == 0)
    def _():
        xg_ref[...] = jnp.zeros_like(xg_ref)

    slot0 = lax.broadcasted_iota(I32, (SLOT_WIN, tt), 0)
    windows = [_slot_window(cnt_ref, bi * n_e + e, j, cap) for e in range(n_e)]
    for e in range(0, n_e, 2):
        onehot = jnp.concatenate(
            [(posm_ref[0, e + i:e + i + 1, :] == slot0 + windows[e + i][1]).astype(BF16) for i in range(2)], axis=0)
        rows = jnp.dot(onehot, h2_ref[0], preferred_element_type=F32).astype(BF16)
        for i in range(2):
            xg_ref[0, e + i, pl.ds(windows[e + i][1], SLOT_WIN), :] += rows[i * SLOT_WIN:(i + 1) * SLOT_WIN]

    @pl.when(_any_overflow(windows))
    def _():
        wide = lax.broadcasted_iota(I32, (MXU_DEPTH, tt), 0)
        for e, (hi, w0) in enumerate(windows):
            @pl.when(hi > w0 + SLOT_WIN)
            def _(e=e, w0=w0):
                for ws in range(0, cap, MXU_DEPTH):
                    slot = wide + ws
                    onehot = ((posm_ref[0, e:e + 1, :] == slot) & (slot >= w0 + SLOT_WIN)).astype(BF16)
                    extra = jnp.dot(onehot, h2_ref[0], preferred_element_type=F32)
                    xg_ref[0, e, ws:ws + MXU_DEPTH, :] += extra.astype(BF16)


def _gather(cnt, posm, h2, cap):
    b, n_e, s = posm.shape
    d = h2.shape[2]
    return pl.pallas_call(
        _gather_kernel,
        grid_spec=pltpu.PrefetchScalarGridSpec(
            num_scalar_prefetch=1,
            grid=(b, s // MOE_TILE),
            in_specs=[
                pl.BlockSpec((1, n_e, MOE_TILE), lambda bi, j, cnt: (bi, 0, j)),
                pl.BlockSpec((1, MOE_TILE, d), lambda bi, j, cnt: (bi, j, 0)),
            ],
            out_specs=pl.BlockSpec((1, n_e, cap, d), lambda bi, j, cnt: (bi, 0, 0, 0)),
        ),
        out_shape=jax.ShapeDtypeStruct((b, n_e, cap, d), BF16),
        compiler_params=_params("parallel", "arbitrary"),
        name="gather",
    )(cnt, posm, h2)


def _ffn_kernel(xg_ref, wg_hbm, wu_hbm, wd_hbm, y_ref, wg_buf, wu_buf, wd_buf, wg_stage, wu_stage, wd_stage,
                sem, acc_ref, *, tf):
    e, bi = pl.program_id(0), pl.program_id(1)
    n_e, n_b = pl.num_programs(0), pl.num_programs(1)
    rows_in, rows_dn = wg_stage.shape[1], wd_stage.shape[1]
    f_total = wg_buf.shape[2]

    def chunk_copies(expert, c):
        st = c % 2
        return (
            pltpu.make_async_copy(wg_hbm.at[expert, pl.ds(c * rows_in, rows_in), :], wg_stage.at[st], sem.at[0, st]),
            pltpu.make_async_copy(wu_hbm.at[expert, pl.ds(c * rows_in, rows_in), :], wu_stage.at[st], sem.at[1, st]),
            pltpu.make_async_copy(wd_hbm.at[expert, pl.ds(c * rows_dn, rows_dn), :], wd_stage.at[st], sem.at[2, st]),
        )

    def finish_chunk(expert, c):
        for cp in chunk_copies(expert, c):
            cp.wait()
        slot, st = expert % 2, c % 2
        wg_buf[slot, pl.ds(c * rows_in, rows_in), :] = wg_stage[st].astype(BF16)
        wu_buf[slot, pl.ds(c * rows_in, rows_in), :] = wu_stage[st].astype(BF16)
        wd_buf[slot, pl.ds(c * rows_dn, rows_dn), :] = wd_stage[st].astype(BF16)

    @pl.when((e == 0) & (bi == 0))
    def _():
        for cp in chunk_copies(0, 0):
            cp.start()

        def load_first(c, carry):
            @pl.when(c + 1 < n_b)
            def _():
                for cp in chunk_copies(0, c + 1):
                    cp.start()

            finish_chunk(0, c)
            return carry

        lax.fori_loop(0, n_b, load_first, 0)

    pending = jnp.where(bi > 0, e + 1 < n_e, e > 0)

    @pl.when(pending)
    def _():
        finish_chunk(jnp.where(bi > 0, e + 1, e), jnp.where(bi > 0, bi - 1, n_b - 1))

    @pl.when(e + 1 < n_e)
    def _():
        for cp in chunk_copies(e + 1, bi):
            cp.start()

    def swiglu(slot):
        xg = xg_ref[0, 0]
        for fi in range(f_total // tf):
            fs = slice(fi * tf, (fi + 1) * tf)
            gate = jnp.dot(xg, wg_buf[slot, :, fs], preferred_element_type=F32)
            up = jnp.dot(xg, wu_buf[slot, :, fs], preferred_element_type=F32)
            hid = ((gate * jax.nn.sigmoid(gate)) * up).astype(BF16)
            part = jnp.dot(hid, wd_buf[slot, fs, :], preferred_element_type=F32)
            if fi == 0:
                acc_ref[...] = part
            else:
                acc_ref[...] += part
        y_ref[0, 0] = acc_ref[...].astype(y_ref.dtype)

    for slot in range(2):
        pl.when(e % 2 == slot)(functools.partial(swiglu, slot))


def _ffn(xg, wg, wu, wd, tf=512):
    b, n_e, cap, d = xg.shape
    f = wg.shape[2]
    assert d % b == 0 and f % b == 0 and (d // b) % SLOT_ALIGN == 0, "one weight row chunk per batch step"
    tok = pl.BlockSpec((1, 1, cap, d), lambda e, bi: (bi, e, 0, 0))
    hbm = pl.BlockSpec(memory_space=pl.ANY)
    return pl.pallas_call(
        functools.partial(_ffn_kernel, tf=tf),
        grid=(n_e, b),
        in_specs=[tok, hbm, hbm, hbm],
        out_specs=tok,
        out_shape=jax.ShapeDtypeStruct((b, n_e, cap, d), BF16),
        scratch_shapes=[
            pltpu.VMEM((2, d, f), BF16), pltpu.VMEM((2, d, f), BF16), pltpu.VMEM((2, f, d), BF16),
            pltpu.VMEM((2, d // b, f), F32), pltpu.VMEM((2, d // b, f), F32), pltpu.VMEM((2, f // b, d), F32),
            pltpu.SemaphoreType.DMA((3, 2)),
            pltpu.VMEM((cap, d), F32),
        ],
        compiler_params=_params("arbitrary", "arbitrary"),
        name="ffn",
    )(xg, wg, wu, wd)


def _combine_kernel(cnt_ref, pos_ref, gate_ref, y_ref, x1_ref, g_ref, out_ref, acc_ref):
    bi, j = pl.program_id(0), pl.program_id(1)
    tt = x1_ref.shape[1]
    n_e, cap = y_ref.shape[1:3]
    slot0 = lax.broadcasted_iota(I32, (tt, SLOT_WIN), 1)
    windows = [_slot_window(cnt_ref, bi * n_e + e, j, cap) for e in range(n_e)]
    acc = x1_ref[0]
    for e in range(0, n_e, 2):
        scatter = jnp.concatenate(
            [jnp.where(pos_ref[0, :, e + i:e + i + 1] == slot0 + windows[e + i][1],
                       gate_ref[0, :, e + i:e + i + 1], 0.0).astype(BF16) for i in range(2)], axis=1)
        y_pair = jnp.concatenate(
            [y_ref[0, e + i, pl.ds(windows[e + i][1], SLOT_WIN), :] for i in range(2)], axis=0)
        acc = acc + jnp.dot(scatter, y_pair, preferred_element_type=F32)
    acc_ref[...] = acc

    @pl.when(_any_overflow(windows))
    def _():
        wide = lax.broadcasted_iota(I32, (tt, MXU_DEPTH), 1)
        for e, (hi, w0) in enumerate(windows):
            @pl.when(hi > w0 + SLOT_WIN)
            def _(e=e, w0=w0):
                for ws in range(0, cap, MXU_DEPTH):
                    slot = wide + ws
                    extra = jnp.where((pos_ref[0, :, e:e + 1] == slot) & (slot >= w0 + SLOT_WIN),
                                      gate_ref[0, :, e:e + 1], 0.0).astype(BF16)
                    acc_ref[...] += jnp.dot(extra, y_ref[0, e, ws:ws + MXU_DEPTH, :],
                                            preferred_element_type=F32)

    acc = acc_ref[...]
    ms = jnp.mean(acc * acc, axis=-1, keepdims=True)
    out_ref[0] = acc * lax.rsqrt(ms + EPS) * g_ref[...]


def _combine(cnt, pos_t, gate_t, y, x1, gain):
    b, s, d = x1.shape
    n_e, cap = y.shape[1:3]
    tt = MOE_TILE
    return pl.pallas_call(
        _combine_kernel,
        grid_spec=pltpu.PrefetchScalarGridSpec(
            num_scalar_prefetch=1,
            grid=(b, s // tt),
            in_specs=[
                pl.BlockSpec((1, tt, n_e), lambda bi, j, cnt: (bi, j, 0)),
                pl.BlockSpec((1, tt, n_e), lambda bi, j, cnt: (bi, j, 0)),
                pl.BlockSpec((1, n_e, cap, d), lambda bi, j, cnt: (bi, 0, 0, 0)),
                pl.BlockSpec((1, tt, d), lambda bi, j, cnt: (bi, j, 0)),
                pl.BlockSpec((1, d), lambda bi, j, cnt: (0, 0)),
            ],
            out_specs=pl.BlockSpec((1, tt, d), lambda bi, j, cnt: (bi, j, 0)),
            scratch_shapes=[pltpu.VMEM((tt, d), F32)],
        ),
        out_shape=jax.ShapeDtypeStruct((b, s, d), F32),
        compiler_params=_params("parallel", "parallel"),
        name="combine",
    )(cnt, pos_t, gate_t, y, x1, gain)


def _rope_tables(seq):
    half = HEAD_DIM // 2
    pos = jnp.arange(seq, dtype=F32)
    inv = ROPE_BASE ** (-jnp.arange(0, HEAD_DIM, 2, dtype=F32) / HEAD_DIM)
    ang = pos[:, None] * inv[None, :]
    cos, sin = jnp.cos(ang), jnp.sin(ang)
    reps = LANES // HEAD_DIM
    cos_t = jnp.tile(jnp.concatenate([cos, cos], axis=1), (1, reps))
    sin_t = jnp.tile(jnp.concatenate([-sin, sin], axis=1), (1, reps))
    return cos_t, sin_t


def kernel(x, norm1_gain, w_in, ret_log_decay, ret_gn_gain, rel_bias, w_out, norm2_gain, w_router,
           w_gate, w_up, w_down, final_gain):
    b, s, d = x.shape
    depth = w_in.shape[0]
    ret_width = ret_gn_gain.shape[1]
    n_ret_pairs = ret_width // LANES
    att_width = (w_in.shape[2] - 4 * ret_width) // 3
    n_att_pairs = att_width // LANES
    cap = CAPACITY_FACTOR * s // N_EXPERTS
    cos_t, sin_t = _rope_tables(s)

    assert depth == 1, "single-layer block: the final norm is fused into the combine kernel"
    layer = 0
    ret, att = _proj(x.reshape(b * s, d), norm1_gain[layer][None, :], w_in[layer].astype(BF16),
                     cos_t, sin_t, n_ret=4 * ret_width, ret_width=ret_width)
    r = _retention(ret.reshape(b, s, -1), ret_log_decay[layer], ret_gn_gain[layer][None, :], n_ret_pairs)
    a = _dilated(att.reshape(b, s, -1), rel_bias, n_att_pairs)
    x1, h2, aff = _outproj(r, a, x, w_out[layer].astype(BF16), norm2_gain[layer][None, :],
                           w_router[layer])
    assert s % MOE_TILE == 0 and s // MOE_TILE < CNT_STRIDE and cap % MXU_DEPTH == 0 and N_EXPERTS % 2 == 0
    posm, cnt = _select(aff.reshape(b * N_EXPERTS, s), cap)
    posm = posm.reshape(b, N_EXPERTS, s)
    cnt = cnt[:, :CNT_STRIDE].reshape(-1)
    xg = _gather(cnt, posm, h2.reshape(b, s, d), cap)
    y = _ffn(xg, w_gate[layer], w_up[layer], w_down[layer])
    return _combine(cnt, jnp.swapaxes(posm, 1, 2), jnp.swapaxes(aff, 1, 2), y, x1.reshape(b, s, d),
                    final_gain[None, :])
```

```python
import functools
import math

import numpy as np
import jax
import jax.numpy as jnp
from jax import lax
from jax.experimental import pallas as pl
from jax.experimental.pallas import tpu as pltpu

F32 = jnp.float32
BF16 = jnp.bfloat16
I32 = jnp.int32

HEAD_DIM = 64
LANES = 128
RET_CHUNK = 128
RET_GROUP = 4
ROPE_BASE = 10000.0
DILATED_PATTERNS = ((128, 1), (512, 4), (2048, 16))
SIDE = 64
N_BUCKETS = 32
MAX_DISTANCE = 1024
N_EXPERTS = 16
CAPACITY_FACTOR = 2
MOE_TILE = 512
MXU_DEPTH = 256
SLOT_WIN = MXU_DEPTH // 2
SLOT_ALIGN = 16
CNT_STRIDE = 16
EPS = 1e-6
NEG = -1e30
VMEM_LIMIT = 56 * 1024 * 1024

_NT = (((1,), (1,)), ((), ()))
_TN = (((0,), (0,)), ((), ()))


def _params(*sem):
    return pltpu.CompilerParams(dimension_semantics=sem, vmem_limit_bytes=VMEM_LIMIT)


def _proj_kernel(x_ref, g_ref, w_ref, cos_ref, sin_ref, ret_ref, att_ref, *, n_ret, ret_width, chunk):
    x = x_ref[...]
    ms = jnp.mean(x * x, axis=-1, keepdims=True)
    h = (x * lax.rsqrt(ms + EPS) * g_ref[...]).astype(BF16)
    n_cols = w_ref.shape[1]
    cos, sin = cos_ref[...], sin_ref[...]
    first_half = (lax.broadcasted_iota(I32, cos.shape, 1) % HEAD_DIM) < (HEAD_DIM // 2)
    for c0 in range(0, n_cols, chunk):
        o = jnp.dot(h, w_ref[:, c0:c0 + chunk], preferred_element_type=F32)
        for l0 in range(0, chunk, LANES):
            t = o[:, l0:l0 + LANES]
            col = c0 + l0
            if col >= n_ret:
                att_ref[(col - n_ret) // LANES] = t
            elif col >= 2 * ret_width:
                ret_ref[col // LANES] = t.astype(BF16)
            else:
                scale = 1.0 if col < ret_width else HEAD_DIM ** -0.5
                swapped = jnp.where(first_half, pltpu.roll(t, LANES - HEAD_DIM // 2, 1),
                                    pltpu.roll(t, HEAD_DIM // 2, 1))
                ret_ref[col // LANES] = ((t * cos + swapped * sin) * scale).astype(BF16)


def _proj(x2d, gain, w_bf16, cos_t, sin_t, n_ret, ret_width, tm=512):
    t, d = x2d.shape
    n_cols = w_bf16.shape[1]
    n_att = n_cols - n_ret
    seq_tiles = cos_t.shape[0] // tm
    assert ret_width % 512 == 0 and n_ret % 512 == 0 and n_cols % 512 == 0
    return pl.pallas_call(
        functools.partial(_proj_kernel, n_ret=n_ret, ret_width=ret_width, chunk=512),
        grid=(t // tm,),
        in_specs=[
            pl.BlockSpec((tm, d), lambda i: (i, 0)),
            pl.BlockSpec((1, d), lambda i: (0, 0)),
            pl.BlockSpec((d, n_cols), lambda i: (0, 0)),
            pl.BlockSpec((tm, LANES), lambda i: (i % seq_tiles, 0)),
            pl.BlockSpec((tm, LANES), lambda i: (i % seq_tiles, 0)),
        ],
        out_specs=[
            pl.BlockSpec((n_ret // LANES, tm, LANES), lambda i: (0, i, 0)),
            pl.BlockSpec((n_att // LANES, tm, LANES), lambda i: (0, i, 0)),
        ],
        out_shape=[
            jax.ShapeDtypeStruct((n_ret // LANES, t, LANES), BF16),
            jax.ShapeDtypeStruct((n_att // LANES, t, LANES), F32),
        ],
        compiler_params=_params("parallel"),
        name="proj",
    )(x2d, gain, w_bf16, cos_t, sin_t)


def _retention_kernel(decay_ref, q_ref, k_ref, v_ref, g_ref, gn_ref, out_ref,
                      rf_ref, rb_ref, kvf_ref, kvb_ref, p0_ref, p1_ref, *, seq):
    c = RET_CHUNK
    n_chunks = seq // c
    p = pl.program_id(1)
    lane = lax.broadcasted_iota(I32, (c, LANES), 1)
    row = lax.broadcasted_iota(I32, (c, LANES), 0)
    head0 = lane < HEAD_DIM
    rowf = row.astype(F32)

    lgf0, lgf1 = decay_ref[0, 2 * p], decay_ref[0, 2 * p + 1]
    lgb0, lgb1 = decay_ref[1, 2 * p], decay_ref[1, 2 * p + 1]
    lgf_lane = jnp.where(head0, lgf0, lgf1)
    lgb_lane = jnp.where(head0, lgb0, lgb1)
    lgf_row = jnp.where(row < HEAD_DIM, lgf0, lgf1)
    lgb_row = jnp.where(row < HEAD_DIM, lgb0, lgb1)
    same_head = (row < HEAD_DIM) == head0

    zeta_f = jnp.exp((c - 1 - rowf) * lgf_lane)
    zeta_b = jnp.exp(rowf * lgb_lane)
    xi_f = jnp.exp((rowf + 1.0) * lgf_lane)
    xi_b = jnp.exp((c - rowf) * lgb_lane)
    gch_f = jnp.where(same_head, jnp.exp(c * lgf_row), 0.0)
    gch_b = jnp.where(same_head, jnp.exp(c * lgb_row), 0.0)

    diff = (row - lane).astype(F32)

    def dmat(lf, lb):
        return jnp.exp(jnp.where(diff >= 0, diff * lf, -diff * lb))

    d_stack = jnp.concatenate([dmat(lgf0, lgb0), dmat(lgf1, lgb1)], axis=0)

    def kv_step(n, carry):
        sl = pl.ds(pl.multiple_of(n * c, c), c)
        k = k_ref[0, sl, :].astype(F32)
        v = v_ref[0, sl, :]
        kvf = lax.dot_general((k * zeta_f).astype(BF16), v, _TN, preferred_element_type=F32)
        kvb = lax.dot_general((k * zeta_b).astype(BF16), v, _TN, preferred_element_type=F32)
        kvf_ref[n] = jnp.where(same_head, kvf, 0.0)
        kvb_ref[n] = jnp.where(same_head, kvb, 0.0)
        return carry

    lax.fori_loop(0, n_chunks, kv_step, 0, unroll=8)

    def scan_step(i, states):
        sf, sb = states
        nb = n_chunks - 1 - i
        rf_ref[i] = sf.astype(BF16)
        rb_ref[nb] = sb.astype(BF16)
        return gch_f * sf + kvf_ref[i], gch_b * sb + kvb_ref[nb]

    zero_state = jnp.zeros((LANES, LANES), F32)
    lax.fori_loop(0, n_chunks, scan_step, (zero_state, zero_state), unroll=4)

    gn_gain = gn_ref[...]

    def score_stage(g, p_ref):
        for jc in range(RET_GROUP):
            sl = pl.ds(pl.multiple_of((g * RET_GROUP + jc) * c, c), c)
            q = q_ref[0, sl, :]
            zero = jnp.zeros_like(q)
            q_stack = jnp.concatenate([jnp.where(head0, q, zero), jnp.where(head0, zero, q)], axis=0)
            scores = lax.dot_general(q_stack, k_ref[0, sl, :], _NT, preferred_element_type=F32) * d_stack
            sb = scores.astype(BF16)
            p_ref[jc] = jnp.concatenate([sb[:c], sb[c:]], axis=1)

    def half_mean(t):
        s0 = jnp.sum(jnp.where(head0, t, 0.0), axis=-1, keepdims=True)
        s1 = jnp.sum(jnp.where(head0, 0.0, t), axis=-1, keepdims=True)
        return jnp.where(head0, s0, s1) * (1.0 / HEAD_DIM)

    def value_stage(g, p_ref):
        for jc in range(RET_GROUP):
            n = g * RET_GROUP + jc
            sl = pl.ds(pl.multiple_of(n * c, c), c)
            q = q_ref[0, sl, :]
            v = v_ref[0, sl, :]
            zero = jnp.zeros_like(v)
            v_stack = jnp.concatenate([jnp.where(head0, v, zero), jnp.where(head0, zero, v)], axis=0)
            inner = jnp.dot(p_ref[jc], v_stack, preferred_element_type=F32)
            cross_f = jnp.dot(q, rf_ref[n], preferred_element_type=F32) * xi_f
            cross_b = jnp.dot(q, rb_ref[n], preferred_element_type=F32) * xi_b
            r = inner + cross_f + cross_b
            dlt = r - half_mean(r)
            var = half_mean(dlt * dlt)
            gate = g_ref[0, sl, :].astype(F32)
            y = dlt * lax.rsqrt(var + EPS) * gn_gain * (gate * jax.nn.sigmoid(gate))
            out_ref[0, sl, :] = y.astype(out_ref.dtype)

    n_groups = n_chunks // RET_GROUP
    assert n_groups % 2 == 0 and n_groups >= 2
    score_stage(0, p0_ref)

    def group_pair(i, carry):
        g = 2 * i
        score_stage(g + 1, p1_ref)
        value_stage(g, p0_ref)
        score_stage(g + 2, p0_ref)
        value_stage(g + 1, p1_ref)
        return carry

    lax.fori_loop(0, n_groups // 2 - 1, group_pair, 0)
    score_stage(n_groups - 1, p1_ref)
    value_stage(n_groups - 2, p0_ref)
    value_stage(n_groups - 1, p1_ref)


def _retention(ret, decay, gn_gain, n_pairs, b):
    s = ret.shape[1] // b
    blk = lambda off: pl.BlockSpec((1, s, LANES), lambda bi, pi: (off + pi, bi, 0))
    return pl.pallas_call(
        functools.partial(_retention_kernel, seq=s),
        grid=(b, n_pairs),
        in_specs=[
            pl.BlockSpec(memory_space=pltpu.SMEM),
            blk(0), blk(n_pairs), blk(2 * n_pairs), blk(3 * n_pairs),
            pl.BlockSpec((1, LANES), lambda bi, pi: (0, pi)),
        ],
        out_specs=pl.BlockSpec((1, s, LANES), lambda bi, pi: (pi, bi, 0)),
        out_shape=jax.ShapeDtypeStruct((n_pairs, b * s, LANES), BF16),
        scratch_shapes=[
            pltpu.VMEM((s // RET_CHUNK, LANES, LANES), BF16),
            pltpu.VMEM((s // RET_CHUNK, LANES, LANES), BF16),
            pltpu.VMEM((s // RET_CHUNK, LANES, LANES), F32),
            pltpu.VMEM((s // RET_CHUNK, LANES, LANES), F32),
            pltpu.VMEM((RET_GROUP, RET_CHUNK, 2 * RET_CHUNK), BF16),
            pltpu.VMEM((RET_GROUP, RET_CHUNK, 2 * RET_CHUNK), BF16),
        ],
        compiler_params=_params("parallel", "parallel"),
        name="retention",
    )(decay, ret, ret, ret, ret, gn_gain)


QBLK = 2 * SIDE
KWIN = 4 * SIDE
BASE_W = 512
ATT_GROUP = 4


def _t5_bucket_np(rel):
    half = N_BUCKETS // 2
    max_exact = half // 2
    bucket = np.where(rel > 0, half, 0)
    n = np.abs(rel)
    nf = np.maximum(n, 1).astype(np.float32)
    large = max_exact + (np.log(nf / np.float32(max_exact)) / np.float32(math.log(MAX_DISTANCE / max_exact))
                         * np.float32(half - max_exact)).astype(np.int32)
    large = np.minimum(large, half - 1)
    return (bucket + np.where(n < max_exact, n, large)).astype(np.int32)


def _bucket_rows():
    k = np.arange(BASE_W)
    off = k - SIDE
    rows = []
    for _, dilation in DILATED_PATTERNS:
        rows.append(np.where(k <= 2 * SIDE, _t5_bucket_np(off * dilation), -1))
    return np.stack(rows).astype(np.int32)


def _dilated_kernel(bias_tab_ref, bucket_ref, q_ref, k_ref, v_ref, out_ref,
                    bias_ref, qp_ref, kp_ref, vp_ref, stage_ref, o_ref, lse_ref,
                    e0_ref, e1_ref, m0_ref, m1_ref, *, seq):
    assert len(DILATED_PATTERNS) == 3 and DILATED_PATTERNS[0][1] == 1
    p = pl.program_id(1)
    lane = lax.broadcasted_iota(I32, (QBLK, LANES), 1)
    head0 = lane < HEAD_DIM
    n_blocks = seq // QBLK

    for pi in range(len(DILATED_PATTERNS)):
        bucket = jnp.broadcast_to(bucket_ref[pi:pi + 1, :], (QBLK, BASE_W))
        for hh in range(2):
            base = jnp.full((QBLK, BASE_W), NEG, F32)
            for bk in range(N_BUCKETS):
                base = jnp.where(bucket == bk, bias_tab_ref[bk, 2 * p + hh], base)
            for var, shift in enumerate((BASE_W - SIDE, 0, SIDE)):
                tile = pltpu.roll(base, shift, 1, stride=1, stride_axis=0)
                bias_ref[pi * 3 + var, hh * QBLK:(hh + 1) * QBLK, :] = tile[:, :KWIN]

    head0_k = lax.broadcasted_iota(I32, (KWIN, LANES), 1) < HEAD_DIM
    order = list(range(1, len(DILATED_PATTERNS))) + [0]
    prev_d = 1
    for slot, pi in enumerate(order):
        d = DILATED_PATTERNS[pi][1]
        final = slot == len(order) - 1
        seg_len = seq // d
        blocks_per_seg = seg_len // QBLK

        if d == 1:
            prev_d = 1
        step, prev_len = d // prev_d, seq // prev_d
        assert step * prev_d == d
        keep_f32 = d > 1 and not final and slot + 1 < len(order) - 1
        for i, (src_ref, dst_ref, scale) in enumerate(
                ((q_ref, qp_ref, HEAD_DIM ** -0.5), (k_ref, kp_ref, None), (v_ref, vp_ref, None))):
            for r_prev in range(prev_d):
                for r_step in range(step):
                    src = pl.ds(r_prev * prev_len + r_step, seg_len, stride=step) if step > 1 else pl.ds(0, seq)
                    dst = pl.ds((r_prev + prev_d * r_step) * seg_len, seg_len)
                    val = src_ref[0, src, :] if prev_d == 1 else stage_ref[i, src, :]
                    if keep_f32:
                        stage_ref[i, dst, :] = val
                    dst_ref[dst, :] = (val if scale is None else val * scale).astype(BF16)
        prev_d = d

        def placement(t, seg_len=seg_len, blocks_per_seg=blocks_per_seg):
            seg = t // blocks_per_seg
            u = t % blocks_per_seg
            q0 = pl.multiple_of(t * QBLK, QBLK)
            seg0 = seg * seg_len
            k0 = pl.multiple_of(jnp.clip(q0 - SIDE, seg0, seg0 + seg_len - KWIN), SIDE)
            var = jnp.where(u == 0, 0, jnp.where(u == blocks_per_seg - 1, 2, 1))
            return seg, u, q0, k0, var

        def logits_stage(g, e_ref, m_ref, pi=pi):
            for jb in range(ATT_GROUP):
                _, _, q0, k0, var = placement(g * ATT_GROUP + jb)
                q = qp_ref[pl.ds(q0, QBLK), :]
                kw = kp_ref[pl.ds(k0, KWIN), :]
                zero = jnp.zeros_like(q)
                q_stack = jnp.concatenate([jnp.where(head0, q, zero), jnp.where(head0, zero, q)], axis=0)
                s = lax.dot_general(q_stack, kw, _NT, preferred_element_type=F32) + bias_ref[pi * 3 + var]
                m = jnp.max(s, axis=-1, keepdims=True)
                e_ref[jb] = jnp.exp(s - m).astype(BF16)
                m_ref[jb] = jnp.broadcast_to(m, (2 * QBLK, LANES))

        def value_stage(g, e_ref, m_ref, slot=slot, d=d, final=final):
            for jb in range(ATT_GROUP):
                seg, u, q0, k0, _ = placement(g * ATT_GROUP + jb)
                vw = vp_ref[pl.ds(k0, KWIN), :]
                one = jnp.ones_like(vw)
                pv0 = jnp.dot(e_ref[jb, :QBLK, :], jnp.where(head0_k, vw, one), preferred_element_type=F32)
                pv1 = jnp.dot(e_ref[jb, QBLK:, :], jnp.where(head0_k, one, vw), preferred_element_type=F32)
                num = jnp.where(head0, pv0, pv1)
                den = pltpu.roll(jnp.where(head0, pv1, pv0), HEAD_DIM, 1)
                o = num * (1.0 / den)
                lse = jnp.where(head0, m_ref[jb, :QBLK, :], m_ref[jb, QBLK:, :]) + jnp.log(den)
                dst = pl.ds(seg + d * (u * QBLK), QBLK, stride=d) if d > 1 else pl.ds(q0, QBLK)
                if not final:
                    o_ref[slot, dst, :] = o
                    lse_ref[slot, dst, :] = lse
                else:
                    lses = [lse_ref[i, dst, :] for i in range(slot)] + [lse]
                    outs = [o_ref[i, dst, :] for i in range(slot)] + [o]
                    mx = functools.reduce(jnp.maximum, lses)
                    ws = [jnp.exp(l - mx) for l in lses]
                    inv = 1.0 / functools.reduce(jnp.add, ws)
                    acc = functools.reduce(jnp.add, [(w * inv) * t for w, t in zip(ws, outs)])
                    out_ref[0, dst, :] = acc.astype(out_ref.dtype)

        n_groups = n_blocks // ATT_GROUP
        assert n_groups % 2 == 0 and n_groups >= 2
        logits_stage(0, e0_ref, m0_ref)

        def group_pair(i, carry):
            g = 2 * i
            logits_stage(g + 1, e1_ref, m1_ref)
            value_stage(g, e0_ref, m0_ref)
            logits_stage(g + 2, e0_ref, m0_ref)
            value_stage(g + 1, e1_ref, m1_ref)
            return carry

        lax.fori_loop(0, n_groups // 2 - 1, group_pair, 0)
        logits_stage(n_groups - 1, e1_ref, m1_ref)
        value_stage(n_groups - 2, e0_ref, m0_ref)
        value_stage(n_groups - 1, e1_ref, m1_ref)


def _dilated(att, rel_bias, n_pairs, b):
    s = att.shape[1] // b
    n_pat = len(DILATED_PATTERNS)
    blk = lambda off: pl.BlockSpec((1, s, LANES), lambda bi, pi: (off + pi, bi, 0))
    bucket_rows = jnp.asarray(_bucket_rows())
    return pl.pallas_call(
        functools.partial(_dilated_kernel, seq=s),
        grid=(b, n_pairs),
        in_specs=[
            pl.BlockSpec(memory_space=pltpu.SMEM),
            pl.BlockSpec((n_pat, BASE_W), lambda bi, pi: (0, 0)),
            blk(0), blk(n_pairs), blk(2 * n_pairs),
        ],
        out_specs=pl.BlockSpec((1, s, LANES), lambda bi, pi: (pi, bi, 0)),
        out_shape=jax.ShapeDtypeStruct((n_pairs, b * s, LANES), BF16),
        scratch_shapes=[
            pltpu.VMEM((n_pat * 3, 2 * QBLK, KWIN), F32),
            pltpu.VMEM((s, LANES), BF16),
            pltpu.VMEM((s, LANES), BF16),
            pltpu.VMEM((s, LANES), BF16),
            pltpu.VMEM((3, s, LANES), F32),
            pltpu.VMEM((n_pat - 1, s, LANES), F32),
            pltpu.VMEM((n_pat - 1, s, LANES), F32),
            pltpu.VMEM((ATT_GROUP, 2 * QBLK, KWIN), BF16),
            pltpu.VMEM((ATT_GROUP, 2 * QBLK, KWIN), BF16),
            pltpu.VMEM((ATT_GROUP, 2 * QBLK, LANES), F32),
            pltpu.VMEM((ATT_GROUP, 2 * QBLK, LANES), F32),
        ],
        compiler_params=_params("parallel", "parallel"),
        name="dilated",
    )(rel_bias, bucket_rows, att, att, att)


def _outproj_kernel(r_ref, a_ref, x_ref, w_ref, g_ref, wr_ref, x1_ref, h2_ref, aff_ref):
    mixed = jnp.concatenate([r_ref[i] for i in range(r_ref.shape[0])]
                            + [a_ref[i] for i in range(a_ref.shape[0])], axis=1)
    x1 = x_ref[...] + jnp.dot(mixed, w_ref[...], preferred_element_type=F32)
    x1_ref[...] = x1
    ms = jnp.mean(x1 * x1, axis=-1, keepdims=True)
    h2 = x1 * lax.rsqrt(ms + EPS) * g_ref[...]
    h_hi = h2.astype(BF16)
    h2_ref[...] = h_hi
    n_e = aff_ref.shape[1]
    tm = h2.shape[0]
    h_lo = (h2 - h_hi.astype(F32)).astype(BF16)
    wr = wr_ref[...]
    w_hi = wr.astype(BF16)
    w_lo = (wr - w_hi.astype(F32)).astype(BF16)
    w_parts = jnp.where(lax.broadcasted_iota(I32, wr.shape, 1) < n_e, w_hi, w_lo)
    prod = jnp.dot(jnp.concatenate([h_hi, h_lo], axis=0), w_parts, preferred_element_type=F32)
    logits = prod[:tm] + (pltpu.roll(prod[:tm], LANES - n_e, 1) + prod[tm:])
    expert_lane = lax.broadcasted_iota(I32, logits.shape, 1) < n_e
    logits = jnp.where(expert_lane, logits, NEG)
    m = jnp.max(logits, axis=1, keepdims=True)
    e = jnp.exp(logits - m)
    aff = e / jnp.sum(e, axis=1, keepdims=True)
    aff_ref[0] = aff.T[:n_e, :]


def _outproj(r, a, x, w_bf16, gain, w_router, tm=512):
    b, s, d = x.shape
    n_r, n_a = r.shape[0], a.shape[0]
    n_e = w_router.shape[1]
    assert 2 * n_e <= LANES
    w_router = jnp.pad(jnp.concatenate([w_router, w_router], axis=1), ((0, 0), (0, LANES - 2 * n_e)))
    spt = s // tm
    tok = lambda width: pl.BlockSpec((tm, width), lambda i: (i, 0))
    slab = lambda blocks: pl.BlockSpec((blocks, tm, LANES), lambda i: (0, i, 0))
    const = lambda shape: pl.BlockSpec(shape, lambda i: (0, 0))
    return pl.pallas_call(
        _outproj_kernel,
        grid=(b * spt,),
        in_specs=[slab(n_r), slab(n_a), tok(d), const(((n_r + n_a) * LANES, d)), const((1, d)), const((d, LANES))],
        out_specs=[tok(d), tok(d), pl.BlockSpec((1, n_e, tm), lambda i: (i // spt, 0, i % spt))],
        out_shape=[
            jax.ShapeDtypeStruct((b * s, d), F32),
            jax.ShapeDtypeStruct((b * s, d), BF16),
            jax.ShapeDtypeStruct((b, n_e, s), F32),
        ],
        compiler_params=_params("parallel"),
        name="outproj",
    )(r, a, x.reshape(b * s, d), w_bf16, gain, w_router)


def _select_kernel(aff_ref, posm_ref, cnt_ref, *, cap):
    rows, s = aff_ref.shape
    bits = pltpu.bitcast(aff_ref[...], I32)

    def search(i, thr):
        cand = thr | jnp.left_shift(jnp.int32(1), 30 - i)
        cnt = jnp.sum((bits >= cand).astype(I32), axis=1, keepdims=True)
        return jnp.where(cnt >= cap, cand, thr)

    thr = lax.fori_loop(0, 31, search, jnp.zeros((rows, 1), I32))
    gt = bits > thr
    eq = bits == thr
    need = cap - jnp.sum(gt.astype(I32), axis=1, keepdims=True)

    w = LANES
    tri = (lax.broadcasted_iota(I32, (w, w), 0) <= lax.broadcasted_iota(I32, (w, w), 1)).astype(BF16)

    def excl_prefix(flags):
        carry = jnp.zeros((rows, 1), F32)
        out = []
        for c0 in range(0, s, w):
            f = flags[:, c0:c0 + w].astype(BF16)
            inc = jnp.dot(f, tri, preferred_element_type=F32)
            out.append(inc - f.astype(F32) + carry)
            carry = carry + inc[:, w - 1:w]
        return jnp.concatenate(out, axis=1).astype(I32)

    sel = gt | (eq & (excl_prefix(eq) < need))
    posm_ref[...] = jnp.where(sel, excl_prefix(sel), -1)

    tok = lax.broadcasted_iota(I32, (w, w), 0)
    edge = lax.broadcasted_iota(I32, (w, w), 1) * MOE_TILE
    cnt = jnp.zeros((rows, w), F32)
    for c0 in range(0, s, w):
        before = ((tok + c0) < edge).astype(BF16)
        cnt = cnt + jnp.dot(sel[:, c0:c0 + w].astype(BF16), before, preferred_element_type=F32)
    cnt_ref[...] = cnt.astype(I32)


def _select(aff_rows, cap):
    rows, s = aff_rows.shape
    return pl.pallas_call(
        functools.partial(_select_kernel, cap=cap),
        out_shape=[jax.ShapeDtypeStruct((rows, s), I32), jax.ShapeDtypeStruct((rows, LANES), I32)],
        compiler_params=pltpu.CompilerParams(vmem_limit_bytes=VMEM_LIMIT),
        name="select",
    )(aff_rows)


def _slot_window(cnt_ref, row, j, cap):
    lo, hi = cnt_ref[row * CNT_STRIDE + j], cnt_ref[row * CNT_STRIDE + j + 1]
    w0 = pl.multiple_of(jnp.minimum((lo // SLOT_ALIGN) * SLOT_ALIGN, cap - SLOT_WIN), SLOT_ALIGN)
    return hi, w0


def _any_overflow(windows):
    return functools.reduce(jnp.logical_or, [hi > w0 + SLOT_WIN for hi, w0 in windows])


def _gather_kernel(cnt_ref, posm_ref, h2_ref, xg_ref):
    bi, j = pl.program_id(0), pl.program_id(1)
    n_e, cap = xg_ref.shape[1:3]
    tt = h2_ref.shape[1]

    @pl.when(j == 0)
    def _():
        xg_ref[...] = jnp.zeros_like(xg_ref)

    slot0 = lax.broadcasted_iota(I32, (SLOT_WIN, tt), 0)
    windows = [_slot_window(cnt_ref, bi * n_e + e, j, cap) for e in range(n_e)]
    for e in range(0, n_e, 2):
        onehot = jnp.concatenate(
            [(posm_ref[0, e + i:e + i + 1, :] == slot0 + windows[e + i][1]).astype(BF16) for i in range(2)], axis=0)
        rows = jnp.dot(onehot, h2_ref[0], preferred_element_type=F32).astype(BF16)
        for i in range(2):
            xg_ref[0, e + i, pl.ds(windows[e + i][1], SLOT_WIN), :] += rows[i * SLOT_WIN:(i + 1) * SLOT_WIN]

    @pl.when(_any_overflow(windows))
    def _():
        wide = lax.broadcasted_iota(I32, (MXU_DEPTH, tt), 0)
        for e, (hi, w0) in enumerate(windows):
            @pl.when(hi > w0 + SLOT_WIN)
            def _(e=e, w0=w0):
                for ws in range(0, cap, MXU_DEPTH):
                    slot = wide + ws
                    onehot = ((posm_ref[0, e:e + 1, :] == slot) & (slot >= w0 + SLOT_WIN)).astype(BF16)
                    extra = jnp.dot(onehot, h2_ref[0], preferred_element_type=F32)
                    xg_ref[0, e, ws:ws + MXU_DEPTH, :] += extra.astype(BF16)


def _gather(cnt, posm, h2, cap):
    b, n_e, s = posm.shape
    d = h2.shape[2]
    return pl.pallas_call(
        _gather_kernel,
        grid_spec=pltpu.PrefetchScalarGridSpec(
            num_scalar_prefetch=1,
            grid=(b, s // MOE_TILE),
            in_specs=[
                pl.BlockSpec((1, n_e, MOE_TILE), lambda bi, j, cnt: (bi, 0, j)),
                pl.BlockSpec((1, MOE_TILE, d), lambda bi, j, cnt: (bi, j, 0)),
            ],
            out_specs=pl.BlockSpec((1, n_e, cap, d), lambda bi, j, cnt: (bi, 0, 0, 0)),
        ),
        out_shape=jax.ShapeDtypeStruct((b, n_e, cap, d), BF16),
        compiler_params=_params("parallel", "arbitrary"),
        name="gather",
    )(cnt, posm, h2)


def _ffn_kernel(xg_ref, wg_hbm, wu_hbm, wd_hbm, y_ref, wg_buf, wu_buf, wd_buf, wg_stage, wu_stage, wd_stage,
                sem, acc_ref, *, tf):
    e, bi = pl.program_id(0), pl.program_id(1)
    n_e, n_b = pl.num_programs(0), pl.num_programs(1)
    rows_in, rows_dn = wg_stage.shape[1], wd_stage.shape[1]
    f_total = wg_buf.shape[2]

    def chunk_copies(expert, c):
        st = c % 2
        return (
            pltpu.make_async_copy(wg_hbm.at[expert, pl.ds(c * rows_in, rows_in), :], wg_stage.at[st], sem.at[0, st]),
            pltpu.make_async_copy(wu_hbm.at[expert, pl.ds(c * rows_in, rows_in), :], wu_stage.at[st], sem.at[1, st]),
            pltpu.make_async_copy(wd_hbm.at[expert, pl.ds(c * rows_dn, rows_dn), :], wd_stage.at[st], sem.at[2, st]),
        )

    def finish_chunk(expert, c):
        for cp in chunk_copies(expert, c):
            cp.wait()
        slot, st = expert % 2, c % 2
        wg_buf[slot, pl.ds(c * rows_in, rows_in), :] = wg_stage[st].astype(BF16)
        wu_buf[slot, pl.ds(c * rows_in, rows_in), :] = wu_stage[st].astype(BF16)
        wd_buf[slot, pl.ds(c * rows_dn, rows_dn), :] = wd_stage[st].astype(BF16)

    @pl.when((e == 0) & (bi == 0))
    def _():
        for cp in chunk_copies(0, 0):
            cp.start()

        def load_first(c, carry):
            @pl.when(c + 1 < n_b)
            def _():
                for cp in chunk_copies(0, c + 1):
                    cp.start()

            finish_chunk(0, c)
            return carry

        lax.fori_loop(0, n_b, load_first, 0)

    pending = jnp.where(bi > 0, e + 1 < n_e, e > 0)

    @pl.when(pending)
    def _():
        finish_chunk(jnp.where(bi > 0, e + 1, e), jnp.where(bi > 0, bi - 1, n_b - 1))

    @pl.when(e + 1 < n_e)
    def _():
        for cp in chunk_copies(e + 1, bi):
            cp.start()

    def swiglu(slot):
        xg = xg_ref[0, 0]
        for fi in range(f_total // tf):
            fs = slice(fi * tf, (fi + 1) * tf)
            gate = jnp.dot(xg, wg_buf[slot, :, fs], preferred_element_type=F32)
            up = jnp.dot(xg, wu_buf[slot, :, fs], preferred_element_type=F32)
            hid = ((gate * jax.nn.sigmoid(gate)) * up).astype(BF16)
            part = jnp.dot(hid, wd_buf[slot, fs, :], preferred_element_type=F32)
            if fi == 0:
                acc_ref[...] = part
            else:
                acc_ref[...] += part
        y_ref[0, 0] = acc_ref[...].astype(y_ref.dtype)

    for slot in range(2):
        pl.when(e % 2 == slot)(functools.partial(swiglu, slot))


def _ffn(xg, wg, wu, wd, tf=512):
    b, n_e, cap, d = xg.shape
    f = wg.shape[2]
    assert d % b == 0 and f % b == 0 and (d // b) % SLOT_ALIGN == 0, "one weight row chunk per batch step"
    tok = pl.BlockSpec((1, 1, cap, d), lambda e, bi: (bi, e, 0, 0))
    hbm = pl.BlockSpec(memory_space=pl.ANY)
    return pl.pallas_call(
        functools.partial(_ffn_kernel, tf=tf),
        grid=(n_e, b),
        in_specs=[tok, hbm, hbm, hbm],
        out_specs=tok,
        out_shape=jax.ShapeDtypeStruct((b, n_e, cap, d), BF16),
        scratch_shapes=[
            pltpu.VMEM((2, d, f), BF16), pltpu.VMEM((2, d, f), BF16), pltpu.VMEM((2, f, d), BF16),
            pltpu.VMEM((2, d // b, f), F32), pltpu.VMEM((2, d // b, f), F32), pltpu.VMEM((2, f // b, d), F32),
            pltpu.SemaphoreType.DMA((3, 2)),
            pltpu.VMEM((cap, d), F32),
        ],
        compiler_params=_params("arbitrary", "arbitrary"),
        name="ffn",
    )(xg, wg, wu, wd)


def _combine_kernel(cnt_ref, pos_ref, gate_ref, y_ref, x1_ref, g_ref, out_ref, acc_ref):
    bi, j = pl.program_id(0), pl.program_id(1)
    tt = x1_ref.shape[1]
    n_e, cap = y_ref.shape[1:3]
    slot0 = lax.broadcasted_iota(I32, (tt, SLOT_WIN), 1)
    windows = [_slot_window(cnt_ref, bi * n_e + e, j, cap) for e in range(n_e)]
    acc = x1_ref[0]
    for e in range(0, n_e, 2):
        scatter = jnp.concatenate(
            [jnp.where(pos_ref[0, :, e + i:e + i + 1] == slot0 + windows[e + i][1],
                       gate_ref[0, :, e + i:e + i + 1], 0.0).astype(BF16) for i in range(2)], axis=1)
        y_pair = jnp.concatenate(
            [y_ref[0, e + i, pl.ds(windows[e + i][1], SLOT_WIN), :] for i in range(2)], axis=0)
        acc = acc + jnp.dot(scatter, y_pair, preferred_element_type=F32)
    acc_ref[...] = acc

    @pl.when(_any_overflow(windows))
    def _():
        wide = lax.broadcasted_iota(I32, (tt, MXU_DEPTH), 1)
        for e, (hi, w0) in enumerate(windows):
            @pl.when(hi > w0 + SLOT_WIN)
            def _(e=e, w0=w0):
                for ws in range(0, cap, MXU_DEPTH):
                    slot = wide + ws
                    extra = jnp.where((pos_ref[0, :, e:e + 1] == slot) & (slot >= w0 + SLOT_WIN),
                                      gate_ref[0, :, e:e + 1], 0.0).astype(BF16)
                    acc_ref[...] += jnp.dot(extra, y_ref[0, e, ws:ws + MXU_DEPTH, :],
                                            preferred_element_type=F32)

    acc = acc_ref[...]
    ms = jnp.mean(acc * acc, axis=-1, keepdims=True)
    out_ref[0] = acc * lax.rsqrt(ms + EPS) * g_ref[...]


def _combine(cnt, pos_t, gate_t, y, x1, gain):
    b, s, d = x1.shape
    n_e, cap = y.shape[1:3]
    tt = MOE_TILE
    return pl.pallas_call(
        _combine_kernel,
        grid_spec=pltpu.PrefetchScalarGridSpec(
            num_scalar_prefetch=1,
            grid=(b, s // tt),
            in_specs=[
                pl.BlockSpec((1, tt, n_e), lambda bi, j, cnt: (bi, j, 0)),
                pl.BlockSpec((1, tt, n_e), lambda bi, j, cnt: (bi, j, 0)),
                pl.BlockSpec((1, n_e, cap, d), lambda bi, j, cnt: (bi, 0, 0, 0)),
                pl.BlockSpec((1, tt, d), lambda bi, j, cnt: (bi, j, 0)),
                pl.BlockSpec((1, d), lambda bi, j, cnt: (0, 0)),
            ],
            out_specs=pl.BlockSpec((1, tt, d), lambda bi, j, cnt: (bi, j, 0)),
            scratch_shapes=[pltpu.VMEM((tt, d), F32)],
        ),
        out_shape=jax.ShapeDtypeStruct((b, s, d), F32),
        compiler_params=_params("parallel", "parallel"),
        name="combine",
    )(cnt, pos_t, gate_t, y, x1, gain)


def _rope_tables(seq):
    half = HEAD_DIM // 2
    pos = jnp.arange(seq, dtype=F32)
    inv = ROPE_BASE ** (-jnp.arange(0, HEAD_DIM, 2, dtype=F32) / HEAD_DIM)
    ang = pos[:, None] * inv[None, :]
    cos, sin = jnp.cos(ang), jnp.sin(ang)
    reps = LANES // HEAD_DIM
    cos_t = jnp.tile(jnp.concatenate([cos, cos], axis=1), (1, reps))
    sin_t = jnp.tile(jnp.concatenate([-sin, sin], axis=1), (1, reps))
    return cos_t, sin_t


def kernel(x, norm1_gain, w_in, ret_log_decay, ret_gn_gain, rel_bias, w_out, norm2_gain, w_router,
           w_gate, w_up, w_down, final_gain):
    b, s, d = x.shape
    depth = w_in.shape[0]
    ret_width = ret_gn_gain.shape[1]
    n_ret_pairs = ret_width // LANES
    att_width = (w_in.shape[2] - 4 * ret_width) // 3
    n_att_pairs = att_width // LANES
    cap = CAPACITY_FACTOR * s // N_EXPERTS
    cos_t, sin_t = _rope_tables(s)

    assert depth == 1, "single-layer block: the final norm is fused into the combine kernel"
    layer = 0
    ret, att = _proj(x.reshape(b * s, d), norm1_gain[layer][None, :], w_in[layer].astype(BF16),
                     cos_t, sin_t, n_ret=4 * ret_width, ret_width=ret_width)
    r = _retention(ret, ret_log_decay[layer], ret_gn_gain[layer][None, :], n_ret_pairs, b)
    a = _dilated(att, rel_bias, n_att_pairs, b)
    x1, h2, aff = _outproj(r, a, x, w_out[layer].astype(BF16), norm2_gain[layer][None, :],
                           w_router[layer])
    assert s % MOE_TILE == 0 and s // MOE_TILE < CNT_STRIDE and cap % MXU_DEPTH == 0 and N_EXPERTS % 2 == 0
    posm, cnt = _select(aff.reshape(b * N_EXPERTS, s), cap)
    posm = posm.reshape(b, N_EXPERTS, s)
    cnt = cnt[:, :CNT_STRIDE].reshape(-1)
    xg = _gather(cnt, posm, h2.reshape(b, s, d), cap)
    y = _ffn(xg, w_gate[layer], w_up[layer], w_down[layer])
    return _combine(cnt, jnp.swapaxes(posm, 1, 2), jnp.swapaxes(aff, 1, 2), y, x1.reshape(b, s, d),
                    final_gain[None, :])
```

```python
import functools
import math

import numpy as np
import jax
import jax.numpy as jnp
from jax import lax
from jax.experimental import pallas as pl
from jax.experimental.pallas import tpu as pltpu

F32 = jnp.float32
BF16 = jnp.bfloat16
I32 = jnp.int32

HEAD_DIM = 64
LANES = 128
RET_CHUNK = 128
RET_GROUP = 4
ROPE_BASE = 10000.0
DILATED_PATTERNS = ((128, 1), (512, 4), (2048, 16))
SIDE = 64
N_BUCKETS = 32
MAX_DISTANCE = 1024
N_EXPERTS = 16
CAPACITY_FACTOR = 2
MOE_TILE = 256
MXU_DEPTH = 256
WIN_GROUP = 4
SLOT_WIN = MXU_DEPTH // WIN_GROUP
SLOT_ALIGN = 16
CNT_STRIDE = 32
EPS = 1e-6
NEG = -1e30
VMEM_LIMIT = 56 * 1024 * 1024

_NT = (((1,), (1,)), ((), ()))
_TN = (((0,), (0,)), ((), ()))


def _params(*sem):
    return pltpu.CompilerParams(dimension_semantics=sem, vmem_limit_bytes=VMEM_LIMIT)


def _proj_kernel(x_ref, g_ref, w_ref, cos_ref, sin_ref, ret_ref, att_ref, *, n_ret, ret_width, chunk):
    x = x_ref[...]
    ms = jnp.mean(x * x, axis=-1, keepdims=True)
    h = (x * lax.rsqrt(ms + EPS) * g_ref[...]).astype(BF16)
    n_cols = w_ref.shape[1]
    cos, sin = cos_ref[...], sin_ref[...]
    first_half = (lax.broadcasted_iota(I32, cos.shape, 1) % HEAD_DIM) < (HEAD_DIM // 2)
    for c0 in range(0, n_cols, chunk):
        o = jnp.dot(h, w_ref[:, c0:c0 + chunk], preferred_element_type=F32)
        for l0 in range(0, chunk, LANES):
            t = o[:, l0:l0 + LANES]
            col = c0 + l0
            if col >= n_ret:
                att_ref[(col - n_ret) // LANES] = t
            elif col >= 2 * ret_width:
                ret_ref[col // LANES] = t.astype(BF16)
            else:
                scale = 1.0 if col < ret_width else HEAD_DIM ** -0.5
                swapped = jnp.where(first_half, pltpu.roll(t, LANES - HEAD_DIM // 2, 1),
                                    pltpu.roll(t, HEAD_DIM // 2, 1))
                ret_ref[col // LANES] = ((t * cos + swapped * sin) * scale).astype(BF16)


def _proj(x2d, gain, w_bf16, cos_t, sin_t, n_ret, ret_width, tm=512):
    t, d = x2d.shape
    n_cols = w_bf16.shape[1]
    n_att = n_cols - n_ret
    seq_tiles = cos_t.shape[0] // tm
    assert ret_width % 512 == 0 and n_ret % 512 == 0 and n_cols % 512 == 0
    return pl.pallas_call(
        functools.partial(_proj_kernel, n_ret=n_ret, ret_width=ret_width, chunk=512),
        grid=(t // tm,),
        in_specs=[
            pl.BlockSpec((tm, d), lambda i: (i, 0)),
            pl.BlockSpec((1, d), lambda i: (0, 0)),
            pl.BlockSpec((d, n_cols), lambda i: (0, 0)),
            pl.BlockSpec((tm, LANES), lambda i: (i % seq_tiles, 0)),
            pl.BlockSpec((tm, LANES), lambda i: (i % seq_tiles, 0)),
        ],
        out_specs=[
            pl.BlockSpec((n_ret // LANES, tm, LANES), lambda i: (0, i, 0)),
            pl.BlockSpec((n_att // LANES, tm, LANES), lambda i: (0, i, 0)),
        ],
        out_shape=[
            jax.ShapeDtypeStruct((n_ret // LANES, t, LANES), BF16),
            jax.ShapeDtypeStruct((n_att // LANES, t, LANES), F32),
        ],
        compiler_params=_params("parallel"),
        name="proj",
    )(x2d, gain, w_bf16, cos_t, sin_t)


def _retention_kernel(decay_ref, q_ref, k_ref, v_ref, g_ref, gn_ref, out_ref,
                      rf_ref, rb_ref, kvf_ref, kvb_ref, p0_ref, p1_ref, *, seq):
    c = RET_CHUNK
    n_chunks = seq // c
    p = pl.program_id(1)
    lane = lax.broadcasted_iota(I32, (c, LANES), 1)
    row = lax.broadcasted_iota(I32, (c, LANES), 0)
    head0 = lane < HEAD_DIM
    rowf = row.astype(F32)

    lgf0, lgf1 = decay_ref[0, 2 * p], decay_ref[0, 2 * p + 1]
    lgb0, lgb1 = decay_ref[1, 2 * p], decay_ref[1, 2 * p + 1]
    lgf_lane = jnp.where(head0, lgf0, lgf1)
    lgb_lane = jnp.where(head0, lgb0, lgb1)
    lgf_row = jnp.where(row < HEAD_DIM, lgf0, lgf1)
    lgb_row = jnp.where(row < HEAD_DIM, lgb0, lgb1)
    same_head = (row < HEAD_DIM) == head0

    zeta_f = jnp.exp((c - 1 - rowf) * lgf_lane)
    zeta_b = jnp.exp(rowf * lgb_lane)
    xi_f = jnp.exp((rowf + 1.0) * lgf_lane)
    xi_b = jnp.exp((c - rowf) * lgb_lane)
    gch_f = jnp.where(same_head, jnp.exp(c * lgf_row), 0.0)
    gch_b = jnp.where(same_head, jnp.exp(c * lgb_row), 0.0)

    diff = (row - lane).astype(F32)

    def dmat(lf, lb):
        return jnp.exp(jnp.where(diff >= 0, diff * lf, -diff * lb))

    d_stack = jnp.concatenate([dmat(lgf0, lgb0), dmat(lgf1, lgb1)], axis=0)

    def kv_step(n, carry):
        sl = pl.ds(pl.multiple_of(n * c, c), c)
        k = k_ref[0, sl, :].astype(F32)
        v = v_ref[0, sl, :]
        kvf = lax.dot_general((k * zeta_f).astype(BF16), v, _TN, preferred_element_type=F32)
        kvb = lax.dot_general((k * zeta_b).astype(BF16), v, _TN, preferred_element_type=F32)
        kvf_ref[n] = jnp.where(same_head, kvf, 0.0)
        kvb_ref[n] = jnp.where(same_head, kvb, 0.0)
        return carry

    lax.fori_loop(0, n_chunks, kv_step, 0, unroll=8)

    def scan_step(i, states):
        sf, sb = states
        nb = n_chunks - 1 - i
        rf_ref[i] = sf.astype(BF16)
        rb_ref[nb] = sb.astype(BF16)
        return gch_f * sf + kvf_ref[i], gch_b * sb + kvb_ref[nb]

    zero_state = jnp.zeros((LANES, LANES), F32)
    lax.fori_loop(0, n_chunks, scan_step, (zero_state, zero_state), unroll=4)

    gn_gain = gn_ref[...]

    def score_stage(g, p_ref):
        for jc in range(RET_GROUP):
            sl = pl.ds(pl.multiple_of((g * RET_GROUP + jc) * c, c), c)
            q = q_ref[0, sl, :]
            zero = jnp.zeros_like(q)
            q_stack = jnp.concatenate([jnp.where(head0, q, zero), jnp.where(head0, zero, q)], axis=0)
            scores = lax.dot_general(q_stack, k_ref[0, sl, :], _NT, preferred_element_type=F32) * d_stack
            sb = scores.astype(BF16)
            p_ref[jc] = jnp.concatenate([sb[:c], sb[c:]], axis=1)

    def half_mean(t):
        s0 = jnp.sum(jnp.where(head0, t, 0.0), axis=-1, keepdims=True)
        s1 = jnp.sum(jnp.where(head0, 0.0, t), axis=-1, keepdims=True)
        return jnp.where(head0, s0, s1) * (1.0 / HEAD_DIM)

    def value_stage(g, p_ref):
        for jc in range(RET_GROUP):
            n = g * RET_GROUP + jc
            sl = pl.ds(pl.multiple_of(n * c, c), c)
            q = q_ref[0, sl, :]
            v = v_ref[0, sl, :]
            zero = jnp.zeros_like(v)
            v_stack = jnp.concatenate([jnp.where(head0, v, zero), jnp.where(head0, zero, v)], axis=0)
            inner = jnp.dot(p_ref[jc], v_stack, preferred_element_type=F32)
            cross_f = jnp.dot(q, rf_ref[n], preferred_element_type=F32) * xi_f
            cross_b = jnp.dot(q, rb_ref[n], preferred_element_type=F32) * xi_b
            r = inner + cross_f + cross_b
            dlt = r - half_mean(r)
            var = half_mean(dlt * dlt)
            gate = g_ref[0, sl, :].astype(F32)
            y = dlt * lax.rsqrt(var + EPS) * gn_gain * (gate * jax.nn.sigmoid(gate))
            out_ref[0, sl, :] = y.astype(out_ref.dtype)

    n_groups = n_chunks // RET_GROUP
    assert n_groups % 2 == 0 and n_groups >= 2
    score_stage(0, p0_ref)

    def group_pair(i, carry):
        g = 2 * i
        score_stage(g + 1, p1_ref)
        value_stage(g, p0_ref)
        score_stage(g + 2, p0_ref)
        value_stage(g + 1, p1_ref)
        return carry

    lax.fori_loop(0, n_groups // 2 - 1, group_pair, 0)
    score_stage(n_groups - 1, p1_ref)
    value_stage(n_groups - 2, p0_ref)
    value_stage(n_groups - 1, p1_ref)


def _retention(ret, decay, gn_gain, n_pairs, b):
    s = ret.shape[1] // b
    blk = lambda off: pl.BlockSpec((1, s, LANES), lambda bi, pi: (off + pi, bi, 0))
    return pl.pallas_call(
        functools.partial(_retention_kernel, seq=s),
        grid=(b, n_pairs),
        in_specs=[
            pl.BlockSpec(memory_space=pltpu.SMEM),
            blk(0), blk(n_pairs), blk(2 * n_pairs), blk(3 * n_pairs),
            pl.BlockSpec((1, LANES), lambda bi, pi: (0, pi)),
        ],
        out_specs=pl.BlockSpec((1, s, LANES), lambda bi, pi: (pi, bi, 0)),
        out_shape=jax.ShapeDtypeStruct((n_pairs, b * s, LANES), BF16),
        scratch_shapes=[
            pltpu.VMEM((s // RET_CHUNK, LANES, LANES), BF16),
            pltpu.VMEM((s // RET_CHUNK, LANES, LANES), BF16),
            pltpu.VMEM((s // RET_CHUNK, LANES, LANES), F32),
            pltpu.VMEM((s // RET_CHUNK, LANES, LANES), F32),
            pltpu.VMEM((RET_GROUP, RET_CHUNK, 2 * RET_CHUNK), BF16),
            pltpu.VMEM((RET_GROUP, RET_CHUNK, 2 * RET_CHUNK), BF16),
        ],
        compiler_params=_params("parallel", "parallel"),
        name="retention",
    )(decay, ret, ret, ret, ret, gn_gain)


QBLK = 2 * SIDE
KWIN = 4 * SIDE
BASE_W = 512
ATT_GROUP = 4


def _t5_bucket_np(rel):
    half = N_BUCKETS // 2
    max_exact = half // 2
    bucket = np.where(rel > 0, half, 0)
    n = np.abs(rel)
    nf = np.maximum(n, 1).astype(np.float32)
    large = max_exact + (np.log(nf / np.float32(max_exact)) / np.float32(math.log(MAX_DISTANCE / max_exact))
                         * np.float32(half - max_exact)).astype(np.int32)
    large = np.minimum(large, half - 1)
    return (bucket + np.where(n < max_exact, n, large)).astype(np.int32)


def _bucket_rows():
    k = np.arange(BASE_W)
    off = k - SIDE
    rows = []
    for _, dilation in DILATED_PATTERNS:
        rows.append(np.where(k <= 2 * SIDE, _t5_bucket_np(off * dilation), -1))
    return np.stack(rows).astype(np.int32)


def _dilated_kernel(bias_tab_ref, bucket_ref, q_ref, k_ref, v_ref, out_ref,
                    bias_ref, qp_ref, kp_ref, vp_ref, stage_ref, o_ref, lse_ref,
                    e0_ref, e1_ref, m0_ref, m1_ref, *, seq):
    assert len(DILATED_PATTERNS) == 3 and DILATED_PATTERNS[0][1] == 1
    p = pl.program_id(1)
    lane = lax.broadcasted_iota(I32, (QBLK, LANES), 1)
    head0 = lane < HEAD_DIM
    n_blocks = seq // QBLK

    for pi in range(len(DILATED_PATTERNS)):
        bucket = jnp.broadcast_to(bucket_ref[pi:pi + 1, :], (QBLK, BASE_W))
        for hh in range(2):
            base = jnp.full((QBLK, BASE_W), NEG, F32)
            for bk in range(N_BUCKETS):
                base = jnp.where(bucket == bk, bias_tab_ref[bk, 2 * p + hh], base)
            for var, shift in enumerate((BASE_W - SIDE, 0, SIDE)):
                tile = pltpu.roll(base, shift, 1, stride=1, stride_axis=0)
                bias_ref[pi * 3 + var, hh * QBLK:(hh + 1) * QBLK, :] = tile[:, :KWIN]

    head0_k = lax.broadcasted_iota(I32, (KWIN, LANES), 1) < HEAD_DIM
    order = list(range(1, len(DILATED_PATTERNS))) + [0]
    prev_d = 1
    for slot, pi in enumerate(order):
        d = DILATED_PATTERNS[pi][1]
        final = slot == len(order) - 1
        seg_len = seq // d
        blocks_per_seg = seg_len // QBLK

        if d == 1:
            prev_d = 1
        step, prev_len = d // prev_d, seq // prev_d
        assert step * prev_d == d
        keep_f32 = d > 1 and not final and slot + 1 < len(order) - 1
        for i, (src_ref, dst_ref, scale) in enumerate(
                ((q_ref, qp_ref, HEAD_DIM ** -0.5), (k_ref, kp_ref, None), (v_ref, vp_ref, None))):
            for r_prev in range(prev_d):
                for r_step in range(step):
                    src = pl.ds(r_prev * prev_len + r_step, seg_len, stride=step) if step > 1 else pl.ds(0, seq)
                    dst = pl.ds((r_prev + prev_d * r_step) * seg_len, seg_len)
                    val = src_ref[0, src, :] if prev_d == 1 else stage_ref[i, src, :]
                    if keep_f32:
                        stage_ref[i, dst, :] = val
                    dst_ref[dst, :] = (val if scale is None else val * scale).astype(BF16)
        prev_d = d

        def placement(t, seg_len=seg_len, blocks_per_seg=blocks_per_seg):
            seg = t // blocks_per_seg
            u = t % blocks_per_seg
            q0 = pl.multiple_of(t * QBLK, QBLK)
            seg0 = seg * seg_len
            k0 = pl.multiple_of(jnp.clip(q0 - SIDE, seg0, seg0 + seg_len - KWIN), SIDE)
            var = jnp.where(u == 0, 0, jnp.where(u == blocks_per_seg - 1, 2, 1))
            return seg, u, q0, k0, var

        def logits_stage(g, e_ref, m_ref, pi=pi):
            for jb in range(ATT_GROUP):
                _, _, q0, k0, var = placement(g * ATT_GROUP + jb)
                q = qp_ref[pl.ds(q0, QBLK), :]
                kw = kp_ref[pl.ds(k0, KWIN), :]
                zero = jnp.zeros_like(q)
                q_stack = jnp.concatenate([jnp.where(head0, q, zero), jnp.where(head0, zero, q)], axis=0)
                s = lax.dot_general(q_stack, kw, _NT, preferred_element_type=F32) + bias_ref[pi * 3 + var]
                m = jnp.max(s, axis=-1, keepdims=True)
                e_ref[jb] = jnp.exp(s - m).astype(BF16)
                m_ref[jb] = jnp.broadcast_to(m, (2 * QBLK, LANES))

        def value_stage(g, e_ref, m_ref, slot=slot, d=d, final=final):
            for jb in range(ATT_GROUP):
                seg, u, q0, k0, _ = placement(g * ATT_GROUP + jb)
                vw = vp_ref[pl.ds(k0, KWIN), :]
                one = jnp.ones_like(vw)
                pv0 = jnp.dot(e_ref[jb, :QBLK, :], jnp.where(head0_k, vw, one), preferred_element_type=F32)
                pv1 = jnp.dot(e_ref[jb, QBLK:, :], jnp.where(head0_k, one, vw), preferred_element_type=F32)
                num = jnp.where(head0, pv0, pv1)
                den = pltpu.roll(jnp.where(head0, pv1, pv0), HEAD_DIM, 1)
                o = num * (1.0 / den)
                lse = jnp.where(head0, m_ref[jb, :QBLK, :], m_ref[jb, QBLK:, :]) + jnp.log(den)
                dst = pl.ds(seg + d * (u * QBLK), QBLK, stride=d) if d > 1 else pl.ds(q0, QBLK)
                if not final:
                    o_ref[slot, dst, :] = o
                    lse_ref[slot, dst, :] = lse
                else:
                    lses = [lse_ref[i, dst, :] for i in range(slot)] + [lse]
                    outs = [o_ref[i, dst, :] for i in range(slot)] + [o]
                    mx = functools.reduce(jnp.maximum, lses)
                    ws = [jnp.exp(l - mx) for l in lses]
                    inv = 1.0 / functools.reduce(jnp.add, ws)
                    acc = functools.reduce(jnp.add, [(w * inv) * t for w, t in zip(ws, outs)])
                    out_ref[0, dst, :] = acc.astype(out_ref.dtype)

        n_groups = n_blocks // ATT_GROUP
        assert n_groups % 2 == 0 and n_groups >= 2
        logits_stage(0, e0_ref, m0_ref)

        def group_pair(i, carry):
            g = 2 * i
            logits_stage(g + 1, e1_ref, m1_ref)
            value_stage(g, e0_ref, m0_ref)
            logits_stage(g + 2, e0_ref, m0_ref)
            value_stage(g + 1, e1_ref, m1_ref)
            return carry

        lax.fori_loop(0, n_groups // 2 - 1, group_pair, 0)
        logits_stage(n_groups - 1, e1_ref, m1_ref)
        value_stage(n_groups - 2, e0_ref, m0_ref)
        value_stage(n_groups - 1, e1_ref, m1_ref)


def _dilated(att, rel_bias, n_pairs, b):
    s = att.shape[1] // b
    n_pat = len(DILATED_PATTERNS)
    blk = lambda off: pl.BlockSpec((1, s, LANES), lambda bi, pi: (off + pi, bi, 0))
    bucket_rows = jnp.asarray(_bucket_rows())
    return pl.pallas_call(
        functools.partial(_dilated_kernel, seq=s),
        grid=(b, n_pairs),
        in_specs=[
            pl.BlockSpec(memory_space=pltpu.SMEM),
            pl.BlockSpec((n_pat, BASE_W), lambda bi, pi: (0, 0)),
            blk(0), blk(n_pairs), blk(2 * n_pairs),
        ],
        out_specs=pl.BlockSpec((1, s, LANES), lambda bi, pi: (pi, bi, 0)),
        out_shape=jax.ShapeDtypeStruct((n_pairs, b * s, LANES), BF16),
        scratch_shapes=[
            pltpu.VMEM((n_pat * 3, 2 * QBLK, KWIN), F32),
            pltpu.VMEM((s, LANES), BF16),
            pltpu.VMEM((s, LANES), BF16),
            pltpu.VMEM((s, LANES), BF16),
            pltpu.VMEM((3, s, LANES), F32),
            pltpu.VMEM((n_pat - 1, s, LANES), F32),
            pltpu.VMEM((n_pat - 1, s, LANES), F32),
            pltpu.VMEM((ATT_GROUP, 2 * QBLK, KWIN), BF16),
            pltpu.VMEM((ATT_GROUP, 2 * QBLK, KWIN), BF16),
            pltpu.VMEM((ATT_GROUP, 2 * QBLK, LANES), F32),
            pltpu.VMEM((ATT_GROUP, 2 * QBLK, LANES), F32),
        ],
        compiler_params=_params("parallel", "parallel"),
        name="dilated",
    )(rel_bias, bucket_rows, att, att, att)


def _outproj_kernel(r_ref, a_ref, x_ref, w_ref, g_ref, wr_ref, x1_ref, h2_ref, aff_ref):
    mixed = jnp.concatenate([r_ref[i] for i in range(r_ref.shape[0])]
                            + [a_ref[i] for i in range(a_ref.shape[0])], axis=1)
    x1 = x_ref[...] + jnp.dot(mixed, w_ref[...], preferred_element_type=F32)
    x1_ref[...] = x1
    ms = jnp.mean(x1 * x1, axis=-1, keepdims=True)
    h2 = x1 * lax.rsqrt(ms + EPS) * g_ref[...]
    h_hi = h2.astype(BF16)
    h2_ref[...] = h_hi
    n_e = aff_ref.shape[1]
    tm = h2.shape[0]
    h_lo = (h2 - h_hi.astype(F32)).astype(BF16)
    wr = wr_ref[...]
    w_hi = wr.astype(BF16)
    w_lo = (wr - w_hi.astype(F32)).astype(BF16)
    w_parts = jnp.where(lax.broadcasted_iota(I32, wr.shape, 1) < n_e, w_hi, w_lo)
    prod = jnp.dot(jnp.concatenate([h_hi, h_lo], axis=0), w_parts, preferred_element_type=F32)
    logits = prod[:tm] + (pltpu.roll(prod[:tm], LANES - n_e, 1) + prod[tm:])
    expert_lane = lax.broadcasted_iota(I32, logits.shape, 1) < n_e
    logits = jnp.where(expert_lane, logits, NEG)
    m = jnp.max(logits, axis=1, keepdims=True)
    e = jnp.exp(logits - m)
    aff = e / jnp.sum(e, axis=1, keepdims=True)
    aff_ref[0] = aff.T[:n_e, :]


def _outproj(r, a, x, w_bf16, gain, w_router, tm=512):
    b, s, d = x.shape
    n_r, n_a = r.shape[0], a.shape[0]
    n_e = w_router.shape[1]
    assert 2 * n_e <= LANES
    w_router = jnp.pad(jnp.concatenate([w_router, w_router], axis=1), ((0, 0), (0, LANES - 2 * n_e)))
    spt = s // tm
    tok = lambda width: pl.BlockSpec((tm, width), lambda i: (i, 0))
    slab = lambda blocks: pl.BlockSpec((blocks, tm, LANES), lambda i: (0, i, 0))
    const = lambda shape: pl.BlockSpec(shape, lambda i: (0, 0))
    return pl.pallas_call(
        _outproj_kernel,
        grid=(b * spt,),
        in_specs=[slab(n_r), slab(n_a), tok(d), const(((n_r + n_a) * LANES, d)), const((1, d)), const((d, LANES))],
        out_specs=[tok(d), tok(d), pl.BlockSpec((1, n_e, tm), lambda i: (i // spt, 0, i % spt))],
        out_shape=[
            jax.ShapeDtypeStruct((b * s, d), F32),
            jax.ShapeDtypeStruct((b * s, d), BF16),
            jax.ShapeDtypeStruct((b, n_e, s), F32),
        ],
        compiler_params=_params("parallel"),
        name="outproj",
    )(r, a, x.reshape(b * s, d), w_bf16, gain, w_router)


def _select_kernel(aff_ref, posm_ref, cnt_ref, *, cap):
    rows, s = aff_ref.shape
    bits = pltpu.bitcast(aff_ref[...], I32)

    def search(i, thr):
        cand = thr | jnp.left_shift(jnp.int32(1), 30 - i)
        cnt = jnp.sum((bits >= cand).astype(I32), axis=1, keepdims=True)
        return jnp.where(cnt >= cap, cand, thr)

    thr = lax.fori_loop(0, 31, search, jnp.zeros((rows, 1), I32))
    gt = bits > thr
    eq = bits == thr
    need = cap - jnp.sum(gt.astype(I32), axis=1, keepdims=True)

    w = LANES
    tri = (lax.broadcasted_iota(I32, (w, w), 0) <= lax.broadcasted_iota(I32, (w, w), 1)).astype(BF16)

    def excl_prefix(flags):
        carry = jnp.zeros((rows, 1), F32)
        out = []
        for c0 in range(0, s, w):
            f = flags[:, c0:c0 + w].astype(BF16)
            inc = jnp.dot(f, tri, preferred_element_type=F32)
            out.append(inc - f.astype(F32) + carry)
            carry = carry + inc[:, w - 1:w]
        return jnp.concatenate(out, axis=1).astype(I32)

    sel = gt | (eq & (excl_prefix(eq) < need))
    posm_ref[...] = jnp.where(sel, excl_prefix(sel), -1)

    tok = lax.broadcasted_iota(I32, (w, w), 0)
    edge = lax.broadcasted_iota(I32, (w, w), 1) * MOE_TILE
    cnt = jnp.zeros((rows, w), F32)
    for c0 in range(0, s, w):
        before = ((tok + c0) < edge).astype(BF16)
        cnt = cnt + jnp.dot(sel[:, c0:c0 + w].astype(BF16), before, preferred_element_type=F32)
    cnt_ref[...] = cnt.astype(I32)


def _select(aff_rows, cap):
    rows, s = aff_rows.shape
    return pl.pallas_call(
        functools.partial(_select_kernel, cap=cap),
        out_shape=[jax.ShapeDtypeStruct((rows, s), I32), jax.ShapeDtypeStruct((rows, LANES), I32)],
        compiler_params=pltpu.CompilerParams(vmem_limit_bytes=VMEM_LIMIT),
        name="select",
    )(aff_rows)


def _slot_window(cnt_ref, row, j, cap):
    lo, hi = cnt_ref[row * CNT_STRIDE + j], cnt_ref[row * CNT_STRIDE + j + 1]
    w0 = pl.multiple_of(jnp.minimum((lo // SLOT_ALIGN) * SLOT_ALIGN, cap - SLOT_WIN), SLOT_ALIGN)
    return hi, w0


def _any_overflow(windows):
    return functools.reduce(jnp.logical_or, [hi > w0 + SLOT_WIN for hi, w0 in windows])


def _gather_kernel(cnt_ref, posm_ref, h2_ref, xg_ref):
    bi, j = pl.program_id(0), pl.program_id(1)
    n_e, cap = xg_ref.shape[1:3]
    tt = h2_ref.shape[1]

    @pl.when(j == 0)
    def _():
        xg_ref[...] = jnp.zeros_like(xg_ref)

    slot0 = lax.broadcasted_iota(I32, (SLOT_WIN, tt), 0)
    windows = [_slot_window(cnt_ref, bi * n_e + e, j, cap) for e in range(n_e)]
    for e in range(0, n_e, WIN_GROUP):
        onehot = jnp.concatenate(
            [(posm_ref[0, e + i:e + i + 1, :] == slot0 + windows[e + i][1]).astype(BF16)
             for i in range(WIN_GROUP)], axis=0)
        rows = jnp.dot(onehot, h2_ref[0], preferred_element_type=F32).astype(BF16)
        for i in range(WIN_GROUP):
            xg_ref[0, e + i, pl.ds(windows[e + i][1], SLOT_WIN), :] += rows[i * SLOT_WIN:(i + 1) * SLOT_WIN]

    @pl.when(_any_overflow(windows))
    def _():
        wide = lax.broadcasted_iota(I32, (MXU_DEPTH, tt), 0)
        for e, (hi, w0) in enumerate(windows):
            @pl.when(hi > w0 + SLOT_WIN)
            def _(e=e, w0=w0):
                for ws in range(0, cap, MXU_DEPTH):
                    slot = wide + ws
                    onehot = ((posm_ref[0, e:e + 1, :] == slot) & (slot >= w0 + SLOT_WIN)).astype(BF16)
                    extra = jnp.dot(onehot, h2_ref[0], preferred_element_type=F32)
                    xg_ref[0, e, ws:ws + MXU_DEPTH, :] += extra.astype(BF16)


def _gather(cnt, posm, h2, cap):
    b, n_e, s = posm.shape
    d = h2.shape[2]
    return pl.pallas_call(
        _gather_kernel,
        grid_spec=pltpu.PrefetchScalarGridSpec(
            num_scalar_prefetch=1,
            grid=(b, s // MOE_TILE),
            in_specs=[
                pl.BlockSpec((1, n_e, MOE_TILE), lambda bi, j, cnt: (bi, 0, j)),
                pl.BlockSpec((1, MOE_TILE, d), lambda bi, j, cnt: (bi, j, 0)),
            ],
            out_specs=pl.BlockSpec((1, n_e, cap, d), lambda bi, j, cnt: (bi, 0, 0, 0)),
        ),
        out_shape=jax.ShapeDtypeStruct((b, n_e, cap, d), BF16),
        compiler_params=_params("parallel", "arbitrary"),
        name="gather",
    )(cnt, posm, h2)


def _ffn_kernel(xg_ref, wg_hbm, wu_hbm, wd_hbm, y_ref, wg_buf, wu_buf, wd_buf, wg_stage, wu_stage, wd_stage,
                sem, acc_ref, *, tf):
    e, bi = pl.program_id(0), pl.program_id(1)
    n_e, n_b = pl.num_programs(0), pl.num_programs(1)
    rows_in, rows_dn = wg_stage.shape[1], wd_stage.shape[1]
    f_total = wg_buf.shape[2]

    def chunk_copies(expert, c):
        st = c % 2
        return (
            pltpu.make_async_copy(wg_hbm.at[expert, pl.ds(c * rows_in, rows_in), :], wg_stage.at[st], sem.at[0, st]),
            pltpu.make_async_copy(wu_hbm.at[expert, pl.ds(c * rows_in, rows_in), :], wu_stage.at[st], sem.at[1, st]),
            pltpu.make_async_copy(wd_hbm.at[expert, pl.ds(c * rows_dn, rows_dn), :], wd_stage.at[st], sem.at[2, st]),
        )

    def finish_chunk(expert, c):
        for cp in chunk_copies(expert, c):
            cp.wait()
        slot, st = expert % 2, c % 2
        wg_buf[slot, pl.ds(c * rows_in, rows_in), :] = wg_stage[st].astype(BF16)
        wu_buf[slot, pl.ds(c * rows_in, rows_in), :] = wu_stage[st].astype(BF16)
        wd_buf[slot, pl.ds(c * rows_dn, rows_dn), :] = wd_stage[st].astype(BF16)

    @pl.when((e == 0) & (bi == 0))
    def _():
        for cp in chunk_copies(0, 0):
            cp.start()

        def load_first(c, carry):
            @pl.when(c + 1 < n_b)
            def _():
                for cp in chunk_copies(0, c + 1):
                    cp.start()

            finish_chunk(0, c)
            return carry

        lax.fori_loop(0, n_b, load_first, 0)

    pending = jnp.where(bi > 0, e + 1 < n_e, e > 0)

    @pl.when(pending)
    def _():
        finish_chunk(jnp.where(bi > 0, e + 1, e), jnp.where(bi > 0, bi - 1, n_b - 1))

    @pl.when(e + 1 < n_e)
    def _():
        for cp in chunk_copies(e + 1, bi):
            cp.start()

    def swiglu(slot):
        xg = xg_ref[0, 0]
        for fi in range(f_total // tf):
            fs = slice(fi * tf, (fi + 1) * tf)
            gate = jnp.dot(xg, wg_buf[slot, :, fs], preferred_element_type=F32)
            up = jnp.dot(xg, wu_buf[slot, :, fs], preferred_element_type=F32)
            hid = ((gate * jax.nn.sigmoid(gate)) * up).astype(BF16)
            part = jnp.dot(hid, wd_buf[slot, fs, :], preferred_element_type=F32)
            if fi == 0:
                acc_ref[...] = part
            else:
                acc_ref[...] += part
        y_ref[0, 0] = acc_ref[...].astype(y_ref.dtype)

    for slot in range(2):
        pl.when(e % 2 == slot)(functools.partial(swiglu, slot))


def _ffn(xg, wg, wu, wd, tf=512):
    b, n_e, cap, d = xg.shape
    f = wg.shape[2]
    assert d % b == 0 and f % b == 0 and (d // b) % SLOT_ALIGN == 0, "one weight row chunk per batch step"
    tok = pl.BlockSpec((1, 1, cap, d), lambda e, bi: (bi, e, 0, 0))
    hbm = pl.BlockSpec(memory_space=pl.ANY)
    return pl.pallas_call(
        functools.partial(_ffn_kernel, tf=tf),
        grid=(n_e, b),
        in_specs=[tok, hbm, hbm, hbm],
        out_specs=tok,
        out_shape=jax.ShapeDtypeStruct((b, n_e, cap, d), BF16),
        scratch_shapes=[
            pltpu.VMEM((2, d, f), BF16), pltpu.VMEM((2, d, f), BF16), pltpu.VMEM((2, f, d), BF16),
            pltpu.VMEM((2, d // b, f), F32), pltpu.VMEM((2, d // b, f), F32), pltpu.VMEM((2, f // b, d), F32),
            pltpu.SemaphoreType.DMA((3, 2)),
            pltpu.VMEM((cap, d), F32),
        ],
        compiler_params=_params("arbitrary", "arbitrary"),
        name="ffn",
    )(xg, wg, wu, wd)


def _combine_kernel(cnt_ref, pos_ref, gate_ref, y_ref, x1_ref, g_ref, out_ref, acc_ref):
    bi, j = pl.program_id(0), pl.program_id(1)
    tt = x1_ref.shape[1]
    n_e, cap = y_ref.shape[1:3]
    slot0 = lax.broadcasted_iota(I32, (SLOT_WIN, tt), 0)
    windows = [_slot_window(cnt_ref, bi * n_e + e, j, cap) for e in range(n_e)]
    acc = x1_ref[0]
    for e in range(0, n_e, WIN_GROUP):
        scatter_t = jnp.concatenate(
            [jnp.where(pos_ref[0, e + i:e + i + 1, :] == slot0 + windows[e + i][1],
                       gate_ref[0, e + i:e + i + 1, :], 0.0).astype(BF16) for i in range(WIN_GROUP)], axis=0)
        y_group = jnp.concatenate(
            [y_ref[0, e + i, pl.ds(windows[e + i][1], SLOT_WIN), :] for i in range(WIN_GROUP)], axis=0)
        acc = acc + lax.dot_general(scatter_t, y_group, _TN, preferred_element_type=F32)
    acc_ref[...] = acc

    @pl.when(_any_overflow(windows))
    def _():
        wide = lax.broadcasted_iota(I32, (MXU_DEPTH, tt), 0)
        for e, (hi, w0) in enumerate(windows):
            @pl.when(hi > w0 + SLOT_WIN)
            def _(e=e, w0=w0):
                for ws in range(0, cap, MXU_DEPTH):
                    slot = wide + ws
                    extra = jnp.where((pos_ref[0, e:e + 1, :] == slot) & (slot >= w0 + SLOT_WIN),
                                      gate_ref[0, e:e + 1, :], 0.0).astype(BF16)
                    acc_ref[...] += lax.dot_general(extra, y_ref[0, e, ws:ws + MXU_DEPTH, :], _TN,
                                                    preferred_element_type=F32)

    acc = acc_ref[...]
    ms = jnp.mean(acc * acc, axis=-1, keepdims=True)
    out_ref[0] = acc * lax.rsqrt(ms + EPS) * g_ref[...]


def _combine(cnt, posm, aff, y, x1, gain):
    b, s, d = x1.shape
    n_e, cap = y.shape[1:3]
    tt = MOE_TILE
    return pl.pallas_call(
        _combine_kernel,
        grid_spec=pltpu.PrefetchScalarGridSpec(
            num_scalar_prefetch=1,
            grid=(b, s // tt),
            in_specs=[
                pl.BlockSpec((1, n_e, tt), lambda bi, j, cnt: (bi, 0, j)),
                pl.BlockSpec((1, n_e, tt), lambda bi, j, cnt: (bi, 0, j)),
                pl.BlockSpec((1, n_e, cap, d), lambda bi, j, cnt: (bi, 0, 0, 0)),
                pl.BlockSpec((1, tt, d), lambda bi, j, cnt: (bi, j, 0)),
                pl.BlockSpec((1, d), lambda bi, j, cnt: (0, 0)),
            ],
            out_specs=pl.BlockSpec((1, tt, d), lambda bi, j, cnt: (bi, j, 0)),
            scratch_shapes=[pltpu.VMEM((tt, d), F32)],
        ),
        out_shape=jax.ShapeDtypeStruct((b, s, d), F32),
        compiler_params=_params("parallel", "parallel"),
        name="combine",
    )(cnt, posm, aff, y, x1, gain)


def _rope_tables(seq):
    half = HEAD_DIM // 2
    pos = jnp.arange(seq, dtype=F32)
    inv = ROPE_BASE ** (-jnp.arange(0, HEAD_DIM, 2, dtype=F32) / HEAD_DIM)
    ang = pos[:, None] * inv[None, :]
    cos, sin = jnp.cos(ang), jnp.sin(ang)
    reps = LANES // HEAD_DIM
    cos_t = jnp.tile(jnp.concatenate([cos, cos], axis=1), (1, reps))
    sin_t = jnp.tile(jnp.concatenate([-sin, sin], axis=1), (1, reps))
    return cos_t, sin_t


def kernel(x, norm1_gain, w_in, ret_log_decay, ret_gn_gain, rel_bias, w_out, norm2_gain, w_router,
           w_gate, w_up, w_down, final_gain):
    b, s, d = x.shape
    depth = w_in.shape[0]
    ret_width = ret_gn_gain.shape[1]
    n_ret_pairs = ret_width // LANES
    att_width = (w_in.shape[2] - 4 * ret_width) // 3
    n_att_pairs = att_width // LANES
    cap = CAPACITY_FACTOR * s // N_EXPERTS
    cos_t, sin_t = _rope_tables(s)

    assert depth == 1, "single-layer block: the final norm is fused into the combine kernel"
    layer = 0
    ret, att = _proj(x.reshape(b * s, d), norm1_gain[layer][None, :], w_in[layer].astype(BF16),
                     cos_t, sin_t, n_ret=4 * ret_width, ret_width=ret_width)
    r = _retention(ret, ret_log_decay[layer], ret_gn_gain[layer][None, :], n_ret_pairs, b)
    a = _dilated(att, rel_bias, n_att_pairs, b)
    x1, h2, aff = _outproj(r, a, x, w_out[layer].astype(BF16), norm2_gain[layer][None, :],
                           w_router[layer])
    assert s % MOE_TILE == 0 and s // MOE_TILE < CNT_STRIDE and cap % MXU_DEPTH == 0 and N_EXPERTS % WIN_GROUP == 0
    posm, cnt = _select(aff.reshape(b * N_EXPERTS, s), cap)
    posm = posm.reshape(b, N_EXPERTS, s)
    cnt = cnt[:, :CNT_STRIDE].reshape(-1)
    xg = _gather(cnt, posm, h2.reshape(b, s, d), cap)
    y = _ffn(xg, w_gate[layer], w_up[layer], w_down[layer])
    return _combine(cnt, posm, aff, y, x1.reshape(b, s, d),
                    final_gain[None, :])
```

```python
import functools
import math

import numpy as np
import jax
import jax.numpy as jnp
from jax import lax
from jax.experimental import pallas as pl
from jax.experimental.pallas import tpu as pltpu

F32 = jnp.float32
BF16 = jnp.bfloat16
I32 = jnp.int32

HEAD_DIM = 64
LANES = 128
RET_CHUNK = 128
RET_GROUP = 4
ROPE_BASE = 10000.0
DILATED_PATTERNS = ((128, 1), (512, 4), (2048, 16))
SIDE = 64
N_BUCKETS = 32
MAX_DISTANCE = 1024
N_EXPERTS = 16
CAPACITY_FACTOR = 2
MOE_TILE = 256
GATHER_TILES = 4
COMBINE_TILES = 2
MXU_DEPTH = 256
WIN_GROUP = 4
SLOT_WIN = MXU_DEPTH // WIN_GROUP
SLOT_ALIGN = 16
CNT_STRIDE = 32
EPS = 1e-6
NEG = -1e30
VMEM_LIMIT = 56 * 1024 * 1024

_NT = (((1,), (1,)), ((), ()))
_TN = (((0,), (0,)), ((), ()))


def _params(*sem):
    return pltpu.CompilerParams(dimension_semantics=sem, vmem_limit_bytes=VMEM_LIMIT)


def _proj_kernel(x_ref, g_ref, w_ref, cos_ref, sin_ref, ret_ref, att_ref, *, n_ret, ret_width, chunk):
    x = x_ref[...]
    ms = jnp.mean(x * x, axis=-1, keepdims=True)
    h = (x * lax.rsqrt(ms + EPS) * g_ref[...]).astype(BF16)
    n_cols = w_ref.shape[1]
    cos, sin = cos_ref[...], sin_ref[...]
    first_half = (lax.broadcasted_iota(I32, cos.shape, 1) % HEAD_DIM) < (HEAD_DIM // 2)
    for c0 in range(0, n_cols, chunk):
        o = jnp.dot(h, w_ref[:, c0:c0 + chunk], preferred_element_type=F32)
        for l0 in range(0, chunk, LANES):
            t = o[:, l0:l0 + LANES]
            col = c0 + l0
            if col >= n_ret:
                att_ref[(col - n_ret) // LANES] = t
            elif col >= 2 * ret_width:
                ret_ref[col // LANES] = t.astype(BF16)
            else:
                scale = 1.0 if col < ret_width else HEAD_DIM ** -0.5
                swapped = jnp.where(first_half, pltpu.roll(t, LANES - HEAD_DIM // 2, 1),
                                    pltpu.roll(t, HEAD_DIM // 2, 1))
                ret_ref[col // LANES] = ((t * cos + swapped * sin) * scale).astype(BF16)


def _proj(x2d, gain, w_bf16, cos_t, sin_t, n_ret, ret_width, tm=512):
    t, d = x2d.shape
    n_cols = w_bf16.shape[1]
    n_att = n_cols - n_ret
    seq_tiles = cos_t.shape[0] // tm
    assert ret_width % 512 == 0 and n_ret % 512 == 0 and n_cols % 512 == 0
    return pl.pallas_call(
        functools.partial(_proj_kernel, n_ret=n_ret, ret_width=ret_width, chunk=512),
        grid=(t // tm,),
        in_specs=[
            pl.BlockSpec((tm, d), lambda i: (i, 0)),
            pl.BlockSpec((1, d), lambda i: (0, 0)),
            pl.BlockSpec((d, n_cols), lambda i: (0, 0)),
            pl.BlockSpec((tm, LANES), lambda i: (i % seq_tiles, 0)),
            pl.BlockSpec((tm, LANES), lambda i: (i % seq_tiles, 0)),
        ],
        out_specs=[
            pl.BlockSpec((n_ret // LANES, tm, LANES), lambda i: (0, i, 0)),
            pl.BlockSpec((n_att // LANES, tm, LANES), lambda i: (0, i, 0)),
        ],
        out_shape=[
            jax.ShapeDtypeStruct((n_ret // LANES, t, LANES), BF16),
            jax.ShapeDtypeStruct((n_att // LANES, t, LANES), F32),
        ],
        compiler_params=_params("parallel"),
        name="proj",
    )(x2d, gain, w_bf16, cos_t, sin_t)


def _retention_kernel(decay_ref, q_ref, k_ref, v_ref, g_ref, gn_ref, out_ref,
                      rf_ref, rb_ref, kvf_ref, kvb_ref, p0_ref, p1_ref, *, seq):
    c = RET_CHUNK
    n_chunks = seq // c
    p = pl.program_id(1)
    lane = lax.broadcasted_iota(I32, (c, LANES), 1)
    row = lax.broadcasted_iota(I32, (c, LANES), 0)
    head0 = lane < HEAD_DIM
    rowf = row.astype(F32)

    lgf0, lgf1 = decay_ref[0, 2 * p], decay_ref[0, 2 * p + 1]
    lgb0, lgb1 = decay_ref[1, 2 * p], decay_ref[1, 2 * p + 1]
    lgf_lane = jnp.where(head0, lgf0, lgf1)
    lgb_lane = jnp.where(head0, lgb0, lgb1)
    lgf_row = jnp.where(row < HEAD_DIM, lgf0, lgf1)
    lgb_row = jnp.where(row < HEAD_DIM, lgb0, lgb1)
    same_head = (row < HEAD_DIM) == head0

    zeta_f = jnp.exp((c - 1 - rowf) * lgf_lane)
    zeta_b = jnp.exp(rowf * lgb_lane)
    xi_f = jnp.exp((rowf + 1.0) * lgf_lane)
    xi_b = jnp.exp((c - rowf) * lgb_lane)
    gch_f = jnp.where(same_head, jnp.exp(c * lgf_row), 0.0)
    gch_b = jnp.where(same_head, jnp.exp(c * lgb_row), 0.0)

    diff = (row - lane).astype(F32)

    def dmat(lf, lb):
        return jnp.exp(jnp.where(diff >= 0, diff * lf, -diff * lb))

    d_stack = jnp.concatenate([dmat(lgf0, lgb0), dmat(lgf1, lgb1)], axis=0)

    def kv_step(n, carry):
        sl = pl.ds(pl.multiple_of(n * c, c), c)
        k = k_ref[0, sl, :].astype(F32)
        v = v_ref[0, sl, :]
        kvf = lax.dot_general((k * zeta_f).astype(BF16), v, _TN, preferred_element_type=F32)
        kvb = lax.dot_general((k * zeta_b).astype(BF16), v, _TN, preferred_element_type=F32)
        kvf_ref[n] = jnp.where(same_head, kvf, 0.0)
        kvb_ref[n] = jnp.where(same_head, kvb, 0.0)
        return carry

    lax.fori_loop(0, n_chunks, kv_step, 0, unroll=8)

    def scan_step(i, states):
        sf, sb = states
        nb = n_chunks - 1 - i
        rf_ref[i] = sf.astype(BF16)
        rb_ref[nb] = sb.astype(BF16)
        return gch_f * sf + kvf_ref[i], gch_b * sb + kvb_ref[nb]

    zero_state = jnp.zeros((LANES, LANES), F32)
    lax.fori_loop(0, n_chunks, scan_step, (zero_state, zero_state), unroll=4)

    gn_gain = gn_ref[...]

    def score_stage(g, p_ref):
        for jc in range(RET_GROUP):
            sl = pl.ds(pl.multiple_of((g * RET_GROUP + jc) * c, c), c)
            q = q_ref[0, sl, :]
            zero = jnp.zeros_like(q)
            q_stack = jnp.concatenate([jnp.where(head0, q, zero), jnp.where(head0, zero, q)], axis=0)
            scores = lax.dot_general(q_stack, k_ref[0, sl, :], _NT, preferred_element_type=F32) * d_stack
            sb = scores.astype(BF16)
            p_ref[jc] = jnp.concatenate([sb[:c], sb[c:]], axis=1)

    def half_mean(t):
        s0 = jnp.sum(jnp.where(head0, t, 0.0), axis=-1, keepdims=True)
        s1 = jnp.sum(jnp.where(head0, 0.0, t), axis=-1, keepdims=True)
        return jnp.where(head0, s0, s1) * (1.0 / HEAD_DIM)

    def value_stage(g, p_ref):
        for jc in range(RET_GROUP):
            n = g * RET_GROUP + jc
            sl = pl.ds(pl.multiple_of(n * c, c), c)
            q = q_ref[0, sl, :]
            v = v_ref[0, sl, :]
            zero = jnp.zeros_like(v)
            v_stack = jnp.concatenate([jnp.where(head0, v, zero), jnp.where(head0, zero, v)], axis=0)
            inner = jnp.dot(p_ref[jc], v_stack, preferred_element_type=F32)
            cross_f = jnp.dot(q, rf_ref[n], preferred_element_type=F32) * xi_f
            cross_b = jnp.dot(q, rb_ref[n], preferred_element_type=F32) * xi_b
            r = inner + cross_f + cross_b
            dlt = r - half_mean(r)
            var = half_mean(dlt * dlt)
            gate = g_ref[0, sl, :].astype(F32)
            y = dlt * lax.rsqrt(var + EPS) * gn_gain * (gate * jax.nn.sigmoid(gate))
            out_ref[0, sl, :] = y.astype(out_ref.dtype)

    n_groups = n_chunks // RET_GROUP
    assert n_groups % 2 == 0 and n_groups >= 2
    score_stage(0, p0_ref)

    def group_pair(i, carry):
        g = 2 * i
        score_stage(g + 1, p1_ref)
        value_stage(g, p0_ref)
        score_stage(g + 2, p0_ref)
        value_stage(g + 1, p1_ref)
        return carry

    lax.fori_loop(0, n_groups // 2 - 1, group_pair, 0)
    score_stage(n_groups - 1, p1_ref)
    value_stage(n_groups - 2, p0_ref)
    value_stage(n_groups - 1, p1_ref)


def _retention(ret, decay, gn_gain, n_pairs, b):
    s = ret.shape[1] // b
    blk = lambda off: pl.BlockSpec((1, s, LANES), lambda bi, pi: (off + pi, bi, 0))
    return pl.pallas_call(
        functools.partial(_retention_kernel, seq=s),
        grid=(b, n_pairs),
        in_specs=[
            pl.BlockSpec(memory_space=pltpu.SMEM),
            blk(0), blk(n_pairs), blk(2 * n_pairs), blk(3 * n_pairs),
            pl.BlockSpec((1, LANES), lambda bi, pi: (0, pi)),
        ],
        out_specs=pl.BlockSpec((1, s, LANES), lambda bi, pi: (pi, bi, 0)),
        out_shape=jax.ShapeDtypeStruct((n_pairs, b * s, LANES), BF16),
        scratch_shapes=[
            pltpu.VMEM((s // RET_CHUNK, LANES, LANES), BF16),
            pltpu.VMEM((s // RET_CHUNK, LANES, LANES), BF16),
            pltpu.VMEM((s // RET_CHUNK, LANES, LANES), F32),
            pltpu.VMEM((s // RET_CHUNK, LANES, LANES), F32),
            pltpu.VMEM((RET_GROUP, RET_CHUNK, 2 * RET_CHUNK), BF16),
            pltpu.VMEM((RET_GROUP, RET_CHUNK, 2 * RET_CHUNK), BF16),
        ],
        compiler_params=_params("parallel", "parallel"),
        name="retention",
    )(decay, ret, ret, ret, ret, gn_gain)


QBLK = 2 * SIDE
KWIN = 4 * SIDE
BASE_W = 512
ATT_GROUP = 4


def _t5_bucket_np(rel):
    half = N_BUCKETS // 2
    max_exact = half // 2
    bucket = np.where(rel > 0, half, 0)
    n = np.abs(rel)
    nf = np.maximum(n, 1).astype(np.float32)
    large = max_exact + (np.log(nf / np.float32(max_exact)) / np.float32(math.log(MAX_DISTANCE / max_exact))
                         * np.float32(half - max_exact)).astype(np.int32)
    large = np.minimum(large, half - 1)
    return (bucket + np.where(n < max_exact, n, large)).astype(np.int32)


def _bucket_rows():
    k = np.arange(BASE_W)
    off = k - SIDE
    rows = []
    for _, dilation in DILATED_PATTERNS:
        rows.append(np.where(k <= 2 * SIDE, _t5_bucket_np(off * dilation), -1))
    return np.stack(rows).astype(np.int32)


def _dilated_kernel(bias_tab_ref, bucket_ref, q_ref, k_ref, v_ref, out_ref,
                    bias_ref, qp_ref, kp_ref, vp_ref, stage_ref, o_ref, lse_ref,
                    e0_ref, e1_ref, m0_ref, m1_ref, *, seq):
    assert len(DILATED_PATTERNS) == 3 and DILATED_PATTERNS[0][1] == 1
    p = pl.program_id(1)
    lane = lax.broadcasted_iota(I32, (QBLK, LANES), 1)
    head0 = lane < HEAD_DIM
    n_blocks = seq // QBLK

    for pi in range(len(DILATED_PATTERNS)):
        bucket = jnp.broadcast_to(bucket_ref[pi:pi + 1, :], (QBLK, BASE_W))
        for hh in range(2):
            base = jnp.full((QBLK, BASE_W), NEG, F32)
            for bk in range(N_BUCKETS):
                base = jnp.where(bucket == bk, bias_tab_ref[bk, 2 * p + hh], base)
            for var, shift in enumerate((BASE_W - SIDE, 0, SIDE)):
                tile = pltpu.roll(base, shift, 1, stride=1, stride_axis=0)
                bias_ref[pi * 3 + var, hh * QBLK:(hh + 1) * QBLK, :] = tile[:, :KWIN]

    head0_k = lax.broadcasted_iota(I32, (KWIN, LANES), 1) < HEAD_DIM
    order = list(range(1, len(DILATED_PATTERNS))) + [0]
    prev_d = 1
    for slot, pi in enumerate(order):
        d = DILATED_PATTERNS[pi][1]
        final = slot == len(order) - 1
        seg_len = seq // d
        blocks_per_seg = seg_len // QBLK

        if d == 1:
            prev_d = 1
        step, prev_len = d // prev_d, seq // prev_d
        assert step * prev_d == d
        keep_f32 = d > 1 and not final and slot + 1 < len(order) - 1
        for i, (src_ref, dst_ref, scale) in enumerate(
                ((q_ref, qp_ref, HEAD_DIM ** -0.5), (k_ref, kp_ref, None), (v_ref, vp_ref, None))):
            for r_prev in range(prev_d):
                for r_step in range(step):
                    src = pl.ds(r_prev * prev_len + r_step, seg_len, stride=step) if step > 1 else pl.ds(0, seq)
                    dst = pl.ds((r_prev + prev_d * r_step) * seg_len, seg_len)
                    val = src_ref[0, src, :] if prev_d == 1 else stage_ref[i, src, :]
                    if keep_f32:
                        stage_ref[i, dst, :] = val
                    dst_ref[dst, :] = (val if scale is None else val * scale).astype(BF16)
        prev_d = d

        def placement(t, seg_len=seg_len, blocks_per_seg=blocks_per_seg):
            seg = t // blocks_per_seg
            u = t % blocks_per_seg
            q0 = pl.multiple_of(t * QBLK, QBLK)
            seg0 = seg * seg_len
            k0 = pl.multiple_of(jnp.clip(q0 - SIDE, seg0, seg0 + seg_len - KWIN), SIDE)
            var = jnp.where(u == 0, 0, jnp.where(u == blocks_per_seg - 1, 2, 1))
            return seg, u, q0, k0, var

        def logits_stage(g, e_ref, m_ref, pi=pi):
            for jb in range(ATT_GROUP):
                _, _, q0, k0, var = placement(g * ATT_GROUP + jb)
                q = qp_ref[pl.ds(q0, QBLK), :]
                kw = kp_ref[pl.ds(k0, KWIN), :]
                zero = jnp.zeros_like(q)
                q_stack = jnp.concatenate([jnp.where(head0, q, zero), jnp.where(head0, zero, q)], axis=0)
                s = lax.dot_general(q_stack, kw, _NT, preferred_element_type=F32) + bias_ref[pi * 3 + var]
                m = jnp.max(s, axis=-1, keepdims=True)
                e_ref[jb] = jnp.exp(s - m).astype(BF16)
                m_ref[jb] = jnp.broadcast_to(m, (2 * QBLK, LANES))

        def value_stage(g, e_ref, m_ref, slot=slot, d=d, final=final):
            for jb in range(ATT_GROUP):
                seg, u, q0, k0, _ = placement(g * ATT_GROUP + jb)
                vw = vp_ref[pl.ds(k0, KWIN), :]
                one = jnp.ones_like(vw)
                pv0 = jnp.dot(e_ref[jb, :QBLK, :], jnp.where(head0_k, vw, one), preferred_element_type=F32)
                pv1 = jnp.dot(e_ref[jb, QBLK:, :], jnp.where(head0_k, one, vw), preferred_element_type=F32)
                num = jnp.where(head0, pv0, pv1)
                den = pltpu.roll(jnp.where(head0, pv1, pv0), HEAD_DIM, 1)
                o = num * (1.0 / den)
                lse = jnp.where(head0, m_ref[jb, :QBLK, :], m_ref[jb, QBLK:, :]) + jnp.log(den)
                dst = pl.ds(seg + d * (u * QBLK), QBLK, stride=d) if d > 1 else pl.ds(q0, QBLK)
                if not final:
                    o_ref[slot, dst, :] = o
                    lse_ref[slot, dst, :] = lse
                else:
                    lses = [lse_ref[i, dst, :] for i in range(slot)] + [lse]
                    outs = [o_ref[i, dst, :] for i in range(slot)] + [o]
                    mx = functools.reduce(jnp.maximum, lses)
                    ws = [jnp.exp(l - mx) for l in lses]
                    inv = 1.0 / functools.reduce(jnp.add, ws)
                    acc = functools.reduce(jnp.add, [(w * inv) * t for w, t in zip(ws, outs)])
                    out_ref[0, dst, :] = acc.astype(out_ref.dtype)

        n_groups = n_blocks // ATT_GROUP
        assert n_groups % 2 == 0 and n_groups >= 2
        logits_stage(0, e0_ref, m0_ref)

        def group_pair(i, carry):
            g = 2 * i
            logits_stage(g + 1, e1_ref, m1_ref)
            value_stage(g, e0_ref, m0_ref)
            logits_stage(g + 2, e0_ref, m0_ref)
            value_stage(g + 1, e1_ref, m1_ref)
            return carry

        lax.fori_loop(0, n_groups // 2 - 1, group_pair, 0)
        logits_stage(n_groups - 1, e1_ref, m1_ref)
        value_stage(n_groups - 2, e0_ref, m0_ref)
        value_stage(n_groups - 1, e1_ref, m1_ref)


def _dilated(att, rel_bias, n_pairs, b):
    s = att.shape[1] // b
    n_pat = len(DILATED_PATTERNS)
    blk = lambda off: pl.BlockSpec((1, s, LANES), lambda bi, pi: (off + pi, bi, 0))
    bucket_rows = jnp.asarray(_bucket_rows())
    return pl.pallas_call(
        functools.partial(_dilated_kernel, seq=s),
        grid=(b, n_pairs),
        in_specs=[
            pl.BlockSpec(memory_space=pltpu.SMEM),
            pl.BlockSpec((n_pat, BASE_W), lambda bi, pi: (0, 0)),
            blk(0), blk(n_pairs), blk(2 * n_pairs),
        ],
        out_specs=pl.BlockSpec((1, s, LANES), lambda bi, pi: (pi, bi, 0)),
        out_shape=jax.ShapeDtypeStruct((n_pairs, b * s, LANES), BF16),
        scratch_shapes=[
            pltpu.VMEM((n_pat * 3, 2 * QBLK, KWIN), F32),
            pltpu.VMEM((s, LANES), BF16),
            pltpu.VMEM((s, LANES), BF16),
            pltpu.VMEM((s, LANES), BF16),
            pltpu.VMEM((3, s, LANES), F32),
            pltpu.VMEM((n_pat - 1, s, LANES), F32),
            pltpu.VMEM((n_pat - 1, s, LANES), F32),
            pltpu.VMEM((ATT_GROUP, 2 * QBLK, KWIN), BF16),
            pltpu.VMEM((ATT_GROUP, 2 * QBLK, KWIN), BF16),
            pltpu.VMEM((ATT_GROUP, 2 * QBLK, LANES), F32),
            pltpu.VMEM((ATT_GROUP, 2 * QBLK, LANES), F32),
        ],
        compiler_params=_params("parallel", "parallel"),
        name="dilated",
    )(rel_bias, bucket_rows, att, att, att)


def _outproj_kernel(r_ref, a_ref, x_ref, w_ref, g_ref, wr_ref, x1_ref, h2_ref, aff_ref):
    mixed = jnp.concatenate([r_ref[i] for i in range(r_ref.shape[0])]
                            + [a_ref[i] for i in range(a_ref.shape[0])], axis=1)
    x1 = x_ref[...] + jnp.dot(mixed, w_ref[...], preferred_element_type=F32)
    x1_ref[...] = x1
    ms = jnp.mean(x1 * x1, axis=-1, keepdims=True)
    h2 = x1 * lax.rsqrt(ms + EPS) * g_ref[...]
    h_hi = h2.astype(BF16)
    h2_ref[...] = h_hi
    n_e = aff_ref.shape[1]
    tm = h2.shape[0]
    h_lo = (h2 - h_hi.astype(F32)).astype(BF16)
    wr = wr_ref[...]
    w_hi = wr.astype(BF16)
    w_lo = (wr - w_hi.astype(F32)).astype(BF16)
    w_parts = jnp.where(lax.broadcasted_iota(I32, wr.shape, 1) < n_e, w_hi, w_lo)
    prod = jnp.dot(jnp.concatenate([h_hi, h_lo], axis=0), w_parts, preferred_element_type=F32)
    logits = prod[:tm] + (pltpu.roll(prod[:tm], LANES - n_e, 1) + prod[tm:])
    expert_lane = lax.broadcasted_iota(I32, logits.shape, 1) < n_e
    logits = jnp.where(expert_lane, logits, NEG)
    m = jnp.max(logits, axis=1, keepdims=True)
    e = jnp.exp(logits - m)
    aff = e / jnp.sum(e, axis=1, keepdims=True)
    aff_ref[0] = aff.T[:n_e, :]


def _outproj(r, a, x, w_bf16, gain, w_router, tm=512):
    b, s, d = x.shape
    n_r, n_a = r.shape[0], a.shape[0]
    n_e = w_router.shape[1]
    assert 2 * n_e <= LANES
    w_router = jnp.pad(jnp.concatenate([w_router, w_router], axis=1), ((0, 0), (0, LANES - 2 * n_e)))
    spt = s // tm
    tok = lambda width: pl.BlockSpec((tm, width), lambda i: (i, 0))
    slab = lambda blocks: pl.BlockSpec((blocks, tm, LANES), lambda i: (0, i, 0))
    const = lambda shape: pl.BlockSpec(shape, lambda i: (0, 0))
    return pl.pallas_call(
        _outproj_kernel,
        grid=(b * spt,),
        in_specs=[slab(n_r), slab(n_a), tok(d), const(((n_r + n_a) * LANES, d)), const((1, d)), const((d, LANES))],
        out_specs=[tok(d), tok(d), pl.BlockSpec((1, n_e, tm), lambda i: (i // spt, 0, i % spt))],
        out_shape=[
            jax.ShapeDtypeStruct((b * s, d), F32),
            jax.ShapeDtypeStruct((b * s, d), BF16),
            jax.ShapeDtypeStruct((b, n_e, s), F32),
        ],
        compiler_params=_params("parallel"),
        name="outproj",
    )(r, a, x.reshape(b * s, d), w_bf16, gain, w_router)


def _select_kernel(aff_ref, posm_ref, cnt_ref, *, cap):
    rows, s = aff_ref.shape
    bits = pltpu.bitcast(aff_ref[...], I32)

    def search(i, thr):
        cand = thr | jnp.left_shift(jnp.int32(1), 30 - i)
        cnt = jnp.sum((bits >= cand).astype(I32), axis=1, keepdims=True)
        return jnp.where(cnt >= cap, cand, thr)

    thr = lax.fori_loop(0, 31, search, jnp.zeros((rows, 1), I32))
    gt = bits > thr
    eq = bits == thr
    need = cap - jnp.sum(gt.astype(I32), axis=1, keepdims=True)

    w = LANES
    tri = (lax.broadcasted_iota(I32, (w, w), 0) <= lax.broadcasted_iota(I32, (w, w), 1)).astype(BF16)

    def excl_prefix(flags):
        carry = jnp.zeros((rows, 1), F32)
        out = []
        for c0 in range(0, s, w):
            f = flags[:, c0:c0 + w].astype(BF16)
            inc = jnp.dot(f, tri, preferred_element_type=F32)
            out.append(inc - f.astype(F32) + carry)
            carry = carry + inc[:, w - 1:w]
        return jnp.concatenate(out, axis=1).astype(I32)

    sel = gt | (eq & (excl_prefix(eq) < need))
    posm_ref[...] = jnp.where(sel, excl_prefix(sel), -1)

    tok = lax.broadcasted_iota(I32, (w, w), 0)
    edge = lax.broadcasted_iota(I32, (w, w), 1) * MOE_TILE
    cnt = jnp.zeros((rows, w), F32)
    for c0 in range(0, s, w):
        before = ((tok + c0) < edge).astype(BF16)
        cnt = cnt + jnp.dot(sel[:, c0:c0 + w].astype(BF16), before, preferred_element_type=F32)
    cnt_ref[...] = cnt.astype(I32)


def _select(aff_rows, cap):
    rows, s = aff_rows.shape
    return pl.pallas_call(
        functools.partial(_select_kernel, cap=cap),
        out_shape=[jax.ShapeDtypeStruct((rows, s), I32), jax.ShapeDtypeStruct((rows, LANES), I32)],
        compiler_params=pltpu.CompilerParams(vmem_limit_bytes=VMEM_LIMIT),
        name="select",
    )(aff_rows)


def _slot_window(cnt_ref, row, j, cap):
    lo, hi = cnt_ref[row * CNT_STRIDE + j], cnt_ref[row * CNT_STRIDE + j + 1]
    w0 = pl.multiple_of(jnp.minimum((lo // SLOT_ALIGN) * SLOT_ALIGN, cap - SLOT_WIN), SLOT_ALIGN)
    return hi, w0


def _any_overflow(windows):
    return functools.reduce(jnp.logical_or, [hi > w0 + SLOT_WIN for hi, w0 in windows])


def _gather_kernel(cnt_ref, posm_ref, h2_ref, xg_ref):
    bi, j = pl.program_id(0), pl.program_id(1)
    n_e, cap = xg_ref.shape[1:3]
    tt = MOE_TILE

    @pl.when(j == 0)
    def _():
        xg_ref[...] = jnp.zeros_like(xg_ref)

    slot0 = lax.broadcasted_iota(I32, (SLOT_WIN, tt), 0)
    for sub in range(h2_ref.shape[1] // tt):
        tile = j * (h2_ref.shape[1] // tt) + sub
        tok = slice(sub * tt, (sub + 1) * tt)
        windows = [_slot_window(cnt_ref, bi * n_e + e, tile, cap) for e in range(n_e)]
        for e in range(0, n_e, WIN_GROUP):
            onehot = jnp.concatenate(
                [(posm_ref[0, e + i:e + i + 1, tok] == slot0 + windows[e + i][1]).astype(BF16)
                 for i in range(WIN_GROUP)], axis=0)
            rows = jnp.dot(onehot, h2_ref[0, tok, :], preferred_element_type=F32).astype(BF16)
            for i in range(WIN_GROUP):
                xg_ref[0, e + i, pl.ds(windows[e + i][1], SLOT_WIN), :] += rows[i * SLOT_WIN:(i + 1) * SLOT_WIN]

        @pl.when(_any_overflow(windows))
        def _(tile=tile, tok=tok):
            wide = lax.broadcasted_iota(I32, (MXU_DEPTH, tt), 0)

            def one_expert(e, carry):
                hi, w0 = _slot_window(cnt_ref, bi * n_e + e, tile, cap)

                @pl.when(hi > w0 + SLOT_WIN)
                def _():
                    pos = posm_ref[0, pl.ds(e, 1), tok]
                    for ws in range(0, cap, MXU_DEPTH):
                        slot = wide + ws
                        onehot = ((pos == slot) & (slot >= w0 + SLOT_WIN)).astype(BF16)
                        extra = jnp.dot(onehot, h2_ref[0, tok, :], preferred_element_type=F32)
                        xg_ref[0, e, ws:ws + MXU_DEPTH, :] += extra.astype(BF16)

                return carry

            lax.fori_loop(0, n_e, one_expert, 0)


def _gather(cnt, posm, h2, cap):
    b, n_e, s = posm.shape
    d = h2.shape[2]
    step = MOE_TILE * GATHER_TILES
    return pl.pallas_call(
        _gather_kernel,
        grid_spec=pltpu.PrefetchScalarGridSpec(
            num_scalar_prefetch=1,
            grid=(b, s // step),
            in_specs=[
                pl.BlockSpec((1, n_e, step), lambda bi, j, cnt: (bi, 0, j)),
                pl.BlockSpec((1, step, d), lambda bi, j, cnt: (bi, j, 0)),
            ],
            out_specs=pl.BlockSpec((1, n_e, cap, d), lambda bi, j, cnt: (bi, 0, 0, 0)),
        ),
        out_shape=jax.ShapeDtypeStruct((b, n_e, cap, d), BF16),
        compiler_params=_params("parallel", "arbitrary"),
        name="gather",
    )(cnt, posm, h2)


def _ffn_kernel(xg_ref, wg_hbm, wu_hbm, wd_hbm, y_ref, wg_buf, wu_buf, wd_buf, wg_stage, wu_stage, wd_stage,
                sem, acc_ref, *, tf):
    e, bi = pl.program_id(0), pl.program_id(1)
    n_e, n_b = pl.num_programs(0), pl.num_programs(1)
    rows_in, rows_dn = wg_stage.shape[1], wd_stage.shape[1]
    f_total = wg_buf.shape[2]

    def chunk_copies(expert, c):
        st = c % 2
        return (
            pltpu.make_async_copy(wg_hbm.at[expert, pl.ds(c * rows_in, rows_in), :], wg_stage.at[st], sem.at[0, st]),
            pltpu.make_async_copy(wu_hbm.at[expert, pl.ds(c * rows_in, rows_in), :], wu_stage.at[st], sem.at[1, st]),
            pltpu.make_async_copy(wd_hbm.at[expert, pl.ds(c * rows_dn, rows_dn), :], wd_stage.at[st], sem.at[2, st]),
        )

    def finish_chunk(expert, c):
        for cp in chunk_copies(expert, c):
            cp.wait()
        slot, st = expert % 2, c % 2
        wg_buf[slot, pl.ds(c * rows_in, rows_in), :] = wg_stage[st].astype(BF16)
        wu_buf[slot, pl.ds(c * rows_in, rows_in), :] = wu_stage[st].astype(BF16)
        wd_buf[slot, pl.ds(c * rows_dn, rows_dn), :] = wd_stage[st].astype(BF16)

    @pl.when((e == 0) & (bi == 0))
    def _():
        for cp in chunk_copies(0, 0):
            cp.start()

        def load_first(c, carry):
            @pl.when(c + 1 < n_b)
            def _():
                for cp in chunk_copies(0, c + 1):
                    cp.start()

            finish_chunk(0, c)
            return carry

        lax.fori_loop(0, n_b, load_first, 0)

    pending = jnp.where(bi > 0, e + 1 < n_e, e > 0)

    @pl.when(pending)
    def _():
        finish_chunk(jnp.where(bi > 0, e + 1, e), jnp.where(bi > 0, bi - 1, n_b - 1))

    @pl.when(e + 1 < n_e)
    def _():
        for cp in chunk_copies(e + 1, bi):
            cp.start()

    def swiglu(slot):
        xg = xg_ref[0, 0]
        for fi in range(f_total // tf):
            fs = slice(fi * tf, (fi + 1) * tf)
            gate = jnp.dot(xg, wg_buf[slot, :, fs], preferred_element_type=F32)
            up = jnp.dot(xg, wu_buf[slot, :, fs], preferred_element_type=F32)
            hid = ((gate * jax.nn.sigmoid(gate)) * up).astype(BF16)
            part = jnp.dot(hid, wd_buf[slot, fs, :], preferred_element_type=F32)
            if fi == 0:
                acc_ref[...] = part
            else:
                acc_ref[...] += part
        y_ref[0, 0] = acc_ref[...].astype(y_ref.dtype)

    for slot in range(2):
        pl.when(e % 2 == slot)(functools.partial(swiglu, slot))


def _ffn(xg, wg, wu, wd, tf=512):
    b, n_e, cap, d = xg.shape
    f = wg.shape[2]
    assert d % b == 0 and f % b == 0 and (d // b) % SLOT_ALIGN == 0, "one weight row chunk per batch step"
    tok = pl.BlockSpec((1, 1, cap, d), lambda e, bi: (bi, e, 0, 0))
    hbm = pl.BlockSpec(memory_space=pl.ANY)
    return pl.pallas_call(
        functools.partial(_ffn_kernel, tf=tf),
        grid=(n_e, b),
        in_specs=[tok, hbm, hbm, hbm],
        out_specs=tok,
        out_shape=jax.ShapeDtypeStruct((b, n_e, cap, d), BF16),
        scratch_shapes=[
            pltpu.VMEM((2, d, f), BF16), pltpu.VMEM((2, d, f), BF16), pltpu.VMEM((2, f, d), BF16),
            pltpu.VMEM((2, d // b, f), F32), pltpu.VMEM((2, d // b, f), F32), pltpu.VMEM((2, f // b, d), F32),
            pltpu.SemaphoreType.DMA((3, 2)),
            pltpu.VMEM((cap, d), F32),
        ],
        compiler_params=_params("arbitrary", "arbitrary"),
        name="ffn",
    )(xg, wg, wu, wd)


def _combine_kernel(cnt_ref, pos_ref, gate_ref, y_ref, x1_ref, g_ref, out_ref, acc_ref):
    bi, j = pl.program_id(0), pl.program_id(1)
    tt = MOE_TILE
    n_e, cap = y_ref.shape[1:3]
    slot0 = lax.broadcasted_iota(I32, (SLOT_WIN, tt), 0)
    for sub in range(x1_ref.shape[1] // tt):
        tile = j * (x1_ref.shape[1] // tt) + sub
        tok = slice(sub * tt, (sub + 1) * tt)
        windows = [_slot_window(cnt_ref, bi * n_e + e, tile, cap) for e in range(n_e)]
        acc = x1_ref[0, tok, :]
        for e in range(0, n_e, WIN_GROUP):
            scatter_t = jnp.concatenate(
                [jnp.where(pos_ref[0, e + i:e + i + 1, tok] == slot0 + windows[e + i][1],
                           gate_ref[0, e + i:e + i + 1, tok], 0.0).astype(BF16) for i in range(WIN_GROUP)], axis=0)
            y_group = jnp.concatenate(
                [y_ref[0, e + i, pl.ds(windows[e + i][1], SLOT_WIN), :] for i in range(WIN_GROUP)], axis=0)
            acc = acc + lax.dot_general(scatter_t, y_group, _TN, preferred_element_type=F32)
        acc_ref[...] = acc

        @pl.when(_any_overflow(windows))
        def _(tile=tile, tok=tok):
            wide = lax.broadcasted_iota(I32, (MXU_DEPTH, tt), 0)

            def one_expert(e, carry):
                hi, w0 = _slot_window(cnt_ref, bi * n_e + e, tile, cap)

                @pl.when(hi > w0 + SLOT_WIN)
                def _():
                    pos = pos_ref[0, pl.ds(e, 1), tok]
                    gate = gate_ref[0, pl.ds(e, 1), tok]
                    for ws in range(0, cap, MXU_DEPTH):
                        slot = wide + ws
                        extra = jnp.where((pos == slot) & (slot >= w0 + SLOT_WIN), gate, 0.0).astype(BF16)
                        acc_ref[...] += lax.dot_general(extra, y_ref[0, e, ws:ws + MXU_DEPTH, :], _TN,
                                                        preferred_element_type=F32)

                return carry

            lax.fori_loop(0, n_e, one_expert, 0)

        acc = acc_ref[...]
        ms = jnp.mean(acc * acc, axis=-1, keepdims=True)
        out_ref[0, tok, :] = acc * lax.rsqrt(ms + EPS) * g_ref[...]


def _combine(cnt, posm, aff, y, x1, gain):
    b, s, d = x1.shape
    n_e, cap = y.shape[1:3]
    step = MOE_TILE * COMBINE_TILES
    return pl.pallas_call(
        _combine_kernel,
        grid_spec=pltpu.PrefetchScalarGridSpec(
            num_scalar_prefetch=1,
            grid=(b, s // step),
            in_specs=[
                pl.BlockSpec((1, n_e, step), lambda bi, j, cnt: (bi, 0, j)),
                pl.BlockSpec((1, n_e, step), lambda bi, j, cnt: (bi, 0, j)),
                pl.BlockSpec((1, n_e, cap, d), lambda bi, j, cnt: (bi, 0, 0, 0)),
                pl.BlockSpec((1, step, d), lambda bi, j, cnt: (bi, j, 0)),
                pl.BlockSpec((1, d), lambda bi, j, cnt: (0, 0)),
            ],
            out_specs=pl.BlockSpec((1, step, d), lambda bi, j, cnt: (bi, j, 0)),
            scratch_shapes=[pltpu.VMEM((MOE_TILE, d), F32)],
        ),
        out_shape=jax.ShapeDtypeStruct((b, s, d), F32),
        compiler_params=_params("parallel", "parallel"),
        name="combine",
    )(cnt, posm, aff, y, x1, gain)


def _rope_tables(seq):
    half = HEAD_DIM // 2
    pos = jnp.arange(seq, dtype=F32)
    inv = ROPE_BASE ** (-jnp.arange(0, HEAD_DIM, 2, dtype=F32) / HEAD_DIM)
    ang = pos[:, None] * inv[None, :]
    cos, sin = jnp.cos(ang), jnp.sin(ang)
    reps = LANES // HEAD_DIM
    cos_t = jnp.tile(jnp.concatenate([cos, cos], axis=1), (1, reps))
    sin_t = jnp.tile(jnp.concatenate([-sin, sin], axis=1), (1, reps))
    return cos_t, sin_t


def kernel(x, norm1_gain, w_in, ret_log_decay, ret_gn_gain, rel_bias, w_out, norm2_gain, w_router,
           w_gate, w_up, w_down, final_gain):
    b, s, d = x.shape
    depth = w_in.shape[0]
    ret_width = ret_gn_gain.shape[1]
    n_ret_pairs = ret_width // LANES
    att_width = (w_in.shape[2] - 4 * ret_width) // 3
    n_att_pairs = att_width // LANES
    cap = CAPACITY_FACTOR * s // N_EXPERTS
    cos_t, sin_t = _rope_tables(s)

    assert depth == 1, "single-layer block: the final norm is fused into the combine kernel"
    layer = 0
    ret, att = _proj(x.reshape(b * s, d), norm1_gain[layer][None, :], w_in[layer].astype(BF16),
                     cos_t, sin_t, n_ret=4 * ret_width, ret_width=ret_width)
    r = _retention(ret, ret_log_decay[layer], ret_gn_gain[layer][None, :], n_ret_pairs, b)
    a = _dilated(att, rel_bias, n_att_pairs, b)
    x1, h2, aff = _outproj(r, a, x, w_out[layer].astype(BF16), norm2_gain[layer][None, :],
                           w_router[layer])
    assert s % (MOE_TILE * max(GATHER_TILES, COMBINE_TILES)) == 0 and s // MOE_TILE < CNT_STRIDE and cap % MXU_DEPTH == 0 and N_EXPERTS % WIN_GROUP == 0
    posm, cnt = _select(aff.reshape(b * N_EXPERTS, s), cap)
    posm = posm.reshape(b, N_EXPERTS, s)
    cnt = cnt[:, :CNT_STRIDE].reshape(-1)
    xg = _gather(cnt, posm, h2.reshape(b, s, d), cap)
    y = _ffn(xg, w_gate[layer], w_up[layer], w_down[layer])
    return _combine(cnt, posm, aff, y, x1.reshape(b, s, d),
                    final_gain[None, :])
```

```python
import functools
import math

import numpy as np
import jax
import jax.numpy as jnp
from jax import lax
from jax.experimental import pallas as pl
from jax.experimental.pallas import tpu as pltpu

F32 = jnp.float32
BF16 = jnp.bfloat16
I32 = jnp.int32

HEAD_DIM = 64
LANES = 128
RET_CHUNK = 128
RET_GROUP = 4
ROPE_BASE = 10000.0
DILATED_PATTERNS = ((128, 1), (512, 4), (2048, 16))
SIDE = 64
N_BUCKETS = 32
MAX_DISTANCE = 1024
N_EXPERTS = 16
CAPACITY_FACTOR = 2
MOE_TILE = 256
GATHER_TILES = 8
COMBINE_TILES = 4
MXU_DEPTH = 256
WIN_GROUP = 4
SLOT_WIN = MXU_DEPTH // WIN_GROUP
SLOT_ALIGN = 16
CNT_STRIDE = 32
EPS = 1e-6
NEG = -1e30
VMEM_LIMIT = 56 * 1024 * 1024

_NT = (((1,), (1,)), ((), ()))
_TN = (((0,), (0,)), ((), ()))


def _params(*sem):
    return pltpu.CompilerParams(dimension_semantics=sem, vmem_limit_bytes=VMEM_LIMIT)


def _proj_kernel(x_ref, g_ref, w_ref, cos_ref, sin_ref, ret_ref, att_ref, *, n_ret, ret_width, chunk):
    x = x_ref[...]
    ms = jnp.mean(x * x, axis=-1, keepdims=True)
    h = (x * lax.rsqrt(ms + EPS) * g_ref[...]).astype(BF16)
    n_cols = w_ref.shape[1]
    cos, sin = cos_ref[...], sin_ref[...]
    first_half = (lax.broadcasted_iota(I32, cos.shape, 1) % HEAD_DIM) < (HEAD_DIM // 2)
    for c0 in range(0, n_cols, chunk):
        o = jnp.dot(h, w_ref[:, c0:c0 + chunk], preferred_element_type=F32)
        for l0 in range(0, chunk, LANES):
            t = o[:, l0:l0 + LANES]
            col = c0 + l0
            if col >= n_ret:
                att_ref[(col - n_ret) // LANES] = t
            elif col >= 2 * ret_width:
                ret_ref[col // LANES] = t.astype(BF16)
            else:
                scale = 1.0 if col < ret_width else HEAD_DIM ** -0.5
                swapped = jnp.where(first_half, pltpu.roll(t, LANES - HEAD_DIM // 2, 1),
                                    pltpu.roll(t, HEAD_DIM // 2, 1))
                ret_ref[col // LANES] = ((t * cos + swapped * sin) * scale).astype(BF16)


def _proj(x2d, gain, w_bf16, cos_t, sin_t, n_ret, ret_width, tm=512):
    t, d = x2d.shape
    n_cols = w_bf16.shape[1]
    n_att = n_cols - n_ret
    seq_tiles = cos_t.shape[0] // tm
    assert ret_width % 512 == 0 and n_ret % 512 == 0 and n_cols % 512 == 0
    return pl.pallas_call(
        functools.partial(_proj_kernel, n_ret=n_ret, ret_width=ret_width, chunk=512),
        grid=(t // tm,),
        in_specs=[
            pl.BlockSpec((tm, d), lambda i: (i, 0)),
            pl.BlockSpec((1, d), lambda i: (0, 0)),
            pl.BlockSpec((d, n_cols), lambda i: (0, 0)),
            pl.BlockSpec((tm, LANES), lambda i: (i % seq_tiles, 0)),
            pl.BlockSpec((tm, LANES), lambda i: (i % seq_tiles, 0)),
        ],
        out_specs=[
            pl.BlockSpec((n_ret // LANES, tm, LANES), lambda i: (0, i, 0)),
            pl.BlockSpec((n_att // LANES, tm, LANES), lambda i: (0, i, 0)),
        ],
        out_shape=[
            jax.ShapeDtypeStruct((n_ret // LANES, t, LANES), BF16),
            jax.ShapeDtypeStruct((n_att // LANES, t, LANES), F32),
        ],
        compiler_params=_params("parallel"),
        name="proj",
    )(x2d, gain, w_bf16, cos_t, sin_t)


def _retention_kernel(decay_ref, q_ref, k_ref, v_ref, g_ref, gn_ref, out_ref,
                      rf_ref, rb_ref, kvf_ref, kvb_ref, p0_ref, p1_ref, *, seq):
    c = RET_CHUNK
    n_chunks = seq // c
    p = pl.program_id(1)
    lane = lax.broadcasted_iota(I32, (c, LANES), 1)
    row = lax.broadcasted_iota(I32, (c, LANES), 0)
    head0 = lane < HEAD_DIM
    rowf = row.astype(F32)

    lgf0, lgf1 = decay_ref[0, 2 * p], decay_ref[0, 2 * p + 1]
    lgb0, lgb1 = decay_ref[1, 2 * p], decay_ref[1, 2 * p + 1]
    lgf_lane = jnp.where(head0, lgf0, lgf1)
    lgb_lane = jnp.where(head0, lgb0, lgb1)
    lgf_row = jnp.where(row < HEAD_DIM, lgf0, lgf1)
    lgb_row = jnp.where(row < HEAD_DIM, lgb0, lgb1)
    same_head = (row < HEAD_DIM) == head0

    zeta_f = jnp.exp((c - 1 - rowf) * lgf_lane)
    zeta_b = jnp.exp(rowf * lgb_lane)
    xi_f = jnp.exp((rowf + 1.0) * lgf_lane)
    xi_b = jnp.exp((c - rowf) * lgb_lane)
    gch_f = jnp.where(same_head, jnp.exp(c * lgf_row), 0.0)
    gch_b = jnp.where(same_head, jnp.exp(c * lgb_row), 0.0)

    diff = (row - lane).astype(F32)

    def dmat(lf, lb):
        return jnp.exp(jnp.where(diff >= 0, diff * lf, -diff * lb))

    d_stack = jnp.concatenate([dmat(lgf0, lgb0), dmat(lgf1, lgb1)], axis=0)

    def kv_step(n, carry):
        sl = pl.ds(pl.multiple_of(n * c, c), c)
        k = k_ref[0, sl, :].astype(F32)
        v = v_ref[0, sl, :]
        kvf = lax.dot_general((k * zeta_f).astype(BF16), v, _TN, preferred_element_type=F32)
        kvb = lax.dot_general((k * zeta_b).astype(BF16), v, _TN, preferred_element_type=F32)
        kvf_ref[n] = jnp.where(same_head, kvf, 0.0)
        kvb_ref[n] = jnp.where(same_head, kvb, 0.0)
        return carry

    lax.fori_loop(0, n_chunks, kv_step, 0, unroll=8)

    def scan_step(i, states):
        sf, sb = states
        nb = n_chunks - 1 - i
        rf_ref[i] = sf.astype(BF16)
        rb_ref[nb] = sb.astype(BF16)
        return gch_f * sf + kvf_ref[i], gch_b * sb + kvb_ref[nb]

    zero_state = jnp.zeros((LANES, LANES), F32)
    lax.fori_loop(0, n_chunks, scan_step, (zero_state, zero_state), unroll=4)

    gn_gain = gn_ref[...]

    def score_stage(g, p_ref):
        for jc in range(RET_GROUP):
            sl = pl.ds(pl.multiple_of((g * RET_GROUP + jc) * c, c), c)
            q = q_ref[0, sl, :]
            zero = jnp.zeros_like(q)
            q_stack = jnp.concatenate([jnp.where(head0, q, zero), jnp.where(head0, zero, q)], axis=0)
            scores = lax.dot_general(q_stack, k_ref[0, sl, :], _NT, preferred_element_type=F32) * d_stack
            sb = scores.astype(BF16)
            p_ref[jc] = jnp.concatenate([sb[:c], sb[c:]], axis=1)

    def half_mean(t):
        s0 = jnp.sum(jnp.where(head0, t, 0.0), axis=-1, keepdims=True)
        s1 = jnp.sum(jnp.where(head0, 0.0, t), axis=-1, keepdims=True)
        return jnp.where(head0, s0, s1) * (1.0 / HEAD_DIM)

    def value_stage(g, p_ref):
        for jc in range(RET_GROUP):
            n = g * RET_GROUP + jc
            sl = pl.ds(pl.multiple_of(n * c, c), c)
            q = q_ref[0, sl, :]
            v = v_ref[0, sl, :]
            zero = jnp.zeros_like(v)
            v_stack = jnp.concatenate([jnp.where(head0, v, zero), jnp.where(head0, zero, v)], axis=0)
            inner = jnp.dot(p_ref[jc], v_stack, preferred_element_type=F32)
            cross_f = jnp.dot(q, rf_ref[n], preferred_element_type=F32) * xi_f
            cross_b = jnp.dot(q, rb_ref[n], preferred_element_type=F32) * xi_b
            r = inner + cross_f + cross_b
            dlt = r - half_mean(r)
            var = half_mean(dlt * dlt)
            gate = g_ref[0, sl, :].astype(F32)
            y = dlt * lax.rsqrt(var + EPS) * gn_gain * (gate * jax.nn.sigmoid(gate))
            out_ref[0, sl, :] = y.astype(out_ref.dtype)

    n_groups = n_chunks // RET_GROUP
    assert n_groups % 2 == 0 and n_groups >= 2
    score_stage(0, p0_ref)

    def group_pair(i, carry):
        g = 2 * i
        score_stage(g + 1, p1_ref)
        value_stage(g, p0_ref)
        score_stage(g + 2, p0_ref)
        value_stage(g + 1, p1_ref)
        return carry

    lax.fori_loop(0, n_groups // 2 - 1, group_pair, 0)
    score_stage(n_groups - 1, p1_ref)
    value_stage(n_groups - 2, p0_ref)
    value_stage(n_groups - 1, p1_ref)


def _retention(ret, decay, gn_gain, n_pairs, b):
    s = ret.shape[1] // b
    blk = lambda off: pl.BlockSpec((1, s, LANES), lambda bi, pi: (off + pi, bi, 0))
    return pl.pallas_call(
        functools.partial(_retention_kernel, seq=s),
        grid=(b, n_pairs),
        in_specs=[
            pl.BlockSpec(memory_space=pltpu.SMEM),
            blk(0), blk(n_pairs), blk(2 * n_pairs), blk(3 * n_pairs),
            pl.BlockSpec((1, LANES), lambda bi, pi: (0, pi)),
        ],
        out_specs=pl.BlockSpec((1, s, LANES), lambda bi, pi: (pi, bi, 0)),
        out_shape=jax.ShapeDtypeStruct((n_pairs, b * s, LANES), BF16),
        scratch_shapes=[
            pltpu.VMEM((s // RET_CHUNK, LANES, LANES), BF16),
            pltpu.VMEM((s // RET_CHUNK, LANES, LANES), BF16),
            pltpu.VMEM((s // RET_CHUNK, LANES, LANES), F32),
            pltpu.VMEM((s // RET_CHUNK, LANES, LANES), F32),
            pltpu.VMEM((RET_GROUP, RET_CHUNK, 2 * RET_CHUNK), BF16),
            pltpu.VMEM((RET_GROUP, RET_CHUNK, 2 * RET_CHUNK), BF16),
        ],
        compiler_params=_params("parallel", "parallel"),
        name="retention",
    )(decay, ret, ret, ret, ret, gn_gain)


QBLK = 2 * SIDE
KWIN = 4 * SIDE
BASE_W = 512
ATT_GROUP = 4


def _t5_bucket_np(rel):
    half = N_BUCKETS // 2
    max_exact = half // 2
    bucket = np.where(rel > 0, half, 0)
    n = np.abs(rel)
    nf = np.maximum(n, 1).astype(np.float32)
    large = max_exact + (np.log(nf / np.float32(max_exact)) / np.float32(math.log(MAX_DISTANCE / max_exact))
                         * np.float32(half - max_exact)).astype(np.int32)
    large = np.minimum(large, half - 1)
    return (bucket + np.where(n < max_exact, n, large)).astype(np.int32)


def _bucket_rows():
    k = np.arange(BASE_W)
    off = k - SIDE
    rows = []
    for _, dilation in DILATED_PATTERNS:
        rows.append(np.where(k <= 2 * SIDE, _t5_bucket_np(off * dilation), -1))
    return np.stack(rows).astype(np.int32)


def _dilated_kernel(bias_tab_ref, bucket_ref, q_ref, k_ref, v_ref, out_ref,
                    bias_ref, qp_ref, kp_ref, vp_ref, stage_ref, o_ref, lse_ref,
                    e0_ref, e1_ref, m0_ref, m1_ref, *, seq):
    assert len(DILATED_PATTERNS) == 3 and DILATED_PATTERNS[0][1] == 1
    p = pl.program_id(1)
    lane = lax.broadcasted_iota(I32, (QBLK, LANES), 1)
    head0 = lane < HEAD_DIM
    n_blocks = seq // QBLK

    for pi in range(len(DILATED_PATTERNS)):
        bucket = jnp.broadcast_to(bucket_ref[pi:pi + 1, :], (QBLK, BASE_W))
        for hh in range(2):
            base = jnp.full((QBLK, BASE_W), NEG, F32)
            for bk in range(N_BUCKETS):
                base = jnp.where(bucket == bk, bias_tab_ref[bk, 2 * p + hh], base)
            for var, shift in enumerate((BASE_W - SIDE, 0, SIDE)):
                tile = pltpu.roll(base, shift, 1, stride=1, stride_axis=0)
                bias_ref[pi * 3 + var, hh * QBLK:(hh + 1) * QBLK, :] = tile[:, :KWIN]

    head0_k = lax.broadcasted_iota(I32, (KWIN, LANES), 1) < HEAD_DIM
    order = list(range(1, len(DILATED_PATTERNS))) + [0]
    prev_d = 1
    for slot, pi in enumerate(order):
        d = DILATED_PATTERNS[pi][1]
        final = slot == len(order) - 1
        seg_len = seq // d
        blocks_per_seg = seg_len // QBLK

        if d == 1:
            prev_d = 1
        step, prev_len = d // prev_d, seq // prev_d
        assert step * prev_d == d
        keep_f32 = d > 1 and not final and slot + 1 < len(order) - 1
        for i, (src_ref, dst_ref, scale) in enumerate(
                ((q_ref, qp_ref, HEAD_DIM ** -0.5), (k_ref, kp_ref, None), (v_ref, vp_ref, None))):
            for r_prev in range(prev_d):
                for r_step in range(step):
                    src = pl.ds(r_prev * prev_len + r_step, seg_len, stride=step) if step > 1 else pl.ds(0, seq)
                    dst = pl.ds((r_prev + prev_d * r_step) * seg_len, seg_len)
                    val = src_ref[0, src, :] if prev_d == 1 else stage_ref[i, src, :]
                    if keep_f32:
                        stage_ref[i, dst, :] = val
                    dst_ref[dst, :] = (val if scale is None else val * scale).astype(BF16)
        prev_d = d

        def placement(t, seg_len=seg_len, blocks_per_seg=blocks_per_seg):
            seg = t // blocks_per_seg
            u = t % blocks_per_seg
            q0 = pl.multiple_of(t * QBLK, QBLK)
            seg0 = seg * seg_len
            k0 = pl.multiple_of(jnp.clip(q0 - SIDE, seg0, seg0 + seg_len - KWIN), SIDE)
            var = jnp.where(u == 0, 0, jnp.where(u == blocks_per_seg - 1, 2, 1))
            return seg, u, q0, k0, var

        def logits_stage(g, e_ref, m_ref, pi=pi):
            for jb in range(ATT_GROUP):
                _, _, q0, k0, var = placement(g * ATT_GROUP + jb)
                q = qp_ref[pl.ds(q0, QBLK), :]
                kw = kp_ref[pl.ds(k0, KWIN), :]
                zero = jnp.zeros_like(q)
                q_stack = jnp.concatenate([jnp.where(head0, q, zero), jnp.where(head0, zero, q)], axis=0)
                s = lax.dot_general(q_stack, kw, _NT, preferred_element_type=F32) + bias_ref[pi * 3 + var]
                m = jnp.max(s, axis=-1, keepdims=True)
                e_ref[jb] = jnp.exp(s - m).astype(BF16)
                m_ref[jb] = jnp.broadcast_to(m, (2 * QBLK, LANES))

        def value_stage(g, e_ref, m_ref, slot=slot, d=d, final=final):
            for jb in range(ATT_GROUP):
                seg, u, q0, k0, _ = placement(g * ATT_GROUP + jb)
                vw = vp_ref[pl.ds(k0, KWIN), :]
                one = jnp.ones_like(vw)
                pv0 = jnp.dot(e_ref[jb, :QBLK, :], jnp.where(head0_k, vw, one), preferred_element_type=F32)
                pv1 = jnp.dot(e_ref[jb, QBLK:, :], jnp.where(head0_k, one, vw), preferred_element_type=F32)
                num = jnp.where(head0, pv0, pv1)
                den = pltpu.roll(jnp.where(head0, pv1, pv0), HEAD_DIM, 1)
                o = num * (1.0 / den)
                lse = jnp.where(head0, m_ref[jb, :QBLK, :], m_ref[jb, QBLK:, :]) + jnp.log(den)
                dst = pl.ds(seg + d * (u * QBLK), QBLK, stride=d) if d > 1 else pl.ds(q0, QBLK)
                if not final:
                    o_ref[slot, dst, :] = o
                    lse_ref[slot, dst, :] = lse
                else:
                    lses = [lse_ref[i, dst, :] for i in range(slot)] + [lse]
                    outs = [o_ref[i, dst, :] for i in range(slot)] + [o]
                    mx = functools.reduce(jnp.maximum, lses)
                    ws = [jnp.exp(l - mx) for l in lses]
                    inv = 1.0 / functools.reduce(jnp.add, ws)
                    acc = functools.reduce(jnp.add, [(w * inv) * t for w, t in zip(ws, outs)])
                    out_ref[0, dst, :] = acc.astype(out_ref.dtype)

        n_groups = n_blocks // ATT_GROUP
        assert n_groups % 2 == 0 and n_groups >= 2
        logits_stage(0, e0_ref, m0_ref)

        def group_pair(i, carry):
            g = 2 * i
            logits_stage(g + 1, e1_ref, m1_ref)
            value_stage(g, e0_ref, m0_ref)
            logits_stage(g + 2, e0_ref, m0_ref)
            value_stage(g + 1, e1_ref, m1_ref)
            return carry

        lax.fori_loop(0, n_groups // 2 - 1, group_pair, 0)
        logits_stage(n_groups - 1, e1_ref, m1_ref)
        value_stage(n_groups - 2, e0_ref, m0_ref)
        value_stage(n_groups - 1, e1_ref, m1_ref)


def _dilated(att, rel_bias, n_pairs, b):
    s = att.shape[1] // b
    n_pat = len(DILATED_PATTERNS)
    blk = lambda off: pl.BlockSpec((1, s, LANES), lambda bi, pi: (off + pi, bi, 0))
    bucket_rows = jnp.asarray(_bucket_rows())
    return pl.pallas_call(
        functools.partial(_dilated_kernel, seq=s),
        grid=(b, n_pairs),
        in_specs=[
            pl.BlockSpec(memory_space=pltpu.SMEM),
            pl.BlockSpec((n_pat, BASE_W), lambda bi, pi: (0, 0)),
            blk(0), blk(n_pairs), blk(2 * n_pairs),
        ],
        out_specs=pl.BlockSpec((1, s, LANES), lambda bi, pi: (pi, bi, 0)),
        out_shape=jax.ShapeDtypeStruct((n_pairs, b * s, LANES), BF16),
        scratch_shapes=[
            pltpu.VMEM((n_pat * 3, 2 * QBLK, KWIN), F32),
            pltpu.VMEM((s, LANES), BF16),
            pltpu.VMEM((s, LANES), BF16),
            pltpu.VMEM((s, LANES), BF16),
            pltpu.VMEM((3, s, LANES), F32),
            pltpu.VMEM((n_pat - 1, s, LANES), F32),
            pltpu.VMEM((n_pat - 1, s, LANES), F32),
            pltpu.VMEM((ATT_GROUP, 2 * QBLK, KWIN), BF16),
            pltpu.VMEM((ATT_GROUP, 2 * QBLK, KWIN), BF16),
            pltpu.VMEM((ATT_GROUP, 2 * QBLK, LANES), F32),
            pltpu.VMEM((ATT_GROUP, 2 * QBLK, LANES), F32),
        ],
        compiler_params=_params("parallel", "parallel"),
        name="dilated",
    )(rel_bias, bucket_rows, att, att, att)


def _outproj_kernel(r_ref, a_ref, x_ref, w_ref, g_ref, wr_ref, x1_ref, h2_ref, aff_ref):
    n_e = aff_ref.shape[1]
    wr = wr_ref[...]
    w_hi = wr.astype(BF16)
    w_lo = (wr - w_hi.astype(F32)).astype(BF16)
    w_parts = jnp.where(lax.broadcasted_iota(I32, wr.shape, 1) < n_e, w_hi, w_lo)
    tm = x_ref.shape[0]
    mixed = jnp.concatenate([r_ref[i] for i in range(r_ref.shape[0])]
                            + [a_ref[i] for i in range(a_ref.shape[0])], axis=1)
    x1 = x_ref[...] + jnp.dot(mixed, w_ref[...], preferred_element_type=F32)
    x1_ref[...] = x1
    ms = jnp.mean(x1 * x1, axis=-1, keepdims=True)
    h2 = x1 * lax.rsqrt(ms + EPS) * g_ref[...]
    h_hi = h2.astype(BF16)
    h2_ref[...] = h_hi
    h_lo = (h2 - h_hi.astype(F32)).astype(BF16)
    prod = jnp.dot(jnp.concatenate([h_hi, h_lo], axis=0), w_parts, preferred_element_type=F32)
    logits = prod[:tm] + (pltpu.roll(prod[:tm], LANES - n_e, 1) + prod[tm:])
    expert_lane = lax.broadcasted_iota(I32, logits.shape, 1) < n_e
    logits = jnp.where(expert_lane, logits, NEG)
    m = jnp.max(logits, axis=1, keepdims=True)
    e = jnp.exp(logits - m)
    aff = e / jnp.sum(e, axis=1, keepdims=True)
    aff_ref[0] = aff.T[:n_e, :]


def _outproj(r, a, x, w_bf16, gain, w_router, tm=512):
    b, s, d = x.shape
    n_r, n_a = r.shape[0], a.shape[0]
    n_e = w_router.shape[1]
    assert 2 * n_e <= LANES
    w_router = jnp.pad(jnp.concatenate([w_router, w_router], axis=1), ((0, 0), (0, LANES - 2 * n_e)))
    spt = s // tm
    tok = lambda width: pl.BlockSpec((tm, width), lambda i: (i, 0))
    slab = lambda blocks: pl.BlockSpec((blocks, tm, LANES), lambda i: (0, i, 0))
    const = lambda shape: pl.BlockSpec(shape, lambda i: (0, 0))
    return pl.pallas_call(
        _outproj_kernel,
        grid=(b * spt,),
        in_specs=[slab(n_r), slab(n_a), tok(d), const(((n_r + n_a) * LANES, d)), const((1, d)), const((d, LANES))],
        out_specs=[tok(d), tok(d), pl.BlockSpec((1, n_e, tm), lambda i: (i // spt, 0, i % spt))],
        out_shape=[
            jax.ShapeDtypeStruct((b * s, d), F32),
            jax.ShapeDtypeStruct((b * s, d), BF16),
            jax.ShapeDtypeStruct((b, n_e, s), F32),
        ],
        compiler_params=_params("parallel"),
        name="outproj",
    )(r, a, x.reshape(b * s, d), w_bf16, gain, w_router)


def _select_kernel(aff_ref, posm_ref, cnt_ref, *, cap):
    rows, s = aff_ref.shape
    bits = pltpu.bitcast(aff_ref[...], I32)

    def search(i, thr):
        cand = thr | jnp.left_shift(jnp.int32(1), 30 - i)
        cnt = jnp.sum((bits >= cand).astype(I32), axis=1, keepdims=True)
        return jnp.where(cnt >= cap, cand, thr)

    thr = lax.fori_loop(0, 31, search, jnp.zeros((rows, 1), I32))
    gt = bits > thr
    eq = bits == thr
    need = cap - jnp.sum(gt.astype(I32), axis=1, keepdims=True)

    w = LANES
    tri = (lax.broadcasted_iota(I32, (w, w), 0) <= lax.broadcasted_iota(I32, (w, w), 1)).astype(BF16)

    def excl_prefix(flags):
        carry = jnp.zeros((rows, 1), F32)
        out = []
        for c0 in range(0, s, w):
            f = flags[:, c0:c0 + w].astype(BF16)
            inc = jnp.dot(f, tri, preferred_element_type=F32)
            out.append(inc - f.astype(F32) + carry)
            carry = carry + inc[:, w - 1:w]
        return jnp.concatenate(out, axis=1).astype(I32)

    sel = gt | (eq & (excl_prefix(eq) < need))
    posm_ref[...] = jnp.where(sel, excl_prefix(sel), -1)

    tok = lax.broadcasted_iota(I32, (w, w), 0)
    edge = lax.broadcasted_iota(I32, (w, w), 1) * MOE_TILE
    cnt = jnp.zeros((rows, w), F32)
    for c0 in range(0, s, w):
        before = ((tok + c0) < edge).astype(BF16)
        cnt = cnt + jnp.dot(sel[:, c0:c0 + w].astype(BF16), before, preferred_element_type=F32)
    cnt_ref[...] = cnt.astype(I32)


def _select(aff_rows, cap):
    rows, s = aff_rows.shape
    return pl.pallas_call(
        functools.partial(_select_kernel, cap=cap),
        out_shape=[jax.ShapeDtypeStruct((rows, s), I32), jax.ShapeDtypeStruct((rows, LANES), I32)],
        compiler_params=pltpu.CompilerParams(vmem_limit_bytes=VMEM_LIMIT),
        name="select",
    )(aff_rows)


def _slot_window(cnt_ref, row, j, cap):
    lo, hi = cnt_ref[row * CNT_STRIDE + j], cnt_ref[row * CNT_STRIDE + j + 1]
    w0 = pl.multiple_of(jnp.minimum((lo // SLOT_ALIGN) * SLOT_ALIGN, cap - SLOT_WIN), SLOT_ALIGN)
    return hi, w0


def _any_overflow(windows):
    return functools.reduce(jnp.logical_or, [hi > w0 + SLOT_WIN for hi, w0 in windows])


def _gather_kernel(cnt_ref, posm_ref, h2_ref, xg_ref):
    bi, j = pl.program_id(0), pl.program_id(1)
    n_e, cap = xg_ref.shape[1:3]
    tt = MOE_TILE

    @pl.when(j == 0)
    def _():
        xg_ref[...] = jnp.zeros_like(xg_ref)

    slot0 = lax.broadcasted_iota(I32, (SLOT_WIN, tt), 0)
    for sub in range(h2_ref.shape[1] // tt):
        tile = j * (h2_ref.shape[1] // tt) + sub
        tok = slice(sub * tt, (sub + 1) * tt)
        windows = [_slot_window(cnt_ref, bi * n_e + e, tile, cap) for e in range(n_e)]
        for e in range(0, n_e, WIN_GROUP):
            onehot = jnp.concatenate(
                [(posm_ref[0, e + i:e + i + 1, tok] == slot0 + windows[e + i][1]).astype(BF16)
                 for i in range(WIN_GROUP)], axis=0)
            rows = jnp.dot(onehot, h2_ref[0, tok, :], preferred_element_type=F32).astype(BF16)
            for i in range(WIN_GROUP):
                xg_ref[0, e + i, pl.ds(windows[e + i][1], SLOT_WIN), :] += rows[i * SLOT_WIN:(i + 1) * SLOT_WIN]

        @pl.when(_any_overflow(windows))
        def _(tile=tile, tok=tok):
            wide = lax.broadcasted_iota(I32, (MXU_DEPTH, tt), 0)

            def one_expert(e, carry):
                hi, w0 = _slot_window(cnt_ref, bi * n_e + e, tile, cap)

                @pl.when(hi > w0 + SLOT_WIN)
                def _():
                    pos = posm_ref[0, pl.ds(e, 1), tok]
                    for ws in range(0, cap, MXU_DEPTH):
                        slot = wide + ws
                        onehot = ((pos == slot) & (slot >= w0 + SLOT_WIN)).astype(BF16)
                        extra = jnp.dot(onehot, h2_ref[0, tok, :], preferred_element_type=F32)
                        xg_ref[0, e, ws:ws + MXU_DEPTH, :] += extra.astype(BF16)

                return carry

            lax.fori_loop(0, n_e, one_expert, 0)


def _gather(cnt, posm, h2, cap):
    b, n_e, s = posm.shape
    d = h2.shape[2]
    step = MOE_TILE * GATHER_TILES
    return pl.pallas_call(
        _gather_kernel,
        grid_spec=pltpu.PrefetchScalarGridSpec(
            num_scalar_prefetch=1,
            grid=(b, s // step),
            in_specs=[
                pl.BlockSpec((1, n_e, step), lambda bi, j, cnt: (bi, 0, j)),
                pl.BlockSpec((1, step, d), lambda bi, j, cnt: (bi, j, 0)),
            ],
            out_specs=pl.BlockSpec((1, n_e, cap, d), lambda bi, j, cnt: (bi, 0, 0, 0)),
        ),
        out_shape=jax.ShapeDtypeStruct((b, n_e, cap, d), BF16),
        compiler_params=_params("parallel", "arbitrary"),
        name="gather",
    )(cnt, posm, h2)


def _ffn_kernel(xg_ref, wg_hbm, wu_hbm, wd_hbm, y_ref, wg_buf, wu_buf, wd_buf, wg_stage, wu_stage, wd_stage,
                sem, acc_ref, *, tf):
    e, bi = pl.program_id(0), pl.program_id(1)
    n_e, n_b = pl.num_programs(0), pl.num_programs(1)
    rows_in, rows_dn = wg_stage.shape[1], wd_stage.shape[1]
    f_total = wg_buf.shape[2]

    def chunk_copies(expert, c):
        st = c % 2
        return (
            pltpu.make_async_copy(wg_hbm.at[expert, pl.ds(c * rows_in, rows_in), :], wg_stage.at[st], sem.at[0, st]),
            pltpu.make_async_copy(wu_hbm.at[expert, pl.ds(c * rows_in, rows_in), :], wu_stage.at[st], sem.at[1, st]),
            pltpu.make_async_copy(wd_hbm.at[expert, pl.ds(c * rows_dn, rows_dn), :], wd_stage.at[st], sem.at[2, st]),
        )

    def finish_chunk(expert, c):
        for cp in chunk_copies(expert, c):
            cp.wait()
        slot, st = expert % 2, c % 2
        wg_buf[slot, pl.ds(c * rows_in, rows_in), :] = wg_stage[st].astype(BF16)
        wu_buf[slot, pl.ds(c * rows_in, rows_in), :] = wu_stage[st].astype(BF16)
        wd_buf[slot, pl.ds(c * rows_dn, rows_dn), :] = wd_stage[st].astype(BF16)

    @pl.when((e == 0) & (bi == 0))
    def _():
        for cp in chunk_copies(0, 0):
            cp.start()

        def load_first(c, carry):
            @pl.when(c + 1 < n_b)
            def _():
                for cp in chunk_copies(0, c + 1):
                    cp.start()

            finish_chunk(0, c)
            return carry

        lax.fori_loop(0, n_b, load_first, 0)

    pending = jnp.where(bi > 0, e + 1 < n_e, e > 0)

    @pl.when(pending)
    def _():
        finish_chunk(jnp.where(bi > 0, e + 1, e), jnp.where(bi > 0, bi - 1, n_b - 1))

    @pl.when(e + 1 < n_e)
    def _():
        for cp in chunk_copies(e + 1, bi):
            cp.start()

    def swiglu(slot):
        xg = xg_ref[0, 0]
        for fi in range(f_total // tf):
            fs = slice(fi * tf, (fi + 1) * tf)
            gate = jnp.dot(xg, wg_buf[slot, :, fs], preferred_element_type=F32)
            up = jnp.dot(xg, wu_buf[slot, :, fs], preferred_element_type=F32)
            hid = ((gate * jax.nn.sigmoid(gate)) * up).astype(BF16)
            part = jnp.dot(hid, wd_buf[slot, fs, :], preferred_element_type=F32)
            if fi == 0:
                acc_ref[...] = part
            else:
                acc_ref[...] += part
        y_ref[0, 0] = acc_ref[...].astype(y_ref.dtype)

    for slot in range(2):
        pl.when(e % 2 == slot)(functools.partial(swiglu, slot))


def _ffn(xg, wg, wu, wd, tf=512):
    b, n_e, cap, d = xg.shape
    f = wg.shape[2]
    assert d % b == 0 and f % b == 0 and (d // b) % SLOT_ALIGN == 0, "one weight row chunk per batch step"
    tok = pl.BlockSpec((1, 1, cap, d), lambda e, bi: (bi, e, 0, 0))
    hbm = pl.BlockSpec(memory_space=pl.ANY)
    return pl.pallas_call(
        functools.partial(_ffn_kernel, tf=tf),
        grid=(n_e, b),
        in_specs=[tok, hbm, hbm, hbm],
        out_specs=tok,
        out_shape=jax.ShapeDtypeStruct((b, n_e, cap, d), BF16),
        scratch_shapes=[
            pltpu.VMEM((2, d, f), BF16), pltpu.VMEM((2, d, f), BF16), pltpu.VMEM((2, f, d), BF16),
            pltpu.VMEM((2, d // b, f), F32), pltpu.VMEM((2, d // b, f), F32), pltpu.VMEM((2, f // b, d), F32),
            pltpu.SemaphoreType.DMA((3, 2)),
            pltpu.VMEM((cap, d), F32),
        ],
        compiler_params=_params("arbitrary", "arbitrary"),
        name="ffn",
    )(xg, wg, wu, wd)


def _combine_kernel(cnt_ref, pos_ref, gate_ref, y_ref, x1_ref, g_ref, out_ref, acc_ref):
    bi, j = pl.program_id(0), pl.program_id(1)
    tt = MOE_TILE
    n_e, cap = y_ref.shape[1:3]
    slot0 = lax.broadcasted_iota(I32, (SLOT_WIN, tt), 0)
    for sub in range(x1_ref.shape[1] // tt):
        tile = j * (x1_ref.shape[1] // tt) + sub
        tok = slice(sub * tt, (sub + 1) * tt)
        windows = [_slot_window(cnt_ref, bi * n_e + e, tile, cap) for e in range(n_e)]
        acc = x1_ref[0, tok, :]
        for e in range(0, n_e, WIN_GROUP):
            scatter_t = jnp.concatenate(
                [jnp.where(pos_ref[0, e + i:e + i + 1, tok] == slot0 + windows[e + i][1],
                           gate_ref[0, e + i:e + i + 1, tok], 0.0).astype(BF16) for i in range(WIN_GROUP)], axis=0)
            y_group = jnp.concatenate(
                [y_ref[0, e + i, pl.ds(windows[e + i][1], SLOT_WIN), :] for i in range(WIN_GROUP)], axis=0)
            acc = acc + lax.dot_general(scatter_t, y_group, _TN, preferred_element_type=F32)
        acc_ref[...] = acc

        @pl.when(_any_overflow(windows))
        def _(tile=tile, tok=tok):
            wide = lax.broadcasted_iota(I32, (MXU_DEPTH, tt), 0)

            def one_expert(e, carry):
                hi, w0 = _slot_window(cnt_ref, bi * n_e + e, tile, cap)

                @pl.when(hi > w0 + SLOT_WIN)
                def _():
                    pos = pos_ref[0, pl.ds(e, 1), tok]
                    gate = gate_ref[0, pl.ds(e, 1), tok]
                    for ws in range(0, cap, MXU_DEPTH):
                        slot = wide + ws
                        extra = jnp.where((pos == slot) & (slot >= w0 + SLOT_WIN), gate, 0.0).astype(BF16)
                        acc_ref[...] += lax.dot_general(extra, y_ref[0, e, ws:ws + MXU_DEPTH, :], _TN,
                                                        preferred_element_type=F32)

                return carry

            lax.fori_loop(0, n_e, one_expert, 0)

        acc = acc_ref[...]
        ms = jnp.mean(acc * acc, axis=-1, keepdims=True)
        out_ref[0, tok, :] = acc * lax.rsqrt(ms + EPS) * g_ref[...]


def _combine(cnt, posm, aff, y, x1, gain):
    b, s, d = x1.shape
    n_e, cap = y.shape[1:3]
    step = MOE_TILE * COMBINE_TILES
    return pl.pallas_call(
        _combine_kernel,
        grid_spec=pltpu.PrefetchScalarGridSpec(
            num_scalar_prefetch=1,
            grid=(b, s // step),
            in_specs=[
                pl.BlockSpec((1, n_e, step), lambda bi, j, cnt: (bi, 0, j)),
                pl.BlockSpec((1, n_e, step), lambda bi, j, cnt: (bi, 0, j)),
                pl.BlockSpec((1, n_e, cap, d), lambda bi, j, cnt: (bi, 0, 0, 0)),
                pl.BlockSpec((1, step, d), lambda bi, j, cnt: (bi, j, 0)),
                pl.BlockSpec((1, d), lambda bi, j, cnt: (0, 0)),
            ],
            out_specs=pl.BlockSpec((1, step, d), lambda bi, j, cnt: (bi, j, 0)),
            scratch_shapes=[pltpu.VMEM((MOE_TILE, d), F32)],
        ),
        out_shape=jax.ShapeDtypeStruct((b, s, d), F32),
        compiler_params=_params("parallel", "parallel"),
        name="combine",
    )(cnt, posm, aff, y, x1, gain)


def _rope_tables(seq):
    half = HEAD_DIM // 2
    pos = jnp.arange(seq, dtype=F32)
    inv = ROPE_BASE ** (-jnp.arange(0, HEAD_DIM, 2, dtype=F32) / HEAD_DIM)
    ang = pos[:, None] * inv[None, :]
    cos, sin = jnp.cos(ang), jnp.sin(ang)
    reps = LANES // HEAD_DIM
    cos_t = jnp.tile(jnp.concatenate([cos, cos], axis=1), (1, reps))
    sin_t = jnp.tile(jnp.concatenate([-sin, sin], axis=1), (1, reps))
    return cos_t, sin_t


def kernel(x, norm1_gain, w_in, ret_log_decay, ret_gn_gain, rel_bias, w_out, norm2_gain, w_router,
           w_gate, w_up, w_down, final_gain):
    b, s, d = x.shape
    depth = w_in.shape[0]
    ret_width = ret_gn_gain.shape[1]
    n_ret_pairs = ret_width // LANES
    att_width = (w_in.shape[2] - 4 * ret_width) // 3
    n_att_pairs = att_width // LANES
    cap = CAPACITY_FACTOR * s // N_EXPERTS
    cos_t, sin_t = _rope_tables(s)

    assert depth == 1, "single-layer block: the final norm is fused into the combine kernel"
    layer = 0
    ret, att = _proj(x.reshape(b * s, d), norm1_gain[layer][None, :], w_in[layer].astype(BF16),
                     cos_t, sin_t, n_ret=4 * ret_width, ret_width=ret_width)
    r = _retention(ret, ret_log_decay[layer], ret_gn_gain[layer][None, :], n_ret_pairs, b)
    a = _dilated(att, rel_bias, n_att_pairs, b)
    x1, h2, aff = _outproj(r, a, x, w_out[layer].astype(BF16), norm2_gain[layer][None, :],
                           w_router[layer])
    assert s % (MOE_TILE * max(GATHER_TILES, COMBINE_TILES)) == 0 and s // MOE_TILE < CNT_STRIDE and cap % MXU_DEPTH == 0 and N_EXPERTS % WIN_GROUP == 0
    posm, cnt = _select(aff.reshape(b * N_EXPERTS, s), cap)
    posm = posm.reshape(b, N_EXPERTS, s)
    cnt = cnt[:, :CNT_STRIDE].reshape(-1)
    xg = _gather(cnt, posm, h2.reshape(b, s, d), cap)
    y = _ffn(xg, w_gate[layer], w_up[layer], w_down[layer])
    return _combine(cnt, posm, aff, y, x1.reshape(b, s, d),
                    final_gain[None, :])
```

```python
import functools
import math

import numpy as np
import jax
import jax.numpy as jnp
from jax import lax
from jax.experimental import pallas as pl
from jax.experimental.pallas import tpu as pltpu

F32 = jnp.float32
BF16 = jnp.bfloat16
I32 = jnp.int32

HEAD_DIM = 64
LANES = 128
RET_CHUNK = 128
RET_GROUP = 4
ROPE_BASE = 10000.0
DILATED_PATTERNS = ((128, 1), (512, 4), (2048, 16))
SIDE = 64
N_BUCKETS = 32
MAX_DISTANCE = 1024
N_EXPERTS = 16
CAPACITY_FACTOR = 2
MOE_TILE = 256
GATHER_TILES = 8
COMBINE_TILES = 4
MXU_DEPTH = 256
WIN_GROUP = 4
SLOT_WIN = MXU_DEPTH // WIN_GROUP
SLOT_ALIGN = 16
CNT_STRIDE = 32
EPS = 1e-6
NEG = -1e30
VMEM_LIMIT = 56 * 1024 * 1024

_NT = (((1,), (1,)), ((), ()))
_TN = (((0,), (0,)), ((), ()))


def _params(*sem):
    return pltpu.CompilerParams(dimension_semantics=sem, vmem_limit_bytes=VMEM_LIMIT)


def _proj_kernel(x_ref, g_ref, w_ref, cos_ref, sin_ref, ret_ref, att_ref, *, n_ret, ret_width, chunk):
    x = x_ref[...]
    ms = jnp.mean(x * x, axis=-1, keepdims=True)
    h = (x * lax.rsqrt(ms + EPS) * g_ref[...]).astype(BF16)
    n_cols = w_ref.shape[1]
    cos, sin = cos_ref[...], sin_ref[...]
    first_half = (lax.broadcasted_iota(I32, cos.shape, 1) % HEAD_DIM) < (HEAD_DIM // 2)
    for c0 in range(0, n_cols, chunk):
        o = jnp.dot(h, w_ref[:, c0:c0 + chunk], preferred_element_type=F32)
        for l0 in range(0, chunk, LANES):
            t = o[:, l0:l0 + LANES]
            col = c0 + l0
            if col >= n_ret:
                att_ref[(col - n_ret) // LANES] = t
            elif col >= 2 * ret_width:
                ret_ref[col // LANES] = t.astype(BF16)
            else:
                scale = 1.0 if col < ret_width else HEAD_DIM ** -0.5
                swapped = jnp.where(first_half, pltpu.roll(t, LANES - HEAD_DIM // 2, 1),
                                    pltpu.roll(t, HEAD_DIM // 2, 1))
                ret_ref[col // LANES] = ((t * cos + swapped * sin) * scale).astype(BF16)


def _proj(x2d, gain, w_bf16, cos_t, sin_t, n_ret, ret_width, tm=512):
    t, d = x2d.shape
    n_cols = w_bf16.shape[1]
    n_att = n_cols - n_ret
    seq_tiles = cos_t.shape[0] // tm
    assert ret_width % 512 == 0 and n_ret % 512 == 0 and n_cols % 512 == 0
    return pl.pallas_call(
        functools.partial(_proj_kernel, n_ret=n_ret, ret_width=ret_width, chunk=512),
        grid=(t // tm,),
        in_specs=[
            pl.BlockSpec((tm, d), lambda i: (i, 0)),
            pl.BlockSpec((1, d), lambda i: (0, 0)),
            pl.BlockSpec((d, n_cols), lambda i: (0, 0)),
            pl.BlockSpec((tm, LANES), lambda i: (i % seq_tiles, 0)),
            pl.BlockSpec((tm, LANES), lambda i: (i % seq_tiles, 0)),
        ],
        out_specs=[
            pl.BlockSpec((n_ret // LANES, tm, LANES), lambda i: (0, i, 0)),
            pl.BlockSpec((n_att // LANES, tm, LANES), lambda i: (0, i, 0)),
        ],
        out_shape=[
            jax.ShapeDtypeStruct((n_ret // LANES, t, LANES), BF16),
            jax.ShapeDtypeStruct((n_att // LANES, t, LANES), F32),
        ],
        compiler_params=_params("parallel"),
        name="proj",
    )(x2d, gain, w_bf16, cos_t, sin_t)


def _retention_kernel(decay_ref, q_ref, k_ref, v_ref, g_ref, gn_ref, out_ref,
                      rf_ref, rb_ref, kvf_ref, kvb_ref, p0_ref, p1_ref, *, seq):
    c = RET_CHUNK
    n_chunks = seq // c
    p = pl.program_id(1)
    lane = lax.broadcasted_iota(I32, (c, LANES), 1)
    row = lax.broadcasted_iota(I32, (c, LANES), 0)
    head0 = lane < HEAD_DIM
    rowf = row.astype(F32)

    lgf0, lgf1 = decay_ref[0, 2 * p], decay_ref[0, 2 * p + 1]
    lgb0, lgb1 = decay_ref[1, 2 * p], decay_ref[1, 2 * p + 1]
    lgf_lane = jnp.where(head0, lgf0, lgf1)
    lgb_lane = jnp.where(head0, lgb0, lgb1)
    lgf_row = jnp.where(row < HEAD_DIM, lgf0, lgf1)
    lgb_row = jnp.where(row < HEAD_DIM, lgb0, lgb1)
    same_head = (row < HEAD_DIM) == head0

    zeta_f = jnp.exp((c - 1 - rowf) * lgf_lane)
    zeta_b = jnp.exp(rowf * lgb_lane)
    xi_f = jnp.exp((rowf + 1.0) * lgf_lane)
    xi_b = jnp.exp((c - rowf) * lgb_lane)
    gch_f = jnp.where(same_head, jnp.exp(c * lgf_row), 0.0)
    gch_b = jnp.where(same_head, jnp.exp(c * lgb_row), 0.0)

    diff = (row - lane).astype(F32)

    def dmat(lf, lb):
        return jnp.exp(jnp.where(diff >= 0, diff * lf, -diff * lb))

    d_stack = jnp.concatenate([dmat(lgf0, lgb0), dmat(lgf1, lgb1)], axis=0)

    def kv_step(n, carry):
        sl = pl.ds(pl.multiple_of(n * c, c), c)
        k = k_ref[0, sl, :].astype(F32)
        v = v_ref[0, sl, :]
        kvf = lax.dot_general((k * zeta_f).astype(BF16), v, _TN, preferred_element_type=F32)
        kvb = lax.dot_general((k * zeta_b).astype(BF16), v, _TN, preferred_element_type=F32)
        kvf_ref[n] = jnp.where(same_head, kvf, 0.0)
        kvb_ref[n] = jnp.where(same_head, kvb, 0.0)
        return carry

    lax.fori_loop(0, n_chunks, kv_step, 0, unroll=8)

    def scan_step(i, states):
        sf, sb = states
        nb = n_chunks - 1 - i
        rf_ref[i] = sf.astype(BF16)
        rb_ref[nb] = sb.astype(BF16)
        return gch_f * sf + kvf_ref[i], gch_b * sb + kvb_ref[nb]

    zero_state = jnp.zeros((LANES, LANES), F32)
    lax.fori_loop(0, n_chunks, scan_step, (zero_state, zero_state), unroll=4)

    gn_gain = gn_ref[...]

    def score_stage(g, p_ref):
        for jc in range(RET_GROUP):
            sl = pl.ds(pl.multiple_of((g * RET_GROUP + jc) * c, c), c)
            q = q_ref[0, sl, :]
            zero = jnp.zeros_like(q)
            q_stack = jnp.concatenate([jnp.where(head0, q, zero), jnp.where(head0, zero, q)], axis=0)
            scores = lax.dot_general(q_stack, k_ref[0, sl, :], _NT, preferred_element_type=F32) * d_stack
            sb = scores.astype(BF16)
            p_ref[jc] = jnp.concatenate([sb[:c], sb[c:]], axis=1)

    def half_mean(t):
        s0 = jnp.sum(jnp.where(head0, t, 0.0), axis=-1, keepdims=True)
        s1 = jnp.sum(jnp.where(head0, 0.0, t), axis=-1, keepdims=True)
        return jnp.where(head0, s0, s1) * (1.0 / HEAD_DIM)

    def value_stage(g, p_ref):
        for jc in range(RET_GROUP):
            n = g * RET_GROUP + jc
            sl = pl.ds(pl.multiple_of(n * c, c), c)
            q = q_ref[0, sl, :]
            v = v_ref[0, sl, :]
            zero = jnp.zeros_like(v)
            v_stack = jnp.concatenate([jnp.where(head0, v, zero), jnp.where(head0, zero, v)], axis=0)
            inner = jnp.dot(p_ref[jc], v_stack, preferred_element_type=F32)
            cross_f = jnp.dot(q, rf_ref[n], preferred_element_type=F32) * xi_f
            cross_b = jnp.dot(q, rb_ref[n], preferred_element_type=F32) * xi_b
            r = inner + cross_f + cross_b
            dlt = r - half_mean(r)
            var = half_mean(dlt * dlt)
            gate = g_ref[0, sl, :].astype(F32)
            y = dlt * lax.rsqrt(var + EPS) * gn_gain * (gate * jax.nn.sigmoid(gate))
            out_ref[0, sl, :] = y.astype(out_ref.dtype)

    n_groups = n_chunks // RET_GROUP
    assert n_groups % 2 == 0 and n_groups >= 2
    score_stage(0, p0_ref)

    def group_pair(i, carry):
        g = 2 * i
        score_stage(g + 1, p1_ref)
        value_stage(g, p0_ref)
        score_stage(g + 2, p0_ref)
        value_stage(g + 1, p1_ref)
        return carry

    lax.fori_loop(0, n_groups // 2 - 1, group_pair, 0)
    score_stage(n_groups - 1, p1_ref)
    value_stage(n_groups - 2, p0_ref)
    value_stage(n_groups - 1, p1_ref)


def _retention(ret, decay, gn_gain, n_pairs, b):
    s = ret.shape[1] // b
    blk = lambda off: pl.BlockSpec((1, s, LANES), lambda bi, pi: (off + pi, bi, 0))
    return pl.pallas_call(
        functools.partial(_retention_kernel, seq=s),
        grid=(b, n_pairs),
        in_specs=[
            pl.BlockSpec(memory_space=pltpu.SMEM),
            blk(0), blk(n_pairs), blk(2 * n_pairs), blk(3 * n_pairs),
            pl.BlockSpec((1, LANES), lambda bi, pi: (0, pi)),
        ],
        out_specs=pl.BlockSpec((1, s, LANES), lambda bi, pi: (pi, bi, 0)),
        out_shape=jax.ShapeDtypeStruct((n_pairs, b * s, LANES), BF16),
        scratch_shapes=[
            pltpu.VMEM((s // RET_CHUNK, LANES, LANES), BF16),
            pltpu.VMEM((s // RET_CHUNK, LANES, LANES), BF16),
            pltpu.VMEM((s // RET_CHUNK, LANES, LANES), F32),
            pltpu.VMEM((s // RET_CHUNK, LANES, LANES), F32),
            pltpu.VMEM((RET_GROUP, RET_CHUNK, 2 * RET_CHUNK), BF16),
            pltpu.VMEM((RET_GROUP, RET_CHUNK, 2 * RET_CHUNK), BF16),
        ],
        compiler_params=_params("parallel", "parallel"),
        name="retention",
    )(decay, ret, ret, ret, ret, gn_gain)


QBLK = 2 * SIDE
KWIN = 4 * SIDE
BASE_W = 512
ATT_GROUP = 4


def _t5_bucket_np(rel):
    half = N_BUCKETS // 2
    max_exact = half // 2
    bucket = np.where(rel > 0, half, 0)
    n = np.abs(rel)
    nf = np.maximum(n, 1).astype(np.float32)
    large = max_exact + (np.log(nf / np.float32(max_exact)) / np.float32(math.log(MAX_DISTANCE / max_exact))
                         * np.float32(half - max_exact)).astype(np.int32)
    large = np.minimum(large, half - 1)
    return (bucket + np.where(n < max_exact, n, large)).astype(np.int32)


def _bucket_rows():
    k = np.arange(BASE_W)
    off = k - SIDE
    rows = []
    for _, dilation in DILATED_PATTERNS:
        rows.append(np.where(k <= 2 * SIDE, _t5_bucket_np(off * dilation), -1))
    return np.stack(rows).astype(np.int32)


def _dilated_kernel(bias_tab_ref, bucket_ref, qkv_ref, out_ref,
                    bias_ref, perm_ref, stage_ref, o_ref, lse_ref, e0_ref, e1_ref, m0_ref, m1_ref, *, seq):
    order = list(range(1, len(DILATED_PATTERNS))) + [0]
    dil = [DILATED_PATTERNS[pi][1] for pi in order]
    n_slots = len(order)
    assert n_slots == 3 and dil[0] > 1 and dil[1] % dil[0] == 0 and dil[2] == 1
    p = pl.program_id(1)
    lane = lax.broadcasted_iota(I32, (QBLK, LANES), 1)
    head0 = lane < HEAD_DIM
    head0_k = lax.broadcasted_iota(I32, (KWIN, LANES), 1) < HEAD_DIM
    n_groups = seq // QBLK // ATT_GROUP
    n_loop = n_groups // 2 - 1
    assert n_groups % 2 == 0 and n_loop >= 0

    for pi in range(len(DILATED_PATTERNS)):
        bucket = jnp.broadcast_to(bucket_ref[pi:pi + 1, :], (QBLK, BASE_W))
        for hh in range(2):
            base = jnp.full((QBLK, BASE_W), NEG, F32)
            for bk in range(N_BUCKETS):
                base = jnp.where(bucket == bk, bias_tab_ref[bk, 2 * p + hh], base)
            for var, shift in enumerate((BASE_W - SIDE, 0, SIDE)):
                tile = pltpu.roll(base, shift, 1, stride=1, stride_axis=0)
                bias_ref[pi * 3 + var, hh * QBLK:(hh + 1) * QBLK, :] = tile[:, :KWIN]

    def relayout(slot, i):
        d = dil[slot]
        seg_len = seq // d
        scale = jnp.where(i == 0, HEAD_DIM ** -0.5, 1.0).astype(F32)
        if d == 1:
            perm_ref[slot, i] = (qkv_ref[i, 0] * scale).astype(BF16)
        elif slot == 0:
            for r in range(d):
                val = qkv_ref[i, 0, pl.ds(r, seg_len, stride=d), :]
                stage_ref[i, r * seg_len:(r + 1) * seg_len, :] = val
                perm_ref[slot, i, r * seg_len:(r + 1) * seg_len, :] = (val * scale).astype(BF16)
        else:
            prev_d = dil[slot - 1]
            step, prev_len = d // prev_d, seq // prev_d
            for r_prev in range(prev_d):
                for r_step in range(step):
                    val = stage_ref[i, pl.ds(r_prev * prev_len + r_step, seg_len, stride=step), :]
                    r = r_prev + prev_d * r_step
                    perm_ref[slot, i, r * seg_len:(r + 1) * seg_len, :] = (val * scale).astype(BF16)

    def placement(slot, t):
        seg_len = seq // dil[slot]
        blocks_per_seg = seg_len // QBLK
        seg = t // blocks_per_seg
        u = t % blocks_per_seg
        q0 = pl.multiple_of(t * QBLK, QBLK)
        seg0 = seg * seg_len
        k0 = pl.multiple_of(jnp.clip(q0 - SIDE, seg0, seg0 + seg_len - KWIN), SIDE)
        var = jnp.where(u == 0, 0, jnp.where(u == blocks_per_seg - 1, 2, 1))
        return seg, u, q0, k0, var

    def logits_stage(slot, g, e_ref, m_ref):
        for jb in range(ATT_GROUP):
            _, _, q0, k0, var = placement(slot, g * ATT_GROUP + jb)
            q = perm_ref[slot, 0, pl.ds(q0, QBLK), :]
            kw = perm_ref[slot, 1, pl.ds(k0, KWIN), :]
            zero = jnp.zeros_like(q)
            q_stack = jnp.concatenate([jnp.where(head0, q, zero), jnp.where(head0, zero, q)], axis=0)
            s = lax.dot_general(q_stack, kw, _NT, preferred_element_type=F32) + bias_ref[order[slot] * 3 + var]
            m = jnp.max(s, axis=-1, keepdims=True)
            e_ref[jb] = jnp.exp(s - m).astype(BF16)
            m_ref[jb] = jnp.broadcast_to(m, (2 * QBLK, LANES))

    def value_stage(slot, g, e_ref, m_ref):
        d = dil[slot]
        for jb in range(ATT_GROUP):
            seg, u, q0, k0, _ = placement(slot, g * ATT_GROUP + jb)
            vw = perm_ref[slot, 2, pl.ds(k0, KWIN), :]
            one = jnp.ones_like(vw)
            pv0 = jnp.dot(e_ref[jb, :QBLK, :], jnp.where(head0_k, vw, one), preferred_element_type=F32)
            pv1 = jnp.dot(e_ref[jb, QBLK:, :], jnp.where(head0_k, one, vw), preferred_element_type=F32)
            num = jnp.where(head0, pv0, pv1)
            den = pltpu.roll(jnp.where(head0, pv1, pv0), HEAD_DIM, 1)
            o = num * (1.0 / den)
            lse = jnp.where(head0, m_ref[jb, :QBLK, :], m_ref[jb, QBLK:, :]) + jnp.log(den)
            dst = pl.ds(seg + d * (u * QBLK), QBLK, stride=d) if d > 1 else pl.ds(q0, QBLK)
            if slot < n_slots - 1:
                o_ref[slot, dst, :] = o
                lse_ref[slot, dst, :] = lse
            else:
                lses = [lse_ref[i, dst, :] for i in range(slot)] + [lse]
                outs = [o_ref[i, dst, :] for i in range(slot)] + [o]
                mx = functools.reduce(jnp.maximum, lses)
                ws = [jnp.exp(l - mx) for l in lses]
                inv = 1.0 / functools.reduce(jnp.add, ws)
                acc = functools.reduce(jnp.add, [(w * inv) * t for w, t in zip(ws, outs)])
                out_ref[0, dst, :] = acc.astype(out_ref.dtype)

    for slot in range(n_slots):
        for i in range(3):
            relayout(slot, i)
    logits_stage(0, 0, e0_ref, m0_ref)
    for slot in range(n_slots):
        nxt = slot + 1 if slot + 1 < n_slots else None

        def group_pair(i, carry, slot=slot):
            g = 2 * i
            logits_stage(slot, g + 1, e1_ref, m1_ref)
            value_stage(slot, g, e0_ref, m0_ref)
            logits_stage(slot, g + 2, e0_ref, m0_ref)
            value_stage(slot, g + 1, e1_ref, m1_ref)
            return carry

        for i in range(n_loop):
            group_pair(i, 0)
        logits_stage(slot, n_groups - 1, e1_ref, m1_ref)
        value_stage(slot, n_groups - 2, e0_ref, m0_ref)
        if nxt is not None:
            logits_stage(nxt, 0, e0_ref, m0_ref)
        value_stage(slot, n_groups - 1, e1_ref, m1_ref)


def _dilated(att, rel_bias, n_pairs, b):
    s = att.shape[1] // b
    n_pat = len(DILATED_PATTERNS)
    bucket_rows = jnp.asarray(_bucket_rows())
    return pl.pallas_call(
        functools.partial(_dilated_kernel, seq=s),
        grid=(b, n_pairs),
        in_specs=[
            pl.BlockSpec(memory_space=pltpu.SMEM),
            pl.BlockSpec((n_pat, BASE_W), lambda bi, pi: (0, 0)),
            pl.BlockSpec((3, 1, s, LANES), lambda bi, pi: (0, pi, bi, 0)),
        ],
        out_specs=pl.BlockSpec((1, s, LANES), lambda bi, pi: (pi, bi, 0)),
        out_shape=jax.ShapeDtypeStruct((n_pairs, b * s, LANES), BF16),
        scratch_shapes=[
            pltpu.VMEM((n_pat * 3, 2 * QBLK, KWIN), F32),
            pltpu.VMEM((n_pat, 3, s, LANES), BF16),
            pltpu.VMEM((3, s, LANES), F32),
            pltpu.VMEM((n_pat - 1, s, LANES), F32),
            pltpu.VMEM((n_pat - 1, s, LANES), F32),
            pltpu.VMEM((ATT_GROUP, 2 * QBLK, KWIN), BF16),
            pltpu.VMEM((ATT_GROUP, 2 * QBLK, KWIN), BF16),
            pltpu.VMEM((ATT_GROUP, 2 * QBLK, LANES), F32),
            pltpu.VMEM((ATT_GROUP, 2 * QBLK, LANES), F32),
        ],
        compiler_params=_params("parallel", "parallel"),
        name="dilated",
    )(rel_bias, bucket_rows, att.reshape(3, n_pairs, b * s, LANES))


def _outproj_kernel(r_ref, a_ref, x_ref, w_ref, g_ref, wr_ref, x1_ref, h2_ref, aff_ref):
    n_e = aff_ref.shape[1]
    wr = wr_ref[...]
    w_hi = wr.astype(BF16)
    w_lo = (wr - w_hi.astype(F32)).astype(BF16)
    w_parts = jnp.where(lax.broadcasted_iota(I32, wr.shape, 1) < n_e, w_hi, w_lo)
    tm = x_ref.shape[0]
    mixed = jnp.concatenate([r_ref[i] for i in range(r_ref.shape[0])]
                            + [a_ref[i] for i in range(a_ref.shape[0])], axis=1)
    x1 = x_ref[...] + jnp.dot(mixed, w_ref[...], preferred_element_type=F32)
    x1_ref[...] = x1
    ms = jnp.mean(x1 * x1, axis=-1, keepdims=True)
    h2 = x1 * lax.rsqrt(ms + EPS) * g_ref[...]
    h_hi = h2.astype(BF16)
    h2_ref[...] = h_hi
    h_lo = (h2 - h_hi.astype(F32)).astype(BF16)
    prod = jnp.dot(jnp.concatenate([h_hi, h_lo], axis=0), w_parts, preferred_element_type=F32)
    logits = prod[:tm] + (pltpu.roll(prod[:tm], LANES - n_e, 1) + prod[tm:])
    expert_lane = lax.broadcasted_iota(I32, logits.shape, 1) < n_e
    logits = jnp.where(expert_lane, logits, NEG)
    m = jnp.max(logits, axis=1, keepdims=True)
    e = jnp.exp(logits - m)
    aff = e / jnp.sum(e, axis=1, keepdims=True)
    aff_ref[0] = aff.T[:n_e, :]


def _outproj(r, a, x, w_bf16, gain, w_router, tm=512):
    b, s, d = x.shape
    n_r, n_a = r.shape[0], a.shape[0]
    n_e = w_router.shape[1]
    assert 2 * n_e <= LANES
    w_router = jnp.pad(jnp.concatenate([w_router, w_router], axis=1), ((0, 0), (0, LANES - 2 * n_e)))
    spt = s // tm
    tok = lambda width: pl.BlockSpec((tm, width), lambda i: (i, 0))
    slab = lambda blocks: pl.BlockSpec((blocks, tm, LANES), lambda i: (0, i, 0))
    const = lambda shape: pl.BlockSpec(shape, lambda i: (0, 0))
    return pl.pallas_call(
        _outproj_kernel,
        grid=(b * spt,),
        in_specs=[slab(n_r), slab(n_a), tok(d), const(((n_r + n_a) * LANES, d)), const((1, d)), const((d, LANES))],
        out_specs=[tok(d), tok(d), pl.BlockSpec((1, n_e, tm), lambda i: (i // spt, 0, i % spt))],
        out_shape=[
            jax.ShapeDtypeStruct((b * s, d), F32),
            jax.ShapeDtypeStruct((b * s, d), BF16),
            jax.ShapeDtypeStruct((b, n_e, s), F32),
        ],
        compiler_params=_params("parallel"),
        name="outproj",
    )(r, a, x.reshape(b * s, d), w_bf16, gain, w_router)


def _select_kernel(aff_ref, posm_ref, cnt_ref, *, cap):
    rows, s = aff_ref.shape
    bits = pltpu.bitcast(aff_ref[...], I32)

    def search(i, thr):
        cand = thr | jnp.left_shift(jnp.int32(1), 30 - i)
        cnt = jnp.sum((bits >= cand).astype(I32), axis=1, keepdims=True)
        return jnp.where(cnt >= cap, cand, thr)

    thr = lax.fori_loop(0, 31, search, jnp.zeros((rows, 1), I32))
    gt = bits > thr
    eq = bits == thr
    need = cap - jnp.sum(gt.astype(I32), axis=1, keepdims=True)

    w = LANES
    tri = (lax.broadcasted_iota(I32, (w, w), 0) <= lax.broadcasted_iota(I32, (w, w), 1)).astype(BF16)

    def excl_prefix(flags):
        carry = jnp.zeros((rows, 1), F32)
        out = []
        for c0 in range(0, s, w):
            f = flags[:, c0:c0 + w].astype(BF16)
            inc = jnp.dot(f, tri, preferred_element_type=F32)
            out.append(inc - f.astype(F32) + carry)
            carry = carry + inc[:, w - 1:w]
        return jnp.concatenate(out, axis=1).astype(I32)

    sel = gt | (eq & (excl_prefix(eq) < need))
    posm_ref[...] = jnp.where(sel, excl_prefix(sel), -1)

    tok = lax.broadcasted_iota(I32, (w, w), 0)
    edge = lax.broadcasted_iota(I32, (w, w), 1) * MOE_TILE
    cnt = jnp.zeros((rows, w), F32)
    for c0 in range(0, s, w):
        before = ((tok + c0) < edge).astype(BF16)
        cnt = cnt + jnp.dot(sel[:, c0:c0 + w].astype(BF16), before, preferred_element_type=F32)
    cnt_ref[...] = cnt.astype(I32)


def _select(aff_rows, cap):
    rows, s = aff_rows.shape
    return pl.pallas_call(
        functools.partial(_select_kernel, cap=cap),
        out_shape=[jax.ShapeDtypeStruct((rows, s), I32), jax.ShapeDtypeStruct((rows, LANES), I32)],
        compiler_params=pltpu.CompilerParams(vmem_limit_bytes=VMEM_LIMIT),
        name="select",
    )(aff_rows)


def _slot_window(cnt_ref, row, j, cap):
    lo, hi = cnt_ref[row * CNT_STRIDE + j], cnt_ref[row * CNT_STRIDE + j + 1]
    w0 = pl.multiple_of(jnp.minimum((lo // SLOT_ALIGN) * SLOT_ALIGN, cap - SLOT_WIN), SLOT_ALIGN)
    return hi, w0


def _any_overflow(windows):
    return functools.reduce(jnp.logical_or, [hi > w0 + SLOT_WIN for hi, w0 in windows])


def _gather_kernel(cnt_ref, posm_ref, h2_ref, xg_ref):
    bi, j = pl.program_id(0), pl.program_id(1)
    n_e, cap = xg_ref.shape[1:3]
    tt = MOE_TILE

    @pl.when(j == 0)
    def _():
        xg_ref[...] = jnp.zeros_like(xg_ref)

    slot0 = lax.broadcasted_iota(I32, (SLOT_WIN, tt), 0)
    for sub in range(h2_ref.shape[1] // tt):
        tile = j * (h2_ref.shape[1] // tt) + sub
        tok = slice(sub * tt, (sub + 1) * tt)
        windows = [_slot_window(cnt_ref, bi * n_e + e, tile, cap) for e in range(n_e)]
        for e in range(0, n_e, WIN_GROUP):
            onehot = jnp.concatenate(
                [(posm_ref[0, e + i:e + i + 1, tok] == slot0 + windows[e + i][1]).astype(BF16)
                 for i in range(WIN_GROUP)], axis=0)
            rows = jnp.dot(onehot, h2_ref[0, tok, :], preferred_element_type=F32).astype(BF16)
            for i in range(WIN_GROUP):
                xg_ref[0, e + i, pl.ds(windows[e + i][1], SLOT_WIN), :] += rows[i * SLOT_WIN:(i + 1) * SLOT_WIN]

        @pl.when(_any_overflow(windows))
        def _(tile=tile, tok=tok):
            wide = lax.broadcasted_iota(I32, (MXU_DEPTH, tt), 0)

            def one_expert(e, carry):
                hi, w0 = _slot_window(cnt_ref, bi * n_e + e, tile, cap)

                @pl.when(hi > w0 + SLOT_WIN)
                def _():
                    pos = posm_ref[0, pl.ds(e, 1), tok]
                    for ws in range(0, cap, MXU_DEPTH):
                        slot = wide + ws
                        onehot = ((pos == slot) & (slot >= w0 + SLOT_WIN)).astype(BF16)
                        extra = jnp.dot(onehot, h2_ref[0, tok, :], preferred_element_type=F32)
                        xg_ref[0, e, ws:ws + MXU_DEPTH, :] += extra.astype(BF16)

                return carry

            lax.fori_loop(0, n_e, one_expert, 0)


def _gather(cnt, posm, h2, cap):
    b, n_e, s = posm.shape
    d = h2.shape[2]
    step = MOE_TILE * GATHER_TILES
    return pl.pallas_call(
        _gather_kernel,
        grid_spec=pltpu.PrefetchScalarGridSpec(
            num_scalar_prefetch=1,
            grid=(b, s // step),
            in_specs=[
                pl.BlockSpec((1, n_e, step), lambda bi, j, cnt: (bi, 0, j)),
                pl.BlockSpec((1, step, d), lambda bi, j, cnt: (bi, j, 0)),
            ],
            out_specs=pl.BlockSpec((1, n_e, cap, d), lambda bi, j, cnt: (bi, 0, 0, 0)),
        ),
        out_shape=jax.ShapeDtypeStruct((b, n_e, cap, d), BF16),
        compiler_params=_params("parallel", "arbitrary"),
        name="gather",
    )(cnt, posm, h2)


def _ffn_kernel(xg_ref, wg_hbm, wu_hbm, wd_hbm, y_ref, wg_buf, wu_buf, wd_buf, wg_stage, wu_stage, wd_stage,
                sem, acc_ref, *, tf):
    e, bi = pl.program_id(0), pl.program_id(1)
    n_e, n_b = pl.num_programs(0), pl.num_programs(1)
    rows_in, rows_dn = wg_stage.shape[1], wd_stage.shape[1]
    f_total = wg_buf.shape[2]

    def chunk_copies(expert, c):
        st = c % 2
        return (
            pltpu.make_async_copy(wg_hbm.at[expert, pl.ds(c * rows_in, rows_in), :], wg_stage.at[st], sem.at[0, st]),
            pltpu.make_async_copy(wu_hbm.at[expert, pl.ds(c * rows_in, rows_in), :], wu_stage.at[st], sem.at[1, st]),
            pltpu.make_async_copy(wd_hbm.at[expert, pl.ds(c * rows_dn, rows_dn), :], wd_stage.at[st], sem.at[2, st]),
        )

    def finish_chunk(expert, c):
        for cp in chunk_copies(expert, c):
            cp.wait()
        slot, st = expert % 2, c % 2
        wg_buf[slot, pl.ds(c * rows_in, rows_in), :] = wg_stage[st].astype(BF16)
        wu_buf[slot, pl.ds(c * rows_in, rows_in), :] = wu_stage[st].astype(BF16)
        wd_buf[slot, pl.ds(c * rows_dn, rows_dn), :] = wd_stage[st].astype(BF16)

    @pl.when((e == 0) & (bi == 0))
    def _():
        for cp in chunk_copies(0, 0):
            cp.start()

        def load_first(c, carry):
            @pl.when(c + 1 < n_b)
            def _():
                for cp in chunk_copies(0, c + 1):
                    cp.start()

            finish_chunk(0, c)
            return carry

        lax.fori_loop(0, n_b, load_first, 0)

    pending = jnp.where(bi > 0, e + 1 < n_e, e > 0)

    @pl.when(pending)
    def _():
        finish_chunk(jnp.where(bi > 0, e + 1, e), jnp.where(bi > 0, bi - 1, n_b - 1))

    @pl.when(e + 1 < n_e)
    def _():
        for cp in chunk_copies(e + 1, bi):
            cp.start()

    def swiglu(slot):
        xg = xg_ref[0, 0]
        for fi in range(f_total // tf):
            fs = slice(fi * tf, (fi + 1) * tf)
            gate = jnp.dot(xg, wg_buf[slot, :, fs], preferred_element_type=F32)
            up = jnp.dot(xg, wu_buf[slot, :, fs], preferred_element_type=F32)
            hid = ((gate * jax.nn.sigmoid(gate)) * up).astype(BF16)
            part = jnp.dot(hid, wd_buf[slot, fs, :], preferred_element_type=F32)
            if fi == 0:
                acc_ref[...] = part
            else:
                acc_ref[...] += part
        y_ref[0, 0] = acc_ref[...].astype(y_ref.dtype)

    for slot in range(2):
        pl.when(e % 2 == slot)(functools.partial(swiglu, slot))


def _ffn(xg, wg, wu, wd, tf=512):
    b, n_e, cap, d = xg.shape
    f = wg.shape[2]
    assert d % b == 0 and f % b == 0 and (d // b) % SLOT_ALIGN == 0, "one weight row chunk per batch step"
    tok = pl.BlockSpec((1, 1, cap, d), lambda e, bi: (bi, e, 0, 0))
    hbm = pl.BlockSpec(memory_space=pl.ANY)
    return pl.pallas_call(
        functools.partial(_ffn_kernel, tf=tf),
        grid=(n_e, b),
        in_specs=[tok, hbm, hbm, hbm],
        out_specs=tok,
        out_shape=jax.ShapeDtypeStruct((b, n_e, cap, d), BF16),
        scratch_shapes=[
            pltpu.VMEM((2, d, f), BF16), pltpu.VMEM((2, d, f), BF16), pltpu.VMEM((2, f, d), BF16),
            pltpu.VMEM((2, d // b, f), F32), pltpu.VMEM((2, d // b, f), F32), pltpu.VMEM((2, f // b, d), F32),
            pltpu.SemaphoreType.DMA((3, 2)),
            pltpu.VMEM((cap, d), F32),
        ],
        compiler_params=_params("arbitrary", "arbitrary"),
        name="ffn",
    )(xg, wg, wu, wd)


def _combine_kernel(cnt_ref, pos_ref, gate_ref, y_ref, x1_ref, g_ref, out_ref, acc_ref):
    bi, j = pl.program_id(0), pl.program_id(1)
    tt = MOE_TILE
    n_e, cap = y_ref.shape[1:3]
    slot0 = lax.broadcasted_iota(I32, (SLOT_WIN, tt), 0)
    for sub in range(x1_ref.shape[1] // tt):
        tile = j * (x1_ref.shape[1] // tt) + sub
        tok = slice(sub * tt, (sub + 1) * tt)
        windows = [_slot_window(cnt_ref, bi * n_e + e, tile, cap) for e in range(n_e)]
        acc = x1_ref[0, tok, :]
        for e in range(0, n_e, WIN_GROUP):
            scatter_t = jnp.concatenate(
                [jnp.where(pos_ref[0, e + i:e + i + 1, tok] == slot0 + windows[e + i][1],
                           gate_ref[0, e + i:e + i + 1, tok], 0.0).astype(BF16) for i in range(WIN_GROUP)], axis=0)
            y_group = jnp.concatenate(
                [y_ref[0, e + i, pl.ds(windows[e + i][1], SLOT_WIN), :] for i in range(WIN_GROUP)], axis=0)
            acc = acc + lax.dot_general(scatter_t, y_group, _TN, preferred_element_type=F32)
        acc_ref[...] = acc

        @pl.when(_any_overflow(windows))
        def _(tile=tile, tok=tok):
            wide = lax.broadcasted_iota(I32, (MXU_DEPTH, tt), 0)

            def one_expert(e, carry):
                hi, w0 = _slot_window(cnt_ref, bi * n_e + e, tile, cap)

                @pl.when(hi > w0 + SLOT_WIN)
                def _():
                    pos = pos_ref[0, pl.ds(e, 1), tok]
                    gate = gate_ref[0, pl.ds(e, 1), tok]
                    for ws in range(0, cap, MXU_DEPTH):
                        slot = wide + ws
                        extra = jnp.where((pos == slot) & (slot >= w0 + SLOT_WIN), gate, 0.0).astype(BF16)
                        acc_ref[...] += lax.dot_general(extra, y_ref[0, e, ws:ws + MXU_DEPTH, :], _TN,
                                                        preferred_element_type=F32)

                return carry

            lax.fori_loop(0, n_e, one_expert, 0)

        acc = acc_ref[...]
        ms = jnp.mean(acc * acc, axis=-1, keepdims=True)
        out_ref[0, tok, :] = acc * lax.rsqrt(ms + EPS) * g_ref[...]


def _combine(cnt, posm, aff, y, x1, gain):
    b, s, d = x1.shape
    n_e, cap = y.shape[1:3]
    step = MOE_TILE * COMBINE_TILES
    return pl.pallas_call(
        _combine_kernel,
        grid_spec=pltpu.PrefetchScalarGridSpec(
            num_scalar_prefetch=1,
            grid=(b, s // step),
            in_specs=[
                pl.BlockSpec((1, n_e, step), lambda bi, j, cnt: (bi, 0, j)),
                pl.BlockSpec((1, n_e, step), lambda bi, j, cnt: (bi, 0, j)),
                pl.BlockSpec((1, n_e, cap, d), lambda bi, j, cnt: (bi, 0, 0, 0)),
                pl.BlockSpec((1, step, d), lambda bi, j, cnt: (bi, j, 0)),
                pl.BlockSpec((1, d), lambda bi, j, cnt: (0, 0)),
            ],
            out_specs=pl.BlockSpec((1, step, d), lambda bi, j, cnt: (bi, j, 0)),
            scratch_shapes=[pltpu.VMEM((MOE_TILE, d), F32)],
        ),
        out_shape=jax.ShapeDtypeStruct((b, s, d), F32),
        compiler_params=_params("parallel", "parallel"),
        name="combine",
    )(cnt, posm, aff, y, x1, gain)


def _rope_tables(seq):
    half = HEAD_DIM // 2
    pos = jnp.arange(seq, dtype=F32)
    inv = ROPE_BASE ** (-jnp.arange(0, HEAD_DIM, 2, dtype=F32) / HEAD_DIM)
    ang = pos[:, None] * inv[None, :]
    cos, sin = jnp.cos(ang), jnp.sin(ang)
    reps = LANES // HEAD_DIM
    cos_t = jnp.tile(jnp.concatenate([cos, cos], axis=1), (1, reps))
    sin_t = jnp.tile(jnp.concatenate([-sin, sin], axis=1), (1, reps))
    return cos_t, sin_t


def kernel(x, norm1_gain, w_in, ret_log_decay, ret_gn_gain, rel_bias, w_out, norm2_gain, w_router,
           w_gate, w_up, w_down, final_gain):
    b, s, d = x.shape
    depth = w_in.shape[0]
    ret_width = ret_gn_gain.shape[1]
    n_ret_pairs = ret_width // LANES
    att_width = (w_in.shape[2] - 4 * ret_width) // 3
    n_att_pairs = att_width // LANES
    cap = CAPACITY_FACTOR * s // N_EXPERTS
    cos_t, sin_t = _rope_tables(s)

    assert depth == 1, "single-layer block: the final norm is fused into the combine kernel"
    layer = 0
    ret, att = _proj(x.reshape(b * s, d), norm1_gain[layer][None, :], w_in[layer].astype(BF16),
                     cos_t, sin_t, n_ret=4 * ret_width, ret_width=ret_width)
    r = _retention(ret, ret_log_decay[layer], ret_gn_gain[layer][None, :], n_ret_pairs, b)
    a = _dilated(att, rel_bias, n_att_pairs, b)
    x1, h2, aff = _outproj(r, a, x, w_out[layer].astype(BF16), norm2_gain[layer][None, :],
                           w_router[layer])
    assert s % (MOE_TILE * max(GATHER_TILES, COMBINE_TILES)) == 0 and s // MOE_TILE < CNT_STRIDE and cap % MXU_DEPTH == 0 and N_EXPERTS % WIN_GROUP == 0
    posm, cnt = _select(aff.reshape(b * N_EXPERTS, s), cap)
    posm = posm.reshape(b, N_EXPERTS, s)
    cnt = cnt[:, :CNT_STRIDE].reshape(-1)
    xg = _gather(cnt, posm, h2.reshape(b, s, d), cap)
    y = _ffn(xg, w_gate[layer], w_up[layer], w_down[layer])
    return _combine(cnt, posm, aff, y, x1.reshape(b, s, d),
                    final_gain[None, :])
```

```python
import functools
import math

import numpy as np
import jax
import jax.numpy as jnp
from jax import lax
from jax.experimental import pallas as pl
from jax.experimental.pallas import tpu as pltpu

F32 = jnp.float32
BF16 = jnp.bfloat16
I32 = jnp.int32

HEAD_DIM = 64
LANES = 128
RET_CHUNK = 128
RET_GROUP = 4
ROPE_BASE = 10000.0
DILATED_PATTERNS = ((128, 1), (512, 4), (2048, 16))
SIDE = 64
N_BUCKETS = 32
MAX_DISTANCE = 1024
N_EXPERTS = 16
CAPACITY_FACTOR = 2
MOE_TILE = 256
GATHER_TILES = 8
COMBINE_TILES = 4
MXU_DEPTH = 256
WIN_GROUP = 4
SLOT_WIN = MXU_DEPTH // WIN_GROUP
SLOT_ALIGN = 16
CNT_STRIDE = 32
EPS = 1e-6
NEG = -1e30
VMEM_LIMIT = 56 * 1024 * 1024

_NT = (((1,), (1,)), ((), ()))
_TN = (((0,), (0,)), ((), ()))


def _params(*sem):
    return pltpu.CompilerParams(dimension_semantics=sem, vmem_limit_bytes=VMEM_LIMIT)


def _proj_kernel(x_ref, g_ref, w_ref, cos_ref, sin_ref, ret_ref, att_ref, *, n_ret, ret_width, chunk):
    x = x_ref[...]
    ms = jnp.mean(x * x, axis=-1, keepdims=True)
    h = (x * lax.rsqrt(ms + EPS) * g_ref[...]).astype(BF16)
    n_cols = w_ref.shape[1]
    cos, sin = cos_ref[...], sin_ref[...]
    first_half = (lax.broadcasted_iota(I32, cos.shape, 1) % HEAD_DIM) < (HEAD_DIM // 2)
    for c0 in range(0, n_cols, chunk):
        o = jnp.dot(h, w_ref[:, c0:c0 + chunk], preferred_element_type=F32)
        for l0 in range(0, chunk, LANES):
            t = o[:, l0:l0 + LANES]
            col = c0 + l0
            if col >= n_ret:
                att_ref[(col - n_ret) // LANES] = t
            elif col >= 2 * ret_width:
                ret_ref[col // LANES] = t.astype(BF16)
            else:
                scale = 1.0 if col < ret_width else HEAD_DIM ** -0.5
                swapped = jnp.where(first_half, pltpu.roll(t, LANES - HEAD_DIM // 2, 1),
                                    pltpu.roll(t, HEAD_DIM // 2, 1))
                ret_ref[col // LANES] = ((t * cos + swapped * sin) * scale).astype(BF16)


def _proj(x2d, gain, w_bf16, cos_t, sin_t, n_ret, ret_width, tm=512):
    t, d = x2d.shape
    n_cols = w_bf16.shape[1]
    n_att = n_cols - n_ret
    seq_tiles = cos_t.shape[0] // tm
    assert ret_width % 512 == 0 and n_ret % 512 == 0 and n_cols % 512 == 0
    return pl.pallas_call(
        functools.partial(_proj_kernel, n_ret=n_ret, ret_width=ret_width, chunk=512),
        grid=(t // tm,),
        in_specs=[
            pl.BlockSpec((tm, d), lambda i: (i, 0)),
            pl.BlockSpec((1, d), lambda i: (0, 0)),
            pl.BlockSpec((d, n_cols), lambda i: (0, 0)),
            pl.BlockSpec((tm, LANES), lambda i: (i % seq_tiles, 0)),
            pl.BlockSpec((tm, LANES), lambda i: (i % seq_tiles, 0)),
        ],
        out_specs=[
            pl.BlockSpec((n_ret // LANES, tm, LANES), lambda i: (0, i, 0)),
            pl.BlockSpec((n_att // LANES, tm, LANES), lambda i: (0, i, 0)),
        ],
        out_shape=[
            jax.ShapeDtypeStruct((n_ret // LANES, t, LANES), BF16),
            jax.ShapeDtypeStruct((n_att // LANES, t, LANES), F32),
        ],
        compiler_params=_params("parallel"),
        name="proj",
    )(x2d, gain, w_bf16, cos_t, sin_t)


def _retention_kernel(decay_ref, q_ref, k_ref, v_ref, g_ref, gn_ref, out_ref,
                      rf_ref, rb_ref, kvf_ref, kvb_ref, p0_ref, p1_ref, *, seq):
    c = RET_CHUNK
    n_chunks = seq // c
    p = pl.program_id(1)
    lane = lax.broadcasted_iota(I32, (c, LANES), 1)
    row = lax.broadcasted_iota(I32, (c, LANES), 0)
    head0 = lane < HEAD_DIM
    rowf = row.astype(F32)

    lgf0, lgf1 = decay_ref[0, 2 * p], decay_ref[0, 2 * p + 1]
    lgb0, lgb1 = decay_ref[1, 2 * p], decay_ref[1, 2 * p + 1]
    lgf_lane = jnp.where(head0, lgf0, lgf1)
    lgb_lane = jnp.where(head0, lgb0, lgb1)
    lgf_row = jnp.where(row < HEAD_DIM, lgf0, lgf1)
    lgb_row = jnp.where(row < HEAD_DIM, lgb0, lgb1)
    same_head = (row < HEAD_DIM) == head0

    zeta_f = jnp.exp((c - 1 - rowf) * lgf_lane)
    zeta_b = jnp.exp(rowf * lgb_lane)
    xi_f = jnp.exp((rowf + 1.0) * lgf_lane)
    xi_b = jnp.exp((c - rowf) * lgb_lane)
    gch_f = jnp.where(same_head, jnp.exp(c * lgf_row), 0.0)
    gch_b = jnp.where(same_head, jnp.exp(c * lgb_row), 0.0)

    diff = (row - lane).astype(F32)

    def dmat(lf, lb):
        return jnp.exp(jnp.where(diff >= 0, diff * lf, -diff * lb))

    d_stack = jnp.concatenate([dmat(lgf0, lgb0), dmat(lgf1, lgb1)], axis=0)

    def kv_step(n, carry):
        sl = pl.ds(pl.multiple_of(n * c, c), c)
        k = k_ref[0, sl, :].astype(F32)
        v = v_ref[0, sl, :]
        kvf = lax.dot_general((k * zeta_f).astype(BF16), v, _TN, preferred_element_type=F32)
        kvb = lax.dot_general((k * zeta_b).astype(BF16), v, _TN, preferred_element_type=F32)
        kvf_ref[n] = jnp.where(same_head, kvf, 0.0)
        kvb_ref[n] = jnp.where(same_head, kvb, 0.0)
        return carry

    lax.fori_loop(0, n_chunks, kv_step, 0, unroll=8)

    def scan_step(i, states):
        sf, sb = states
        nb = n_chunks - 1 - i
        rf_ref[i] = sf.astype(BF16)
        rb_ref[nb] = sb.astype(BF16)
        return gch_f * sf + kvf_ref[i], gch_b * sb + kvb_ref[nb]

    zero_state = jnp.zeros((LANES, LANES), F32)
    lax.fori_loop(0, n_chunks, scan_step, (zero_state, zero_state), unroll=4)

    gn_gain = gn_ref[...]

    def score_stage(g, p_ref):
        for jc in range(RET_GROUP):
            sl = pl.ds(pl.multiple_of((g * RET_GROUP + jc) * c, c), c)
            q = q_ref[0, sl, :]
            zero = jnp.zeros_like(q)
            q_stack = jnp.concatenate([jnp.where(head0, q, zero), jnp.where(head0, zero, q)], axis=0)
            scores = lax.dot_general(q_stack, k_ref[0, sl, :], _NT, preferred_element_type=F32) * d_stack
            sb = scores.astype(BF16)
            p_ref[jc] = jnp.concatenate([sb[:c], sb[c:]], axis=1)

    def half_mean(t):
        s0 = jnp.sum(jnp.where(head0, t, 0.0), axis=-1, keepdims=True)
        s1 = jnp.sum(jnp.where(head0, 0.0, t), axis=-1, keepdims=True)
        return jnp.where(head0, s0, s1) * (1.0 / HEAD_DIM)

    def value_stage(g, p_ref):
        for jc in range(RET_GROUP):
            n = g * RET_GROUP + jc
            sl = pl.ds(pl.multiple_of(n * c, c), c)
            q = q_ref[0, sl, :]
            v = v_ref[0, sl, :]
            zero = jnp.zeros_like(v)
            v_stack = jnp.concatenate([jnp.where(head0, v, zero), jnp.where(head0, zero, v)], axis=0)
            inner = jnp.dot(p_ref[jc], v_stack, preferred_element_type=F32)
            cross_f = jnp.dot(q, rf_ref[n], preferred_element_type=F32) * xi_f
            cross_b = jnp.dot(q, rb_ref[n], preferred_element_type=F32) * xi_b
            r = inner + cross_f + cross_b
            dlt = r - half_mean(r)
            var = half_mean(dlt * dlt)
            gate = g_ref[0, sl, :].astype(F32)
            y = dlt * lax.rsqrt(var + EPS) * gn_gain * (gate * jax.nn.sigmoid(gate))
            out_ref[0, sl, :] = y.astype(out_ref.dtype)

    n_groups = n_chunks // RET_GROUP
    assert n_groups % 2 == 0 and n_groups >= 2
    score_stage(0, p0_ref)

    def group_pair(i, carry):
        g = 2 * i
        score_stage(g + 1, p1_ref)
        value_stage(g, p0_ref)
        score_stage(g + 2, p0_ref)
        value_stage(g + 1, p1_ref)
        return carry

    lax.fori_loop(0, n_groups // 2 - 1, group_pair, 0)
    score_stage(n_groups - 1, p1_ref)
    value_stage(n_groups - 2, p0_ref)
    value_stage(n_groups - 1, p1_ref)


def _retention(ret, decay, gn_gain, n_pairs, b):
    s = ret.shape[1] // b
    blk = lambda off: pl.BlockSpec((1, s, LANES), lambda bi, pi: (off + pi, bi, 0))
    return pl.pallas_call(
        functools.partial(_retention_kernel, seq=s),
        grid=(b, n_pairs),
        in_specs=[
            pl.BlockSpec(memory_space=pltpu.SMEM),
            blk(0), blk(n_pairs), blk(2 * n_pairs), blk(3 * n_pairs),
            pl.BlockSpec((1, LANES), lambda bi, pi: (0, pi)),
        ],
        out_specs=pl.BlockSpec((1, s, LANES), lambda bi, pi: (pi, bi, 0)),
        out_shape=jax.ShapeDtypeStruct((n_pairs, b * s, LANES), BF16),
        scratch_shapes=[
            pltpu.VMEM((s // RET_CHUNK, LANES, LANES), BF16),
            pltpu.VMEM((s // RET_CHUNK, LANES, LANES), BF16),
            pltpu.VMEM((s // RET_CHUNK, LANES, LANES), F32),
            pltpu.VMEM((s // RET_CHUNK, LANES, LANES), F32),
            pltpu.VMEM((RET_GROUP, RET_CHUNK, 2 * RET_CHUNK), BF16),
            pltpu.VMEM((RET_GROUP, RET_CHUNK, 2 * RET_CHUNK), BF16),
        ],
        compiler_params=_params("parallel", "parallel"),
        name="retention",
    )(decay, ret, ret, ret, ret, gn_gain)


QBLK = 2 * SIDE
KWIN = 4 * SIDE
BASE_W = 512
ATT_GROUP = 4


def _t5_bucket_np(rel):
    half = N_BUCKETS // 2
    max_exact = half // 2
    bucket = np.where(rel > 0, half, 0)
    n = np.abs(rel)
    nf = np.maximum(n, 1).astype(np.float32)
    large = max_exact + (np.log(nf / np.float32(max_exact)) / np.float32(math.log(MAX_DISTANCE / max_exact))
                         * np.float32(half - max_exact)).astype(np.int32)
    large = np.minimum(large, half - 1)
    return (bucket + np.where(n < max_exact, n, large)).astype(np.int32)


def _bucket_rows():
    k = np.arange(BASE_W)
    off = k - SIDE
    rows = []
    for _, dilation in DILATED_PATTERNS:
        rows.append(np.where(k <= 2 * SIDE, _t5_bucket_np(off * dilation), -1))
    return np.stack(rows).astype(np.int32)


def _dilated_kernel(bias_tab_ref, bucket_ref, qkv_ref, out_ref,
                    bias_ref, perm_ref, stage_ref, o_ref, lse_ref, e0_ref, e1_ref, m0_ref, m1_ref, *, seq):
    order = list(range(1, len(DILATED_PATTERNS))) + [0]
    dil = [DILATED_PATTERNS[pi][1] for pi in order]
    n_slots = len(order)
    assert n_slots == 3 and dil[0] > 1 and dil[1] % dil[0] == 0 and dil[2] == 1
    p = pl.program_id(1)
    lane = lax.broadcasted_iota(I32, (QBLK, LANES), 1)
    head0 = lane < HEAD_DIM
    n_groups = seq // QBLK // ATT_GROUP
    n_loop = n_groups // 2 - 1
    assert n_groups % 2 == 0 and n_loop >= 0

    for pi in range(len(DILATED_PATTERNS)):
        bucket = jnp.broadcast_to(bucket_ref[pi:pi + 1, :], (QBLK, BASE_W))
        for hh in range(2):
            base = jnp.full((QBLK, BASE_W), NEG, F32)
            for bk in range(N_BUCKETS):
                base = jnp.where(bucket == bk, bias_tab_ref[bk, 2 * p + hh], base)
            for var, shift in enumerate((BASE_W - SIDE, 0, SIDE)):
                tile = pltpu.roll(base, shift, 1, stride=1, stride_axis=0)
                bias_ref[pi * 3 + var, hh * QBLK:(hh + 1) * QBLK, :] = tile[:, :KWIN]

    def relayout(slot, i):
        d = dil[slot]
        seg_len = seq // d
        scale = jnp.where(i == 0, HEAD_DIM ** -0.5, 1.0).astype(F32)
        if d == 1:
            perm_ref[slot, i] = (qkv_ref[i, 0] * scale).astype(BF16)
        elif slot == 0:
            for r in range(d):
                val = qkv_ref[i, 0, pl.ds(r, seg_len, stride=d), :]
                stage_ref[i, r * seg_len:(r + 1) * seg_len, :] = val
                perm_ref[slot, i, r * seg_len:(r + 1) * seg_len, :] = (val * scale).astype(BF16)
        else:
            prev_d = dil[slot - 1]
            step, prev_len = d // prev_d, seq // prev_d
            for r_prev in range(prev_d):
                for r_step in range(step):
                    val = stage_ref[i, pl.ds(r_prev * prev_len + r_step, seg_len, stride=step), :]
                    r = r_prev + prev_d * r_step
                    perm_ref[slot, i, r * seg_len:(r + 1) * seg_len, :] = (val * scale).astype(BF16)

    def placement(slot, t):
        seg_len = seq // dil[slot]
        blocks_per_seg = seg_len // QBLK
        seg = t // blocks_per_seg
        u = t % blocks_per_seg
        q0 = pl.multiple_of(t * QBLK, QBLK)
        seg0 = seg * seg_len
        k0 = pl.multiple_of(jnp.clip(q0 - SIDE, seg0, seg0 + seg_len - KWIN), SIDE)
        var = jnp.where(u == 0, 0, jnp.where(u == blocks_per_seg - 1, 2, 1))
        return seg, u, q0, k0, var

    def logits_stage(slot, g, e_ref, m_ref):
        for jb in range(ATT_GROUP):
            _, _, q0, k0, var = placement(slot, g * ATT_GROUP + jb)
            q = perm_ref[slot, 0, pl.ds(q0, QBLK), :]
            kw = perm_ref[slot, 1, pl.ds(k0, KWIN), :]
            zero = jnp.zeros_like(q)
            q_stack = jnp.concatenate([jnp.where(head0, q, zero), jnp.where(head0, zero, q)], axis=0)
            s = lax.dot_general(q_stack, kw, _NT, preferred_element_type=F32) + bias_ref[order[slot] * 3 + var]
            m = jnp.max(s, axis=-1, keepdims=True)
            e = jnp.exp(s - m)
            e_ref[jb] = e.astype(BF16)
            m_ref[jb, 0] = jnp.broadcast_to(m, (2 * QBLK, LANES))
            m_ref[jb, 1] = jnp.broadcast_to(jnp.sum(e, axis=-1, keepdims=True), (2 * QBLK, LANES))

    def value_stage(slot, g, e_ref, m_ref):
        d = dil[slot]
        for jb in range(ATT_GROUP):
            seg, u, q0, k0, _ = placement(slot, g * ATT_GROUP + jb)
            vw = perm_ref[slot, 2, pl.ds(k0, KWIN), :]
            pv = jnp.dot(e_ref[jb], vw, preferred_element_type=F32)
            num = jnp.where(head0, pv[:QBLK], pv[QBLK:])
            den = jnp.where(head0, m_ref[jb, 1, :QBLK, :], m_ref[jb, 1, QBLK:, :])
            o = num * (1.0 / den)
            lse = jnp.where(head0, m_ref[jb, 0, :QBLK, :], m_ref[jb, 0, QBLK:, :]) + jnp.log(den)
            dst = pl.ds(seg + d * (u * QBLK), QBLK, stride=d) if d > 1 else pl.ds(q0, QBLK)
            if slot < n_slots - 1:
                o_ref[slot, dst, :] = o
                lse_ref[slot, dst, :] = lse
            else:
                lses = [lse_ref[i, dst, :] for i in range(slot)] + [lse]
                outs = [o_ref[i, dst, :] for i in range(slot)] + [o]
                mx = functools.reduce(jnp.maximum, lses)
                ws = [jnp.exp(l - mx) for l in lses]
                inv = 1.0 / functools.reduce(jnp.add, ws)
                acc = functools.reduce(jnp.add, [(w * inv) * t for w, t in zip(ws, outs)])
                out_ref[0, dst, :] = acc.astype(out_ref.dtype)

    for slot in range(n_slots):
        for i in range(3):
            relayout(slot, i)
    logits_stage(0, 0, e0_ref, m0_ref)
    for slot in range(n_slots):
        nxt = slot + 1 if slot + 1 < n_slots else None

        def group_pair(i, carry, slot=slot):
            g = 2 * i
            logits_stage(slot, g + 1, e1_ref, m1_ref)
            value_stage(slot, g, e0_ref, m0_ref)
            logits_stage(slot, g + 2, e0_ref, m0_ref)
            value_stage(slot, g + 1, e1_ref, m1_ref)
            return carry

        lax.fori_loop(0, n_loop, group_pair, 0)
        logits_stage(slot, n_groups - 1, e1_ref, m1_ref)
        value_stage(slot, n_groups - 2, e0_ref, m0_ref)
        if nxt is not None:
            logits_stage(nxt, 0, e0_ref, m0_ref)
        value_stage(slot, n_groups - 1, e1_ref, m1_ref)


def _dilated(att, rel_bias, n_pairs, b):
    s = att.shape[1] // b
    n_pat = len(DILATED_PATTERNS)
    bucket_rows = jnp.asarray(_bucket_rows())
    return pl.pallas_call(
        functools.partial(_dilated_kernel, seq=s),
        grid=(b, n_pairs),
        in_specs=[
            pl.BlockSpec(memory_space=pltpu.SMEM),
            pl.BlockSpec((n_pat, BASE_W), lambda bi, pi: (0, 0)),
            pl.BlockSpec((3, 1, s, LANES), lambda bi, pi: (0, pi, bi, 0)),
        ],
        out_specs=pl.BlockSpec((1, s, LANES), lambda bi, pi: (pi, bi, 0)),
        out_shape=jax.ShapeDtypeStruct((n_pairs, b * s, LANES), BF16),
        scratch_shapes=[
            pltpu.VMEM((n_pat * 3, 2 * QBLK, KWIN), F32),
            pltpu.VMEM((n_pat, 3, s, LANES), BF16),
            pltpu.VMEM((3, s, LANES), F32),
            pltpu.VMEM((n_pat - 1, s, LANES), F32),
            pltpu.VMEM((n_pat - 1, s, LANES), F32),
            pltpu.VMEM((ATT_GROUP, 2 * QBLK, KWIN), BF16),
            pltpu.VMEM((ATT_GROUP, 2 * QBLK, KWIN), BF16),
            pltpu.VMEM((ATT_GROUP, 2, 2 * QBLK, LANES), F32),
            pltpu.VMEM((ATT_GROUP, 2, 2 * QBLK, LANES), F32),
        ],
        compiler_params=_params("parallel", "parallel"),
        name="dilated",
    )(rel_bias, bucket_rows, att.reshape(3, n_pairs, b * s, LANES))


def _outproj_kernel(r_ref, a_ref, x_ref, w_ref, g_ref, wr_ref, x1_ref, h2_ref, aff_ref):
    n_e = aff_ref.shape[1]
    wr = wr_ref[...]
    w_hi = wr.astype(BF16)
    w_lo = (wr - w_hi.astype(F32)).astype(BF16)
    w_parts = jnp.where(lax.broadcasted_iota(I32, wr.shape, 1) < n_e, w_hi, w_lo)
    tm = x_ref.shape[0]
    mixed = jnp.concatenate([r_ref[i] for i in range(r_ref.shape[0])]
                            + [a_ref[i] for i in range(a_ref.shape[0])], axis=1)
    x1 = x_ref[...] + jnp.dot(mixed, w_ref[...], preferred_element_type=F32)
    x1_ref[...] = x1
    ms = jnp.mean(x1 * x1, axis=-1, keepdims=True)
    h2 = x1 * lax.rsqrt(ms + EPS) * g_ref[...]
    h_hi = h2.astype(BF16)
    h2_ref[...] = h_hi
    h_lo = (h2 - h_hi.astype(F32)).astype(BF16)
    prod = jnp.dot(jnp.concatenate([h_hi, h_lo], axis=0), w_parts, preferred_element_type=F32)
    logits = prod[:tm] + (pltpu.roll(prod[:tm], LANES - n_e, 1) + prod[tm:])
    expert_lane = lax.broadcasted_iota(I32, logits.shape, 1) < n_e
    logits = jnp.where(expert_lane, logits, NEG)
    m = jnp.max(logits, axis=1, keepdims=True)
    e = jnp.exp(logits - m)
    aff = e / jnp.sum(e, axis=1, keepdims=True)
    aff_ref[0] = aff.T[:n_e, :]


def _outproj(r, a, x, w_bf16, gain, w_router, tm=512):
    b, s, d = x.shape
    n_r, n_a = r.shape[0], a.shape[0]
    n_e = w_router.shape[1]
    assert 2 * n_e <= LANES
    w_router = jnp.pad(jnp.concatenate([w_router, w_router], axis=1), ((0, 0), (0, LANES - 2 * n_e)))
    spt = s // tm
    tok = lambda width: pl.BlockSpec((tm, width), lambda i: (i, 0))
    slab = lambda blocks: pl.BlockSpec((blocks, tm, LANES), lambda i: (0, i, 0))
    const = lambda shape: pl.BlockSpec(shape, lambda i: (0, 0))
    return pl.pallas_call(
        _outproj_kernel,
        grid=(b * spt,),
        in_specs=[slab(n_r), slab(n_a), tok(d), const(((n_r + n_a) * LANES, d)), const((1, d)), const((d, LANES))],
        out_specs=[tok(d), tok(d), pl.BlockSpec((1, n_e, tm), lambda i: (i // spt, 0, i % spt))],
        out_shape=[
            jax.ShapeDtypeStruct((b * s, d), F32),
            jax.ShapeDtypeStruct((b * s, d), BF16),
            jax.ShapeDtypeStruct((b, n_e, s), F32),
        ],
        compiler_params=_params("parallel"),
        name="outproj",
    )(r, a, x.reshape(b * s, d), w_bf16, gain, w_router)


def _select_kernel(aff_ref, posm_ref, cnt_ref, *, cap):
    rows, s = aff_ref.shape
    bits = pltpu.bitcast(aff_ref[...], I32)

    def search(i, thr):
        cand = thr | jnp.left_shift(jnp.int32(1), 30 - i)
        cnt = jnp.sum((bits >= cand).astype(I32), axis=1, keepdims=True)
        return jnp.where(cnt >= cap, cand, thr)

    thr = lax.fori_loop(0, 31, search, jnp.zeros((rows, 1), I32))
    gt = bits > thr
    eq = bits == thr
    need = cap - jnp.sum(gt.astype(I32), axis=1, keepdims=True)

    w = LANES
    tri = (lax.broadcasted_iota(I32, (w, w), 0) <= lax.broadcasted_iota(I32, (w, w), 1)).astype(BF16)

    def excl_prefix(flags):
        carry = jnp.zeros((rows, 1), F32)
        out = []
        for c0 in range(0, s, w):
            f = flags[:, c0:c0 + w].astype(BF16)
            inc = jnp.dot(f, tri, preferred_element_type=F32)
            out.append(inc - f.astype(F32) + carry)
            carry = carry + inc[:, w - 1:w]
        return jnp.concatenate(out, axis=1).astype(I32)

    sel = gt | (eq & (excl_prefix(eq) < need))
    posm_ref[...] = jnp.where(sel, excl_prefix(sel), -1)

    tok = lax.broadcasted_iota(I32, (w, w), 0)
    edge = lax.broadcasted_iota(I32, (w, w), 1) * MOE_TILE
    cnt = jnp.zeros((rows, w), F32)
    for c0 in range(0, s, w):
        before = ((tok + c0) < edge).astype(BF16)
        cnt = cnt + jnp.dot(sel[:, c0:c0 + w].astype(BF16), before, preferred_element_type=F32)
    cnt_ref[...] = cnt.astype(I32)


def _select(aff_rows, cap):
    rows, s = aff_rows.shape
    return pl.pallas_call(
        functools.partial(_select_kernel, cap=cap),
        out_shape=[jax.ShapeDtypeStruct((rows, s), I32), jax.ShapeDtypeStruct((rows, LANES), I32)],
        compiler_params=pltpu.CompilerParams(vmem_limit_bytes=VMEM_LIMIT),
        name="select",
    )(aff_rows)


def _slot_window(cnt_ref, row, j, cap):
    lo, hi = cnt_ref[row * CNT_STRIDE + j], cnt_ref[row * CNT_STRIDE + j + 1]
    w0 = pl.multiple_of(jnp.minimum((lo // SLOT_ALIGN) * SLOT_ALIGN, cap - SLOT_WIN), SLOT_ALIGN)
    return hi, w0


def _any_overflow(windows):
    return functools.reduce(jnp.logical_or, [hi > w0 + SLOT_WIN for hi, w0 in windows])


def _gather_kernel(cnt_ref, posm_ref, h2_ref, xg_ref):
    bi, j = pl.program_id(0), pl.program_id(1)
    n_e, cap = xg_ref.shape[1:3]
    tt = MOE_TILE

    @pl.when(j == 0)
    def _():
        xg_ref[...] = jnp.zeros_like(xg_ref)

    slot0 = lax.broadcasted_iota(I32, (SLOT_WIN, tt), 0)
    for sub in range(h2_ref.shape[1] // tt):
        tile = j * (h2_ref.shape[1] // tt) + sub
        tok = slice(sub * tt, (sub + 1) * tt)
        windows = [_slot_window(cnt_ref, bi * n_e + e, tile, cap) for e in range(n_e)]
        for e in range(0, n_e, WIN_GROUP):
            onehot = jnp.concatenate(
                [(posm_ref[0, e + i:e + i + 1, tok] == slot0 + windows[e + i][1]).astype(BF16)
                 for i in range(WIN_GROUP)], axis=0)
            rows = jnp.dot(onehot, h2_ref[0, tok, :], preferred_element_type=F32).astype(BF16)
            for i in range(WIN_GROUP):
                xg_ref[0, e + i, pl.ds(windows[e + i][1], SLOT_WIN), :] += rows[i * SLOT_WIN:(i + 1) * SLOT_WIN]

        @pl.when(_any_overflow(windows))
        def _(tile=tile, tok=tok):
            wide = lax.broadcasted_iota(I32, (MXU_DEPTH, tt), 0)

            def one_expert(e, carry):
                hi, w0 = _slot_window(cnt_ref, bi * n_e + e, tile, cap)

                @pl.when(hi > w0 + SLOT_WIN)
                def _():
                    pos = posm_ref[0, pl.ds(e, 1), tok]
                    for ws in range(0, cap, MXU_DEPTH):
                        slot = wide + ws
                        onehot = ((pos == slot) & (slot >= w0 + SLOT_WIN)).astype(BF16)
                        extra = jnp.dot(onehot, h2_ref[0, tok, :], preferred_element_type=F32)
                        xg_ref[0, e, ws:ws + MXU_DEPTH, :] += extra.astype(BF16)

                return carry

            lax.fori_loop(0, n_e, one_expert, 0)


def _gather(cnt, posm, h2, cap):
    b, n_e, s = posm.shape
    d = h2.shape[2]
    step = MOE_TILE * GATHER_TILES
    return pl.pallas_call(
        _gather_kernel,
        grid_spec=pltpu.PrefetchScalarGridSpec(
            num_scalar_prefetch=1,
            grid=(b, s // step),
            in_specs=[
                pl.BlockSpec((1, n_e, step), lambda bi, j, cnt: (bi, 0, j)),
                pl.BlockSpec((1, step, d), lambda bi, j, cnt: (bi, j, 0)),
            ],
            out_specs=pl.BlockSpec((1, n_e, cap, d), lambda bi, j, cnt: (bi, 0, 0, 0)),
        ),
        out_shape=jax.ShapeDtypeStruct((b, n_e, cap, d), BF16),
        compiler_params=_params("parallel", "arbitrary"),
        name="gather",
    )(cnt, posm, h2)


def _ffn_kernel(xg_ref, wg_hbm, wu_hbm, wd_hbm, y_ref, wg_buf, wu_buf, wd_buf, wg_stage, wu_stage, wd_stage,
                sem, acc_ref, *, tf):
    e, bi = pl.program_id(0), pl.program_id(1)
    n_e, n_b = pl.num_programs(0), pl.num_programs(1)
    rows_in, rows_dn = wg_stage.shape[1], wd_stage.shape[1]
    f_total = wg_buf.shape[2]

    def chunk_copies(expert, c):
        st = c % 2
        return (
            pltpu.make_async_copy(wg_hbm.at[expert, pl.ds(c * rows_in, rows_in), :], wg_stage.at[st], sem.at[0, st]),
            pltpu.make_async_copy(wu_hbm.at[expert, pl.ds(c * rows_in, rows_in), :], wu_stage.at[st], sem.at[1, st]),
            pltpu.make_async_copy(wd_hbm.at[expert, pl.ds(c * rows_dn, rows_dn), :], wd_stage.at[st], sem.at[2, st]),
        )

    def finish_chunk(expert, c):
        for cp in chunk_copies(expert, c):
            cp.wait()
        slot, st = expert % 2, c % 2
        wg_buf[slot, pl.ds(c * rows_in, rows_in), :] = wg_stage[st].astype(BF16)
        wu_buf[slot, pl.ds(c * rows_in, rows_in), :] = wu_stage[st].astype(BF16)
        wd_buf[slot, pl.ds(c * rows_dn, rows_dn), :] = wd_stage[st].astype(BF16)

    @pl.when((e == 0) & (bi == 0))
    def _():
        for cp in chunk_copies(0, 0):
            cp.start()

        def load_first(c, carry):
            @pl.when(c + 1 < n_b)
            def _():
                for cp in chunk_copies(0, c + 1):
                    cp.start()

            finish_chunk(0, c)
            return carry

        lax.fori_loop(0, n_b, load_first, 0)

    pending = jnp.where(bi > 0, e + 1 < n_e, e > 0)

    @pl.when(pending)
    def _():
        finish_chunk(jnp.where(bi > 0, e + 1, e), jnp.where(bi > 0, bi - 1, n_b - 1))

    @pl.when(e + 1 < n_e)
    def _():
        for cp in chunk_copies(e + 1, bi):
            cp.start()

    def swiglu(slot):
        xg = xg_ref[0, 0]
        for fi in range(f_total // tf):
            fs = slice(fi * tf, (fi + 1) * tf)
            gate = jnp.dot(xg, wg_buf[slot, :, fs], preferred_element_type=F32)
            up = jnp.dot(xg, wu_buf[slot, :, fs], preferred_element_type=F32)
            hid = ((gate * jax.nn.sigmoid(gate)) * up).astype(BF16)
            part = jnp.dot(hid, wd_buf[slot, fs, :], preferred_element_type=F32)
            if fi == 0:
                acc_ref[...] = part
            else:
                acc_ref[...] += part
        y_ref[0, 0] = acc_ref[...].astype(y_ref.dtype)

    for slot in range(2):
        pl.when(e % 2 == slot)(functools.partial(swiglu, slot))


def _ffn(xg, wg, wu, wd, tf=512):
    b, n_e, cap, d = xg.shape
    f = wg.shape[2]
    assert d % b == 0 and f % b == 0 and (d // b) % SLOT_ALIGN == 0, "one weight row chunk per batch step"
    tok = pl.BlockSpec((1, 1, cap, d), lambda e, bi: (bi, e, 0, 0))
    hbm = pl.BlockSpec(memory_space=pl.ANY)
    return pl.pallas_call(
        functools.partial(_ffn_kernel, tf=tf),
        grid=(n_e, b),
        in_specs=[tok, hbm, hbm, hbm],
        out_specs=tok,
        out_shape=jax.ShapeDtypeStruct((b, n_e, cap, d), BF16),
        scratch_shapes=[
            pltpu.VMEM((2, d, f), BF16), pltpu.VMEM((2, d, f), BF16), pltpu.VMEM((2, f, d), BF16),
            pltpu.VMEM((2, d // b, f), F32), pltpu.VMEM((2, d // b, f), F32), pltpu.VMEM((2, f // b, d), F32),
            pltpu.SemaphoreType.DMA((3, 2)),
            pltpu.VMEM((cap, d), F32),
        ],
        compiler_params=_params("arbitrary", "arbitrary"),
        name="ffn",
    )(xg, wg, wu, wd)


def _combine_kernel(cnt_ref, pos_ref, gate_ref, y_ref, x1_ref, g_ref, out_ref, acc_ref):
    bi, j = pl.program_id(0), pl.program_id(1)
    tt = MOE_TILE
    n_e, cap = y_ref.shape[1:3]
    slot0 = lax.broadcasted_iota(I32, (SLOT_WIN, tt), 0)
    for sub in range(x1_ref.shape[1] // tt):
        tile = j * (x1_ref.shape[1] // tt) + sub
        tok = slice(sub * tt, (sub + 1) * tt)
        windows = [_slot_window(cnt_ref, bi * n_e + e, tile, cap) for e in range(n_e)]
        acc = x1_ref[0, tok, :]
        for e in range(0, n_e, WIN_GROUP):
            scatter_t = jnp.concatenate(
                [jnp.where(pos_ref[0, e + i:e + i + 1, tok] == slot0 + windows[e + i][1],
                           gate_ref[0, e + i:e + i + 1, tok], 0.0).astype(BF16) for i in range(WIN_GROUP)], axis=0)
            y_group = jnp.concatenate(
                [y_ref[0, e + i, pl.ds(windows[e + i][1], SLOT_WIN), :] for i in range(WIN_GROUP)], axis=0)
            acc = acc + lax.dot_general(scatter_t, y_group, _TN, preferred_element_type=F32)
        acc_ref[...] = acc

        @pl.when(_any_overflow(windows))
        def _(tile=tile, tok=tok):
            wide = lax.broadcasted_iota(I32, (MXU_DEPTH, tt), 0)

            def one_expert(e, carry):
                hi, w0 = _slot_window(cnt_ref, bi * n_e + e, tile, cap)

                @pl.when(hi > w0 + SLOT_WIN)
                def _():
                    pos = pos_ref[0, pl.ds(e, 1), tok]
                    gate = gate_ref[0, pl.ds(e, 1), tok]
                    for ws in range(0, cap, MXU_DEPTH):
                        slot = wide + ws
                        extra = jnp.where((pos == slot) & (slot >= w0 + SLOT_WIN), gate, 0.0).astype(BF16)
                        acc_ref[...] += lax.dot_general(extra, y_ref[0, e, ws:ws + MXU_DEPTH, :], _TN,
                                                        preferred_element_type=F32)

                return carry

            lax.fori_loop(0, n_e, one_expert, 0)

        acc = acc_ref[...]
        ms = jnp.mean(acc * acc, axis=-1, keepdims=True)
        out_ref[0, tok, :] = acc * lax.rsqrt(ms + EPS) * g_ref[...]


def _combine(cnt, posm, aff, y, x1, gain):
    b, s, d = x1.shape
    n_e, cap = y.shape[1:3]
    step = MOE_TILE * COMBINE_TILES
    return pl.pallas_call(
        _combine_kernel,
        grid_spec=pltpu.PrefetchScalarGridSpec(
            num_scalar_prefetch=1,
            grid=(b, s // step),
            in_specs=[
                pl.BlockSpec((1, n_e, step), lambda bi, j, cnt: (bi, 0, j)),
                pl.BlockSpec((1, n_e, step), lambda bi, j, cnt: (bi, 0, j)),
                pl.BlockSpec((1, n_e, cap, d), lambda bi, j, cnt: (bi, 0, 0, 0)),
                pl.BlockSpec((1, step, d), lambda bi, j, cnt: (bi, j, 0)),
                pl.BlockSpec((1, d), lambda bi, j, cnt: (0, 0)),
            ],
            out_specs=pl.BlockSpec((1, step, d), lambda bi, j, cnt: (bi, j, 0)),
            scratch_shapes=[pltpu.VMEM((MOE_TILE, d), F32)],
        ),
        out_shape=jax.ShapeDtypeStruct((b, s, d), F32),
        compiler_params=_params("parallel", "parallel"),
        name="combine",
    )(cnt, posm, aff, y, x1, gain)


def _rope_tables(seq):
    half = HEAD_DIM // 2
    pos = jnp.arange(seq, dtype=F32)
    inv = ROPE_BASE ** (-jnp.arange(0, HEAD_DIM, 2, dtype=F32) / HEAD_DIM)
    ang = pos[:, None] * inv[None, :]
    cos, sin = jnp.cos(ang), jnp.sin(ang)
    reps = LANES // HEAD_DIM
    cos_t = jnp.tile(jnp.concatenate([cos, cos], axis=1), (1, reps))
    sin_t = jnp.tile(jnp.concatenate([-sin, sin], axis=1), (1, reps))
    return cos_t, sin_t


def kernel(x, norm1_gain, w_in, ret_log_decay, ret_gn_gain, rel_bias, w_out, norm2_gain, w_router,
           w_gate, w_up, w_down, final_gain):
    b, s, d = x.shape
    depth = w_in.shape[0]
    ret_width = ret_gn_gain.shape[1]
    n_ret_pairs = ret_width // LANES
    att_width = (w_in.shape[2] - 4 * ret_width) // 3
    n_att_pairs = att_width // LANES
    cap = CAPACITY_FACTOR * s // N_EXPERTS
    cos_t, sin_t = _rope_tables(s)

    assert depth == 1, "single-layer block: the final norm is fused into the combine kernel"
    layer = 0
    ret, att = _proj(x.reshape(b * s, d), norm1_gain[layer][None, :], w_in[layer].astype(BF16),
                     cos_t, sin_t, n_ret=4 * ret_width, ret_width=ret_width)
    r = _retention(ret, ret_log_decay[layer], ret_gn_gain[layer][None, :], n_ret_pairs, b)
    a = _dilated(att, rel_bias, n_att_pairs, b)
    x1, h2, aff = _outproj(r, a, x, w_out[layer].astype(BF16), norm2_gain[layer][None, :],
                           w_router[layer])
    assert s % (MOE_TILE * max(GATHER_TILES, COMBINE_TILES)) == 0 and s // MOE_TILE < CNT_STRIDE and cap % MXU_DEPTH == 0 and N_EXPERTS % WIN_GROUP == 0
    posm, cnt = _select(aff.reshape(b * N_EXPERTS, s), cap)
    posm = posm.reshape(b, N_EXPERTS, s)
    cnt = cnt[:, :CNT_STRIDE].reshape(-1)
    xg = _gather(cnt, posm, h2.reshape(b, s, d), cap)
    y = _ffn(xg, w_gate[layer], w_up[layer], w_down[layer])
    return _combine(cnt, posm, aff, y, x1.reshape(b, s, d),
                    final_gain[None, :])
```

```python
import functools
import math

import numpy as np
import jax
import jax.numpy as jnp
from jax import lax
from jax.experimental import pallas as pl
from jax.experimental.pallas import tpu as pltpu

F32 = jnp.float32
BF16 = jnp.bfloat16
I32 = jnp.int32

HEAD_DIM = 64
LANES = 128
RET_CHUNK = 128
RET_GROUP = 4
ROPE_BASE = 10000.0
DILATED_PATTERNS = ((128, 1), (512, 4), (2048, 16))
SIDE = 64
N_BUCKETS = 32
MAX_DISTANCE = 1024
N_EXPERTS = 16
CAPACITY_FACTOR = 2
MOE_TILE = 256
GATHER_TILES = 8
COMBINE_TILES = 4
MXU_DEPTH = 256
WIN_GROUP = 4
SLOT_WIN = MXU_DEPTH // WIN_GROUP
SLOT_ALIGN = 16
CNT_STRIDE = 32
EPS = 1e-6
NEG = -1e30
VMEM_LIMIT = 56 * 1024 * 1024

_NT = (((1,), (1,)), ((), ()))
_TN = (((0,), (0,)), ((), ()))


def _params(*sem):
    return pltpu.CompilerParams(dimension_semantics=sem, vmem_limit_bytes=VMEM_LIMIT)


def _proj_kernel(x_ref, g_ref, w_ref, cos_ref, sin_ref, ret_ref, att_ref, *, n_ret, ret_width, chunk):
    x = x_ref[...]
    ms = jnp.mean(x * x, axis=-1, keepdims=True)
    h = (x * lax.rsqrt(ms + EPS) * g_ref[...]).astype(BF16)
    n_cols = w_ref.shape[1]
    cos, sin = cos_ref[...], sin_ref[...]
    first_half = (lax.broadcasted_iota(I32, cos.shape, 1) % HEAD_DIM) < (HEAD_DIM // 2)
    for c0 in range(0, n_cols, chunk):
        o = jnp.dot(h, w_ref[:, c0:c0 + chunk], preferred_element_type=F32)
        for l0 in range(0, chunk, LANES):
            t = o[:, l0:l0 + LANES]
            col = c0 + l0
            if col >= n_ret:
                att_ref[(col - n_ret) // LANES] = t
            elif col >= 2 * ret_width:
                ret_ref[col // LANES] = t.astype(BF16)
            else:
                scale = 1.0 if col < ret_width else HEAD_DIM ** -0.5
                swapped = jnp.where(first_half, pltpu.roll(t, LANES - HEAD_DIM // 2, 1),
                                    pltpu.roll(t, HEAD_DIM // 2, 1))
                ret_ref[col // LANES] = ((t * cos + swapped * sin) * scale).astype(BF16)


def _proj(x2d, gain, w_bf16, cos_t, sin_t, n_ret, ret_width, tm=1024):
    t, d = x2d.shape
    n_cols = w_bf16.shape[1]
    n_att = n_cols - n_ret
    seq_tiles = cos_t.shape[0] // tm
    assert ret_width % 512 == 0 and n_ret % 512 == 0 and n_cols % 512 == 0
    return pl.pallas_call(
        functools.partial(_proj_kernel, n_ret=n_ret, ret_width=ret_width, chunk=512),
        grid=(t // tm,),
        in_specs=[
            pl.BlockSpec((tm, d), lambda i: (i, 0)),
            pl.BlockSpec((1, d), lambda i: (0, 0)),
            pl.BlockSpec((d, n_cols), lambda i: (0, 0)),
            pl.BlockSpec((tm, LANES), lambda i: (i % seq_tiles, 0)),
            pl.BlockSpec((tm, LANES), lambda i: (i % seq_tiles, 0)),
        ],
        out_specs=[
            pl.BlockSpec((n_ret // LANES, tm, LANES), lambda i: (0, i, 0)),
            pl.BlockSpec((n_att // LANES, tm, LANES), lambda i: (0, i, 0)),
        ],
        out_shape=[
            jax.ShapeDtypeStruct((n_ret // LANES, t, LANES), BF16),
            jax.ShapeDtypeStruct((n_att // LANES, t, LANES), F32),
        ],
        compiler_params=_params("parallel"),
        name="proj",
    )(x2d, gain, w_bf16, cos_t, sin_t)


def _retention_kernel(decay_ref, q_ref, k_ref, v_ref, g_ref, gn_ref, out_ref,
                      rf_ref, rb_ref, kvf_ref, kvb_ref, p0_ref, p1_ref, *, seq):
    c = RET_CHUNK
    n_chunks = seq // c
    p = pl.program_id(1)
    lane = lax.broadcasted_iota(I32, (c, LANES), 1)
    row = lax.broadcasted_iota(I32, (c, LANES), 0)
    head0 = lane < HEAD_DIM
    rowf = row.astype(F32)

    lgf0, lgf1 = decay_ref[0, 2 * p], decay_ref[0, 2 * p + 1]
    lgb0, lgb1 = decay_ref[1, 2 * p], decay_ref[1, 2 * p + 1]
    lgf_lane = jnp.where(head0, lgf0, lgf1)
    lgb_lane = jnp.where(head0, lgb0, lgb1)
    lgf_row = jnp.where(row < HEAD_DIM, lgf0, lgf1)
    lgb_row = jnp.where(row < HEAD_DIM, lgb0, lgb1)
    same_head = (row < HEAD_DIM) == head0

    zeta_f = jnp.exp((c - 1 - rowf) * lgf_lane)
    zeta_b = jnp.exp(rowf * lgb_lane)
    xi_f = jnp.exp((rowf + 1.0) * lgf_lane)
    xi_b = jnp.exp((c - rowf) * lgb_lane)
    gch_f = jnp.where(same_head, jnp.exp(c * lgf_row), 0.0)
    gch_b = jnp.where(same_head, jnp.exp(c * lgb_row), 0.0)

    diff = (row - lane).astype(F32)

    def dmat(lf, lb):
        return jnp.exp(jnp.where(diff >= 0, diff * lf, -diff * lb))

    d_stack = jnp.concatenate([dmat(lgf0, lgb0), dmat(lgf1, lgb1)], axis=0)

    def kv_step(n, carry):
        sl = pl.ds(pl.multiple_of(n * c, c), c)
        k = k_ref[0, sl, :].astype(F32)
        v = v_ref[0, sl, :]
        kvf = lax.dot_general((k * zeta_f).astype(BF16), v, _TN, preferred_element_type=F32)
        kvb = lax.dot_general((k * zeta_b).astype(BF16), v, _TN, preferred_element_type=F32)
        kvf_ref[n] = jnp.where(same_head, kvf, 0.0)
        kvb_ref[n] = jnp.where(same_head, kvb, 0.0)
        return carry

    lax.fori_loop(0, n_chunks, kv_step, 0, unroll=True)

    def scan_step(i, states):
        sf, sb = states
        nb = n_chunks - 1 - i
        rf_ref[i] = sf.astype(BF16)
        rb_ref[nb] = sb.astype(BF16)
        return gch_f * sf + kvf_ref[i], gch_b * sb + kvb_ref[nb]

    zero_state = jnp.zeros((LANES, LANES), F32)
    lax.fori_loop(0, n_chunks, scan_step, (zero_state, zero_state), unroll=True)

    gn_gain = gn_ref[...]

    def score_stage(g, p_ref):
        for jc in range(RET_GROUP):
            sl = pl.ds(pl.multiple_of((g * RET_GROUP + jc) * c, c), c)
            q = q_ref[0, sl, :]
            zero = jnp.zeros_like(q)
            q_stack = jnp.concatenate([jnp.where(head0, q, zero), jnp.where(head0, zero, q)], axis=0)
            scores = lax.dot_general(q_stack, k_ref[0, sl, :], _NT, preferred_element_type=F32) * d_stack
            sb = scores.astype(BF16)
            p_ref[jc] = jnp.concatenate([sb[:c], sb[c:]], axis=1)

    def half_mean(t):
        s0 = jnp.sum(jnp.where(head0, t, 0.0), axis=-1, keepdims=True)
        s1 = jnp.sum(jnp.where(head0, 0.0, t), axis=-1, keepdims=True)
        return jnp.where(head0, s0, s1) * (1.0 / HEAD_DIM)

    def value_stage(g, p_ref):
        for jc in range(RET_GROUP):
            n = g * RET_GROUP + jc
            sl = pl.ds(pl.multiple_of(n * c, c), c)
            q = q_ref[0, sl, :]
            v = v_ref[0, sl, :]
            zero = jnp.zeros_like(v)
            v_stack = jnp.concatenate([jnp.where(head0, v, zero), jnp.where(head0, zero, v)], axis=0)
            inner = jnp.dot(p_ref[jc], v_stack, preferred_element_type=F32)
            cross_f = jnp.dot(q, rf_ref[n], preferred_element_type=F32) * xi_f
            cross_b = jnp.dot(q, rb_ref[n], preferred_element_type=F32) * xi_b
            r = inner + cross_f + cross_b
            dlt = r - half_mean(r)
            var = half_mean(dlt * dlt)
            gate = g_ref[0, sl, :].astype(F32)
            y = dlt * lax.rsqrt(var + EPS) * gn_gain * (gate * jax.nn.sigmoid(gate))
            out_ref[0, sl, :] = y.astype(out_ref.dtype)

    n_groups = n_chunks // RET_GROUP
    assert n_groups % 2 == 0 and n_groups >= 2
    score_stage(0, p0_ref)

    def group_pair(i, carry):
        g = 2 * i
        score_stage(g + 1, p1_ref)
        value_stage(g, p0_ref)
        score_stage(g + 2, p0_ref)
        value_stage(g + 1, p1_ref)
        return carry

    lax.fori_loop(0, n_groups // 2 - 1, group_pair, 0, unroll=True)
    score_stage(n_groups - 1, p1_ref)
    value_stage(n_groups - 2, p0_ref)
    value_stage(n_groups - 1, p1_ref)


def _retention(ret, decay, gn_gain, n_pairs, b):
    s = ret.shape[1] // b
    blk = lambda off: pl.BlockSpec((1, s, LANES), lambda bi, pi: (off + pi, bi, 0))
    return pl.pallas_call(
        functools.partial(_retention_kernel, seq=s),
        grid=(b, n_pairs),
        in_specs=[
            pl.BlockSpec(memory_space=pltpu.SMEM),
            blk(0), blk(n_pairs), blk(2 * n_pairs), blk(3 * n_pairs),
            pl.BlockSpec((1, LANES), lambda bi, pi: (0, pi)),
        ],
        out_specs=pl.BlockSpec((1, s, LANES), lambda bi, pi: (pi, bi, 0)),
        out_shape=jax.ShapeDtypeStruct((n_pairs, b * s, LANES), BF16),
        scratch_shapes=[
            pltpu.VMEM((s // RET_CHUNK, LANES, LANES), BF16),
            pltpu.VMEM((s // RET_CHUNK, LANES, LANES), BF16),
            pltpu.VMEM((s // RET_CHUNK, LANES, LANES), F32),
            pltpu.VMEM((s // RET_CHUNK, LANES, LANES), F32),
            pltpu.VMEM((RET_GROUP, RET_CHUNK, 2 * RET_CHUNK), BF16),
            pltpu.VMEM((RET_GROUP, RET_CHUNK, 2 * RET_CHUNK), BF16),
        ],
        compiler_params=_params("parallel", "parallel"),
        name="retention",
    )(decay, ret, ret, ret, ret, gn_gain)


QBLK = 2 * SIDE
KWIN = 4 * SIDE
BASE_W = 512
ATT_GROUP = 4


def _t5_bucket_np(rel):
    half = N_BUCKETS // 2
    max_exact = half // 2
    bucket = np.where(rel > 0, half, 0)
    n = np.abs(rel)
    nf = np.maximum(n, 1).astype(np.float32)
    large = max_exact + (np.log(nf / np.float32(max_exact)) / np.float32(math.log(MAX_DISTANCE / max_exact))
                         * np.float32(half - max_exact)).astype(np.int32)
    large = np.minimum(large, half - 1)
    return (bucket + np.where(n < max_exact, n, large)).astype(np.int32)


def _bucket_rows():
    k = np.arange(BASE_W)
    off = k - SIDE
    rows = []
    for _, dilation in DILATED_PATTERNS:
        rows.append(np.where(k <= 2 * SIDE, _t5_bucket_np(off * dilation), -1))
    return np.stack(rows).astype(np.int32)


def _dilated_kernel(bias_tab_ref, bucket_ref, qkv_ref, out_ref,
                    bias_ref, perm_ref, stage_ref, o_ref, lse_ref, e0_ref, e1_ref, m0_ref, m1_ref, *, seq):
    order = list(range(1, len(DILATED_PATTERNS))) + [0]
    dil = [DILATED_PATTERNS[pi][1] for pi in order]
    n_slots = len(order)
    assert n_slots == 3 and dil[0] > 1 and dil[1] % dil[0] == 0 and dil[2] == 1
    p = pl.program_id(1)
    lane = lax.broadcasted_iota(I32, (QBLK, LANES), 1)
    head0 = lane < HEAD_DIM
    head0_k = lax.broadcasted_iota(I32, (KWIN, LANES), 1) < HEAD_DIM
    n_groups = seq // QBLK // ATT_GROUP
    n_loop = n_groups // 2 - 1
    assert n_groups % 2 == 0 and n_loop >= 0

    for pi in range(len(DILATED_PATTERNS)):
        bucket = jnp.broadcast_to(bucket_ref[pi:pi + 1, :], (QBLK, BASE_W))
        for hh in range(2):
            base = jnp.full((QBLK, BASE_W), NEG, F32)
            for bk in range(N_BUCKETS):
                base = jnp.where(bucket == bk, bias_tab_ref[bk, 2 * p + hh], base)
            for var, shift in enumerate((BASE_W - SIDE, 0, SIDE)):
                tile = pltpu.roll(base, shift, 1, stride=1, stride_axis=0)
                bias_ref[pi * 3 + var, hh * QBLK:(hh + 1) * QBLK, :] = tile[:, :KWIN]

    def relayout(slot, i):
        d = dil[slot]
        seg_len = seq // d
        scale = jnp.where(i == 0, HEAD_DIM ** -0.5, 1.0).astype(F32)
        if d == 1:
            perm_ref[slot, i] = (qkv_ref[i, 0] * scale).astype(BF16)
        elif slot == 0:
            for r in range(d):
                val = qkv_ref[i, 0, pl.ds(r, seg_len, stride=d), :]
                stage_ref[i, r * seg_len:(r + 1) * seg_len, :] = val
                perm_ref[slot, i, r * seg_len:(r + 1) * seg_len, :] = (val * scale).astype(BF16)
        else:
            prev_d = dil[slot - 1]
            step, prev_len = d // prev_d, seq // prev_d
            for r_prev in range(prev_d):
                for r_step in range(step):
                    val = stage_ref[i, pl.ds(r_prev * prev_len + r_step, seg_len, stride=step), :]
                    r = r_prev + prev_d * r_step
                    perm_ref[slot, i, r * seg_len:(r + 1) * seg_len, :] = (val * scale).astype(BF16)

    def placement(slot, t):
        seg_len = seq // dil[slot]
        blocks_per_seg = seg_len // QBLK
        seg = t // blocks_per_seg
        u = t % blocks_per_seg
        q0 = pl.multiple_of(t * QBLK, QBLK)
        seg0 = seg * seg_len
        k0 = pl.multiple_of(jnp.clip(q0 - SIDE, seg0, seg0 + seg_len - KWIN), SIDE)
        var = jnp.where(u == 0, 0, jnp.where(u == blocks_per_seg - 1, 2, 1))
        return seg, u, q0, k0, var

    def logits_stage(slot, g, e_ref, m_ref):
        for jb in range(ATT_GROUP):
            _, _, q0, k0, var = placement(slot, g * ATT_GROUP + jb)
            q = perm_ref[slot, 0, pl.ds(q0, QBLK), :]
            kw = perm_ref[slot, 1, pl.ds(k0, KWIN), :]
            zero = jnp.zeros_like(q)
            q_stack = jnp.concatenate([jnp.where(head0, q, zero), jnp.where(head0, zero, q)], axis=0)
            s = lax.dot_general(q_stack, kw, _NT, preferred_element_type=F32) + bias_ref[order[slot] * 3 + var]
            m = jnp.max(s, axis=-1, keepdims=True)
            e_ref[jb] = jnp.exp(s - m).astype(BF16)
            m_ref[jb] = jnp.broadcast_to(m, (2 * QBLK, LANES))

    def value_stage(slot, g, e_ref, m_ref):
        d = dil[slot]
        for jb in range(ATT_GROUP):
            seg, u, q0, k0, _ = placement(slot, g * ATT_GROUP + jb)
            vw = perm_ref[slot, 2, pl.ds(k0, KWIN), :]
            one = jnp.ones_like(vw)
            pv0 = jnp.dot(e_ref[jb, :QBLK, :], jnp.where(head0_k, vw, one), preferred_element_type=F32)
            pv1 = jnp.dot(e_ref[jb, QBLK:, :], jnp.where(head0_k, one, vw), preferred_element_type=F32)
            num = jnp.where(head0, pv0, pv1)
            den = pltpu.roll(jnp.where(head0, pv1, pv0), HEAD_DIM, 1)
            o = num * (1.0 / den)
            lse = jnp.where(head0, m_ref[jb, :QBLK, :], m_ref[jb, QBLK:, :]) + jnp.log(den)
            dst = pl.ds(seg + d * (u * QBLK), QBLK, stride=d) if d > 1 else pl.ds(q0, QBLK)
            if slot < n_slots - 1:
                o_ref[slot, dst, :] = o
                lse_ref[slot, dst, :] = lse
            else:
                lses = [lse_ref[i, dst, :] for i in range(slot)] + [lse]
                outs = [o_ref[i, dst, :] for i in range(slot)] + [o]
                mx = functools.reduce(jnp.maximum, lses)
                ws = [jnp.exp(l - mx) for l in lses]
                inv = 1.0 / functools.reduce(jnp.add, ws)
                acc = functools.reduce(jnp.add, [(w * inv) * t for w, t in zip(ws, outs)])
                out_ref[0, dst, :] = acc.astype(out_ref.dtype)

    for slot in range(n_slots):
        for i in range(3):
            relayout(slot, i)
    logits_stage(0, 0, e0_ref, m0_ref)
    for slot in range(n_slots):
        nxt = slot + 1 if slot + 1 < n_slots else None

        def group_pair(i, carry, slot=slot):
            g = 2 * i
            logits_stage(slot, g + 1, e1_ref, m1_ref)
            value_stage(slot, g, e0_ref, m0_ref)
            logits_stage(slot, g + 2, e0_ref, m0_ref)
            value_stage(slot, g + 1, e1_ref, m1_ref)
            return carry

        lax.fori_loop(0, n_loop, group_pair, 0)
        logits_stage(slot, n_groups - 1, e1_ref, m1_ref)
        value_stage(slot, n_groups - 2, e0_ref, m0_ref)
        if nxt is not None:
            logits_stage(nxt, 0, e0_ref, m0_ref)
        value_stage(slot, n_groups - 1, e1_ref, m1_ref)


def _dilated(att, rel_bias, n_pairs, b):
    s = att.shape[1] // b
    n_pat = len(DILATED_PATTERNS)
    bucket_rows = jnp.asarray(_bucket_rows())
    return pl.pallas_call(
        functools.partial(_dilated_kernel, seq=s),
        grid=(b, n_pairs),
        in_specs=[
            pl.BlockSpec(memory_space=pltpu.SMEM),
            pl.BlockSpec((n_pat, BASE_W), lambda bi, pi: (0, 0)),
            pl.BlockSpec((3, 1, s, LANES), lambda bi, pi: (0, pi, bi, 0)),
        ],
        out_specs=pl.BlockSpec((1, s, LANES), lambda bi, pi: (pi, bi, 0)),
        out_shape=jax.ShapeDtypeStruct((n_pairs, b * s, LANES), BF16),
        scratch_shapes=[
            pltpu.VMEM((n_pat * 3, 2 * QBLK, KWIN), F32),
            pltpu.VMEM((n_pat, 3, s, LANES), BF16),
            pltpu.VMEM((3, s, LANES), F32),
            pltpu.VMEM((n_pat - 1, s, LANES), F32),
            pltpu.VMEM((n_pat - 1, s, LANES), F32),
            pltpu.VMEM((ATT_GROUP, 2 * QBLK, KWIN), BF16),
            pltpu.VMEM((ATT_GROUP, 2 * QBLK, KWIN), BF16),
            pltpu.VMEM((ATT_GROUP, 2 * QBLK, LANES), F32),
            pltpu.VMEM((ATT_GROUP, 2 * QBLK, LANES), F32),
        ],
        compiler_params=_params("parallel", "parallel"),
        name="dilated",
    )(rel_bias, bucket_rows, att.reshape(3, n_pairs, b * s, LANES))


def _outproj_kernel(r_ref, a_ref, x_ref, w_ref, g_ref, wr_ref, x1_ref, h2_ref, aff_ref):
    n_e = aff_ref.shape[1]
    wr = wr_ref[...]
    w_hi = wr.astype(BF16)
    w_lo = (wr - w_hi.astype(F32)).astype(BF16)
    w_parts = jnp.where(lax.broadcasted_iota(I32, wr.shape, 1) < n_e, w_hi, w_lo)
    tm = x_ref.shape[0]
    mixed = jnp.concatenate([r_ref[i] for i in range(r_ref.shape[0])]
                            + [a_ref[i] for i in range(a_ref.shape[0])], axis=1)
    x1 = x_ref[...] + jnp.dot(mixed, w_ref[...], preferred_element_type=F32)
    x1_ref[...] = x1
    ms = jnp.mean(x1 * x1, axis=-1, keepdims=True)
    h2 = x1 * lax.rsqrt(ms + EPS) * g_ref[...]
    h_hi = h2.astype(BF16)
    h2_ref[...] = h_hi
    h_lo = (h2 - h_hi.astype(F32)).astype(BF16)
    prod = jnp.dot(jnp.concatenate([h_hi, h_lo], axis=0), w_parts, preferred_element_type=F32)
    logits = prod[:tm] + (pltpu.roll(prod[:tm], LANES - n_e, 1) + prod[tm:])
    expert_lane = lax.broadcasted_iota(I32, logits.shape, 1) < n_e
    logits = jnp.where(expert_lane, logits, NEG)
    m = jnp.max(logits, axis=1, keepdims=True)
    e = jnp.exp(logits - m)
    aff = e / jnp.sum(e, axis=1, keepdims=True)
    aff_ref[0] = aff.T[:n_e, :]


def _outproj(r, a, x, w_bf16, gain, w_router, tm=512):
    b, s, d = x.shape
    n_r, n_a = r.shape[0], a.shape[0]
    n_e = w_router.shape[1]
    assert 2 * n_e <= LANES
    w_router = jnp.pad(jnp.concatenate([w_router, w_router], axis=1), ((0, 0), (0, LANES - 2 * n_e)))
    spt = s // tm
    tok = lambda width: pl.BlockSpec((tm, width), lambda i: (i, 0))
    slab = lambda blocks: pl.BlockSpec((blocks, tm, LANES), lambda i: (0, i, 0))
    const = lambda shape: pl.BlockSpec(shape, lambda i: (0, 0))
    return pl.pallas_call(
        _outproj_kernel,
        grid=(b * spt,),
        in_specs=[slab(n_r), slab(n_a), tok(d), const(((n_r + n_a) * LANES, d)), const((1, d)), const((d, LANES))],
        out_specs=[tok(d), tok(d), pl.BlockSpec((1, n_e, tm), lambda i: (i // spt, 0, i % spt))],
        out_shape=[
            jax.ShapeDtypeStruct((b * s, d), F32),
            jax.ShapeDtypeStruct((b * s, d), BF16),
            jax.ShapeDtypeStruct((b, n_e, s), F32),
        ],
        compiler_params=_params("parallel"),
        name="outproj",
    )(r, a, x.reshape(b * s, d), w_bf16, gain, w_router)


def _select_kernel(aff_ref, posm_ref, cnt_ref, *, cap):
    rows, s = aff_ref.shape
    bits = pltpu.bitcast(aff_ref[...], I32)

    def search(i, thr):
        cand = thr | jnp.left_shift(jnp.int32(1), 30 - i)
        cnt = jnp.sum((bits >= cand).astype(I32), axis=1, keepdims=True)
        return jnp.where(cnt >= cap, cand, thr)

    thr = lax.fori_loop(0, 31, search, jnp.zeros((rows, 1), I32))
    gt = bits > thr
    eq = bits == thr
    need = cap - jnp.sum(gt.astype(I32), axis=1, keepdims=True)

    w = LANES
    tri = (lax.broadcasted_iota(I32, (w, w), 0) <= lax.broadcasted_iota(I32, (w, w), 1)).astype(BF16)

    def excl_prefix(flags):
        carry = jnp.zeros((rows, 1), F32)
        out = []
        for c0 in range(0, s, w):
            f = flags[:, c0:c0 + w].astype(BF16)
            inc = jnp.dot(f, tri, preferred_element_type=F32)
            out.append(inc - f.astype(F32) + carry)
            carry = carry + inc[:, w - 1:w]
        return jnp.concatenate(out, axis=1).astype(I32)

    sel = gt | (eq & (excl_prefix(eq) < need))
    posm_ref[...] = jnp.where(sel, excl_prefix(sel), -1)

    tok = lax.broadcasted_iota(I32, (w, w), 0)
    edge = lax.broadcasted_iota(I32, (w, w), 1) * MOE_TILE
    cnt = jnp.zeros((rows, w), F32)
    for c0 in range(0, s, w):
        before = ((tok + c0) < edge).astype(BF16)
        cnt = cnt + jnp.dot(sel[:, c0:c0 + w].astype(BF16), before, preferred_element_type=F32)
    cnt_ref[...] = cnt.astype(I32)


def _select(aff_rows, cap):
    rows, s = aff_rows.shape
    return pl.pallas_call(
        functools.partial(_select_kernel, cap=cap),
        out_shape=[jax.ShapeDtypeStruct((rows, s), I32), jax.ShapeDtypeStruct((rows, LANES), I32)],
        compiler_params=pltpu.CompilerParams(vmem_limit_bytes=VMEM_LIMIT),
        name="select",
    )(aff_rows)


def _slot_window(cnt_ref, row, j, cap):
    lo, hi = cnt_ref[row * CNT_STRIDE + j], cnt_ref[row * CNT_STRIDE + j + 1]
    w0 = pl.multiple_of(jnp.minimum((lo // SLOT_ALIGN) * SLOT_ALIGN, cap - SLOT_WIN), SLOT_ALIGN)
    return hi, w0


def _any_overflow(windows):
    return functools.reduce(jnp.logical_or, [hi > w0 + SLOT_WIN for hi, w0 in windows])


def _gather_kernel(cnt_ref, posm_ref, h2_ref, xg_ref):
    bi, j = pl.program_id(0), pl.program_id(1)
    n_e, cap = xg_ref.shape[1:3]
    tt = MOE_TILE

    @pl.when(j == 0)
    def _():
        xg_ref[...] = jnp.zeros_like(xg_ref)

    slot0 = lax.broadcasted_iota(I32, (SLOT_WIN, tt), 0)
    for sub in range(h2_ref.shape[1] // tt):
        tile = j * (h2_ref.shape[1] // tt) + sub
        tok = slice(sub * tt, (sub + 1) * tt)
        windows = [_slot_window(cnt_ref, bi * n_e + e, tile, cap) for e in range(n_e)]
        for e in range(0, n_e, WIN_GROUP):
            onehot = jnp.concatenate(
                [(posm_ref[0, e + i:e + i + 1, tok] == slot0 + windows[e + i][1]).astype(BF16)
                 for i in range(WIN_GROUP)], axis=0)
            rows = jnp.dot(onehot, h2_ref[0, tok, :], preferred_element_type=F32).astype(BF16)
            for i in range(WIN_GROUP):
                xg_ref[0, e + i, pl.ds(windows[e + i][1], SLOT_WIN), :] += rows[i * SLOT_WIN:(i + 1) * SLOT_WIN]

        @pl.when(_any_overflow(windows))
        def _(tile=tile, tok=tok):
            wide = lax.broadcasted_iota(I32, (MXU_DEPTH, tt), 0)

            def one_expert(e, carry):
                hi, w0 = _slot_window(cnt_ref, bi * n_e + e, tile, cap)

                @pl.when(hi > w0 + SLOT_WIN)
                def _():
                    pos = posm_ref[0, pl.ds(e, 1), tok]
                    for ws in range(0, cap, MXU_DEPTH):
                        slot = wide + ws
                        onehot = ((pos == slot) & (slot >= w0 + SLOT_WIN)).astype(BF16)
                        extra = jnp.dot(onehot, h2_ref[0, tok, :], preferred_element_type=F32)
                        xg_ref[0, e, ws:ws + MXU_DEPTH, :] += extra.astype(BF16)

                return carry

            lax.fori_loop(0, n_e, one_expert, 0)


def _gather(cnt, posm, h2, cap):
    b, n_e, s = posm.shape
    d = h2.shape[2]
    step = MOE_TILE * GATHER_TILES
    return pl.pallas_call(
        _gather_kernel,
        grid_spec=pltpu.PrefetchScalarGridSpec(
            num_scalar_prefetch=1,
            grid=(b, s // step),
            in_specs=[
                pl.BlockSpec((1, n_e, step), lambda bi, j, cnt: (bi, 0, j)),
                pl.BlockSpec((1, step, d), lambda bi, j, cnt: (bi, j, 0)),
            ],
            out_specs=pl.BlockSpec((1, n_e, cap, d), lambda bi, j, cnt: (bi, 0, 0, 0)),
        ),
        out_shape=jax.ShapeDtypeStruct((b, n_e, cap, d), BF16),
        compiler_params=_params("parallel", "arbitrary"),
        name="gather",
    )(cnt, posm, h2)


def _ffn_kernel(xg_ref, wg_hbm, wu_hbm, wd_hbm, y_ref, wg_buf, wu_buf, wd_buf, wg_stage, wu_stage, wd_stage,
                sem, acc_ref, *, tf):
    e, bi = pl.program_id(0), pl.program_id(1)
    n_e, n_b = pl.num_programs(0), pl.num_programs(1)
    rows_in, rows_dn = wg_stage.shape[1], wd_stage.shape[1]
    f_total = wg_buf.shape[2]

    def chunk_copies(expert, c):
        st = c % 2
        return (
            pltpu.make_async_copy(wg_hbm.at[expert, pl.ds(c * rows_in, rows_in), :], wg_stage.at[st], sem.at[0, st]),
            pltpu.make_async_copy(wu_hbm.at[expert, pl.ds(c * rows_in, rows_in), :], wu_stage.at[st], sem.at[1, st]),
            pltpu.make_async_copy(wd_hbm.at[expert, pl.ds(c * rows_dn, rows_dn), :], wd_stage.at[st], sem.at[2, st]),
        )

    def finish_chunk(expert, c):
        for cp in chunk_copies(expert, c):
            cp.wait()
        slot, st = expert % 2, c % 2
        wg_buf[slot, pl.ds(c * rows_in, rows_in), :] = wg_stage[st].astype(BF16)
        wu_buf[slot, pl.ds(c * rows_in, rows_in), :] = wu_stage[st].astype(BF16)
        wd_buf[slot, pl.ds(c * rows_dn, rows_dn), :] = wd_stage[st].astype(BF16)

    @pl.when((e == 0) & (bi == 0))
    def _():
        for cp in chunk_copies(0, 0):
            cp.start()

        def load_first(c, carry):
            @pl.when(c + 1 < n_b)
            def _():
                for cp in chunk_copies(0, c + 1):
                    cp.start()

            finish_chunk(0, c)
            return carry

        lax.fori_loop(0, n_b, load_first, 0)

    pending = jnp.where(bi > 0, e + 1 < n_e, e > 0)

    @pl.when(pending)
    def _():
        finish_chunk(jnp.where(bi > 0, e + 1, e), jnp.where(bi > 0, bi - 1, n_b - 1))

    @pl.when(e + 1 < n_e)
    def _():
        for cp in chunk_copies(e + 1, bi):
            cp.start()

    def swiglu(slot):
        xg = xg_ref[0, 0]
        for fi in range(f_total // tf):
            fs = slice(fi * tf, (fi + 1) * tf)
            gate = jnp.dot(xg, wg_buf[slot, :, fs], preferred_element_type=F32)
            up = jnp.dot(xg, wu_buf[slot, :, fs], preferred_element_type=F32)
            hid = ((gate * jax.nn.sigmoid(gate)) * up).astype(BF16)
            part = jnp.dot(hid, wd_buf[slot, fs, :], preferred_element_type=F32)
            if fi == 0:
                acc_ref[...] = part
            else:
                acc_ref[...] += part
        y_ref[0, 0] = acc_ref[...].astype(y_ref.dtype)

    for slot in range(2):
        pl.when(e % 2 == slot)(functools.partial(swiglu, slot))


def _ffn(xg, wg, wu, wd, tf=512):
    b, n_e, cap, d = xg.shape
    f = wg.shape[2]
    assert d % b == 0 and f % b == 0 and (d // b) % SLOT_ALIGN == 0, "one weight row chunk per batch step"
    tok = pl.BlockSpec((1, 1, cap, d), lambda e, bi: (bi, e, 0, 0))
    hbm = pl.BlockSpec(memory_space=pl.ANY)
    return pl.pallas_call(
        functools.partial(_ffn_kernel, tf=tf),
        grid=(n_e, b),
        in_specs=[tok, hbm, hbm, hbm],
        out_specs=tok,
        out_shape=jax.ShapeDtypeStruct((b, n_e, cap, d), BF16),
        scratch_shapes=[
            pltpu.VMEM((2, d, f), BF16), pltpu.VMEM((2, d, f), BF16), pltpu.VMEM((2, f, d), BF16),
            pltpu.VMEM((2, d // b, f), F32), pltpu.VMEM((2, d // b, f), F32), pltpu.VMEM((2, f // b, d), F32),
            pltpu.SemaphoreType.DMA((3, 2)),
            pltpu.VMEM((cap, d), F32),
        ],
        compiler_params=_params("arbitrary", "arbitrary"),
        name="ffn",
    )(xg, wg, wu, wd)


def _combine_kernel(cnt_ref, pos_ref, gate_ref, y_ref, x1_ref, g_ref, out_ref, acc_ref):
    bi, j = pl.program_id(0), pl.program_id(1)
    tt = MOE_TILE
    n_e, cap = y_ref.shape[1:3]
    slot0 = lax.broadcasted_iota(I32, (SLOT_WIN, tt), 0)
    for sub in range(x1_ref.shape[1] // tt):
        tile = j * (x1_ref.shape[1] // tt) + sub
        tok = slice(sub * tt, (sub + 1) * tt)
        windows = [_slot_window(cnt_ref, bi * n_e + e, tile, cap) for e in range(n_e)]
        acc = x1_ref[0, tok, :]
        for e in range(0, n_e, WIN_GROUP):
            scatter_t = jnp.concatenate(
                [jnp.where(pos_ref[0, e + i:e + i + 1, tok] == slot0 + windows[e + i][1],
                           gate_ref[0, e + i:e + i + 1, tok], 0.0).astype(BF16) for i in range(WIN_GROUP)], axis=0)
            y_group = jnp.concatenate(
                [y_ref[0, e + i, pl.ds(windows[e + i][1], SLOT_WIN), :] for i in range(WIN_GROUP)], axis=0)
            acc = acc + lax.dot_general(scatter_t, y_group, _TN, preferred_element_type=F32)
        acc_ref[...] = acc

        @pl.when(_any_overflow(windows))
        def _(tile=tile, tok=tok):
            wide = lax.broadcasted_iota(I32, (MXU_DEPTH, tt), 0)

            def one_expert(e, carry):
                hi, w0 = _slot_window(cnt_ref, bi * n_e + e, tile, cap)

                @pl.when(hi > w0 + SLOT_WIN)
                def _():
                    pos = pos_ref[0, pl.ds(e, 1), tok]
                    gate = gate_ref[0, pl.ds(e, 1), tok]
                    for ws in range(0, cap, MXU_DEPTH):
                        slot = wide + ws
                        extra = jnp.where((pos == slot) & (slot >= w0 + SLOT_WIN), gate, 0.0).astype(BF16)
                        acc_ref[...] += lax.dot_general(extra, y_ref[0, e, ws:ws + MXU_DEPTH, :], _TN,
                                                        preferred_element_type=F32)

                return carry

            lax.fori_loop(0, n_e, one_expert, 0)

        acc = acc_ref[...]
        ms = jnp.mean(acc * acc, axis=-1, keepdims=True)
        out_ref[0, tok, :] = acc * lax.rsqrt(ms + EPS) * g_ref[...]


def _combine(cnt, posm, aff, y, x1, gain):
    b, s, d = x1.shape
    n_e, cap = y.shape[1:3]
    step = MOE_TILE * COMBINE_TILES
    return pl.pallas_call(
        _combine_kernel,
        grid_spec=pltpu.PrefetchScalarGridSpec(
            num_scalar_prefetch=1,
            grid=(b, s // step),
            in_specs=[
                pl.BlockSpec((1, n_e, step), lambda bi, j, cnt: (bi, 0, j)),
                pl.BlockSpec((1, n_e, step), lambda bi, j, cnt: (bi, 0, j)),
                pl.BlockSpec((1, n_e, cap, d), lambda bi, j, cnt: (bi, 0, 0, 0)),
                pl.BlockSpec((1, step, d), lambda bi, j, cnt: (bi, j, 0)),
                pl.BlockSpec((1, d), lambda bi, j, cnt: (0, 0)),
            ],
            out_specs=pl.BlockSpec((1, step, d), lambda bi, j, cnt: (bi, j, 0)),
            scratch_shapes=[pltpu.VMEM((MOE_TILE, d), F32)],
        ),
        out_shape=jax.ShapeDtypeStruct((b, s, d), F32),
        compiler_params=_params("parallel", "parallel"),
        name="combine",
    )(cnt, posm, aff, y, x1, gain)


def _rope_tables(seq):
    half = HEAD_DIM // 2
    pos = jnp.arange(seq, dtype=F32)
    inv = ROPE_BASE ** (-jnp.arange(0, HEAD_DIM, 2, dtype=F32) / HEAD_DIM)
    ang = pos[:, None] * inv[None, :]
    cos, sin = jnp.cos(ang), jnp.sin(ang)
    reps = LANES // HEAD_DIM
    cos_t = jnp.tile(jnp.concatenate([cos, cos], axis=1), (1, reps))
    sin_t = jnp.tile(jnp.concatenate([-sin, sin], axis=1), (1, reps))
    return cos_t, sin_t


def kernel(x, norm1_gain, w_in, ret_log_decay, ret_gn_gain, rel_bias, w_out, norm2_gain, w_router,
           w_gate, w_up, w_down, final_gain):
    b, s, d = x.shape
    depth = w_in.shape[0]
    ret_width = ret_gn_gain.shape[1]
    n_ret_pairs = ret_width // LANES
    att_width = (w_in.shape[2] - 4 * ret_width) // 3
    n_att_pairs = att_width // LANES
    cap = CAPACITY_FACTOR * s // N_EXPERTS
    cos_t, sin_t = _rope_tables(s)

    assert depth == 1, "single-layer block: the final norm is fused into the combine kernel"
    layer = 0
    ret, att = _proj(x.reshape(b * s, d), norm1_gain[layer][None, :], w_in[layer].astype(BF16),
                     cos_t, sin_t, n_ret=4 * ret_width, ret_width=ret_width)
    r = _retention(ret, ret_log_decay[layer], ret_gn_gain[layer][None, :], n_ret_pairs, b)
    a = _dilated(att, rel_bias, n_att_pairs, b)
    x1, h2, aff = _outproj(r, a, x, w_out[layer].astype(BF16), norm2_gain[layer][None, :],
                           w_router[layer])
    assert s % (MOE_TILE * max(GATHER_TILES, COMBINE_TILES)) == 0 and s // MOE_TILE < CNT_STRIDE and cap % MXU_DEPTH == 0 and N_EXPERTS % WIN_GROUP == 0
    posm, cnt = _select(aff.reshape(b * N_EXPERTS, s), cap)
    posm = posm.reshape(b, N_EXPERTS, s)
    cnt = cnt[:, :CNT_STRIDE].reshape(-1)
    xg = _gather(cnt, posm, h2.reshape(b, s, d), cap)
    y = _ffn(xg, w_gate[layer], w_up[layer], w_down[layer])
    return _combine(cnt, posm, aff, y, x1.reshape(b, s, d),
                    final_gain[None, :])
```

```python
import functools
import math

import numpy as np
import jax
import jax.numpy as jnp
from jax import lax
from jax.experimental import pallas as pl
from jax.experimental.pallas import tpu as pltpu

F32 = jnp.float32
BF16 = jnp.bfloat16
I32 = jnp.int32

HEAD_DIM = 64
LANES = 128
RET_CHUNK = 128
RET_GROUP = 4
ROPE_BASE = 10000.0
DILATED_PATTERNS = ((128, 1), (512, 4), (2048, 16))
SIDE = 64
N_BUCKETS = 32
MAX_DISTANCE = 1024
N_EXPERTS = 16
CAPACITY_FACTOR = 2
MOE_TILE = 256
GATHER_TILES = 8
COMBINE_TILES = 4
MXU_DEPTH = 256
WIN_GROUP = 4
SLOT_WIN = MXU_DEPTH // WIN_GROUP
SLOT_ALIGN = 16
CNT_STRIDE = 32
EPS = 1e-6
NEG = -1e30
VMEM_LIMIT = 56 * 1024 * 1024

_NT = (((1,), (1,)), ((), ()))
_TN = (((0,), (0,)), ((), ()))


def _params(*sem):
    return pltpu.CompilerParams(dimension_semantics=sem, vmem_limit_bytes=VMEM_LIMIT)


def _proj_kernel(x_ref, g_ref, w_ref, cos_ref, sin_ref, ret_ref, att_ref, *, n_ret, ret_width, chunk):
    x = x_ref[...]
    ms = jnp.mean(x * x, axis=-1, keepdims=True)
    h = (x * lax.rsqrt(ms + EPS) * g_ref[...]).astype(BF16)
    n_cols = w_ref.shape[1]
    cos, sin = cos_ref[...], sin_ref[...]
    first_half = (lax.broadcasted_iota(I32, cos.shape, 1) % HEAD_DIM) < (HEAD_DIM // 2)
    for c0 in range(0, n_cols, chunk):
        o = jnp.dot(h, w_ref[:, c0:c0 + chunk], preferred_element_type=F32)
        for l0 in range(0, chunk, LANES):
            t = o[:, l0:l0 + LANES]
            col = c0 + l0
            if col >= n_ret:
                att_ref[(col - n_ret) // LANES] = t
            elif col >= 2 * ret_width:
                ret_ref[col // LANES] = t.astype(BF16)
            else:
                scale = 1.0 if col < ret_width else HEAD_DIM ** -0.5
                swapped = jnp.where(first_half, pltpu.roll(t, LANES - HEAD_DIM // 2, 1),
                                    pltpu.roll(t, HEAD_DIM // 2, 1))
                ret_ref[col // LANES] = ((t * cos + swapped * sin) * scale).astype(BF16)


def _proj(x2d, gain, w_bf16, cos_t, sin_t, n_ret, ret_width, tm=1024):
    t, d = x2d.shape
    n_cols = w_bf16.shape[1]
    n_att = n_cols - n_ret
    seq_tiles = cos_t.shape[0] // tm
    assert ret_width % 512 == 0 and n_ret % 512 == 0 and n_cols % 512 == 0
    return pl.pallas_call(
        functools.partial(_proj_kernel, n_ret=n_ret, ret_width=ret_width, chunk=512),
        grid=(t // tm,),
        in_specs=[
            pl.BlockSpec((tm, d), lambda i: (i, 0)),
            pl.BlockSpec((1, d), lambda i: (0, 0)),
            pl.BlockSpec((d, n_cols), lambda i: (0, 0)),
            pl.BlockSpec((tm, LANES), lambda i: (i % seq_tiles, 0)),
            pl.BlockSpec((tm, LANES), lambda i: (i % seq_tiles, 0)),
        ],
        out_specs=[
            pl.BlockSpec((n_ret // LANES, tm, LANES), lambda i: (0, i, 0)),
            pl.BlockSpec((n_att // LANES, tm, LANES), lambda i: (0, i, 0)),
        ],
        out_shape=[
            jax.ShapeDtypeStruct((n_ret // LANES, t, LANES), BF16),
            jax.ShapeDtypeStruct((n_att // LANES, t, LANES), F32),
        ],
        compiler_params=_params("parallel"),
        name="proj",
    )(x2d, gain, w_bf16, cos_t, sin_t)


def _retention_kernel(decay_ref, q_ref, k_ref, v_ref, g_ref, gn_ref, out_ref,
                      rf_ref, rb_ref, kvf_ref, kvb_ref, p0_ref, p1_ref, *, seq):
    c = RET_CHUNK
    n_chunks = seq // c
    p = pl.program_id(1)
    lane = lax.broadcasted_iota(I32, (c, LANES), 1)
    row = lax.broadcasted_iota(I32, (c, LANES), 0)
    head0 = lane < HEAD_DIM
    rowf = row.astype(F32)

    lgf0, lgf1 = decay_ref[0, 2 * p], decay_ref[0, 2 * p + 1]
    lgb0, lgb1 = decay_ref[1, 2 * p], decay_ref[1, 2 * p + 1]
    lgf_lane = jnp.where(head0, lgf0, lgf1)
    lgb_lane = jnp.where(head0, lgb0, lgb1)
    lgf_row = jnp.where(row < HEAD_DIM, lgf0, lgf1)
    lgb_row = jnp.where(row < HEAD_DIM, lgb0, lgb1)
    same_head = (row < HEAD_DIM) == head0

    zeta_f = jnp.exp((c - 1 - rowf) * lgf_lane)
    zeta_b = jnp.exp(rowf * lgb_lane)
    xi_f = jnp.exp((rowf + 1.0) * lgf_lane)
    xi_b = jnp.exp((c - rowf) * lgb_lane)
    gch_f = jnp.where(same_head, jnp.exp(c * lgf_row), 0.0)
    gch_b = jnp.where(same_head, jnp.exp(c * lgb_row), 0.0)

    diff = (row - lane).astype(F32)

    def dmat(lf, lb):
        return jnp.exp(jnp.where(diff >= 0, diff * lf, -diff * lb))

    d_stack = jnp.concatenate([dmat(lgf0, lgb0), dmat(lgf1, lgb1)], axis=0)

    def kv_step(n, carry):
        sl = pl.ds(pl.multiple_of(n * c, c), c)
        k = k_ref[0, sl, :].astype(F32)
        v = v_ref[0, sl, :]
        kvf = lax.dot_general((k * zeta_f).astype(BF16), v, _TN, preferred_element_type=F32)
        kvb = lax.dot_general((k * zeta_b).astype(BF16), v, _TN, preferred_element_type=F32)
        kvf_ref[n] = jnp.where(same_head, kvf, 0.0)
        kvb_ref[n] = jnp.where(same_head, kvb, 0.0)
        return carry

    lax.fori_loop(0, n_chunks, kv_step, 0, unroll=True)

    def scan_step(i, states):
        sf, sb = states
        nb = n_chunks - 1 - i
        rf_ref[i] = sf.astype(BF16)
        rb_ref[nb] = sb.astype(BF16)
        return gch_f * sf + kvf_ref[i], gch_b * sb + kvb_ref[nb]

    zero_state = jnp.zeros((LANES, LANES), F32)
    lax.fori_loop(0, n_chunks, scan_step, (zero_state, zero_state), unroll=True)

    gn_gain = gn_ref[...]

    def score_stage(g, p_ref):
        for jc in range(RET_GROUP):
            sl = pl.ds(pl.multiple_of((g * RET_GROUP + jc) * c, c), c)
            q = q_ref[0, sl, :]
            zero = jnp.zeros_like(q)
            q_stack = jnp.concatenate([jnp.where(head0, q, zero), jnp.where(head0, zero, q)], axis=0)
            scores = lax.dot_general(q_stack, k_ref[0, sl, :], _NT, preferred_element_type=F32) * d_stack
            sb = scores.astype(BF16)
            p_ref[jc] = jnp.concatenate([sb[:c], sb[c:]], axis=1)

    def half_mean(t):
        s0 = jnp.sum(jnp.where(head0, t, 0.0), axis=-1, keepdims=True)
        s1 = jnp.sum(jnp.where(head0, 0.0, t), axis=-1, keepdims=True)
        return jnp.where(head0, s0, s1) * (1.0 / HEAD_DIM)

    def value_stage(g, p_ref):
        for jc in range(RET_GROUP):
            n = g * RET_GROUP + jc
            sl = pl.ds(pl.multiple_of(n * c, c), c)
            q = q_ref[0, sl, :]
            v = v_ref[0, sl, :]
            zero = jnp.zeros_like(v)
            v_stack = jnp.concatenate([jnp.where(head0, v, zero), jnp.where(head0, zero, v)], axis=0)
            inner = jnp.dot(p_ref[jc], v_stack, preferred_element_type=F32)
            cross_f = jnp.dot(q, rf_ref[n], preferred_element_type=F32) * xi_f
            cross_b = jnp.dot(q, rb_ref[n], preferred_element_type=F32) * xi_b
            r = inner + cross_f + cross_b
            dlt = r - half_mean(r)
            var = half_mean(dlt * dlt)
            gate = g_ref[0, sl, :].astype(F32)
            y = dlt * lax.rsqrt(var + EPS) * gn_gain * (gate * jax.nn.sigmoid(gate))
            out_ref[0, sl, :] = y.astype(out_ref.dtype)

    n_groups = n_chunks // RET_GROUP
    assert n_groups % 2 == 0 and n_groups >= 2
    score_stage(0, p0_ref)

    def group_pair(i, carry):
        g = 2 * i
        score_stage(g + 1, p1_ref)
        value_stage(g, p0_ref)
        score_stage(g + 2, p0_ref)
        value_stage(g + 1, p1_ref)
        return carry

    lax.fori_loop(0, n_groups // 2 - 1, group_pair, 0, unroll=True)
    score_stage(n_groups - 1, p1_ref)
    value_stage(n_groups - 2, p0_ref)
    value_stage(n_groups - 1, p1_ref)


def _retention(ret, decay, gn_gain, n_pairs, b):
    s = ret.shape[1] // b
    blk = lambda off: pl.BlockSpec((1, s, LANES), lambda bi, pi: (off + pi, bi, 0))
    return pl.pallas_call(
        functools.partial(_retention_kernel, seq=s),
        grid=(b, n_pairs),
        in_specs=[
            pl.BlockSpec(memory_space=pltpu.SMEM),
            blk(0), blk(n_pairs), blk(2 * n_pairs), blk(3 * n_pairs),
            pl.BlockSpec((1, LANES), lambda bi, pi: (0, pi)),
        ],
        out_specs=pl.BlockSpec((1, s, LANES), lambda bi, pi: (pi, bi, 0)),
        out_shape=jax.ShapeDtypeStruct((n_pairs, b * s, LANES), BF16),
        scratch_shapes=[
            pltpu.VMEM((s // RET_CHUNK, LANES, LANES), BF16),
            pltpu.VMEM((s // RET_CHUNK, LANES, LANES), BF16),
            pltpu.VMEM((s // RET_CHUNK, LANES, LANES), F32),
            pltpu.VMEM((s // RET_CHUNK, LANES, LANES), F32),
            pltpu.VMEM((RET_GROUP, RET_CHUNK, 2 * RET_CHUNK), BF16),
            pltpu.VMEM((RET_GROUP, RET_CHUNK, 2 * RET_CHUNK), BF16),
        ],
        compiler_params=_params("parallel", "parallel"),
        name="retention",
    )(decay, ret, ret, ret, ret, gn_gain)


QBLK = 2 * SIDE
KWIN = 4 * SIDE
BASE_W = 512
ATT_GROUP = 4


def _t5_bucket_np(rel):
    half = N_BUCKETS // 2
    max_exact = half // 2
    bucket = np.where(rel > 0, half, 0)
    n = np.abs(rel)
    nf = np.maximum(n, 1).astype(np.float32)
    large = max_exact + (np.log(nf / np.float32(max_exact)) / np.float32(math.log(MAX_DISTANCE / max_exact))
                         * np.float32(half - max_exact)).astype(np.int32)
    large = np.minimum(large, half - 1)
    return (bucket + np.where(n < max_exact, n, large)).astype(np.int32)


def _bucket_rows():
    k = np.arange(BASE_W)
    off = k - SIDE
    rows = []
    for _, dilation in DILATED_PATTERNS:
        rows.append(np.where(k <= 2 * SIDE, _t5_bucket_np(off * dilation), -1))
    return np.stack(rows).astype(np.int32)


def _dilated_kernel(bias_tab_ref, bucket_ref, qkv_ref, out_ref,
                    bias_ref, perm_ref, stage_ref, o_ref, lse_ref, e0_ref, e1_ref, m0_ref, m1_ref, *, seq):
    order = list(range(1, len(DILATED_PATTERNS))) + [0]
    dil = [DILATED_PATTERNS[pi][1] for pi in order]
    n_slots = len(order)
    assert n_slots == 3 and dil[0] > 1 and dil[1] % dil[0] == 0 and dil[2] == 1
    p = pl.program_id(1)
    lane = lax.broadcasted_iota(I32, (QBLK, LANES), 1)
    head0 = lane < HEAD_DIM
    head0_k = lax.broadcasted_iota(I32, (KWIN, LANES), 1) < HEAD_DIM
    n_groups = seq // QBLK // ATT_GROUP
    n_loop = n_groups // 2 - 1
    assert n_groups % 2 == 0 and n_loop >= 0

    for pi in range(len(DILATED_PATTERNS)):
        bucket = jnp.broadcast_to(bucket_ref[pi:pi + 1, :], (QBLK, BASE_W))
        for hh in range(2):
            base = jnp.full((QBLK, BASE_W), NEG, F32)
            for bk in range(N_BUCKETS):
                base = jnp.where(bucket == bk, bias_tab_ref[bk, 2 * p + hh], base)
            for var, shift in enumerate((BASE_W - SIDE, 0, SIDE)):
                tile = pltpu.roll(base, shift, 1, stride=1, stride_axis=0)
                bias_ref[pi * 3 + var, hh * QBLK:(hh + 1) * QBLK, :] = tile[:, :KWIN]

    def relayout(slot, i):
        d = dil[slot]
        seg_len = seq // d
        scale = jnp.where(i == 0, HEAD_DIM ** -0.5, 1.0).astype(F32)
        if d == 1:
            perm_ref[slot, i] = (qkv_ref[i, 0] * scale).astype(BF16)
        elif slot == 0:
            for r in range(d):
                val = qkv_ref[i, 0, pl.ds(r, seg_len, stride=d), :]
                stage_ref[i, r * seg_len:(r + 1) * seg_len, :] = val
                perm_ref[slot, i, r * seg_len:(r + 1) * seg_len, :] = (val * scale).astype(BF16)
        else:
            prev_d = dil[slot - 1]
            step, prev_len = d // prev_d, seq // prev_d
            for r_prev in range(prev_d):
                for r_step in range(step):
                    val = stage_ref[i, pl.ds(r_prev * prev_len + r_step, seg_len, stride=step), :]
                    r = r_prev + prev_d * r_step
                    perm_ref[slot, i, r * seg_len:(r + 1) * seg_len, :] = (val * scale).astype(BF16)

    def placement(slot, t):
        seg_len = seq // dil[slot]
        blocks_per_seg = seg_len // QBLK
        seg = t // blocks_per_seg
        u = t % blocks_per_seg
        q0 = pl.multiple_of(t * QBLK, QBLK)
        seg0 = seg * seg_len
        k0 = pl.multiple_of(jnp.clip(q0 - SIDE, seg0, seg0 + seg_len - KWIN), SIDE)
        var = jnp.where(u == 0, 0, jnp.where(u == blocks_per_seg - 1, 2, 1))
        return seg, u, q0, k0, var

    def logits_stage(slot, g, e_ref, m_ref):
        for jb in range(ATT_GROUP):
            _, _, q0, k0, var = placement(slot, g * ATT_GROUP + jb)
            q = perm_ref[slot, 0, pl.ds(q0, QBLK), :]
            kw = perm_ref[slot, 1, pl.ds(k0, KWIN), :]
            zero = jnp.zeros_like(q)
            q_stack = jnp.concatenate([jnp.where(head0, q, zero), jnp.where(head0, zero, q)], axis=0)
            s = lax.dot_general(q_stack, kw, _NT, preferred_element_type=F32) + bias_ref[order[slot] * 3 + var]
            m = jnp.max(s, axis=-1, keepdims=True)
            e_ref[jb] = jnp.exp(s - m).astype(BF16)
            m_ref[jb] = jnp.where(head0, m[:QBLK], m[QBLK:])

    def value_stage(slot, g, e_ref, m_ref):
        d = dil[slot]
        for jb in range(ATT_GROUP):
            seg, u, q0, k0, _ = placement(slot, g * ATT_GROUP + jb)
            vw = perm_ref[slot, 2, pl.ds(k0, KWIN), :]
            one = jnp.ones_like(vw)
            pv0 = jnp.dot(e_ref[jb, :QBLK, :], jnp.where(head0_k, vw, one), preferred_element_type=F32)
            pv1 = jnp.dot(e_ref[jb, QBLK:, :], jnp.where(head0_k, one, vw), preferred_element_type=F32)
            num = jnp.where(head0, pv0, pv1)
            den = pltpu.roll(jnp.where(head0, pv1, pv0), HEAD_DIM, 1)
            o = num * (1.0 / den)
            lse = m_ref[jb] + jnp.log(den)
            dst = pl.ds(seg + d * (u * QBLK), QBLK, stride=d) if d > 1 else pl.ds(q0, QBLK)
            if slot < n_slots - 1:
                o_ref[slot, dst, :] = o
                lse_ref[slot, dst, :] = lse
            else:
                lses = [lse_ref[i, dst, :] for i in range(slot)] + [lse]
                outs = [o_ref[i, dst, :] for i in range(slot)] + [o]
                mx = functools.reduce(jnp.maximum, lses)
                ws = [jnp.exp(l - mx) for l in lses]
                inv = 1.0 / functools.reduce(jnp.add, ws)
                acc = functools.reduce(jnp.add, [(w * inv) * t for w, t in zip(ws, outs)])
                out_ref[0, dst, :] = acc.astype(out_ref.dtype)

    for slot in range(n_slots):
        for i in range(3):
            relayout(slot, i)
    logits_stage(0, 0, e0_ref, m0_ref)
    for slot in range(n_slots):
        nxt = slot + 1 if slot + 1 < n_slots else None

        def group_pair(i, carry, slot=slot):
            g = 2 * i
            logits_stage(slot, g + 1, e1_ref, m1_ref)
            value_stage(slot, g, e0_ref, m0_ref)
            logits_stage(slot, g + 2, e0_ref, m0_ref)
            value_stage(slot, g + 1, e1_ref, m1_ref)
            return carry

        lax.fori_loop(0, n_loop, group_pair, 0)
        logits_stage(slot, n_groups - 1, e1_ref, m1_ref)
        value_stage(slot, n_groups - 2, e0_ref, m0_ref)
        if nxt is not None:
            logits_stage(nxt, 0, e0_ref, m0_ref)
        value_stage(slot, n_groups - 1, e1_ref, m1_ref)


def _dilated(att, rel_bias, n_pairs, b):
    s = att.shape[1] // b
    n_pat = len(DILATED_PATTERNS)
    bucket_rows = jnp.asarray(_bucket_rows())
    return pl.pallas_call(
        functools.partial(_dilated_kernel, seq=s),
        grid=(b, n_pairs),
        in_specs=[
            pl.BlockSpec(memory_space=pltpu.SMEM),
            pl.BlockSpec((n_pat, BASE_W), lambda bi, pi: (0, 0)),
            pl.BlockSpec((3, 1, s, LANES), lambda bi, pi: (0, pi, bi, 0)),
        ],
        out_specs=pl.BlockSpec((1, s, LANES), lambda bi, pi: (pi, bi, 0)),
        out_shape=jax.ShapeDtypeStruct((n_pairs, b * s, LANES), BF16),
        scratch_shapes=[
            pltpu.VMEM((n_pat * 3, 2 * QBLK, KWIN), F32),
            pltpu.VMEM((n_pat, 3, s, LANES), BF16),
            pltpu.VMEM((3, s, LANES), F32),
            pltpu.VMEM((n_pat - 1, s, LANES), F32),
            pltpu.VMEM((n_pat - 1, s, LANES), F32),
            pltpu.VMEM((ATT_GROUP, 2 * QBLK, KWIN), BF16),
            pltpu.VMEM((ATT_GROUP, 2 * QBLK, KWIN), BF16),
            pltpu.VMEM((ATT_GROUP, QBLK, LANES), F32),
            pltpu.VMEM((ATT_GROUP, QBLK, LANES), F32),
        ],
        compiler_params=_params("parallel", "parallel"),
        name="dilated",
    )(rel_bias, bucket_rows, att.reshape(3, n_pairs, b * s, LANES))


def _outproj_kernel(r_ref, a_ref, x_ref, w_ref, g_ref, wr_ref, x1_ref, h2_ref, aff_ref):
    n_e = aff_ref.shape[1]
    wr = wr_ref[...]
    w_hi = wr.astype(BF16)
    w_lo = (wr - w_hi.astype(F32)).astype(BF16)
    w_parts = jnp.where(lax.broadcasted_iota(I32, wr.shape, 1) < n_e, w_hi, w_lo)
    tm = x_ref.shape[0]
    mixed = jnp.concatenate([r_ref[i] for i in range(r_ref.shape[0])]
                            + [a_ref[i] for i in range(a_ref.shape[0])], axis=1)
    x1 = x_ref[...] + jnp.dot(mixed, w_ref[...], preferred_element_type=F32)
    x1_ref[...] = x1
    ms = jnp.mean(x1 * x1, axis=-1, keepdims=True)
    h2 = x1 * lax.rsqrt(ms + EPS) * g_ref[...]
    h_hi = h2.astype(BF16)
    h2_ref[...] = h_hi
    h_lo = (h2 - h_hi.astype(F32)).astype(BF16)
    prod = jnp.dot(jnp.concatenate([h_hi, h_lo], axis=0), w_parts, preferred_element_type=F32)
    logits = prod[:tm] + (pltpu.roll(prod[:tm], LANES - n_e, 1) + prod[tm:])
    expert_lane = lax.broadcasted_iota(I32, logits.shape, 1) < n_e
    logits = jnp.where(expert_lane, logits, NEG)
    m = jnp.max(logits, axis=1, keepdims=True)
    e = jnp.exp(logits - m)
    aff = e / jnp.sum(e, axis=1, keepdims=True)
    aff_ref[0] = aff.T[:n_e, :]


def _outproj(r, a, x, w_bf16, gain, w_router, tm=1024):
    b, s, d = x.shape
    n_r, n_a = r.shape[0], a.shape[0]
    n_e = w_router.shape[1]
    assert 2 * n_e <= LANES
    w_router = jnp.pad(jnp.concatenate([w_router, w_router], axis=1), ((0, 0), (0, LANES - 2 * n_e)))
    spt = s // tm
    tok = lambda width: pl.BlockSpec((tm, width), lambda i: (i, 0))
    slab = lambda blocks: pl.BlockSpec((blocks, tm, LANES), lambda i: (0, i, 0))
    const = lambda shape: pl.BlockSpec(shape, lambda i: (0, 0))
    return pl.pallas_call(
        _outproj_kernel,
        grid=(b * spt,),
        in_specs=[slab(n_r), slab(n_a), tok(d), const(((n_r + n_a) * LANES, d)), const((1, d)), const((d, LANES))],
        out_specs=[tok(d), tok(d), pl.BlockSpec((1, n_e, tm), lambda i: (i // spt, 0, i % spt))],
        out_shape=[
            jax.ShapeDtypeStruct((b * s, d), F32),
            jax.ShapeDtypeStruct((b * s, d), BF16),
            jax.ShapeDtypeStruct((b, n_e, s), F32),
        ],
        compiler_params=_params("parallel"),
        name="outproj",
    )(r, a, x.reshape(b * s, d), w_bf16, gain, w_router)


def _select_kernel(aff_ref, posm_ref, cnt_ref, *, cap):
    rows, s = aff_ref.shape
    bits = pltpu.bitcast(aff_ref[...], I32)

    def search(i, thr):
        cand = thr | jnp.left_shift(jnp.int32(1), 30 - i)
        cnt = jnp.sum((bits >= cand).astype(I32), axis=1, keepdims=True)
        return jnp.where(cnt >= cap, cand, thr)

    thr = lax.fori_loop(0, 31, search, jnp.zeros((rows, 1), I32))
    gt = bits > thr
    eq = bits == thr
    need = cap - jnp.sum(gt.astype(I32), axis=1, keepdims=True)

    w = LANES
    tri = (lax.broadcasted_iota(I32, (w, w), 0) <= lax.broadcasted_iota(I32, (w, w), 1)).astype(BF16)

    def excl_prefix(flags):
        carry = jnp.zeros((rows, 1), F32)
        out = []
        for c0 in range(0, s, w):
            f = flags[:, c0:c0 + w].astype(BF16)
            inc = jnp.dot(f, tri, preferred_element_type=F32)
            out.append(inc - f.astype(F32) + carry)
            carry = carry + inc[:, w - 1:w]
        return jnp.concatenate(out, axis=1).astype(I32)

    sel = gt | (eq & (excl_prefix(eq) < need))
    posm_ref[...] = jnp.where(sel, excl_prefix(sel), -1)

    tok = lax.broadcasted_iota(I32, (w, w), 0)
    edge = lax.broadcasted_iota(I32, (w, w), 1) * MOE_TILE
    cnt = jnp.zeros((rows, w), F32)
    for c0 in range(0, s, w):
        before = ((tok + c0) < edge).astype(BF16)
        cnt = cnt + jnp.dot(sel[:, c0:c0 + w].astype(BF16), before, preferred_element_type=F32)
    cnt_ref[...] = cnt.astype(I32)


def _select(aff_rows, cap):
    rows, s = aff_rows.shape
    return pl.pallas_call(
        functools.partial(_select_kernel, cap=cap),
        out_shape=[jax.ShapeDtypeStruct((rows, s), I32), jax.ShapeDtypeStruct((rows, LANES), I32)],
        compiler_params=pltpu.CompilerParams(vmem_limit_bytes=VMEM_LIMIT),
        name="select",
    )(aff_rows)


def _slot_window(cnt_ref, row, j, cap):
    lo, hi = cnt_ref[row * CNT_STRIDE + j], cnt_ref[row * CNT_STRIDE + j + 1]
    w0 = pl.multiple_of(jnp.minimum((lo // SLOT_ALIGN) * SLOT_ALIGN, cap - SLOT_WIN), SLOT_ALIGN)
    return hi, w0


def _any_overflow(windows):
    return functools.reduce(jnp.logical_or, [hi > w0 + SLOT_WIN for hi, w0 in windows])


def _gather_kernel(cnt_ref, posm_ref, h2_ref, xg_ref):
    bi, j = pl.program_id(0), pl.program_id(1)
    n_e, cap = xg_ref.shape[1:3]
    tt = MOE_TILE

    @pl.when(j == 0)
    def _():
        xg_ref[...] = jnp.zeros_like(xg_ref)

    slot0 = lax.broadcasted_iota(I32, (SLOT_WIN, tt), 0)
    for sub in range(h2_ref.shape[1] // tt):
        tile = j * (h2_ref.shape[1] // tt) + sub
        tok = slice(sub * tt, (sub + 1) * tt)
        windows = [_slot_window(cnt_ref, bi * n_e + e, tile, cap) for e in range(n_e)]
        for e in range(0, n_e, WIN_GROUP):
            onehot = jnp.concatenate(
                [(posm_ref[0, e + i:e + i + 1, tok] == slot0 + windows[e + i][1]).astype(BF16)
                 for i in range(WIN_GROUP)], axis=0)
            rows = jnp.dot(onehot, h2_ref[0, tok, :], preferred_element_type=F32).astype(BF16)
            for i in range(WIN_GROUP):
                xg_ref[0, e + i, pl.ds(windows[e + i][1], SLOT_WIN), :] += rows[i * SLOT_WIN:(i + 1) * SLOT_WIN]

        @pl.when(_any_overflow(windows))
        def _(tile=tile, tok=tok):
            wide = lax.broadcasted_iota(I32, (MXU_DEPTH, tt), 0)

            def one_expert(e, carry):
                hi, w0 = _slot_window(cnt_ref, bi * n_e + e, tile, cap)

                @pl.when(hi > w0 + SLOT_WIN)
                def _():
                    pos = posm_ref[0, pl.ds(e, 1), tok]
                    for ws in range(0, cap, MXU_DEPTH):
                        slot = wide + ws
                        onehot = ((pos == slot) & (slot >= w0 + SLOT_WIN)).astype(BF16)
                        extra = jnp.dot(onehot, h2_ref[0, tok, :], preferred_element_type=F32)
                        xg_ref[0, e, ws:ws + MXU_DEPTH, :] += extra.astype(BF16)

                return carry

            lax.fori_loop(0, n_e, one_expert, 0)


def _gather(cnt, posm, h2, cap):
    b, n_e, s = posm.shape
    d = h2.shape[2]
    step = MOE_TILE * GATHER_TILES
    return pl.pallas_call(
        _gather_kernel,
        grid_spec=pltpu.PrefetchScalarGridSpec(
            num_scalar_prefetch=1,
            grid=(b, s // step),
            in_specs=[
                pl.BlockSpec((1, n_e, step), lambda bi, j, cnt: (bi, 0, j)),
                pl.BlockSpec((1, step, d), lambda bi, j, cnt: (bi, j, 0)),
            ],
            out_specs=pl.BlockSpec((1, n_e, cap, d), lambda bi, j, cnt: (bi, 0, 0, 0)),
        ),
        out_shape=jax.ShapeDtypeStruct((b, n_e, cap, d), BF16),
        compiler_params=_params("parallel", "arbitrary"),
        name="gather",
    )(cnt, posm, h2)


def _ffn_kernel(xg_ref, wg_hbm, wu_hbm, wd_hbm, y_ref, wg_buf, wu_buf, wd_buf, wg_stage, wu_stage, wd_stage,
                sem, acc_ref, *, tf):
    e, bi = pl.program_id(0), pl.program_id(1)
    n_e, n_b = pl.num_programs(0), pl.num_programs(1)
    rows_in, rows_dn = wg_stage.shape[1], wd_stage.shape[1]
    f_total = wg_buf.shape[2]

    def chunk_copies(expert, c):
        st = c % 2
        return (
            pltpu.make_async_copy(wg_hbm.at[expert, pl.ds(c * rows_in, rows_in), :], wg_stage.at[st], sem.at[0, st]),
            pltpu.make_async_copy(wu_hbm.at[expert, pl.ds(c * rows_in, rows_in), :], wu_stage.at[st], sem.at[1, st]),
            pltpu.make_async_copy(wd_hbm.at[expert, pl.ds(c * rows_dn, rows_dn), :], wd_stage.at[st], sem.at[2, st]),
        )

    def finish_chunk(expert, c):
        for cp in chunk_copies(expert, c):
            cp.wait()
        slot, st = expert % 2, c % 2
        wg_buf[slot, pl.ds(c * rows_in, rows_in), :] = wg_stage[st].astype(BF16)
        wu_buf[slot, pl.ds(c * rows_in, rows_in), :] = wu_stage[st].astype(BF16)
        wd_buf[slot, pl.ds(c * rows_dn, rows_dn), :] = wd_stage[st].astype(BF16)

    @pl.when((e == 0) & (bi == 0))
    def _():
        for cp in chunk_copies(0, 0):
            cp.start()

        def load_first(c, carry):
            @pl.when(c + 1 < n_b)
            def _():
                for cp in chunk_copies(0, c + 1):
                    cp.start()

            finish_chunk(0, c)
            return carry

        lax.fori_loop(0, n_b, load_first, 0)

    pending = jnp.where(bi > 0, e + 1 < n_e, e > 0)

    @pl.when(pending)
    def _():
        finish_chunk(jnp.where(bi > 0, e + 1, e), jnp.where(bi > 0, bi - 1, n_b - 1))

    @pl.when(e + 1 < n_e)
    def _():
        for cp in chunk_copies(e + 1, bi):
            cp.start()

    def swiglu(slot):
        xg = xg_ref[0, 0]
        for fi in range(f_total // tf):
            fs = slice(fi * tf, (fi + 1) * tf)
            gate = jnp.dot(xg, wg_buf[slot, :, fs], preferred_element_type=F32)
            up = jnp.dot(xg, wu_buf[slot, :, fs], preferred_element_type=F32)
            hid = ((gate * jax.nn.sigmoid(gate)) * up).astype(BF16)
            part = jnp.dot(hid, wd_buf[slot, fs, :], preferred_element_type=F32)
            if fi == 0:
                acc_ref[...] = part
            else:
                acc_ref[...] += part
        y_ref[0, 0] = acc_ref[...].astype(y_ref.dtype)

    for slot in range(2):
        pl.when(e % 2 == slot)(functools.partial(swiglu, slot))


def _ffn(xg, wg, wu, wd, tf=512):
    b, n_e, cap, d = xg.shape
    f = wg.shape[2]
    assert d % b == 0 and f % b == 0 and (d // b) % SLOT_ALIGN == 0, "one weight row chunk per batch step"
    tok = pl.BlockSpec((1, 1, cap, d), lambda e, bi: (bi, e, 0, 0))
    hbm = pl.BlockSpec(memory_space=pl.ANY)
    return pl.pallas_call(
        functools.partial(_ffn_kernel, tf=tf),
        grid=(n_e, b),
        in_specs=[tok, hbm, hbm, hbm],
        out_specs=tok,
        out_shape=jax.ShapeDtypeStruct((b, n_e, cap, d), BF16),
        scratch_shapes=[
            pltpu.VMEM((2, d, f), BF16), pltpu.VMEM((2, d, f), BF16), pltpu.VMEM((2, f, d), BF16),
            pltpu.VMEM((2, d // b, f), F32), pltpu.VMEM((2, d // b, f), F32), pltpu.VMEM((2, f // b, d), F32),
            pltpu.SemaphoreType.DMA((3, 2)),
            pltpu.VMEM((cap, d), F32),
        ],
        compiler_params=_params("arbitrary", "arbitrary"),
        name="ffn",
    )(xg, wg, wu, wd)


def _combine_kernel(cnt_ref, pos_ref, gate_ref, y_ref, x1_ref, g_ref, out_ref, acc_ref):
    bi, j = pl.program_id(0), pl.program_id(1)
    tt = MOE_TILE
    n_e, cap = y_ref.shape[1:3]
    slot0 = lax.broadcasted_iota(I32, (SLOT_WIN, tt), 0)
    for sub in range(x1_ref.shape[1] // tt):
        tile = j * (x1_ref.shape[1] // tt) + sub
        tok = slice(sub * tt, (sub + 1) * tt)
        windows = [_slot_window(cnt_ref, bi * n_e + e, tile, cap) for e in range(n_e)]
        acc = x1_ref[0, tok, :]
        for e in range(0, n_e, WIN_GROUP):
            scatter_t = jnp.concatenate(
                [jnp.where(pos_ref[0, e + i:e + i + 1, tok] == slot0 + windows[e + i][1],
                           gate_ref[0, e + i:e + i + 1, tok], 0.0).astype(BF16) for i in range(WIN_GROUP)], axis=0)
            y_group = jnp.concatenate(
                [y_ref[0, e + i, pl.ds(windows[e + i][1], SLOT_WIN), :] for i in range(WIN_GROUP)], axis=0)
            acc = acc + lax.dot_general(scatter_t, y_group, _TN, preferred_element_type=F32)
        acc_ref[...] = acc

        @pl.when(_any_overflow(windows))
        def _(tile=tile, tok=tok):
            wide = lax.broadcasted_iota(I32, (MXU_DEPTH, tt), 0)

            def one_expert(e, carry):
                hi, w0 = _slot_window(cnt_ref, bi * n_e + e, tile, cap)

                @pl.when(hi > w0 + SLOT_WIN)
                def _():
                    pos = pos_ref[0, pl.ds(e, 1), tok]
                    gate = gate_ref[0, pl.ds(e, 1), tok]
                    for ws in range(0, cap, MXU_DEPTH):
                        slot = wide + ws
                        extra = jnp.where((pos == slot) & (slot >= w0 + SLOT_WIN), gate, 0.0).astype(BF16)
                        acc_ref[...] += lax.dot_general(extra, y_ref[0, e, ws:ws + MXU_DEPTH, :], _TN,
                                                        preferred_element_type=F32)

                return carry

            lax.fori_loop(0, n_e, one_expert, 0)

        acc = acc_ref[...]
        ms = jnp.mean(acc * acc, axis=-1, keepdims=True)
        out_ref[0, tok, :] = acc * lax.rsqrt(ms + EPS) * g_ref[...]


def _combine(cnt, posm, aff, y, x1, gain):
    b, s, d = x1.shape
    n_e, cap = y.shape[1:3]
    step = MOE_TILE * COMBINE_TILES
    return pl.pallas_call(
        _combine_kernel,
        grid_spec=pltpu.PrefetchScalarGridSpec(
            num_scalar_prefetch=1,
            grid=(b, s // step),
            in_specs=[
                pl.BlockSpec((1, n_e, step), lambda bi, j, cnt: (bi, 0, j)),
                pl.BlockSpec((1, n_e, step), lambda bi, j, cnt: (bi, 0, j)),
                pl.BlockSpec((1, n_e, cap, d), lambda bi, j, cnt: (bi, 0, 0, 0)),
                pl.BlockSpec((1, step, d), lambda bi, j, cnt: (bi, j, 0)),
                pl.BlockSpec((1, d), lambda bi, j, cnt: (0, 0)),
            ],
            out_specs=pl.BlockSpec((1, step, d), lambda bi, j, cnt: (bi, j, 0)),
            scratch_shapes=[pltpu.VMEM((MOE_TILE, d), F32)],
        ),
        out_shape=jax.ShapeDtypeStruct((b, s, d), F32),
        compiler_params=_params("parallel", "parallel"),
        name="combine",
    )(cnt, posm, aff, y, x1, gain)


def _rope_tables(seq):
    half = HEAD_DIM // 2
    pos = jnp.arange(seq, dtype=F32)
    inv = ROPE_BASE ** (-jnp.arange(0, HEAD_DIM, 2, dtype=F32) / HEAD_DIM)
    ang = pos[:, None] * inv[None, :]
    cos, sin = jnp.cos(ang), jnp.sin(ang)
    reps = LANES // HEAD_DIM
    cos_t = jnp.tile(jnp.concatenate([cos, cos], axis=1), (1, reps))
    sin_t = jnp.tile(jnp.concatenate([-sin, sin], axis=1), (1, reps))
    return cos_t, sin_t


def kernel(x, norm1_gain, w_in, ret_log_decay, ret_gn_gain, rel_bias, w_out, norm2_gain, w_router,
           w_gate, w_up, w_down, final_gain):
    b, s, d = x.shape
    depth = w_in.shape[0]
    ret_width = ret_gn_gain.shape[1]
    n_ret_pairs = ret_width // LANES
    att_width = (w_in.shape[2] - 4 * ret_width) // 3
    n_att_pairs = att_width // LANES
    cap = CAPACITY_FACTOR * s // N_EXPERTS
    cos_t, sin_t = _rope_tables(s)

    assert depth == 1, "single-layer block: the final norm is fused into the combine kernel"
    layer = 0
    ret, att = _proj(x.reshape(b * s, d), norm1_gain[layer][None, :], w_in[layer].astype(BF16),
                     cos_t, sin_t, n_ret=4 * ret_width, ret_width=ret_width)
    r = _retention(ret, ret_log_decay[layer], ret_gn_gain[layer][None, :], n_ret_pairs, b)
    a = _dilated(att, rel_bias, n_att_pairs, b)
    x1, h2, aff = _outproj(r, a, x, w_out[layer].astype(BF16), norm2_gain[layer][None, :],
                           w_router[layer])
    assert s % (MOE_TILE * max(GATHER_TILES, COMBINE_TILES)) == 0 and s // MOE_TILE < CNT_STRIDE and cap % MXU_DEPTH == 0 and N_EXPERTS % WIN_GROUP == 0
    posm, cnt = _select(aff.reshape(b * N_EXPERTS, s), cap)
    posm = posm.reshape(b, N_EXPERTS, s)
    cnt = cnt[:, :CNT_STRIDE].reshape(-1)
    xg = _gather(cnt, posm, h2.reshape(b, s, d), cap)
    y = _ffn(xg, w_gate[layer], w_up[layer], w_down[layer])
    return _combine(cnt, posm, aff, y, x1.reshape(b, s, d),
                    final_gain[None, :])
```

```python
import functools
import math

import numpy as np
import jax
import jax.numpy as jnp
from jax import lax
from jax.experimental import pallas as pl
from jax.experimental.pallas import tpu as pltpu

F32 = jnp.float32
BF16 = jnp.bfloat16
I32 = jnp.int32

HEAD_DIM = 64
LANES = 128
RET_CHUNK = 128
RET_GROUP = 4
ROPE_BASE = 10000.0
DILATED_PATTERNS = ((128, 1), (512, 4), (2048, 16))
SIDE = 64
N_BUCKETS = 32
MAX_DISTANCE = 1024
N_EXPERTS = 16
CAPACITY_FACTOR = 2
MOE_TILE = 256
GATHER_TILES = 8
COMBINE_TILES = 4
MXU_DEPTH = 256
WIN_GROUP = 4
SLOT_WIN = MXU_DEPTH // WIN_GROUP
SLOT_ALIGN = 16
CNT_STRIDE = 32
FFN_BATCH = 2
STEP_CHUNKS = 2
EPS = 1e-6
NEG = -1e30
VMEM_LIMIT = 56 * 1024 * 1024

_NT = (((1,), (1,)), ((), ()))
_TN = (((0,), (0,)), ((), ()))


def _params(*sem):
    return pltpu.CompilerParams(dimension_semantics=sem, vmem_limit_bytes=VMEM_LIMIT)


def _proj_kernel(x_ref, g_ref, w_ref, cos_ref, sin_ref, ret_ref, att_ref, *, n_ret, ret_width, chunk):
    x = x_ref[...]
    ms = jnp.mean(x * x, axis=-1, keepdims=True)
    h = (x * lax.rsqrt(ms + EPS) * g_ref[...]).astype(BF16)
    n_cols = w_ref.shape[1]
    cos, sin = cos_ref[...], sin_ref[...]
    first_half = (lax.broadcasted_iota(I32, cos.shape, 1) % HEAD_DIM) < (HEAD_DIM // 2)
    for c0 in range(0, n_cols, chunk):
        o = jnp.dot(h, w_ref[:, c0:c0 + chunk], preferred_element_type=F32)
        for l0 in range(0, chunk, LANES):
            t = o[:, l0:l0 + LANES]
            col = c0 + l0
            if col >= n_ret:
                att_ref[(col - n_ret) // LANES] = t
            elif col >= 2 * ret_width:
                ret_ref[col // LANES] = t.astype(BF16)
            else:
                scale = 1.0 if col < ret_width else HEAD_DIM ** -0.5
                swapped = jnp.where(first_half, pltpu.roll(t, LANES - HEAD_DIM // 2, 1),
                                    pltpu.roll(t, HEAD_DIM // 2, 1))
                ret_ref[col // LANES] = ((t * cos + swapped * sin) * scale).astype(BF16)


def _proj(x2d, gain, w_bf16, cos_t, sin_t, n_ret, ret_width, tm=1024):
    t, d = x2d.shape
    n_cols = w_bf16.shape[1]
    n_att = n_cols - n_ret
    seq_tiles = cos_t.shape[0] // tm
    assert ret_width % 512 == 0 and n_ret % 512 == 0 and n_cols % 512 == 0
    return pl.pallas_call(
        functools.partial(_proj_kernel, n_ret=n_ret, ret_width=ret_width, chunk=512),
        grid=(t // tm,),
        in_specs=[
            pl.BlockSpec((tm, d), lambda i: (i, 0)),
            pl.BlockSpec((1, d), lambda i: (0, 0)),
            pl.BlockSpec((d, n_cols), lambda i: (0, 0)),
            pl.BlockSpec((tm, LANES), lambda i: (i % seq_tiles, 0)),
            pl.BlockSpec((tm, LANES), lambda i: (i % seq_tiles, 0)),
        ],
        out_specs=[
            pl.BlockSpec((n_ret // LANES, tm, LANES), lambda i: (0, i, 0)),
            pl.BlockSpec((n_att // LANES, tm, LANES), lambda i: (0, i, 0)),
        ],
        out_shape=[
            jax.ShapeDtypeStruct((n_ret // LANES, t, LANES), BF16),
            jax.ShapeDtypeStruct((n_att // LANES, t, LANES), F32),
        ],
        compiler_params=_params("parallel"),
        name="proj",
    )(x2d, gain, w_bf16, cos_t, sin_t)


def _retention_kernel(decay_ref, q_ref, k_ref, v_ref, g_ref, gn_ref, out_ref,
                      rf_ref, rb_ref, kvf_ref, kvb_ref, p0_ref, p1_ref, *, seq):
    c = RET_CHUNK
    n_chunks = seq // c
    p = pl.program_id(1)
    lane = lax.broadcasted_iota(I32, (c, LANES), 1)
    row = lax.broadcasted_iota(I32, (c, LANES), 0)
    head0 = lane < HEAD_DIM
    rowf = row.astype(F32)

    lgf0, lgf1 = decay_ref[0, 2 * p], decay_ref[0, 2 * p + 1]
    lgb0, lgb1 = decay_ref[1, 2 * p], decay_ref[1, 2 * p + 1]
    lgf_lane = jnp.where(head0, lgf0, lgf1)
    lgb_lane = jnp.where(head0, lgb0, lgb1)
    lgf_row = jnp.where(row < HEAD_DIM, lgf0, lgf1)
    lgb_row = jnp.where(row < HEAD_DIM, lgb0, lgb1)
    same_head = (row < HEAD_DIM) == head0

    zeta_f = jnp.exp((c - 1 - rowf) * lgf_lane)
    zeta_b = jnp.exp(rowf * lgb_lane)
    xi_f = jnp.exp((rowf + 1.0) * lgf_lane)
    xi_b = jnp.exp((c - rowf) * lgb_lane)
    gch_f = jnp.where(same_head, jnp.exp(c * lgf_row), 0.0)
    gch_b = jnp.where(same_head, jnp.exp(c * lgb_row), 0.0)

    diff = (row - lane).astype(F32)

    def dmat(lf, lb):
        return jnp.exp(jnp.where(diff >= 0, diff * lf, -diff * lb))

    d_stack = jnp.concatenate([dmat(lgf0, lgb0), dmat(lgf1, lgb1)], axis=0)

    def kv_step(n, carry):
        sl = pl.ds(pl.multiple_of(n * c, c), c)
        k = k_ref[0, sl, :].astype(F32)
        v = v_ref[0, sl, :]
        kvf = lax.dot_general((k * zeta_f).astype(BF16), v, _TN, preferred_element_type=F32)
        kvb = lax.dot_general((k * zeta_b).astype(BF16), v, _TN, preferred_element_type=F32)
        kvf_ref[n] = jnp.where(same_head, kvf, 0.0)
        kvb_ref[n] = jnp.where(same_head, kvb, 0.0)
        return carry

    lax.fori_loop(0, n_chunks, kv_step, 0, unroll=True)

    def scan_step(i, states):
        sf, sb = states
        nb = n_chunks - 1 - i
        rf_ref[i] = sf.astype(BF16)
        rb_ref[nb] = sb.astype(BF16)
        return gch_f * sf + kvf_ref[i], gch_b * sb + kvb_ref[nb]

    zero_state = jnp.zeros((LANES, LANES), F32)
    lax.fori_loop(0, n_chunks, scan_step, (zero_state, zero_state), unroll=True)

    gn_gain = gn_ref[...]

    def score_stage(g, p_ref):
        for jc in range(RET_GROUP):
            sl = pl.ds(pl.multiple_of((g * RET_GROUP + jc) * c, c), c)
            q = q_ref[0, sl, :]
            zero = jnp.zeros_like(q)
            q_stack = jnp.concatenate([jnp.where(head0, q, zero), jnp.where(head0, zero, q)], axis=0)
            scores = lax.dot_general(q_stack, k_ref[0, sl, :], _NT, preferred_element_type=F32) * d_stack
            sb = scores.astype(BF16)
            p_ref[jc] = jnp.concatenate([sb[:c], sb[c:]], axis=1)

    def half_mean(t):
        s0 = jnp.sum(jnp.where(head0, t, 0.0), axis=-1, keepdims=True)
        s1 = jnp.sum(jnp.where(head0, 0.0, t), axis=-1, keepdims=True)
        return jnp.where(head0, s0, s1) * (1.0 / HEAD_DIM)

    def value_stage(g, p_ref):
        for jc in range(RET_GROUP):
            n = g * RET_GROUP + jc
            sl = pl.ds(pl.multiple_of(n * c, c), c)
            q = q_ref[0, sl, :]
            v = v_ref[0, sl, :]
            zero = jnp.zeros_like(v)
            v_stack = jnp.concatenate([jnp.where(head0, v, zero), jnp.where(head0, zero, v)], axis=0)
            inner = jnp.dot(p_ref[jc], v_stack, preferred_element_type=F32)
            cross_f = jnp.dot(q, rf_ref[n], preferred_element_type=F32) * xi_f
            cross_b = jnp.dot(q, rb_ref[n], preferred_element_type=F32) * xi_b
            r = inner + cross_f + cross_b
            dlt = r - half_mean(r)
            var = half_mean(dlt * dlt)
            gate = g_ref[0, sl, :].astype(F32)
            y = dlt * lax.rsqrt(var + EPS) * gn_gain * (gate * jax.nn.sigmoid(gate))
            out_ref[0, sl, :] = y.astype(out_ref.dtype)

    n_groups = n_chunks // RET_GROUP
    assert n_groups % 2 == 0 and n_groups >= 2
    score_stage(0, p0_ref)

    def group_pair(i, carry):
        g = 2 * i
        score_stage(g + 1, p1_ref)
        value_stage(g, p0_ref)
        score_stage(g + 2, p0_ref)
        value_stage(g + 1, p1_ref)
        return carry

    lax.fori_loop(0, n_groups // 2 - 1, group_pair, 0, unroll=True)
    score_stage(n_groups - 1, p1_ref)
    value_stage(n_groups - 2, p0_ref)
    value_stage(n_groups - 1, p1_ref)


def _retention(ret, decay, gn_gain, n_pairs, b):
    s = ret.shape[1] // b
    blk = lambda off: pl.BlockSpec((1, s, LANES), lambda bi, pi: (off + pi, bi, 0))
    return pl.pallas_call(
        functools.partial(_retention_kernel, seq=s),
        grid=(b, n_pairs),
        in_specs=[
            pl.BlockSpec(memory_space=pltpu.SMEM),
            blk(0), blk(n_pairs), blk(2 * n_pairs), blk(3 * n_pairs),
            pl.BlockSpec((1, LANES), lambda bi, pi: (0, pi)),
        ],
        out_specs=pl.BlockSpec((1, s, LANES), lambda bi, pi: (pi, bi, 0)),
        out_shape=jax.ShapeDtypeStruct((n_pairs, b * s, LANES), BF16),
        scratch_shapes=[
            pltpu.VMEM((s // RET_CHUNK, LANES, LANES), BF16),
            pltpu.VMEM((s // RET_CHUNK, LANES, LANES), BF16),
            pltpu.VMEM((s // RET_CHUNK, LANES, LANES), F32),
            pltpu.VMEM((s // RET_CHUNK, LANES, LANES), F32),
            pltpu.VMEM((RET_GROUP, RET_CHUNK, 2 * RET_CHUNK), BF16),
            pltpu.VMEM((RET_GROUP, RET_CHUNK, 2 * RET_CHUNK), BF16),
        ],
        compiler_params=_params("parallel", "parallel"),
        name="retention",
    )(decay, ret, ret, ret, ret, gn_gain)


QBLK = 2 * SIDE
KWIN = 4 * SIDE
BASE_W = 512
ATT_GROUP = 4


def _t5_bucket_np(rel):
    half = N_BUCKETS // 2
    max_exact = half // 2
    bucket = np.where(rel > 0, half, 0)
    n = np.abs(rel)
    nf = np.maximum(n, 1).astype(np.float32)
    large = max_exact + (np.log(nf / np.float32(max_exact)) / np.float32(math.log(MAX_DISTANCE / max_exact))
                         * np.float32(half - max_exact)).astype(np.int32)
    large = np.minimum(large, half - 1)
    return (bucket + np.where(n < max_exact, n, large)).astype(np.int32)


def _bucket_rows():
    k = np.arange(BASE_W)
    off = k - SIDE
    rows = []
    for _, dilation in DILATED_PATTERNS:
        rows.append(np.where(k <= 2 * SIDE, _t5_bucket_np(off * dilation), -1))
    return np.stack(rows).astype(np.int32)


def _dilated_kernel(bias_tab_ref, bucket_ref, qkv_ref, out_ref,
                    bias_ref, perm_ref, stage_ref, o_ref, lse_ref, e0_ref, e1_ref, m0_ref, m1_ref, *, seq):
    order = list(range(1, len(DILATED_PATTERNS))) + [0]
    dil = [DILATED_PATTERNS[pi][1] for pi in order]
    n_slots = len(order)
    assert n_slots == 3 and dil[0] > 1 and dil[1] % dil[0] == 0 and dil[2] == 1
    p = pl.program_id(1)
    lane = lax.broadcasted_iota(I32, (QBLK, LANES), 1)
    head0 = lane < HEAD_DIM
    head0_k = lax.broadcasted_iota(I32, (KWIN, LANES), 1) < HEAD_DIM
    n_groups = seq // QBLK // ATT_GROUP
    n_loop = n_groups // 2 - 1
    assert n_groups % 2 == 0 and n_loop >= 0

    for pi in range(len(DILATED_PATTERNS)):
        bucket = jnp.broadcast_to(bucket_ref[pi:pi + 1, :], (QBLK, BASE_W))
        for hh in range(2):
            base = jnp.full((QBLK, BASE_W), NEG, F32)
            for bk in range(N_BUCKETS):
                base = jnp.where(bucket == bk, bias_tab_ref[bk, 2 * p + hh], base)
            for var, shift in enumerate((BASE_W - SIDE, 0, SIDE)):
                tile = pltpu.roll(base, shift, 1, stride=1, stride_axis=0)
                bias_ref[pi * 3 + var, hh * QBLK:(hh + 1) * QBLK, :] = tile[:, :KWIN]

    def relayout(slot, i):
        d = dil[slot]
        seg_len = seq // d
        scale = jnp.where(i == 0, HEAD_DIM ** -0.5, 1.0).astype(F32)
        if d == 1:
            perm_ref[slot, i] = (qkv_ref[i, 0] * scale).astype(BF16)
        elif slot == 0:
            for r in range(d):
                val = qkv_ref[i, 0, pl.ds(r, seg_len, stride=d), :]
                stage_ref[i, r * seg_len:(r + 1) * seg_len, :] = val
                perm_ref[slot, i, r * seg_len:(r + 1) * seg_len, :] = (val * scale).astype(BF16)
        else:
            prev_d = dil[slot - 1]
            step, prev_len = d // prev_d, seq // prev_d
            for r_prev in range(prev_d):
                for r_step in range(step):
                    val = stage_ref[i, pl.ds(r_prev * prev_len + r_step, seg_len, stride=step), :]
                    r = r_prev + prev_d * r_step
                    perm_ref[slot, i, r * seg_len:(r + 1) * seg_len, :] = (val * scale).astype(BF16)

    def placement(slot, t):
        seg_len = seq // dil[slot]
        blocks_per_seg = seg_len // QBLK
        seg = t // blocks_per_seg
        u = t % blocks_per_seg
        q0 = pl.multiple_of(t * QBLK, QBLK)
        seg0 = seg * seg_len
        k0 = pl.multiple_of(jnp.clip(q0 - SIDE, seg0, seg0 + seg_len - KWIN), SIDE)
        var = jnp.where(u == 0, 0, jnp.where(u == blocks_per_seg - 1, 2, 1))
        return seg, u, q0, k0, var

    def logits_stage(slot, g, e_ref, m_ref):
        for jb in range(ATT_GROUP):
            _, _, q0, k0, var = placement(slot, g * ATT_GROUP + jb)
            q = perm_ref[slot, 0, pl.ds(q0, QBLK), :]
            kw = perm_ref[slot, 1, pl.ds(k0, KWIN), :]
            zero = jnp.zeros_like(q)
            q_stack = jnp.concatenate([jnp.where(head0, q, zero), jnp.where(head0, zero, q)], axis=0)
            s = lax.dot_general(q_stack, kw, _NT, preferred_element_type=F32) + bias_ref[order[slot] * 3 + var]
            m = jnp.max(s, axis=-1, keepdims=True)
            e_ref[jb] = jnp.exp(s - m).astype(BF16)
            m_ref[jb] = jnp.where(head0, m[:QBLK], m[QBLK:])

    def value_stage(slot, g, e_ref, m_ref):
        d = dil[slot]
        for jb in range(ATT_GROUP):
            seg, u, q0, k0, _ = placement(slot, g * ATT_GROUP + jb)
            vw = perm_ref[slot, 2, pl.ds(k0, KWIN), :]
            one = jnp.ones_like(vw)
            pv0 = jnp.dot(e_ref[jb, :QBLK, :], jnp.where(head0_k, vw, one), preferred_element_type=F32)
            pv1 = jnp.dot(e_ref[jb, QBLK:, :], jnp.where(head0_k, one, vw), preferred_element_type=F32)
            num = jnp.where(head0, pv0, pv1)
            den = pltpu.roll(jnp.where(head0, pv1, pv0), HEAD_DIM, 1)
            o = num * (1.0 / den)
            lse = m_ref[jb] + jnp.log(den)
            dst = pl.ds(seg + d * (u * QBLK), QBLK, stride=d) if d > 1 else pl.ds(q0, QBLK)
            if slot < n_slots - 1:
                o_ref[slot, dst, :] = o
                lse_ref[slot, dst, :] = lse
            else:
                lses = [lse_ref[i, dst, :] for i in range(slot)] + [lse]
                outs = [o_ref[i, dst, :] for i in range(slot)] + [o]
                mx = functools.reduce(jnp.maximum, lses)
                ws = [jnp.exp(l - mx) for l in lses]
                inv = 1.0 / functools.reduce(jnp.add, ws)
                acc = functools.reduce(jnp.add, [(w * inv) * t for w, t in zip(ws, outs)])
                out_ref[0, dst, :] = acc.astype(out_ref.dtype)

    for slot in range(n_slots):
        for i in range(3):
            relayout(slot, i)
    logits_stage(0, 0, e0_ref, m0_ref)
    for slot in range(n_slots):
        nxt = slot + 1 if slot + 1 < n_slots else None

        def group_pair(i, carry, slot=slot):
            g = 2 * i
            logits_stage(slot, g + 1, e1_ref, m1_ref)
            value_stage(slot, g, e0_ref, m0_ref)
            logits_stage(slot, g + 2, e0_ref, m0_ref)
            value_stage(slot, g + 1, e1_ref, m1_ref)
            return carry

        lax.fori_loop(0, n_loop, group_pair, 0)
        logits_stage(slot, n_groups - 1, e1_ref, m1_ref)
        value_stage(slot, n_groups - 2, e0_ref, m0_ref)
        if nxt is not None:
            logits_stage(nxt, 0, e0_ref, m0_ref)
        value_stage(slot, n_groups - 1, e1_ref, m1_ref)


def _dilated(att, rel_bias, n_pairs, b):
    s = att.shape[1] // b
    n_pat = len(DILATED_PATTERNS)
    bucket_rows = jnp.asarray(_bucket_rows())
    return pl.pallas_call(
        functools.partial(_dilated_kernel, seq=s),
        grid=(b, n_pairs),
        in_specs=[
            pl.BlockSpec(memory_space=pltpu.SMEM),
            pl.BlockSpec((n_pat, BASE_W), lambda bi, pi: (0, 0)),
            pl.BlockSpec((3, 1, s, LANES), lambda bi, pi: (0, pi, bi, 0)),
        ],
        out_specs=pl.BlockSpec((1, s, LANES), lambda bi, pi: (pi, bi, 0)),
        out_shape=jax.ShapeDtypeStruct((n_pairs, b * s, LANES), BF16),
        scratch_shapes=[
            pltpu.VMEM((n_pat * 3, 2 * QBLK, KWIN), F32),
            pltpu.VMEM((n_pat, 3, s, LANES), BF16),
            pltpu.VMEM((3, s, LANES), F32),
            pltpu.VMEM((n_pat - 1, s, LANES), F32),
            pltpu.VMEM((n_pat - 1, s, LANES), F32),
            pltpu.VMEM((ATT_GROUP, 2 * QBLK, KWIN), BF16),
            pltpu.VMEM((ATT_GROUP, 2 * QBLK, KWIN), BF16),
            pltpu.VMEM((ATT_GROUP, QBLK, LANES), F32),
            pltpu.VMEM((ATT_GROUP, QBLK, LANES), F32),
        ],
        compiler_params=_params("parallel", "parallel"),
        name="dilated",
    )(rel_bias, bucket_rows, att.reshape(3, n_pairs, b * s, LANES))


def _outproj_kernel(r_ref, a_ref, x_ref, w_ref, g_ref, wr_ref, x1_ref, h2_ref, aff_ref):
    n_e = aff_ref.shape[1]
    wr = wr_ref[...]
    w_hi = wr.astype(BF16)
    w_lo = (wr - w_hi.astype(F32)).astype(BF16)
    w_parts = jnp.where(lax.broadcasted_iota(I32, wr.shape, 1) < n_e, w_hi, w_lo)
    tm = x_ref.shape[0]
    mixed = jnp.concatenate([r_ref[i] for i in range(r_ref.shape[0])]
                            + [a_ref[i] for i in range(a_ref.shape[0])], axis=1)
    x1 = x_ref[...] + jnp.dot(mixed, w_ref[...], preferred_element_type=F32)
    x1_ref[...] = x1
    ms = jnp.mean(x1 * x1, axis=-1, keepdims=True)
    h2 = x1 * lax.rsqrt(ms + EPS) * g_ref[...]
    h_hi = h2.astype(BF16)
    h2_ref[...] = h_hi
    h_lo = (h2 - h_hi.astype(F32)).astype(BF16)
    prod = jnp.dot(jnp.concatenate([h_hi, h_lo], axis=0), w_parts, preferred_element_type=F32)
    logits = prod[:tm] + (pltpu.roll(prod[:tm], LANES - n_e, 1) + prod[tm:])
    expert_lane = lax.broadcasted_iota(I32, logits.shape, 1) < n_e
    logits = jnp.where(expert_lane, logits, NEG)
    m = jnp.max(logits, axis=1, keepdims=True)
    e = jnp.exp(logits - m)
    aff = e / jnp.sum(e, axis=1, keepdims=True)
    aff_ref[0] = aff.T[:n_e, :]


def _outproj(r, a, x, w_bf16, gain, w_router, tm=1024):
    b, s, d = x.shape
    n_r, n_a = r.shape[0], a.shape[0]
    n_e = w_router.shape[1]
    assert 2 * n_e <= LANES
    w_router = jnp.pad(jnp.concatenate([w_router, w_router], axis=1), ((0, 0), (0, LANES - 2 * n_e)))
    spt = s // tm
    tok = lambda width: pl.BlockSpec((tm, width), lambda i: (i, 0))
    slab = lambda blocks: pl.BlockSpec((blocks, tm, LANES), lambda i: (0, i, 0))
    const = lambda shape: pl.BlockSpec(shape, lambda i: (0, 0))
    return pl.pallas_call(
        _outproj_kernel,
        grid=(b * spt,),
        in_specs=[slab(n_r), slab(n_a), tok(d), const(((n_r + n_a) * LANES, d)), const((1, d)), const((d, LANES))],
        out_specs=[tok(d), tok(d), pl.BlockSpec((1, n_e, tm), lambda i: (i // spt, 0, i % spt))],
        out_shape=[
            jax.ShapeDtypeStruct((b * s, d), F32),
            jax.ShapeDtypeStruct((b * s, d), BF16),
            jax.ShapeDtypeStruct((b, n_e, s), F32),
        ],
        compiler_params=_params("parallel"),
        name="outproj",
    )(r, a, x.reshape(b * s, d), w_bf16, gain, w_router)


def _select_kernel(aff_ref, posm_ref, cnt_ref, *, cap):
    rows, s = aff_ref.shape
    bits = pltpu.bitcast(aff_ref[...], I32)

    def search(i, thr):
        cand = thr | jnp.left_shift(jnp.int32(1), 30 - i)
        cnt = jnp.sum((bits >= cand).astype(I32), axis=1, keepdims=True)
        return jnp.where(cnt >= cap, cand, thr)

    thr = lax.fori_loop(0, 31, search, jnp.zeros((rows, 1), I32))
    gt = bits > thr
    eq = bits == thr
    need = cap - jnp.sum(gt.astype(I32), axis=1, keepdims=True)

    w = LANES
    tri = (lax.broadcasted_iota(I32, (w, w), 0) <= lax.broadcasted_iota(I32, (w, w), 1)).astype(BF16)

    def excl_prefix(flags):
        carry = jnp.zeros((rows, 1), F32)
        out = []
        for c0 in range(0, s, w):
            f = flags[:, c0:c0 + w].astype(BF16)
            inc = jnp.dot(f, tri, preferred_element_type=F32)
            out.append(inc - f.astype(F32) + carry)
            carry = carry + inc[:, w - 1:w]
        return jnp.concatenate(out, axis=1).astype(I32)

    sel = gt | (eq & (excl_prefix(eq) < need))
    posm_ref[...] = jnp.where(sel, excl_prefix(sel), -1)

    tok = lax.broadcasted_iota(I32, (w, w), 0)
    edge = lax.broadcasted_iota(I32, (w, w), 1) * MOE_TILE
    cnt = jnp.zeros((rows, w), F32)
    for c0 in range(0, s, w):
        before = ((tok + c0) < edge).astype(BF16)
        cnt = cnt + jnp.dot(sel[:, c0:c0 + w].astype(BF16), before, preferred_element_type=F32)
    cnt_ref[...] = cnt.astype(I32)


def _select(aff_rows, cap):
    rows, s = aff_rows.shape
    return pl.pallas_call(
        functools.partial(_select_kernel, cap=cap),
        out_shape=[jax.ShapeDtypeStruct((rows, s), I32), jax.ShapeDtypeStruct((rows, LANES), I32)],
        compiler_params=pltpu.CompilerParams(vmem_limit_bytes=VMEM_LIMIT),
        name="select",
    )(aff_rows)


def _slot_window(cnt_ref, row, j, cap):
    lo, hi = cnt_ref[row * CNT_STRIDE + j], cnt_ref[row * CNT_STRIDE + j + 1]
    w0 = pl.multiple_of(jnp.minimum((lo // SLOT_ALIGN) * SLOT_ALIGN, cap - SLOT_WIN), SLOT_ALIGN)
    return hi, w0


def _any_overflow(windows):
    return functools.reduce(jnp.logical_or, [hi > w0 + SLOT_WIN for hi, w0 in windows])


def _gather_kernel(cnt_ref, posm_ref, h2_ref, xg_ref):
    bi, j = pl.program_id(0), pl.program_id(1)
    n_e, cap = xg_ref.shape[1:3]
    tt = MOE_TILE

    @pl.when(j == 0)
    def _():
        xg_ref[...] = jnp.zeros_like(xg_ref)

    slot0 = lax.broadcasted_iota(I32, (SLOT_WIN, tt), 0)
    for sub in range(h2_ref.shape[1] // tt):
        tile = j * (h2_ref.shape[1] // tt) + sub
        tok = slice(sub * tt, (sub + 1) * tt)
        windows = [_slot_window(cnt_ref, bi * n_e + e, tile, cap) for e in range(n_e)]
        for e in range(0, n_e, WIN_GROUP):
            onehot = jnp.concatenate(
                [(posm_ref[0, e + i:e + i + 1, tok] == slot0 + windows[e + i][1]).astype(BF16)
                 for i in range(WIN_GROUP)], axis=0)
            rows = jnp.dot(onehot, h2_ref[0, tok, :], preferred_element_type=F32).astype(BF16)
            for i in range(WIN_GROUP):
                xg_ref[0, e + i, pl.ds(windows[e + i][1], SLOT_WIN), :] += rows[i * SLOT_WIN:(i + 1) * SLOT_WIN]

        @pl.when(_any_overflow(windows))
        def _(tile=tile, tok=tok):
            wide = lax.broadcasted_iota(I32, (MXU_DEPTH, tt), 0)

            def one_expert(e, carry):
                hi, w0 = _slot_window(cnt_ref, bi * n_e + e, tile, cap)

                @pl.when(hi > w0 + SLOT_WIN)
                def _():
                    pos = posm_ref[0, pl.ds(e, 1), tok]
                    for ws in range(0, cap, MXU_DEPTH):
                        slot = wide + ws
                        onehot = ((pos == slot) & (slot >= w0 + SLOT_WIN)).astype(BF16)
                        extra = jnp.dot(onehot, h2_ref[0, tok, :], preferred_element_type=F32)
                        xg_ref[0, e, ws:ws + MXU_DEPTH, :] += extra.astype(BF16)

                return carry

            lax.fori_loop(0, n_e, one_expert, 0)


def _gather(cnt, posm, h2, cap):
    b, n_e, s = posm.shape
    d = h2.shape[2]
    step = MOE_TILE * GATHER_TILES
    return pl.pallas_call(
        _gather_kernel,
        grid_spec=pltpu.PrefetchScalarGridSpec(
            num_scalar_prefetch=1,
            grid=(b, s // step),
            in_specs=[
                pl.BlockSpec((1, n_e, step), lambda bi, j, cnt: (bi, 0, j)),
                pl.BlockSpec((1, step, d), lambda bi, j, cnt: (bi, j, 0)),
            ],
            out_specs=pl.BlockSpec((1, n_e, cap, d), lambda bi, j, cnt: (bi, 0, 0, 0)),
        ),
        out_shape=jax.ShapeDtypeStruct((b, n_e, cap, d), BF16),
        compiler_params=_params("parallel", "arbitrary"),
        name="gather",
    )(cnt, posm, h2)


def _ffn_kernel(xg_ref, wg_hbm, wu_hbm, wd_hbm, y_ref, wg_buf, wu_buf, wd_buf, wg_stage, wu_stage, wd_stage,
                sem, acc_ref, *, tf):
    e, bi = pl.program_id(0), pl.program_id(1)
    n_e, n_steps = pl.num_programs(0), pl.num_programs(1)
    rows_in, rows_dn = wg_stage.shape[1], wd_stage.shape[1]
    f_total = wg_buf.shape[2]
    n_chunks = wg_buf.shape[1] // rows_in

    def chunk_copies(expert, c):
        st = c % STEP_CHUNKS
        return (
            pltpu.make_async_copy(wg_hbm.at[expert, pl.ds(c * rows_in, rows_in), :], wg_stage.at[st], sem.at[0, st]),
            pltpu.make_async_copy(wu_hbm.at[expert, pl.ds(c * rows_in, rows_in), :], wu_stage.at[st], sem.at[1, st]),
            pltpu.make_async_copy(wd_hbm.at[expert, pl.ds(c * rows_dn, rows_dn), :], wd_stage.at[st], sem.at[2, st]),
        )

    def finish_chunk(expert, c):
        for cp in chunk_copies(expert, c):
            cp.wait()
        slot, st = expert % 2, c % STEP_CHUNKS
        wg_buf[slot, pl.ds(c * rows_in, rows_in), :] = wg_stage[st].astype(BF16)
        wu_buf[slot, pl.ds(c * rows_in, rows_in), :] = wu_stage[st].astype(BF16)
        wd_buf[slot, pl.ds(c * rows_dn, rows_dn), :] = wd_stage[st].astype(BF16)

    @pl.when((e == 0) & (bi == 0))
    def _():
        for cp in chunk_copies(0, 0):
            cp.start()

        def load_first(c, carry):
            @pl.when(c + 1 < n_chunks)
            def _():
                for cp in chunk_copies(0, c + 1):
                    cp.start()

            finish_chunk(0, c)
            return carry

        lax.fori_loop(0, n_chunks, load_first, 0)

    pending = jnp.where(bi > 0, e + 1 < n_e, e > 0)

    @pl.when(pending)
    def _():
        for k in range(STEP_CHUNKS):
            finish_chunk(jnp.where(bi > 0, e + 1, e),
                         jnp.where(bi > 0, bi - 1, n_steps - 1) * STEP_CHUNKS + k)

    @pl.when(e + 1 < n_e)
    def _():
        for k in range(STEP_CHUNKS):
            for cp in chunk_copies(e + 1, bi * STEP_CHUNKS + k):
                cp.start()

    def swiglu(slot):
        n_rows, cap, d = xg_ref.shape[0], xg_ref.shape[2], xg_ref.shape[3]
        xg = xg_ref[:, 0].reshape(n_rows * cap, d)
        for fi in range(f_total // tf):
            fs = slice(fi * tf, (fi + 1) * tf)
            gate = jnp.dot(xg, wg_buf[slot, :, fs], preferred_element_type=F32)
            up = jnp.dot(xg, wu_buf[slot, :, fs], preferred_element_type=F32)
            hid = ((gate * jax.nn.sigmoid(gate)) * up).astype(BF16)
            part = jnp.dot(hid, wd_buf[slot, fs, :], preferred_element_type=F32)
            if fi == 0:
                acc_ref[...] = part
            else:
                acc_ref[...] += part
        y_ref[:, 0] = acc_ref[...].reshape(n_rows, cap, d).astype(y_ref.dtype)

    for slot in range(2):
        pl.when(e % 2 == slot)(functools.partial(swiglu, slot))


def _ffn(xg, wg, wu, wd, tf=512):
    b, n_e, cap, d = xg.shape
    f = wg.shape[2]
    assert b % FFN_BATCH == 0
    n_steps = b // FFN_BATCH
    n_chunks = n_steps * STEP_CHUNKS
    assert d % n_chunks == 0 and f % n_chunks == 0 and (d // n_chunks) % SLOT_ALIGN == 0
    tok = pl.BlockSpec((FFN_BATCH, 1, cap, d), lambda e, bi: (bi, e, 0, 0))
    hbm = pl.BlockSpec(memory_space=pl.ANY)
    return pl.pallas_call(
        functools.partial(_ffn_kernel, tf=tf),
        grid=(n_e, n_steps),
        in_specs=[tok, hbm, hbm, hbm],
        out_specs=tok,
        out_shape=jax.ShapeDtypeStruct((b, n_e, cap, d), BF16),
        scratch_shapes=[
            pltpu.VMEM((2, d, f), BF16), pltpu.VMEM((2, d, f), BF16), pltpu.VMEM((2, f, d), BF16),
            pltpu.VMEM((STEP_CHUNKS, d // n_chunks, f), F32), pltpu.VMEM((STEP_CHUNKS, d // n_chunks, f), F32),
            pltpu.VMEM((STEP_CHUNKS, f // n_chunks, d), F32),
            pltpu.SemaphoreType.DMA((3, STEP_CHUNKS)),
            pltpu.VMEM((FFN_BATCH * cap, d), F32),
        ],
        compiler_params=_params("arbitrary", "arbitrary"),
        name="ffn",
    )(xg, wg, wu, wd)


def _combine_kernel(cnt_ref, pos_ref, gate_ref, y_ref, x1_ref, g_ref, out_ref, acc_ref):
    bi, j = pl.program_id(0), pl.program_id(1)
    tt = MOE_TILE
    n_e, cap = y_ref.shape[1:3]
    slot0 = lax.broadcasted_iota(I32, (SLOT_WIN, tt), 0)
    for sub in range(x1_ref.shape[1] // tt):
        tile = j * (x1_ref.shape[1] // tt) + sub
        tok = slice(sub * tt, (sub + 1) * tt)
        windows = [_slot_window(cnt_ref, bi * n_e + e, tile, cap) for e in range(n_e)]
        acc = x1_ref[0, tok, :]
        for e in range(0, n_e, WIN_GROUP):
            scatter_t = jnp.concatenate(
                [jnp.where(pos_ref[0, e + i:e + i + 1, tok] == slot0 + windows[e + i][1],
                           gate_ref[0, e + i:e + i + 1, tok], 0.0).astype(BF16) for i in range(WIN_GROUP)], axis=0)
            y_group = jnp.concatenate(
                [y_ref[0, e + i, pl.ds(windows[e + i][1], SLOT_WIN), :] for i in range(WIN_GROUP)], axis=0)
            acc = acc + lax.dot_general(scatter_t, y_group, _TN, preferred_element_type=F32)
        acc_ref[...] = acc

        @pl.when(_any_overflow(windows))
        def _(tile=tile, tok=tok):
            wide = lax.broadcasted_iota(I32, (MXU_DEPTH, tt), 0)

            def one_expert(e, carry):
                hi, w0 = _slot_window(cnt_ref, bi * n_e + e, tile, cap)

                @pl.when(hi > w0 + SLOT_WIN)
                def _():
                    pos = pos_ref[0, pl.ds(e, 1), tok]
                    gate = gate_ref[0, pl.ds(e, 1), tok]
                    for ws in range(0, cap, MXU_DEPTH):
                        slot = wide + ws
                        extra = jnp.where((pos == slot) & (slot >= w0 + SLOT_WIN), gate, 0.0).astype(BF16)
                        acc_ref[...] += lax.dot_general(extra, y_ref[0, e, ws:ws + MXU_DEPTH, :], _TN,
                                                        preferred_element_type=F32)

                return carry

            lax.fori_loop(0, n_e, one_expert, 0)

        acc = acc_ref[...]
        ms = jnp.mean(acc * acc, axis=-1, keepdims=True)
        out_ref[0, tok, :] = acc * lax.rsqrt(ms + EPS) * g_ref[...]


def _combine(cnt, posm, aff, y, x1, gain):
    b, s, d = x1.shape
    n_e, cap = y.shape[1:3]
    step = MOE_TILE * COMBINE_TILES
    return pl.pallas_call(
        _combine_kernel,
        grid_spec=pltpu.PrefetchScalarGridSpec(
            num_scalar_prefetch=1,
            grid=(b, s // step),
            in_specs=[
                pl.BlockSpec((1, n_e, step), lambda bi, j, cnt: (bi, 0, j)),
                pl.BlockSpec((1, n_e, step), lambda bi, j, cnt: (bi, 0, j)),
                pl.BlockSpec((1, n_e, cap, d), lambda bi, j, cnt: (bi, 0, 0, 0)),
                pl.BlockSpec((1, step, d), lambda bi, j, cnt: (bi, j, 0)),
                pl.BlockSpec((1, d), lambda bi, j, cnt: (0, 0)),
            ],
            out_specs=pl.BlockSpec((1, step, d), lambda bi, j, cnt: (bi, j, 0)),
            scratch_shapes=[pltpu.VMEM((MOE_TILE, d), F32)],
        ),
        out_shape=jax.ShapeDtypeStruct((b, s, d), F32),
        compiler_params=_params("parallel", "parallel"),
        name="combine",
    )(cnt, posm, aff, y, x1, gain)


def _rope_tables(seq):
    half = HEAD_DIM // 2
    pos = jnp.arange(seq, dtype=F32)
    inv = ROPE_BASE ** (-jnp.arange(0, HEAD_DIM, 2, dtype=F32) / HEAD_DIM)
    ang = pos[:, None] * inv[None, :]
    cos, sin = jnp.cos(ang), jnp.sin(ang)
    reps = LANES // HEAD_DIM
    cos_t = jnp.tile(jnp.concatenate([cos, cos], axis=1), (1, reps))
    sin_t = jnp.tile(jnp.concatenate([-sin, sin], axis=1), (1, reps))
    return cos_t, sin_t


def kernel(x, norm1_gain, w_in, ret_log_decay, ret_gn_gain, rel_bias, w_out, norm2_gain, w_router,
           w_gate, w_up, w_down, final_gain):
    b, s, d = x.shape
    depth = w_in.shape[0]
    ret_width = ret_gn_gain.shape[1]
    n_ret_pairs = ret_width // LANES
    att_width = (w_in.shape[2] - 4 * ret_width) // 3
    n_att_pairs = att_width // LANES
    cap = CAPACITY_FACTOR * s // N_EXPERTS
    cos_t, sin_t = _rope_tables(s)

    assert depth == 1, "single-layer block: the final norm is fused into the combine kernel"
    layer = 0
    ret, att = _proj(x.reshape(b * s, d), norm1_gain[layer][None, :], w_in[layer].astype(BF16),
                     cos_t, sin_t, n_ret=4 * ret_width, ret_width=ret_width)
    r = _retention(ret, ret_log_decay[layer], ret_gn_gain[layer][None, :], n_ret_pairs, b)
    a = _dilated(att, rel_bias, n_att_pairs, b)
    x1, h2, aff = _outproj(r, a, x, w_out[layer].astype(BF16), norm2_gain[layer][None, :],
                           w_router[layer])
    assert s % (MOE_TILE * max(GATHER_TILES, COMBINE_TILES)) == 0 and s // MOE_TILE < CNT_STRIDE and cap % MXU_DEPTH == 0 and N_EXPERTS % WIN_GROUP == 0
    posm, cnt = _select(aff.reshape(b * N_EXPERTS, s), cap)
    posm = posm.reshape(b, N_EXPERTS, s)
    cnt = cnt[:, :CNT_STRIDE].reshape(-1)
    xg = _gather(cnt, posm, h2.reshape(b, s, d), cap)
    y = _ffn(xg, w_gate[layer], w_up[layer], w_down[layer])
    return _combine(cnt, posm, aff, y, x1.reshape(b, s, d),
                    final_gain[None, :])
```

```python
import functools
import math

import numpy as np
import jax
import jax.numpy as jnp
from jax import lax
from jax.experimental import pallas as pl
from jax.experimental.pallas import tpu as pltpu

F32 = jnp.float32
BF16 = jnp.bfloat16
I32 = jnp.int32

HEAD_DIM = 64
LANES = 128
RET_CHUNK = 128
RET_GROUP = 4
ROPE_BASE = 10000.0
DILATED_PATTERNS = ((128, 1), (512, 4), (2048, 16))
SIDE = 64
N_BUCKETS = 32
MAX_DISTANCE = 1024
N_EXPERTS = 16
CAPACITY_FACTOR = 2
MOE_TILE = 256
GATHER_TILES = 8
COMBINE_TILES = 4
MXU_DEPTH = 256
WIN_GROUP = 4
SLOT_WIN = MXU_DEPTH // WIN_GROUP
SLOT_ALIGN = 16
CNT_STRIDE = 32
FFN_BATCH = 2
STEP_CHUNKS = 2
EPS = 1e-6
NEG = -1e30
VMEM_LIMIT = 56 * 1024 * 1024

_NT = (((1,), (1,)), ((), ()))
_TN = (((0,), (0,)), ((), ()))


def _params(*sem):
    return pltpu.CompilerParams(dimension_semantics=sem, vmem_limit_bytes=VMEM_LIMIT)


def _proj_kernel(x_ref, g_ref, w_ref, cos_ref, sin_ref, ret_ref, att_ref, *, n_ret, ret_width, chunk):
    x = x_ref[...]
    ms = jnp.mean(x * x, axis=-1, keepdims=True)
    h = (x * lax.rsqrt(ms + EPS) * g_ref[...]).astype(BF16)
    n_cols = w_ref.shape[1]
    cos, sin = cos_ref[...], sin_ref[...]
    first_half = (lax.broadcasted_iota(I32, cos.shape, 1) % HEAD_DIM) < (HEAD_DIM // 2)
    for c0 in range(0, n_cols, chunk):
        o = jnp.dot(h, w_ref[:, c0:c0 + chunk], preferred_element_type=F32)
        for l0 in range(0, chunk, LANES):
            t = o[:, l0:l0 + LANES]
            col = c0 + l0
            if col >= n_ret:
                att_ref[(col - n_ret) // LANES] = t
            elif col >= 2 * ret_width:
                ret_ref[col // LANES] = t.astype(BF16)
            else:
                scale = 1.0 if col < ret_width else HEAD_DIM ** -0.5
                swapped = jnp.where(first_half, pltpu.roll(t, LANES - HEAD_DIM // 2, 1),
                                    pltpu.roll(t, HEAD_DIM // 2, 1))
                ret_ref[col // LANES] = ((t * cos + swapped * sin) * scale).astype(BF16)


def _proj(x2d, gain, w_bf16, cos_t, sin_t, n_ret, ret_width, tm=1024):
    t, d = x2d.shape
    n_cols = w_bf16.shape[1]
    n_att = n_cols - n_ret
    seq_tiles = cos_t.shape[0] // tm
    assert ret_width % 512 == 0 and n_ret % 512 == 0 and n_cols % 512 == 0
    return pl.pallas_call(
        functools.partial(_proj_kernel, n_ret=n_ret, ret_width=ret_width, chunk=512),
        grid=(t // tm,),
        in_specs=[
            pl.BlockSpec((tm, d), lambda i: (i, 0)),
            pl.BlockSpec((1, d), lambda i: (0, 0)),
            pl.BlockSpec((d, n_cols), lambda i: (0, 0)),
            pl.BlockSpec((tm, LANES), lambda i: (i % seq_tiles, 0)),
            pl.BlockSpec((tm, LANES), lambda i: (i % seq_tiles, 0)),
        ],
        out_specs=[
            pl.BlockSpec((n_ret // LANES, tm, LANES), lambda i: (0, i, 0)),
            pl.BlockSpec((n_att // LANES, tm, LANES), lambda i: (0, i, 0)),
        ],
        out_shape=[
            jax.ShapeDtypeStruct((n_ret // LANES, t, LANES), BF16),
            jax.ShapeDtypeStruct((n_att // LANES, t, LANES), F32),
        ],
        compiler_params=_params("parallel"),
        name="proj",
    )(x2d, gain, w_bf16, cos_t, sin_t)


def _retention_kernel(decay_ref, q_ref, k_ref, v_ref, g_ref, gn_ref, out_ref,
                      rf_ref, rb_ref, kvf_ref, kvb_ref, p0_ref, p1_ref, *, seq):
    c = RET_CHUNK
    n_chunks = seq // c
    p = pl.program_id(1)
    lane = lax.broadcasted_iota(I32, (c, LANES), 1)
    row = lax.broadcasted_iota(I32, (c, LANES), 0)
    head0 = lane < HEAD_DIM
    rowf = row.astype(F32)

    lgf0, lgf1 = decay_ref[0, 2 * p], decay_ref[0, 2 * p + 1]
    lgb0, lgb1 = decay_ref[1, 2 * p], decay_ref[1, 2 * p + 1]
    lgf_lane = jnp.where(head0, lgf0, lgf1)
    lgb_lane = jnp.where(head0, lgb0, lgb1)
    lgf_row = jnp.where(row < HEAD_DIM, lgf0, lgf1)
    lgb_row = jnp.where(row < HEAD_DIM, lgb0, lgb1)
    same_head = (row < HEAD_DIM) == head0

    zeta_f = jnp.exp((c - 1 - rowf) * lgf_lane)
    zeta_b = jnp.exp(rowf * lgb_lane)
    xi_f = jnp.exp((rowf + 1.0) * lgf_lane)
    xi_b = jnp.exp((c - rowf) * lgb_lane)
    gch_f = jnp.where(same_head, jnp.exp(c * lgf_row), 0.0)
    gch_b = jnp.where(same_head, jnp.exp(c * lgb_row), 0.0)

    diff = (row - lane).astype(F32)

    def dmat(lf, lb):
        return jnp.exp(jnp.where(diff >= 0, diff * lf, -diff * lb))

    d_stack = jnp.concatenate([dmat(lgf0, lgb0), dmat(lgf1, lgb1)], axis=0)

    def kv_step(n, carry):
        sl = pl.ds(pl.multiple_of(n * c, c), c)
        k = k_ref[0, sl, :].astype(F32)
        v = v_ref[0, sl, :]
        kvf = lax.dot_general((k * zeta_f).astype(BF16), v, _TN, preferred_element_type=F32)
        kvb = lax.dot_general((k * zeta_b).astype(BF16), v, _TN, preferred_element_type=F32)
        kvf_ref[n] = jnp.where(same_head, kvf, 0.0)
        kvb_ref[n] = jnp.where(same_head, kvb, 0.0)
        return carry

    lax.fori_loop(0, n_chunks, kv_step, 0, unroll=True)

    def scan_step(i, states):
        sf, sb = states
        nb = n_chunks - 1 - i
        rf_ref[i] = sf.astype(BF16)
        rb_ref[nb] = sb.astype(BF16)
        return gch_f * sf + kvf_ref[i], gch_b * sb + kvb_ref[nb]

    zero_state = jnp.zeros((LANES, LANES), F32)
    lax.fori_loop(0, n_chunks, scan_step, (zero_state, zero_state), unroll=True)

    gn_gain = gn_ref[...]

    def score_stage(g, p_ref):
        for jc in range(RET_GROUP):
            sl = pl.ds(pl.multiple_of((g * RET_GROUP + jc) * c, c), c)
            q = q_ref[0, sl, :]
            zero = jnp.zeros_like(q)
            q_stack = jnp.concatenate([jnp.where(head0, q, zero), jnp.where(head0, zero, q)], axis=0)
            scores = lax.dot_general(q_stack, k_ref[0, sl, :], _NT, preferred_element_type=F32) * d_stack
            sb = scores.astype(BF16)
            p_ref[jc] = jnp.concatenate([sb[:c], sb[c:]], axis=1)

    def half_mean(t):
        s0 = jnp.sum(jnp.where(head0, t, 0.0), axis=-1, keepdims=True)
        s1 = jnp.sum(jnp.where(head0, 0.0, t), axis=-1, keepdims=True)
        return jnp.where(head0, s0, s1) * (1.0 / HEAD_DIM)

    def value_stage(g, p_ref):
        for jc in range(RET_GROUP):
            n = g * RET_GROUP + jc
            sl = pl.ds(pl.multiple_of(n * c, c), c)
            q = q_ref[0, sl, :]
            v = v_ref[0, sl, :]
            zero = jnp.zeros_like(v)
            v_stack = jnp.concatenate([jnp.where(head0, v, zero), jnp.where(head0, zero, v)], axis=0)
            inner = jnp.dot(p_ref[jc], v_stack, preferred_element_type=F32)
            cross_f = jnp.dot(q, rf_ref[n], preferred_element_type=F32) * xi_f
            cross_b = jnp.dot(q, rb_ref[n], preferred_element_type=F32) * xi_b
            r = inner + cross_f + cross_b
            dlt = r - half_mean(r)
            var = half_mean(dlt * dlt)
            gate = g_ref[0, sl, :].astype(F32)
            y = dlt * lax.rsqrt(var + EPS) * gn_gain * (gate * jax.nn.sigmoid(gate))
            out_ref[0, sl, :] = y.astype(out_ref.dtype)

    n_groups = n_chunks // RET_GROUP
    assert n_groups % 2 == 0 and n_groups >= 2
    score_stage(0, p0_ref)

    def group_pair(i, carry):
        g = 2 * i
        score_stage(g + 1, p1_ref)
        value_stage(g, p0_ref)
        score_stage(g + 2, p0_ref)
        value_stage(g + 1, p1_ref)
        return carry

    lax.fori_loop(0, n_groups // 2 - 1, group_pair, 0, unroll=True)
    score_stage(n_groups - 1, p1_ref)
    value_stage(n_groups - 2, p0_ref)
    value_stage(n_groups - 1, p1_ref)


def _retention(ret, decay, gn_gain, n_pairs, b):
    s = ret.shape[1] // b
    blk = lambda off: pl.BlockSpec((1, s, LANES), lambda bi, pi: (off + pi, bi, 0))
    return pl.pallas_call(
        functools.partial(_retention_kernel, seq=s),
        grid=(b, n_pairs),
        in_specs=[
            pl.BlockSpec(memory_space=pltpu.SMEM),
            blk(0), blk(n_pairs), blk(2 * n_pairs), blk(3 * n_pairs),
            pl.BlockSpec((1, LANES), lambda bi, pi: (0, pi)),
        ],
        out_specs=pl.BlockSpec((1, s, LANES), lambda bi, pi: (pi, bi, 0)),
        out_shape=jax.ShapeDtypeStruct((n_pairs, b * s, LANES), BF16),
        scratch_shapes=[
            pltpu.VMEM((s // RET_CHUNK, LANES, LANES), BF16),
            pltpu.VMEM((s // RET_CHUNK, LANES, LANES), BF16),
            pltpu.VMEM((s // RET_CHUNK, LANES, LANES), F32),
            pltpu.VMEM((s // RET_CHUNK, LANES, LANES), F32),
            pltpu.VMEM((RET_GROUP, RET_CHUNK, 2 * RET_CHUNK), BF16),
            pltpu.VMEM((RET_GROUP, RET_CHUNK, 2 * RET_CHUNK), BF16),
        ],
        compiler_params=_params("parallel", "parallel"),
        name="retention",
    )(decay, ret, ret, ret, ret, gn_gain)


QBLK = 2 * SIDE
KWIN = 4 * SIDE
BASE_W = 512
ATT_GROUP = 4


def _t5_bucket_np(rel):
    half = N_BUCKETS // 2
    max_exact = half // 2
    bucket = np.where(rel > 0, half, 0)
    n = np.abs(rel)
    nf = np.maximum(n, 1).astype(np.float32)
    large = max_exact + (np.log(nf / np.float32(max_exact)) / np.float32(math.log(MAX_DISTANCE / max_exact))
                         * np.float32(half - max_exact)).astype(np.int32)
    large = np.minimum(large, half - 1)
    return (bucket + np.where(n < max_exact, n, large)).astype(np.int32)


def _bucket_rows():
    k = np.arange(BASE_W)
    off = k - SIDE
    rows = []
    for _, dilation in DILATED_PATTERNS:
        rows.append(np.where(k <= 2 * SIDE, _t5_bucket_np(off * dilation), -1))
    return np.stack(rows).astype(np.int32)


def _dilated_kernel(bias_tab_ref, bucket_ref, qkv_ref, out_ref,
                    bias_ref, perm_ref, stage_ref, o_ref, lse_ref, e0_ref, e1_ref, m0_ref, m1_ref, *, seq):
    order = list(range(1, len(DILATED_PATTERNS))) + [0]
    dil = [DILATED_PATTERNS[pi][1] for pi in order]
    n_slots = len(order)
    assert n_slots == 3 and dil[0] > 1 and dil[1] % dil[0] == 0 and dil[2] == 1
    p = pl.program_id(1)
    lane = lax.broadcasted_iota(I32, (QBLK, LANES), 1)
    head0 = lane < HEAD_DIM
    head0_k = lax.broadcasted_iota(I32, (KWIN, LANES), 1) < HEAD_DIM
    n_groups = seq // QBLK // ATT_GROUP
    n_loop = n_groups // 2 - 1
    assert n_groups % 2 == 0 and n_loop >= 0

    for pi in range(len(DILATED_PATTERNS)):
        bucket = jnp.broadcast_to(bucket_ref[pi:pi + 1, :], (QBLK, BASE_W))
        for hh in range(2):
            base = jnp.full((QBLK, BASE_W), NEG, F32)
            for bk in range(N_BUCKETS):
                base = jnp.where(bucket == bk, bias_tab_ref[bk, 2 * p + hh], base)
            for var, shift in enumerate((BASE_W - SIDE, 0, SIDE)):
                tile = pltpu.roll(base, shift, 1, stride=1, stride_axis=0)
                bias_ref[pi * 3 + var, hh * QBLK:(hh + 1) * QBLK, :] = tile[:, :KWIN]

    def relayout(slot, i):
        d = dil[slot]
        seg_len = seq // d
        scale = jnp.where(i == 0, HEAD_DIM ** -0.5, 1.0).astype(F32)
        if d == 1:
            perm_ref[slot, i] = (qkv_ref[i, 0] * scale).astype(BF16)
        elif slot == 0:
            for r in range(d):
                val = qkv_ref[i, 0, pl.ds(r, seg_len, stride=d), :]
                stage_ref[i, r * seg_len:(r + 1) * seg_len, :] = val
                perm_ref[slot, i, r * seg_len:(r + 1) * seg_len, :] = (val * scale).astype(BF16)
        else:
            prev_d = dil[slot - 1]
            step, prev_len = d // prev_d, seq // prev_d
            for r_prev in range(prev_d):
                for r_step in range(step):
                    val = stage_ref[i, pl.ds(r_prev * prev_len + r_step, seg_len, stride=step), :]
                    r = r_prev + prev_d * r_step
                    perm_ref[slot, i, r * seg_len:(r + 1) * seg_len, :] = (val * scale).astype(BF16)

    def placement(slot, t):
        seg_len = seq // dil[slot]
        blocks_per_seg = seg_len // QBLK
        seg = t // blocks_per_seg
        u = t % blocks_per_seg
        q0 = pl.multiple_of(t * QBLK, QBLK)
        seg0 = seg * seg_len
        k0 = pl.multiple_of(jnp.clip(q0 - SIDE, seg0, seg0 + seg_len - KWIN), SIDE)
        var = jnp.where(u == 0, 0, jnp.where(u == blocks_per_seg - 1, 2, 1))
        return seg, u, q0, k0, var

    def logits_stage(slot, g, e_ref, m_ref):
        for jb in range(ATT_GROUP):
            _, _, q0, k0, var = placement(slot, g * ATT_GROUP + jb)
            q = perm_ref[slot, 0, pl.ds(q0, QBLK), :]
            kw = perm_ref[slot, 1, pl.ds(k0, KWIN), :]
            zero = jnp.zeros_like(q)
            q_stack = jnp.concatenate([jnp.where(head0, q, zero), jnp.where(head0, zero, q)], axis=0)
            s = lax.dot_general(q_stack, kw, _NT, preferred_element_type=F32) + bias_ref[order[slot] * 3 + var]
            m = jnp.max(s, axis=-1, keepdims=True)
            e_ref[jb] = jnp.exp(s - m).astype(BF16)
            m_ref[jb] = jnp.where(head0, m[:QBLK], m[QBLK:])

    def value_stage(slot, g, e_ref, m_ref):
        d = dil[slot]
        for jb in range(ATT_GROUP):
            seg, u, q0, k0, _ = placement(slot, g * ATT_GROUP + jb)
            vw = perm_ref[slot, 2, pl.ds(k0, KWIN), :]
            one = jnp.ones_like(vw)
            pv0 = jnp.dot(e_ref[jb, :QBLK, :], jnp.where(head0_k, vw, one), preferred_element_type=F32)
            pv1 = jnp.dot(e_ref[jb, QBLK:, :], jnp.where(head0_k, one, vw), preferred_element_type=F32)
            num = jnp.where(head0, pv0, pv1)
            den = pltpu.roll(jnp.where(head0, pv1, pv0), HEAD_DIM, 1)
            o = num * (1.0 / den)
            lse = m_ref[jb] + jnp.log(den)
            dst = pl.ds(seg + d * (u * QBLK), QBLK, stride=d) if d > 1 else pl.ds(q0, QBLK)
            if slot < n_slots - 1:
                o_ref[slot, dst, :] = o
                lse_ref[slot, dst, :] = lse
            else:
                lses = [lse_ref[i, dst, :] for i in range(slot)] + [lse]
                outs = [o_ref[i, dst, :] for i in range(slot)] + [o]
                mx = functools.reduce(jnp.maximum, lses)
                ws = [jnp.exp(l - mx) for l in lses]
                inv = 1.0 / functools.reduce(jnp.add, ws)
                acc = functools.reduce(jnp.add, [(w * inv) * t for w, t in zip(ws, outs)])
                out_ref[0, dst, :] = acc.astype(out_ref.dtype)

    for slot in range(n_slots):
        for i in range(3):
            relayout(slot, i)
    logits_stage(0, 0, e0_ref, m0_ref)
    for slot in range(n_slots):
        nxt = slot + 1 if slot + 1 < n_slots else None

        def group_pair(i, carry, slot=slot):
            g = 2 * i
            logits_stage(slot, g + 1, e1_ref, m1_ref)
            value_stage(slot, g, e0_ref, m0_ref)
            logits_stage(slot, g + 2, e0_ref, m0_ref)
            value_stage(slot, g + 1, e1_ref, m1_ref)
            return carry

        lax.fori_loop(0, n_loop, group_pair, 0)
        logits_stage(slot, n_groups - 1, e1_ref, m1_ref)
        value_stage(slot, n_groups - 2, e0_ref, m0_ref)
        if nxt is not None:
            logits_stage(nxt, 0, e0_ref, m0_ref)
        value_stage(slot, n_groups - 1, e1_ref, m1_ref)


def _dilated(att, rel_bias, n_pairs, b):
    s = att.shape[1] // b
    n_pat = len(DILATED_PATTERNS)
    bucket_rows = jnp.asarray(_bucket_rows())
    return pl.pallas_call(
        functools.partial(_dilated_kernel, seq=s),
        grid=(b, n_pairs),
        in_specs=[
            pl.BlockSpec(memory_space=pltpu.SMEM),
            pl.BlockSpec((n_pat, BASE_W), lambda bi, pi: (0, 0)),
            pl.BlockSpec((3, 1, s, LANES), lambda bi, pi: (0, pi, bi, 0)),
        ],
        out_specs=pl.BlockSpec((1, s, LANES), lambda bi, pi: (pi, bi, 0)),
        out_shape=jax.ShapeDtypeStruct((n_pairs, b * s, LANES), BF16),
        scratch_shapes=[
            pltpu.VMEM((n_pat * 3, 2 * QBLK, KWIN), F32),
            pltpu.VMEM((n_pat, 3, s, LANES), BF16),
            pltpu.VMEM((3, s, LANES), F32),
            pltpu.VMEM((n_pat - 1, s, LANES), F32),
            pltpu.VMEM((n_pat - 1, s, LANES), F32),
            pltpu.VMEM((ATT_GROUP, 2 * QBLK, KWIN), BF16),
            pltpu.VMEM((ATT_GROUP, 2 * QBLK, KWIN), BF16),
            pltpu.VMEM((ATT_GROUP, QBLK, LANES), F32),
            pltpu.VMEM((ATT_GROUP, QBLK, LANES), F32),
        ],
        compiler_params=_params("parallel", "parallel"),
        name="dilated",
    )(rel_bias, bucket_rows, att.reshape(3, n_pairs, b * s, LANES))


def _outproj_kernel(r_ref, a_ref, x_ref, w_ref, g_ref, wr_ref, x1_ref, h2_ref, aff_ref):
    n_e = aff_ref.shape[1]
    wr = wr_ref[...]
    w_hi = wr.astype(BF16)
    w_lo = (wr - w_hi.astype(F32)).astype(BF16)
    w_parts = jnp.where(lax.broadcasted_iota(I32, wr.shape, 1) < n_e, w_hi, w_lo)
    tm = x_ref.shape[0]
    mixed = jnp.concatenate([r_ref[i] for i in range(r_ref.shape[0])]
                            + [a_ref[i] for i in range(a_ref.shape[0])], axis=1)
    x1 = x_ref[...] + jnp.dot(mixed, w_ref[...], preferred_element_type=F32)
    x1_ref[...] = x1
    ms = jnp.mean(x1 * x1, axis=-1, keepdims=True)
    h2 = x1 * lax.rsqrt(ms + EPS) * g_ref[...]
    h_hi = h2.astype(BF16)
    h2_ref[...] = h_hi
    h_lo = (h2 - h_hi.astype(F32)).astype(BF16)
    prod = jnp.dot(jnp.concatenate([h_hi, h_lo], axis=0), w_parts, preferred_element_type=F32)
    logits = prod[:tm] + (pltpu.roll(prod[:tm], LANES - n_e, 1) + prod[tm:])
    expert_lane = lax.broadcasted_iota(I32, logits.shape, 1) < n_e
    logits = jnp.where(expert_lane, logits, NEG)
    m = jnp.max(logits, axis=1, keepdims=True)
    e = jnp.exp(logits - m)
    aff = e / jnp.sum(e, axis=1, keepdims=True)
    aff_ref[0] = aff.T[:n_e, :]


def _outproj(r, a, x, w_bf16, gain, w_router, tm=1024):
    b, s, d = x.shape
    n_r, n_a = r.shape[0], a.shape[0]
    n_e = w_router.shape[1]
    assert 2 * n_e <= LANES
    w_router = jnp.pad(jnp.concatenate([w_router, w_router], axis=1), ((0, 0), (0, LANES - 2 * n_e)))
    spt = s // tm
    tok = lambda width: pl.BlockSpec((tm, width), lambda i: (i, 0))
    slab = lambda blocks: pl.BlockSpec((blocks, tm, LANES), lambda i: (0, i, 0))
    const = lambda shape: pl.BlockSpec(shape, lambda i: (0, 0))
    return pl.pallas_call(
        _outproj_kernel,
        grid=(b * spt,),
        in_specs=[slab(n_r), slab(n_a), tok(d), const(((n_r + n_a) * LANES, d)), const((1, d)), const((d, LANES))],
        out_specs=[tok(d), tok(d), pl.BlockSpec((1, n_e, tm), lambda i: (i // spt, 0, i % spt))],
        out_shape=[
            jax.ShapeDtypeStruct((b * s, d), F32),
            jax.ShapeDtypeStruct((b * s, d), BF16),
            jax.ShapeDtypeStruct((b, n_e, s), F32),
        ],
        compiler_params=_params("parallel"),
        name="outproj",
    )(r, a, x.reshape(b * s, d), w_bf16, gain, w_router)


def _select_kernel(aff_ref, posm_ref, cnt_ref, *, cap):
    rows, s = aff_ref.shape
    bits = pltpu.bitcast(aff_ref[...], I32)

    def search(i, thr):
        cand = thr | jnp.left_shift(jnp.int32(1), 30 - i)
        cnt = jnp.sum((bits >= cand).astype(I32), axis=1, keepdims=True)
        return jnp.where(cnt >= cap, cand, thr)

    thr = lax.fori_loop(0, 31, search, jnp.zeros((rows, 1), I32))
    gt = bits > thr
    eq = bits == thr
    need = cap - jnp.sum(gt.astype(I32), axis=1, keepdims=True)

    w = LANES
    tri = (lax.broadcasted_iota(I32, (w, w), 0) <= lax.broadcasted_iota(I32, (w, w), 1)).astype(BF16)

    def excl_prefix(flags):
        carry = jnp.zeros((rows, 1), F32)
        out = []
        for c0 in range(0, s, w):
            f = flags[:, c0:c0 + w].astype(BF16)
            inc = jnp.dot(f, tri, preferred_element_type=F32)
            out.append(inc - f.astype(F32) + carry)
            carry = carry + inc[:, w - 1:w]
        return jnp.concatenate(out, axis=1).astype(I32)

    sel = gt | (eq & (excl_prefix(eq) < need))
    posm_ref[...] = jnp.where(sel, excl_prefix(sel), -1)

    tok = lax.broadcasted_iota(I32, (w, w), 0)
    edge = lax.broadcasted_iota(I32, (w, w), 1) * MOE_TILE
    cnt = jnp.zeros((rows, w), F32)
    for c0 in range(0, s, w):
        before = ((tok + c0) < edge).astype(BF16)
        cnt = cnt + jnp.dot(sel[:, c0:c0 + w].astype(BF16), before, preferred_element_type=F32)
    cnt_ref[...] = cnt.astype(I32)


def _select(aff_rows, cap):
    rows, s = aff_rows.shape
    return pl.pallas_call(
        functools.partial(_select_kernel, cap=cap),
        out_shape=[jax.ShapeDtypeStruct((rows, s), I32), jax.ShapeDtypeStruct((rows, LANES), I32)],
        compiler_params=pltpu.CompilerParams(vmem_limit_bytes=VMEM_LIMIT),
        name="select",
    )(aff_rows)


def _slot_window(cnt_ref, row, j, cap):
    lo, hi = cnt_ref[row * CNT_STRIDE + j], cnt_ref[row * CNT_STRIDE + j + 1]
    w0 = pl.multiple_of(jnp.minimum((lo // SLOT_ALIGN) * SLOT_ALIGN, cap - SLOT_WIN), SLOT_ALIGN)
    return hi, w0


def _any_overflow(windows):
    return functools.reduce(jnp.logical_or, [hi > w0 + SLOT_WIN for hi, w0 in windows])


def _gather_kernel(cnt_ref, posm_ref, h2_ref, xg_ref):
    bi, j = pl.program_id(0), pl.program_id(1)
    n_e, cap = xg_ref.shape[1:3]
    tt = MOE_TILE

    @pl.when(j == 0)
    def _():
        xg_ref[...] = jnp.zeros_like(xg_ref)

    slot0 = lax.broadcasted_iota(I32, (SLOT_WIN, tt), 0)
    for sub in range(h2_ref.shape[1] // tt):
        tile = j * (h2_ref.shape[1] // tt) + sub
        tok = slice(sub * tt, (sub + 1) * tt)
        windows = [_slot_window(cnt_ref, bi * n_e + e, tile, cap) for e in range(n_e)]
        for e in range(0, n_e, WIN_GROUP):
            onehot = jnp.concatenate(
                [(posm_ref[0, e + i:e + i + 1, tok] == slot0 + windows[e + i][1]).astype(BF16)
                 for i in range(WIN_GROUP)], axis=0)
            rows = jnp.dot(onehot, h2_ref[0, tok, :], preferred_element_type=F32).astype(BF16)
            for i in range(WIN_GROUP):
                xg_ref[0, e + i, pl.ds(windows[e + i][1], SLOT_WIN), :] += rows[i * SLOT_WIN:(i + 1) * SLOT_WIN]

        @pl.when(_any_overflow(windows))
        def _(tile=tile, tok=tok):
            wide = lax.broadcasted_iota(I32, (MXU_DEPTH, tt), 0)

            def one_expert(e, carry):
                hi, w0 = _slot_window(cnt_ref, bi * n_e + e, tile, cap)

                @pl.when(hi > w0 + SLOT_WIN)
                def _():
                    pos = posm_ref[0, pl.ds(e, 1), tok]
                    for ws in range(0, cap, MXU_DEPTH):
                        slot = wide + ws
                        onehot = ((pos == slot) & (slot >= w0 + SLOT_WIN)).astype(BF16)
                        extra = jnp.dot(onehot, h2_ref[0, tok, :], preferred_element_type=F32)
                        xg_ref[0, e, ws:ws + MXU_DEPTH, :] += extra.astype(BF16)

                return carry

            lax.fori_loop(0, n_e, one_expert, 0)


def _gather(cnt, posm, h2, cap):
    b, n_e, s = posm.shape
    d = h2.shape[2]
    step = MOE_TILE * GATHER_TILES
    return pl.pallas_call(
        _gather_kernel,
        grid_spec=pltpu.PrefetchScalarGridSpec(
            num_scalar_prefetch=1,
            grid=(b, s // step),
            in_specs=[
                pl.BlockSpec((1, n_e, step), lambda bi, j, cnt: (bi, 0, j)),
                pl.BlockSpec((1, step, d), lambda bi, j, cnt: (bi, j, 0)),
            ],
            out_specs=pl.BlockSpec((1, n_e, cap, d), lambda bi, j, cnt: (bi, 0, 0, 0)),
        ),
        out_shape=jax.ShapeDtypeStruct((b, n_e, cap, d), BF16),
        compiler_params=_params("parallel", "arbitrary"),
        name="gather",
    )(cnt, posm, h2)


def _ffn_kernel(xg_ref, wg_hbm, wu_hbm, wd_hbm, y_ref, wg_buf, wu_buf, wd_buf, wg_stage, wu_stage, wd_stage,
                sem, acc_ref, *, tf):
    e, bi = pl.program_id(0), pl.program_id(1)
    n_e, n_steps = pl.num_programs(0), pl.num_programs(1)
    rows_in, rows_dn = wg_stage.shape[1], wd_stage.shape[1]
    f_total = wg_buf.shape[2]
    n_chunks = wg_buf.shape[1] // rows_in

    def chunk_copies(expert, c):
        st = c % STEP_CHUNKS
        return (
            pltpu.make_async_copy(wg_hbm.at[expert, pl.ds(c * rows_in, rows_in), :], wg_stage.at[st], sem.at[0, st]),
            pltpu.make_async_copy(wu_hbm.at[expert, pl.ds(c * rows_in, rows_in), :], wu_stage.at[st], sem.at[1, st]),
            pltpu.make_async_copy(wd_hbm.at[expert, pl.ds(c * rows_dn, rows_dn), :], wd_stage.at[st], sem.at[2, st]),
        )

    def finish_chunk(expert, c):
        for cp in chunk_copies(expert, c):
            cp.wait()
        slot, st = expert % 2, c % STEP_CHUNKS
        wg_buf[slot, pl.ds(c * rows_in, rows_in), :] = wg_stage[st].astype(BF16)
        wu_buf[slot, pl.ds(c * rows_in, rows_in), :] = wu_stage[st].astype(BF16)
        wd_buf[slot, pl.ds(c * rows_dn, rows_dn), :] = wd_stage[st].astype(BF16)

    @pl.when((e == 0) & (bi == 0))
    def _():
        for cp in chunk_copies(0, 0):
            cp.start()

        def load_first(c, carry):
            @pl.when(c + 1 < n_chunks)
            def _():
                for cp in chunk_copies(0, c + 1):
                    cp.start()

            finish_chunk(0, c)
            return carry

        lax.fori_loop(0, n_chunks, load_first, 0)

    pending = jnp.where(bi > 0, e + 1 < n_e, e > 0)

    @pl.when(pending)
    def _():
        for k in range(STEP_CHUNKS):
            finish_chunk(jnp.where(bi > 0, e + 1, e),
                         jnp.where(bi > 0, bi - 1, n_steps - 1) * STEP_CHUNKS + k)

    @pl.when(e + 1 < n_e)
    def _():
        for k in range(STEP_CHUNKS):
            for cp in chunk_copies(e + 1, bi * STEP_CHUNKS + k):
                cp.start()

    def swiglu(slot):
        n_rows, cap, d = xg_ref.shape[0], xg_ref.shape[2], xg_ref.shape[3]
        xg = xg_ref[:, 0].reshape(n_rows * cap, d)
        for fi in range(f_total // tf):
            fs = slice(fi * tf, (fi + 1) * tf)
            gate = jnp.dot(xg, wg_buf[slot, :, fs], preferred_element_type=F32)
            up = jnp.dot(xg, wu_buf[slot, :, fs], preferred_element_type=F32)
            hid = ((gate * jax.nn.sigmoid(gate)) * up).astype(BF16)
            part = jnp.dot(hid, wd_buf[slot, fs, :], preferred_element_type=F32)
            if fi == 0:
                acc_ref[...] = part
            else:
                acc_ref[...] += part
        y_ref[:, 0] = acc_ref[...].reshape(n_rows, cap, d).astype(y_ref.dtype)

    for slot in range(2):
        pl.when(e % 2 == slot)(functools.partial(swiglu, slot))


def _ffn(xg, wg, wu, wd, tf=MXU_DEPTH):
    b, n_e, cap, d = xg.shape
    f = wg.shape[2]
    assert b % FFN_BATCH == 0
    n_steps = b // FFN_BATCH
    n_chunks = n_steps * STEP_CHUNKS
    assert d % n_chunks == 0 and f % n_chunks == 0 and (d // n_chunks) % SLOT_ALIGN == 0
    tok = pl.BlockSpec((FFN_BATCH, 1, cap, d), lambda e, bi: (bi, e, 0, 0))
    hbm = pl.BlockSpec(memory_space=pl.ANY)
    return pl.pallas_call(
        functools.partial(_ffn_kernel, tf=tf),
        grid=(n_e, n_steps),
        in_specs=[tok, hbm, hbm, hbm],
        out_specs=tok,
        out_shape=jax.ShapeDtypeStruct((b, n_e, cap, d), BF16),
        scratch_shapes=[
            pltpu.VMEM((2, d, f), BF16), pltpu.VMEM((2, d, f), BF16), pltpu.VMEM((2, f, d), BF16),
            pltpu.VMEM((STEP_CHUNKS, d // n_chunks, f), F32), pltpu.VMEM((STEP_CHUNKS, d // n_chunks, f), F32),
            pltpu.VMEM((STEP_CHUNKS, f // n_chunks, d), F32),
            pltpu.SemaphoreType.DMA((3, STEP_CHUNKS)),
            pltpu.VMEM((FFN_BATCH * cap, d), F32),
        ],
        compiler_params=_params("arbitrary", "arbitrary"),
        name="ffn",
    )(xg, wg, wu, wd)


def _combine_kernel(cnt_ref, pos_ref, gate_ref, y_ref, x1_ref, g_ref, out_ref, acc_ref):
    bi, j = pl.program_id(0), pl.program_id(1)
    tt = MOE_TILE
    n_e, cap = y_ref.shape[1:3]
    slot0 = lax.broadcasted_iota(I32, (SLOT_WIN, tt), 0)
    for sub in range(x1_ref.shape[1] // tt):
        tile = j * (x1_ref.shape[1] // tt) + sub
        tok = slice(sub * tt, (sub + 1) * tt)
        windows = [_slot_window(cnt_ref, bi * n_e + e, tile, cap) for e in range(n_e)]
        acc = x1_ref[0, tok, :]
        for e in range(0, n_e, WIN_GROUP):
            scatter_t = jnp.concatenate(
                [jnp.where(pos_ref[0, e + i:e + i + 1, tok] == slot0 + windows[e + i][1],
                           gate_ref[0, e + i:e + i + 1, tok], 0.0).astype(BF16) for i in range(WIN_GROUP)], axis=0)
            y_group = jnp.concatenate(
                [y_ref[0, e + i, pl.ds(windows[e + i][1], SLOT_WIN), :] for i in range(WIN_GROUP)], axis=0)
            acc = acc + lax.dot_general(scatter_t, y_group, _TN, preferred_element_type=F32)
        acc_ref[...] = acc

        @pl.when(_any_overflow(windows))
        def _(tile=tile, tok=tok):
            wide = lax.broadcasted_iota(I32, (MXU_DEPTH, tt), 0)

            def one_expert(e, carry):
                hi, w0 = _slot_window(cnt_ref, bi * n_e + e, tile, cap)

                @pl.when(hi > w0 + SLOT_WIN)
                def _():
                    pos = pos_ref[0, pl.ds(e, 1), tok]
                    gate = gate_ref[0, pl.ds(e, 1), tok]
                    for ws in range(0, cap, MXU_DEPTH):
                        slot = wide + ws
                        extra = jnp.where((pos == slot) & (slot >= w0 + SLOT_WIN), gate, 0.0).astype(BF16)
                        acc_ref[...] += lax.dot_general(extra, y_ref[0, e, ws:ws + MXU_DEPTH, :], _TN,
                                                        preferred_element_type=F32)

                return carry

            lax.fori_loop(0, n_e, one_expert, 0)

        acc = acc_ref[...]
        ms = jnp.mean(acc * acc, axis=-1, keepdims=True)
        out_ref[0, tok, :] = acc * lax.rsqrt(ms + EPS) * g_ref[...]


def _combine(cnt, posm, aff, y, x1, gain):
    b, s, d = x1.shape
    n_e, cap = y.shape[1:3]
    step = MOE_TILE * COMBINE_TILES
    return pl.pallas_call(
        _combine_kernel,
        grid_spec=pltpu.PrefetchScalarGridSpec(
            num_scalar_prefetch=1,
            grid=(b, s // step),
            in_specs=[
                pl.BlockSpec((1, n_e, step), lambda bi, j, cnt: (bi, 0, j)),
                pl.BlockSpec((1, n_e, step), lambda bi, j, cnt: (bi, 0, j)),
                pl.BlockSpec((1, n_e, cap, d), lambda bi, j, cnt: (bi, 0, 0, 0)),
                pl.BlockSpec((1, step, d), lambda bi, j, cnt: (bi, j, 0)),
                pl.BlockSpec((1, d), lambda bi, j, cnt: (0, 0)),
            ],
            out_specs=pl.BlockSpec((1, step, d), lambda bi, j, cnt: (bi, j, 0)),
            scratch_shapes=[pltpu.VMEM((MOE_TILE, d), F32)],
        ),
        out_shape=jax.ShapeDtypeStruct((b, s, d), F32),
        compiler_params=_params("parallel", "parallel"),
        name="combine",
    )(cnt, posm, aff, y, x1, gain)


def _rope_tables(seq):
    half = HEAD_DIM // 2
    pos = jnp.arange(seq, dtype=F32)
    inv = ROPE_BASE ** (-jnp.arange(0, HEAD_DIM, 2, dtype=F32) / HEAD_DIM)
    ang = pos[:, None] * inv[None, :]
    cos, sin = jnp.cos(ang), jnp.sin(ang)
    reps = LANES // HEAD_DIM
    cos_t = jnp.tile(jnp.concatenate([cos, cos], axis=1), (1, reps))
    sin_t = jnp.tile(jnp.concatenate([-sin, sin], axis=1), (1, reps))
    return cos_t, sin_t


def kernel(x, norm1_gain, w_in, ret_log_decay, ret_gn_gain, rel_bias, w_out, norm2_gain, w_router,
           w_gate, w_up, w_down, final_gain):
    b, s, d = x.shape
    depth = w_in.shape[0]
    ret_width = ret_gn_gain.shape[1]
    n_ret_pairs = ret_width // LANES
    att_width = (w_in.shape[2] - 4 * ret_width) // 3
    n_att_pairs = att_width // LANES
    cap = CAPACITY_FACTOR * s // N_EXPERTS
    cos_t, sin_t = _rope_tables(s)

    assert depth == 1, "single-layer block: the final norm is fused into the combine kernel"
    layer = 0
    ret, att = _proj(x.reshape(b * s, d), norm1_gain[layer][None, :], w_in[layer].astype(BF16),
                     cos_t, sin_t, n_ret=4 * ret_width, ret_width=ret_width)
    r = _retention(ret, ret_log_decay[layer], ret_gn_gain[layer][None, :], n_ret_pairs, b)
    a = _dilated(att, rel_bias, n_att_pairs, b)
    x1, h2, aff = _outproj(r, a, x, w_out[layer].astype(BF16), norm2_gain[layer][None, :],
                           w_router[layer])
    assert s % (MOE_TILE * max(GATHER_TILES, COMBINE_TILES)) == 0 and s // MOE_TILE < CNT_STRIDE and cap % MXU_DEPTH == 0 and N_EXPERTS % WIN_GROUP == 0
    posm, cnt = _select(aff.reshape(b * N_EXPERTS, s), cap)
    posm = posm.reshape(b, N_EXPERTS, s)
    cnt = cnt[:, :CNT_STRIDE].reshape(-1)
    xg = _gather(cnt, posm, h2.reshape(b, s, d), cap)
    y = _ffn(xg, w_gate[layer], w_up[layer], w_down[layer])
    return _combine(cnt, posm, aff, y, x1.reshape(b, s, d),
                    final_gain[None, :])
```

```python
import functools
import math

import numpy as np
import jax
import jax.numpy as jnp
from jax import lax
from jax.experimental import pallas as pl
from jax.experimental.pallas import tpu as pltpu

F32 = jnp.float32
BF16 = jnp.bfloat16
I32 = jnp.int32

HEAD_DIM = 64
LANES = 128
RET_CHUNK = 128
RET_GROUP = 4
ROPE_BASE = 10000.0
DILATED_PATTERNS = ((128, 1), (512, 4), (2048, 16))
SIDE = 64
N_BUCKETS = 32
MAX_DISTANCE = 1024
N_EXPERTS = 16
CAPACITY_FACTOR = 2
MOE_TILE = 256
GATHER_TILES = 8
COMBINE_TILES = 4
MXU_DEPTH = 256
WIN_GROUP = 4
SLOT_WIN = MXU_DEPTH // WIN_GROUP
SLOT_ALIGN = 16
CNT_STRIDE = 32
FFN_BATCH = 2
STEP_CHUNKS = 2
EPS = 1e-6
NEG = -1e30
VMEM_LIMIT = 56 * 1024 * 1024

_NT = (((1,), (1,)), ((), ()))
_TN = (((0,), (0,)), ((), ()))


def _params(*sem):
    return pltpu.CompilerParams(dimension_semantics=sem, vmem_limit_bytes=VMEM_LIMIT)


def _proj_kernel(x_ref, g_ref, w_ref, cos_ref, sin_ref, ret_ref, att_ref, *, n_ret, ret_width, chunk):
    x = x_ref[...]
    ms = jnp.mean(x * x, axis=-1, keepdims=True)
    h = (x * lax.rsqrt(ms + EPS) * g_ref[...]).astype(BF16)
    n_cols = w_ref.shape[1]
    cos, sin = cos_ref[...], sin_ref[...]
    first_half = (lax.broadcasted_iota(I32, cos.shape, 1) % HEAD_DIM) < (HEAD_DIM // 2)
    for c0 in range(0, n_cols, chunk):
        o = jnp.dot(h, w_ref[:, c0:c0 + chunk], preferred_element_type=F32)
        for l0 in range(0, chunk, LANES):
            t = o[:, l0:l0 + LANES]
            col = c0 + l0
            if col >= n_ret:
                att_ref[(col - n_ret) // LANES] = t
            elif col >= 2 * ret_width:
                ret_ref[col // LANES] = t.astype(BF16)
            else:
                scale = 1.0 if col < ret_width else HEAD_DIM ** -0.5
                swapped = jnp.where(first_half, pltpu.roll(t, LANES - HEAD_DIM // 2, 1),
                                    pltpu.roll(t, HEAD_DIM // 2, 1))
                ret_ref[col // LANES] = ((t * cos + swapped * sin) * scale).astype(BF16)


def _proj(x2d, gain, w_bf16, cos_t, sin_t, n_ret, ret_width, tm=1024):
    t, d = x2d.shape
    n_cols = w_bf16.shape[1]
    n_att = n_cols - n_ret
    seq_tiles = cos_t.shape[0] // tm
    assert ret_width % 512 == 0 and n_ret % 512 == 0 and n_cols % 512 == 0
    return pl.pallas_call(
        functools.partial(_proj_kernel, n_ret=n_ret, ret_width=ret_width, chunk=512),
        grid=(t // tm,),
        in_specs=[
            pl.BlockSpec((tm, d), lambda i: (i, 0)),
            pl.BlockSpec((1, d), lambda i: (0, 0)),
            pl.BlockSpec((d, n_cols), lambda i: (0, 0)),
            pl.BlockSpec((tm, LANES), lambda i: (i % seq_tiles, 0)),
            pl.BlockSpec((tm, LANES), lambda i: (i % seq_tiles, 0)),
        ],
        out_specs=[
            pl.BlockSpec((n_ret // LANES, tm, LANES), lambda i: (0, i, 0)),
            pl.BlockSpec((n_att // LANES, tm, LANES), lambda i: (0, i, 0)),
        ],
        out_shape=[
            jax.ShapeDtypeStruct((n_ret // LANES, t, LANES), BF16),
            jax.ShapeDtypeStruct((n_att // LANES, t, LANES), F32),
        ],
        compiler_params=_params("parallel"),
        name="proj",
    )(x2d, gain, w_bf16, cos_t, sin_t)


def _retention_kernel(decay_ref, q_ref, k_ref, v_ref, g_ref, gn_ref, out_ref,
                      rf_ref, rb_ref, kvf_ref, kvb_ref, p0_ref, p1_ref, *, seq):
    c = RET_CHUNK
    n_chunks = seq // c
    p = pl.program_id(1)
    lane = lax.broadcasted_iota(I32, (c, LANES), 1)
    row = lax.broadcasted_iota(I32, (c, LANES), 0)
    head0 = lane < HEAD_DIM
    rowf = row.astype(F32)

    lgf0, lgf1 = decay_ref[0, 2 * p], decay_ref[0, 2 * p + 1]
    lgb0, lgb1 = decay_ref[1, 2 * p], decay_ref[1, 2 * p + 1]
    lgf_lane = jnp.where(head0, lgf0, lgf1)
    lgb_lane = jnp.where(head0, lgb0, lgb1)
    lgf_row = jnp.where(row < HEAD_DIM, lgf0, lgf1)
    lgb_row = jnp.where(row < HEAD_DIM, lgb0, lgb1)
    same_head = (row < HEAD_DIM) == head0

    zeta_f = jnp.exp((c - 1 - rowf) * lgf_lane)
    zeta_b = jnp.exp(rowf * lgb_lane)
    xi_f = jnp.exp((rowf + 1.0) * lgf_lane)
    xi_b = jnp.exp((c - rowf) * lgb_lane)
    gch_f = jnp.where(same_head, jnp.exp(c * lgf_row), 0.0)
    gch_b = jnp.where(same_head, jnp.exp(c * lgb_row), 0.0)

    diff = (row - lane).astype(F32)

    def dmat(lf, lb):
        return jnp.exp(jnp.where(diff >= 0, diff * lf, -diff * lb))

    d_stack = jnp.concatenate([dmat(lgf0, lgb0), dmat(lgf1, lgb1)], axis=0)

    def kv_step(n, carry):
        sl = pl.ds(pl.multiple_of(n * c, c), c)
        k = k_ref[0, sl, :].astype(F32)
        v = v_ref[0, sl, :]
        kvf = lax.dot_general((k * zeta_f).astype(BF16), v, _TN, preferred_element_type=F32)
        kvb = lax.dot_general((k * zeta_b).astype(BF16), v, _TN, preferred_element_type=F32)
        kvf_ref[n] = jnp.where(same_head, kvf, 0.0)
        kvb_ref[n] = jnp.where(same_head, kvb, 0.0)
        return carry

    lax.fori_loop(0, n_chunks, kv_step, 0, unroll=True)

    def scan_step(i, states):
        sf, sb = states
        nb = n_chunks - 1 - i
        rf_ref[i] = sf.astype(BF16)
        rb_ref[nb] = sb.astype(BF16)
        return gch_f * sf + kvf_ref[i], gch_b * sb + kvb_ref[nb]

    zero_state = jnp.zeros((LANES, LANES), F32)
    lax.fori_loop(0, n_chunks, scan_step, (zero_state, zero_state), unroll=True)

    gn_gain = gn_ref[...]

    def score_stage(g, p_ref):
        for jc in range(RET_GROUP):
            sl = pl.ds(pl.multiple_of((g * RET_GROUP + jc) * c, c), c)
            q = q_ref[0, sl, :]
            zero = jnp.zeros_like(q)
            q_stack = jnp.concatenate([jnp.where(head0, q, zero), jnp.where(head0, zero, q)], axis=0)
            scores = lax.dot_general(q_stack, k_ref[0, sl, :], _NT, preferred_element_type=F32) * d_stack
            sb = scores.astype(BF16)
            p_ref[jc] = jnp.concatenate([sb[:c], sb[c:]], axis=1)

    def half_mean(t):
        s0 = jnp.sum(jnp.where(head0, t, 0.0), axis=-1, keepdims=True)
        s1 = jnp.sum(jnp.where(head0, 0.0, t), axis=-1, keepdims=True)
        return jnp.where(head0, s0, s1) * (1.0 / HEAD_DIM)

    def value_stage(g, p_ref):
        for jc in range(RET_GROUP):
            n = g * RET_GROUP + jc
            sl = pl.ds(pl.multiple_of(n * c, c), c)
            q = q_ref[0, sl, :]
            v = v_ref[0, sl, :]
            zero = jnp.zeros_like(v)
            v_stack = jnp.concatenate([jnp.where(head0, v, zero), jnp.where(head0, zero, v)], axis=0)
            inner = jnp.dot(p_ref[jc], v_stack, preferred_element_type=F32)
            cross_f = jnp.dot(q, rf_ref[n], preferred_element_type=F32) * xi_f
            cross_b = jnp.dot(q, rb_ref[n], preferred_element_type=F32) * xi_b
            r = inner + cross_f + cross_b
            dlt = r - half_mean(r)
            var = half_mean(dlt * dlt)
            gate = g_ref[0, sl, :].astype(F32)
            y = dlt * lax.rsqrt(var + EPS) * gn_gain * (gate * jax.nn.sigmoid(gate))
            out_ref[0, sl, :] = y.astype(out_ref.dtype)

    n_groups = n_chunks // RET_GROUP
    assert n_groups % 2 == 0 and n_groups >= 2
    score_stage(0, p0_ref)

    def group_pair(i, carry):
        g = 2 * i
        score_stage(g + 1, p1_ref)
        value_stage(g, p0_ref)
        score_stage(g + 2, p0_ref)
        value_stage(g + 1, p1_ref)
        return carry

    lax.fori_loop(0, n_groups // 2 - 1, group_pair, 0, unroll=True)
    score_stage(n_groups - 1, p1_ref)
    value_stage(n_groups - 2, p0_ref)
    value_stage(n_groups - 1, p1_ref)


def _retention(ret, decay, gn_gain, n_pairs, b):
    s = ret.shape[1] // b
    blk = lambda off: pl.BlockSpec((1, s, LANES), lambda bi, pi: (off + pi, bi, 0))
    return pl.pallas_call(
        functools.partial(_retention_kernel, seq=s),
        grid=(b, n_pairs),
        in_specs=[
            pl.BlockSpec(memory_space=pltpu.SMEM),
            blk(0), blk(n_pairs), blk(2 * n_pairs), blk(3 * n_pairs),
            pl.BlockSpec((1, LANES), lambda bi, pi: (0, pi)),
        ],
        out_specs=pl.BlockSpec((1, s, LANES), lambda bi, pi: (pi, bi, 0)),
        out_shape=jax.ShapeDtypeStruct((n_pairs, b * s, LANES), BF16),
        scratch_shapes=[
            pltpu.VMEM((s // RET_CHUNK, LANES, LANES), BF16),
            pltpu.VMEM((s // RET_CHUNK, LANES, LANES), BF16),
            pltpu.VMEM((s // RET_CHUNK, LANES, LANES), F32),
            pltpu.VMEM((s // RET_CHUNK, LANES, LANES), F32),
            pltpu.VMEM((RET_GROUP, RET_CHUNK, 2 * RET_CHUNK), BF16),
            pltpu.VMEM((RET_GROUP, RET_CHUNK, 2 * RET_CHUNK), BF16),
        ],
        compiler_params=_params("parallel", "parallel"),
        name="retention",
    )(decay, ret, ret, ret, ret, gn_gain)


QBLK = 2 * SIDE
KWIN = 4 * SIDE
BASE_W = 512
ATT_GROUP = 4


def _t5_bucket_np(rel):
    half = N_BUCKETS // 2
    max_exact = half // 2
    bucket = np.where(rel > 0, half, 0)
    n = np.abs(rel)
    nf = np.maximum(n, 1).astype(np.float32)
    large = max_exact + (np.log(nf / np.float32(max_exact)) / np.float32(math.log(MAX_DISTANCE / max_exact))
                         * np.float32(half - max_exact)).astype(np.int32)
    large = np.minimum(large, half - 1)
    return (bucket + np.where(n < max_exact, n, large)).astype(np.int32)


def _bucket_rows():
    k = np.arange(BASE_W)
    off = k - SIDE
    rows = []
    for _, dilation in DILATED_PATTERNS:
        rows.append(np.where(k <= 2 * SIDE, _t5_bucket_np(off * dilation), -1))
    return np.stack(rows).astype(np.int32)


def _dilated_kernel(bias_tab_ref, bucket_ref, qkv_ref, out_ref,
                    bias_ref, perm_ref, stage_ref, o_ref, lse_ref, e0_ref, e1_ref, m0_ref, m1_ref, *, seq):
    order = list(range(len(DILATED_PATTERNS) - 1, -1, -1))
    dil = [DILATED_PATTERNS[pi][1] for pi in order]
    n_slots = len(order)
    assert n_slots == 3 and dil[0] % dil[1] == 0 and dil[1] > 1 and dil[2] == 1
    p = pl.program_id(1)
    lane = lax.broadcasted_iota(I32, (QBLK, LANES), 1)
    head0 = lane < HEAD_DIM
    head0_k = lax.broadcasted_iota(I32, (KWIN, LANES), 1) < HEAD_DIM
    n_groups = seq // QBLK // ATT_GROUP
    n_loop = n_groups // 2 - 1
    assert n_groups % 2 == 0 and n_loop >= 0

    for pi in range(len(DILATED_PATTERNS)):
        bucket = jnp.broadcast_to(bucket_ref[pi:pi + 1, :], (QBLK, BASE_W))
        for hh in range(2):
            base = jnp.full((QBLK, BASE_W), NEG, F32)
            for bk in range(N_BUCKETS):
                base = jnp.where(bucket == bk, bias_tab_ref[bk, 2 * p + hh], base)
            for var, shift in enumerate((BASE_W - SIDE, 0, SIDE)):
                tile = pltpu.roll(base, shift, 1, stride=1, stride_axis=0)
                bias_ref[pi * 3 + var, hh * QBLK:(hh + 1) * QBLK, :] = tile[:, :KWIN]

    def relayout(slot, i):
        d = dil[slot]
        seg_len = seq // d
        scale = jnp.where(i == 0, HEAD_DIM ** -0.5, 1.0).astype(F32)
        if d == 1:
            perm_ref[slot, i] = (qkv_ref[i, 0] * scale).astype(BF16)
        elif slot == 1:
            for r in range(d):
                val = qkv_ref[i, 0, pl.ds(r, seg_len, stride=d), :]
                stage_ref[i, r * seg_len:(r + 1) * seg_len, :] = val
                perm_ref[slot, i, r * seg_len:(r + 1) * seg_len, :] = (val * scale).astype(BF16)
        else:
            prev_d = dil[1]
            step, prev_len = d // prev_d, seq // prev_d
            for r_prev in range(prev_d):
                for r_step in range(step):
                    val = stage_ref[i, pl.ds(r_prev * prev_len + r_step, seg_len, stride=step), :]
                    r = r_prev + prev_d * r_step
                    perm_ref[slot, i, r * seg_len:(r + 1) * seg_len, :] = (val * scale).astype(BF16)

    def placement(slot, t):
        seg_len = seq // dil[slot]
        blocks_per_seg = seg_len // QBLK
        seg = t // blocks_per_seg
        u = t % blocks_per_seg
        q0 = pl.multiple_of(t * QBLK, QBLK)
        seg0 = seg * seg_len
        k0 = pl.multiple_of(jnp.clip(q0 - SIDE, seg0, seg0 + seg_len - KWIN), SIDE)
        var = jnp.where(u == 0, 0, jnp.where(u == blocks_per_seg - 1, 2, 1))
        return seg, u, q0, k0, var

    def logits_stage(slot, g, e_ref, m_ref):
        for jb in range(ATT_GROUP):
            _, _, q0, k0, var = placement(slot, g * ATT_GROUP + jb)
            q = perm_ref[slot, 0, pl.ds(q0, QBLK), :]
            kw = perm_ref[slot, 1, pl.ds(k0, KWIN), :]
            zero = jnp.zeros_like(q)
            q_stack = jnp.concatenate([jnp.where(head0, q, zero), jnp.where(head0, zero, q)], axis=0)
            s = lax.dot_general(q_stack, kw, _NT, preferred_element_type=F32) + bias_ref[order[slot] * 3 + var]
            m = jnp.max(s, axis=-1, keepdims=True)
            e_ref[jb] = jnp.exp(s - m).astype(BF16)
            m_ref[jb] = jnp.where(head0, m[:QBLK], m[QBLK:])

    def value_stage(slot, g, e_ref, m_ref):
        d = dil[slot]
        for jb in range(ATT_GROUP):
            seg, u, q0, k0, _ = placement(slot, g * ATT_GROUP + jb)
            vw = perm_ref[slot, 2, pl.ds(k0, KWIN), :]
            one = jnp.ones_like(vw)
            pv0 = jnp.dot(e_ref[jb, :QBLK, :], jnp.where(head0_k, vw, one), preferred_element_type=F32)
            pv1 = jnp.dot(e_ref[jb, QBLK:, :], jnp.where(head0_k, one, vw), preferred_element_type=F32)
            num = jnp.where(head0, pv0, pv1)
            den = pltpu.roll(jnp.where(head0, pv1, pv0), HEAD_DIM, 1)
            o = num * (1.0 / den)
            lse = m_ref[jb] + jnp.log(den)
            if slot > 0:
                prev_o, prev_lse = o_ref[slot - 1, pl.ds(q0, QBLK), :], lse_ref[slot - 1, pl.ds(q0, QBLK), :]
                mx = jnp.maximum(lse, prev_lse)
                w_new, w_prev = jnp.exp(lse - mx), jnp.exp(prev_lse - mx)
                total = w_new + w_prev
                inv = 1.0 / total
                o = (w_new * inv) * o + (w_prev * inv) * prev_o
                lse = mx + jnp.log(total)
            if slot == n_slots - 1:
                out_ref[0, pl.ds(q0, QBLK), :] = o.astype(out_ref.dtype)
            else:
                d_next = dil[slot + 1]
                step = d // d_next
                start = (seg % d_next) * (seq // d_next) + seg // d_next + step * (u * QBLK)
                o_ref[slot, pl.ds(start, QBLK, stride=step), :] = o
                lse_ref[slot, pl.ds(start, QBLK, stride=step), :] = lse

    for slot in (1, 0, 2):
        for i in range(3):
            relayout(slot, i)
    logits_stage(0, 0, e0_ref, m0_ref)
    for slot in range(n_slots):
        nxt = slot + 1 if slot + 1 < n_slots else None

        def group_pair(i, carry, slot=slot):
            g = 2 * i
            logits_stage(slot, g + 1, e1_ref, m1_ref)
            value_stage(slot, g, e0_ref, m0_ref)
            logits_stage(slot, g + 2, e0_ref, m0_ref)
            value_stage(slot, g + 1, e1_ref, m1_ref)
            return carry

        lax.fori_loop(0, n_loop, group_pair, 0)
        logits_stage(slot, n_groups - 1, e1_ref, m1_ref)
        value_stage(slot, n_groups - 2, e0_ref, m0_ref)
        if nxt is not None:
            logits_stage(nxt, 0, e0_ref, m0_ref)
        value_stage(slot, n_groups - 1, e1_ref, m1_ref)


def _dilated(att, rel_bias, n_pairs, b):
    s = att.shape[1] // b
    n_pat = len(DILATED_PATTERNS)
    bucket_rows = jnp.asarray(_bucket_rows())
    return pl.pallas_call(
        functools.partial(_dilated_kernel, seq=s),
        grid=(b, n_pairs),
        in_specs=[
            pl.BlockSpec(memory_space=pltpu.SMEM),
            pl.BlockSpec((n_pat, BASE_W), lambda bi, pi: (0, 0)),
            pl.BlockSpec((3, 1, s, LANES), lambda bi, pi: (0, pi, bi, 0)),
        ],
        out_specs=pl.BlockSpec((1, s, LANES), lambda bi, pi: (pi, bi, 0)),
        out_shape=jax.ShapeDtypeStruct((n_pairs, b * s, LANES), BF16),
        scratch_shapes=[
            pltpu.VMEM((n_pat * 3, 2 * QBLK, KWIN), F32),
            pltpu.VMEM((n_pat, 3, s, LANES), BF16),
            pltpu.VMEM((3, s, LANES), F32),
            pltpu.VMEM((n_pat - 1, s, LANES), F32),
            pltpu.VMEM((n_pat - 1, s, LANES), F32),
            pltpu.VMEM((ATT_GROUP, 2 * QBLK, KWIN), BF16),
            pltpu.VMEM((ATT_GROUP, 2 * QBLK, KWIN), BF16),
            pltpu.VMEM((ATT_GROUP, QBLK, LANES), F32),
            pltpu.VMEM((ATT_GROUP, QBLK, LANES), F32),
        ],
        compiler_params=_params("parallel", "parallel"),
        name="dilated",
    )(rel_bias, bucket_rows, att.reshape(3, n_pairs, b * s, LANES))


def _outproj_kernel(r_ref, a_ref, x_ref, w_ref, g_ref, wr_ref, x1_ref, h2_ref, aff_ref):
    n_e = aff_ref.shape[1]
    wr = wr_ref[...]
    w_hi = wr.astype(BF16)
    w_lo = (wr - w_hi.astype(F32)).astype(BF16)
    w_parts = jnp.where(lax.broadcasted_iota(I32, wr.shape, 1) < n_e, w_hi, w_lo)
    tm = x_ref.shape[0]
    mixed = jnp.concatenate([r_ref[i] for i in range(r_ref.shape[0])]
                            + [a_ref[i] for i in range(a_ref.shape[0])], axis=1)
    x1 = x_ref[...] + jnp.dot(mixed, w_ref[...], preferred_element_type=F32)
    x1_ref[...] = x1
    ms = jnp.mean(x1 * x1, axis=-1, keepdims=True)
    h2 = x1 * lax.rsqrt(ms + EPS) * g_ref[...]
    h_hi = h2.astype(BF16)
    h2_ref[...] = h_hi
    h_lo = (h2 - h_hi.astype(F32)).astype(BF16)
    prod = jnp.dot(jnp.concatenate([h_hi, h_lo], axis=0), w_parts, preferred_element_type=F32)
    logits = prod[:tm] + (pltpu.roll(prod[:tm], LANES - n_e, 1) + prod[tm:])
    expert_lane = lax.broadcasted_iota(I32, logits.shape, 1) < n_e
    logits = jnp.where(expert_lane, logits, NEG)
    m = jnp.max(logits, axis=1, keepdims=True)
    e = jnp.exp(logits - m)
    aff = e / jnp.sum(e, axis=1, keepdims=True)
    aff_ref[0] = aff.T[:n_e, :]


def _outproj(r, a, x, w_bf16, gain, w_router, tm=1024):
    b, s, d = x.shape
    n_r, n_a = r.shape[0], a.shape[0]
    n_e = w_router.shape[1]
    assert 2 * n_e <= LANES
    w_router = jnp.pad(jnp.concatenate([w_router, w_router], axis=1), ((0, 0), (0, LANES - 2 * n_e)))
    spt = s // tm
    tok = lambda width: pl.BlockSpec((tm, width), lambda i: (i, 0))
    slab = lambda blocks: pl.BlockSpec((blocks, tm, LANES), lambda i: (0, i, 0))
    const = lambda shape: pl.BlockSpec(shape, lambda i: (0, 0))
    return pl.pallas_call(
        _outproj_kernel,
        grid=(b * spt,),
        in_specs=[slab(n_r), slab(n_a), tok(d), const(((n_r + n_a) * LANES, d)), const((1, d)), const((d, LANES))],
        out_specs=[tok(d), tok(d), pl.BlockSpec((1, n_e, tm), lambda i: (i // spt, 0, i % spt))],
        out_shape=[
            jax.ShapeDtypeStruct((b * s, d), F32),
            jax.ShapeDtypeStruct((b * s, d), BF16),
            jax.ShapeDtypeStruct((b, n_e, s), F32),
        ],
        compiler_params=_params("parallel"),
        name="outproj",
    )(r, a, x.reshape(b * s, d), w_bf16, gain, w_router)


def _select_kernel(aff_ref, posm_ref, cnt_ref, *, cap):
    rows, s = aff_ref.shape
    bits = pltpu.bitcast(aff_ref[...], I32)

    def search(i, thr):
        cand = thr | jnp.left_shift(jnp.int32(1), 30 - i)
        cnt = jnp.sum((bits >= cand).astype(I32), axis=1, keepdims=True)
        return jnp.where(cnt >= cap, cand, thr)

    thr = lax.fori_loop(0, 31, search, jnp.zeros((rows, 1), I32))
    gt = bits > thr
    eq = bits == thr
    need = cap - jnp.sum(gt.astype(I32), axis=1, keepdims=True)

    w = LANES
    tri = (lax.broadcasted_iota(I32, (w, w), 0) <= lax.broadcasted_iota(I32, (w, w), 1)).astype(BF16)

    def excl_prefix(flags):
        carry = jnp.zeros((rows, 1), F32)
        out = []
        for c0 in range(0, s, w):
            f = flags[:, c0:c0 + w].astype(BF16)
            inc = jnp.dot(f, tri, preferred_element_type=F32)
            out.append(inc - f.astype(F32) + carry)
            carry = carry + inc[:, w - 1:w]
        return jnp.concatenate(out, axis=1).astype(I32)

    sel = gt | (eq & (excl_prefix(eq) < need))
    posm_ref[...] = jnp.where(sel, excl_prefix(sel), -1)

    tok = lax.broadcasted_iota(I32, (w, w), 0)
    edge = lax.broadcasted_iota(I32, (w, w), 1) * MOE_TILE
    cnt = jnp.zeros((rows, w), F32)
    for c0 in range(0, s, w):
        before = ((tok + c0) < edge).astype(BF16)
        cnt = cnt + jnp.dot(sel[:, c0:c0 + w].astype(BF16), before, preferred_element_type=F32)
    cnt_ref[...] = cnt.astype(I32)


def _select(aff_rows, cap):
    rows, s = aff_rows.shape
    return pl.pallas_call(
        functools.partial(_select_kernel, cap=cap),
        out_shape=[jax.ShapeDtypeStruct((rows, s), I32), jax.ShapeDtypeStruct((rows, LANES), I32)],
        compiler_params=pltpu.CompilerParams(vmem_limit_bytes=VMEM_LIMIT),
        name="select",
    )(aff_rows)


def _slot_window(cnt_ref, row, j, cap):
    lo, hi = cnt_ref[row * CNT_STRIDE + j], cnt_ref[row * CNT_STRIDE + j + 1]
    w0 = pl.multiple_of(jnp.minimum((lo // SLOT_ALIGN) * SLOT_ALIGN, cap - SLOT_WIN), SLOT_ALIGN)
    return hi, w0


def _any_overflow(windows):
    return functools.reduce(jnp.logical_or, [hi > w0 + SLOT_WIN for hi, w0 in windows])


def _gather_kernel(cnt_ref, posm_ref, h2_ref, xg_ref):
    bi, j = pl.program_id(0), pl.program_id(1)
    n_e, cap = xg_ref.shape[1:3]
    tt = MOE_TILE

    @pl.when(j == 0)
    def _():
        xg_ref[...] = jnp.zeros_like(xg_ref)

    slot0 = lax.broadcasted_iota(I32, (SLOT_WIN, tt), 0)
    for sub in range(h2_ref.shape[1] // tt):
        tile = j * (h2_ref.shape[1] // tt) + sub
        tok = slice(sub * tt, (sub + 1) * tt)
        windows = [_slot_window(cnt_ref, bi * n_e + e, tile, cap) for e in range(n_e)]
        for e in range(0, n_e, WIN_GROUP):
            onehot = jnp.concatenate(
                [(posm_ref[0, e + i:e + i + 1, tok] == slot0 + windows[e + i][1]).astype(BF16)
                 for i in range(WIN_GROUP)], axis=0)
            rows = jnp.dot(onehot, h2_ref[0, tok, :], preferred_element_type=F32).astype(BF16)
            for i in range(WIN_GROUP):
                xg_ref[0, e + i, pl.ds(windows[e + i][1], SLOT_WIN), :] += rows[i * SLOT_WIN:(i + 1) * SLOT_WIN]

        @pl.when(_any_overflow(windows))
        def _(tile=tile, tok=tok):
            wide = lax.broadcasted_iota(I32, (MXU_DEPTH, tt), 0)

            def one_expert(e, carry):
                hi, w0 = _slot_window(cnt_ref, bi * n_e + e, tile, cap)

                @pl.when(hi > w0 + SLOT_WIN)
                def _():
                    pos = posm_ref[0, pl.ds(e, 1), tok]
                    for ws in range(0, cap, MXU_DEPTH):
                        slot = wide + ws
                        onehot = ((pos == slot) & (slot >= w0 + SLOT_WIN)).astype(BF16)
                        extra = jnp.dot(onehot, h2_ref[0, tok, :], preferred_element_type=F32)
                        xg_ref[0, e, ws:ws + MXU_DEPTH, :] += extra.astype(BF16)

                return carry

            lax.fori_loop(0, n_e, one_expert, 0)


def _gather(cnt, posm, h2, cap):
    b, n_e, s = posm.shape
    d = h2.shape[2]
    step = MOE_TILE * GATHER_TILES
    return pl.pallas_call(
        _gather_kernel,
        grid_spec=pltpu.PrefetchScalarGridSpec(
            num_scalar_prefetch=1,
            grid=(b, s // step),
            in_specs=[
                pl.BlockSpec((1, n_e, step), lambda bi, j, cnt: (bi, 0, j)),
                pl.BlockSpec((1, step, d), lambda bi, j, cnt: (bi, j, 0)),
            ],
            out_specs=pl.BlockSpec((1, n_e, cap, d), lambda bi, j, cnt: (bi, 0, 0, 0)),
        ),
        out_shape=jax.ShapeDtypeStruct((b, n_e, cap, d), BF16),
        compiler_params=_params("parallel", "arbitrary"),
        name="gather",
    )(cnt, posm, h2)


def _ffn_kernel(xg_ref, wg_hbm, wu_hbm, wd_hbm, y_ref, wg_buf, wu_buf, wd_buf, wg_stage, wu_stage, wd_stage,
                sem, acc_ref, *, tf):
    e, bi = pl.program_id(0), pl.program_id(1)
    n_e, n_steps = pl.num_programs(0), pl.num_programs(1)
    rows_in, rows_dn = wg_stage.shape[1], wd_stage.shape[1]
    f_total = wg_buf.shape[2]
    n_chunks = wg_buf.shape[1] // rows_in

    def chunk_copies(expert, c):
        st = c % STEP_CHUNKS
        return (
            pltpu.make_async_copy(wg_hbm.at[expert, pl.ds(c * rows_in, rows_in), :], wg_stage.at[st], sem.at[0, st]),
            pltpu.make_async_copy(wu_hbm.at[expert, pl.ds(c * rows_in, rows_in), :], wu_stage.at[st], sem.at[1, st]),
            pltpu.make_async_copy(wd_hbm.at[expert, pl.ds(c * rows_dn, rows_dn), :], wd_stage.at[st], sem.at[2, st]),
        )

    def finish_chunk(expert, c):
        for cp in chunk_copies(expert, c):
            cp.wait()
        slot, st = expert % 2, c % STEP_CHUNKS
        wg_buf[slot, pl.ds(c * rows_in, rows_in), :] = wg_stage[st].astype(BF16)
        wu_buf[slot, pl.ds(c * rows_in, rows_in), :] = wu_stage[st].astype(BF16)
        wd_buf[slot, pl.ds(c * rows_dn, rows_dn), :] = wd_stage[st].astype(BF16)

    @pl.when((e == 0) & (bi == 0))
    def _():
        for cp in chunk_copies(0, 0):
            cp.start()

        def load_first(c, carry):
            @pl.when(c + 1 < n_chunks)
            def _():
                for cp in chunk_copies(0, c + 1):
                    cp.start()

            finish_chunk(0, c)
            return carry

        lax.fori_loop(0, n_chunks, load_first, 0)

    pending = jnp.where(bi > 0, e + 1 < n_e, e > 0)

    @pl.when(pending)
    def _():
        for k in range(STEP_CHUNKS):
            finish_chunk(jnp.where(bi > 0, e + 1, e),
                         jnp.where(bi > 0, bi - 1, n_steps - 1) * STEP_CHUNKS + k)

    @pl.when(e + 1 < n_e)
    def _():
        for k in range(STEP_CHUNKS):
            for cp in chunk_copies(e + 1, bi * STEP_CHUNKS + k):
                cp.start()

    def swiglu(slot):
        n_rows, cap, d = xg_ref.shape[0], xg_ref.shape[2], xg_ref.shape[3]
        xg = xg_ref[:, 0].reshape(n_rows * cap, d)
        for fi in range(f_total // tf):
            fs = slice(fi * tf, (fi + 1) * tf)
            gate = jnp.dot(xg, wg_buf[slot, :, fs], preferred_element_type=F32)
            up = jnp.dot(xg, wu_buf[slot, :, fs], preferred_element_type=F32)
            hid = ((gate * jax.nn.sigmoid(gate)) * up).astype(BF16)
            part = jnp.dot(hid, wd_buf[slot, fs, :], preferred_element_type=F32)
            if fi == 0:
                acc_ref[...] = part
            else:
                acc_ref[...] += part
        y_ref[:, 0] = acc_ref[...].reshape(n_rows, cap, d).astype(y_ref.dtype)

    for slot in range(2):
        pl.when(e % 2 == slot)(functools.partial(swiglu, slot))


def _ffn(xg, wg, wu, wd, tf=MXU_DEPTH):
    b, n_e, cap, d = xg.shape
    f = wg.shape[2]
    assert b % FFN_BATCH == 0
    n_steps = b // FFN_BATCH
    n_chunks = n_steps * STEP_CHUNKS
    assert d % n_chunks == 0 and f % n_chunks == 0 and (d // n_chunks) % SLOT_ALIGN == 0
    tok = pl.BlockSpec((FFN_BATCH, 1, cap, d), lambda e, bi: (bi, e, 0, 0))
    hbm = pl.BlockSpec(memory_space=pl.ANY)
    return pl.pallas_call(
        functools.partial(_ffn_kernel, tf=tf),
        grid=(n_e, n_steps),
        in_specs=[tok, hbm, hbm, hbm],
        out_specs=tok,
        out_shape=jax.ShapeDtypeStruct((b, n_e, cap, d), BF16),
        scratch_shapes=[
            pltpu.VMEM((2, d, f), BF16), pltpu.VMEM((2, d, f), BF16), pltpu.VMEM((2, f, d), BF16),
            pltpu.VMEM((STEP_CHUNKS, d // n_chunks, f), F32), pltpu.VMEM((STEP_CHUNKS, d // n_chunks, f), F32),
            pltpu.VMEM((STEP_CHUNKS, f // n_chunks, d), F32),
            pltpu.SemaphoreType.DMA((3, STEP_CHUNKS)),
            pltpu.VMEM((FFN_BATCH * cap, d), F32),
        ],
        compiler_params=_params("arbitrary", "arbitrary"),
        name="ffn",
    )(xg, wg, wu, wd)


def _combine_kernel(cnt_ref, pos_ref, gate_ref, y_ref, x1_ref, g_ref, out_ref, acc_ref):
    bi, j = pl.program_id(0), pl.program_id(1)
    tt = MOE_TILE
    n_e, cap = y_ref.shape[1:3]
    slot0 = lax.broadcasted_iota(I32, (SLOT_WIN, tt), 0)
    for sub in range(x1_ref.shape[1] // tt):
        tile = j * (x1_ref.shape[1] // tt) + sub
        tok = slice(sub * tt, (sub + 1) * tt)
        windows = [_slot_window(cnt_ref, bi * n_e + e, tile, cap) for e in range(n_e)]
        acc = x1_ref[0, tok, :]
        for e in range(0, n_e, WIN_GROUP):
            scatter_t = jnp.concatenate(
                [jnp.where(pos_ref[0, e + i:e + i + 1, tok] == slot0 + windows[e + i][1],
                           gate_ref[0, e + i:e + i + 1, tok], 0.0).astype(BF16) for i in range(WIN_GROUP)], axis=0)
            y_group = jnp.concatenate(
                [y_ref[0, e + i, pl.ds(windows[e + i][1], SLOT_WIN), :] for i in range(WIN_GROUP)], axis=0)
            acc = acc + lax.dot_general(scatter_t, y_group, _TN, preferred_element_type=F32)
        acc_ref[...] = acc

        @pl.when(_any_overflow(windows))
        def _(tile=tile, tok=tok):
            wide = lax.broadcasted_iota(I32, (MXU_DEPTH, tt), 0)

            def one_expert(e, carry):
                hi, w0 = _slot_window(cnt_ref, bi * n_e + e, tile, cap)

                @pl.when(hi > w0 + SLOT_WIN)
                def _():
                    pos = pos_ref[0, pl.ds(e, 1), tok]
                    gate = gate_ref[0, pl.ds(e, 1), tok]
                    for ws in range(0, cap, MXU_DEPTH):
                        slot = wide + ws
                        extra = jnp.where((pos == slot) & (slot >= w0 + SLOT_WIN), gate, 0.0).astype(BF16)
                        acc_ref[...] += lax.dot_general(extra, y_ref[0, e, ws:ws + MXU_DEPTH, :], _TN,
                                                        preferred_element_type=F32)

                return carry

            lax.fori_loop(0, n_e, one_expert, 0)

        acc = acc_ref[...]
        ms = jnp.mean(acc * acc, axis=-1, keepdims=True)
        out_ref[0, tok, :] = acc * lax.rsqrt(ms + EPS) * g_ref[...]


def _combine(cnt, posm, aff, y, x1, gain):
    b, s, d = x1.shape
    n_e, cap = y.shape[1:3]
    step = MOE_TILE * COMBINE_TILES
    return pl.pallas_call(
        _combine_kernel,
        grid_spec=pltpu.PrefetchScalarGridSpec(
            num_scalar_prefetch=1,
            grid=(b, s // step),
            in_specs=[
                pl.BlockSpec((1, n_e, step), lambda bi, j, cnt: (bi, 0, j)),
                pl.BlockSpec((1, n_e, step), lambda bi, j, cnt: (bi, 0, j)),
                pl.BlockSpec((1, n_e, cap, d), lambda bi, j, cnt: (bi, 0, 0, 0)),
                pl.BlockSpec((1, step, d), lambda bi, j, cnt: (bi, j, 0)),
                pl.BlockSpec((1, d), lambda bi, j, cnt: (0, 0)),
            ],
            out_specs=pl.BlockSpec((1, step, d), lambda bi, j, cnt: (bi, j, 0)),
            scratch_shapes=[pltpu.VMEM((MOE_TILE, d), F32)],
        ),
        out_shape=jax.ShapeDtypeStruct((b, s, d), F32),
        compiler_params=_params("parallel", "parallel"),
        name="combine",
    )(cnt, posm, aff, y, x1, gain)


def _rope_tables(seq):
    half = HEAD_DIM // 2
    pos = jnp.arange(seq, dtype=F32)
    inv = ROPE_BASE ** (-jnp.arange(0, HEAD_DIM, 2, dtype=F32) / HEAD_DIM)
    ang = pos[:, None] * inv[None, :]
    cos, sin = jnp.cos(ang), jnp.sin(ang)
    reps = LANES // HEAD_DIM
    cos_t = jnp.tile(jnp.concatenate([cos, cos], axis=1), (1, reps))
    sin_t = jnp.tile(jnp.concatenate([-sin, sin], axis=1), (1, reps))
    return cos_t, sin_t


def kernel(x, norm1_gain, w_in, ret_log_decay, ret_gn_gain, rel_bias, w_out, norm2_gain, w_router,
           w_gate, w_up, w_down, final_gain):
    b, s, d = x.shape
    depth = w_in.shape[0]
    ret_width = ret_gn_gain.shape[1]
    n_ret_pairs = ret_width // LANES
    att_width = (w_in.shape[2] - 4 * ret_width) // 3
    n_att_pairs = att_width // LANES
    cap = CAPACITY_FACTOR * s // N_EXPERTS
    cos_t, sin_t = _rope_tables(s)

    assert depth == 1, "single-layer block: the final norm is fused into the combine kernel"
    layer = 0
    ret, att = _proj(x.reshape(b * s, d), norm1_gain[layer][None, :], w_in[layer].astype(BF16),
                     cos_t, sin_t, n_ret=4 * ret_width, ret_width=ret_width)
    r = _retention(ret, ret_log_decay[layer], ret_gn_gain[layer][None, :], n_ret_pairs, b)
    a = _dilated(att, rel_bias, n_att_pairs, b)
    x1, h2, aff = _outproj(r, a, x, w_out[layer].astype(BF16), norm2_gain[layer][None, :],
                           w_router[layer])
    assert s % (MOE_TILE * max(GATHER_TILES, COMBINE_TILES)) == 0 and s // MOE_TILE < CNT_STRIDE and cap % MXU_DEPTH == 0 and N_EXPERTS % WIN_GROUP == 0
    posm, cnt = _select(aff.reshape(b * N_EXPERTS, s), cap)
    posm = posm.reshape(b, N_EXPERTS, s)
    cnt = cnt[:, :CNT_STRIDE].reshape(-1)
    xg = _gather(cnt, posm, h2.reshape(b, s, d), cap)
    y = _ffn(xg, w_gate[layer], w_up[layer], w_down[layer])
    return _combine(cnt, posm, aff, y, x1.reshape(b, s, d),
                    final_gain[None, :])
```

```python
import functools
import math

import numpy as np
import jax
import jax.numpy as jnp
from jax import lax
from jax.experimental import pallas as pl
from jax.experimental.pallas import tpu as pltpu

F32 = jnp.float32
BF16 = jnp.bfloat16
I32 = jnp.int32

HEAD_DIM = 64
LANES = 128
RET_CHUNK = 128
RET_GROUP = 1
ROPE_BASE = 10000.0
DILATED_PATTERNS = ((128, 1), (512, 4), (2048, 16))
SIDE = 64
N_BUCKETS = 32
MAX_DISTANCE = 1024
N_EXPERTS = 16
CAPACITY_FACTOR = 2
MOE_TILE = 256
GATHER_TILES = 8
COMBINE_TILES = 4
MXU_DEPTH = 256
WIN_GROUP = 4
SLOT_WIN = MXU_DEPTH // WIN_GROUP
SLOT_ALIGN = 16
CNT_STRIDE = 32
FFN_BATCH = 2
STEP_CHUNKS = 2
EPS = 1e-6
NEG = -1e30
VMEM_LIMIT = 56 * 1024 * 1024

_NT = (((1,), (1,)), ((), ()))
_TN = (((0,), (0,)), ((), ()))


def _params(*sem):
    return pltpu.CompilerParams(dimension_semantics=sem, vmem_limit_bytes=VMEM_LIMIT)


def _proj_kernel(x_ref, g_ref, w_ref, cos_ref, sin_ref, ret_ref, att_ref, *, n_ret, ret_width, chunk):
    x = x_ref[...]
    ms = jnp.mean(x * x, axis=-1, keepdims=True)
    h = (x * lax.rsqrt(ms + EPS) * g_ref[...]).astype(BF16)
    n_cols = w_ref.shape[1]
    cos, sin = cos_ref[...], sin_ref[...]
    first_half = (lax.broadcasted_iota(I32, cos.shape, 1) % HEAD_DIM) < (HEAD_DIM // 2)
    for c0 in range(0, n_cols, chunk):
        o = jnp.dot(h, w_ref[:, c0:c0 + chunk], preferred_element_type=F32)
        for l0 in range(0, chunk, LANES):
            t = o[:, l0:l0 + LANES]
            col = c0 + l0
            if col >= n_ret:
                att_ref[(col - n_ret) // LANES] = t
            elif col >= 2 * ret_width:
                ret_ref[col // LANES] = t.astype(BF16)
            else:
                scale = 1.0 if col < ret_width else HEAD_DIM ** -0.5
                swapped = jnp.where(first_half, pltpu.roll(t, LANES - HEAD_DIM // 2, 1),
                                    pltpu.roll(t, HEAD_DIM // 2, 1))
                ret_ref[col // LANES] = ((t * cos + swapped * sin) * scale).astype(BF16)


def _proj(x2d, gain, w_bf16, cos_t, sin_t, n_ret, ret_width, tm=1024):
    t, d = x2d.shape
    n_cols = w_bf16.shape[1]
    n_att = n_cols - n_ret
    seq_tiles = cos_t.shape[0] // tm
    assert ret_width % 512 == 0 and n_ret % 512 == 0 and n_cols % 512 == 0
    return pl.pallas_call(
        functools.partial(_proj_kernel, n_ret=n_ret, ret_width=ret_width, chunk=512),
        grid=(t // tm,),
        in_specs=[
            pl.BlockSpec((tm, d), lambda i: (i, 0)),
            pl.BlockSpec((1, d), lambda i: (0, 0)),
            pl.BlockSpec((d, n_cols), lambda i: (0, 0)),
            pl.BlockSpec((tm, LANES), lambda i: (i % seq_tiles, 0)),
            pl.BlockSpec((tm, LANES), lambda i: (i % seq_tiles, 0)),
        ],
        out_specs=[
            pl.BlockSpec((n_ret // LANES, tm, LANES), lambda i: (0, i, 0)),
            pl.BlockSpec((n_att // LANES, tm, LANES), lambda i: (0, i, 0)),
        ],
        out_shape=[
            jax.ShapeDtypeStruct((n_ret // LANES, t, LANES), BF16),
            jax.ShapeDtypeStruct((n_att // LANES, t, LANES), F32),
        ],
        compiler_params=_params("parallel"),
        name="proj",
    )(x2d, gain, w_bf16, cos_t, sin_t)


def _retention_kernel(decay_ref, q_ref, k_ref, v_ref, g_ref, gn_ref, out_ref,
                      rf_ref, rb_ref, kvf_ref, kvb_ref, p0_ref, p1_ref, *, seq):
    c = RET_CHUNK
    n_chunks = seq // c
    p = pl.program_id(1)
    lane = lax.broadcasted_iota(I32, (c, LANES), 1)
    row = lax.broadcasted_iota(I32, (c, LANES), 0)
    head0 = lane < HEAD_DIM
    rowf = row.astype(F32)

    lgf0, lgf1 = decay_ref[0, 2 * p], decay_ref[0, 2 * p + 1]
    lgb0, lgb1 = decay_ref[1, 2 * p], decay_ref[1, 2 * p + 1]
    lgf_lane = jnp.where(head0, lgf0, lgf1)
    lgb_lane = jnp.where(head0, lgb0, lgb1)
    lgf_row = jnp.where(row < HEAD_DIM, lgf0, lgf1)
    lgb_row = jnp.where(row < HEAD_DIM, lgb0, lgb1)
    same_head = (row < HEAD_DIM) == head0

    zeta_f = jnp.exp((c - 1 - rowf) * lgf_lane)
    zeta_b = jnp.exp(rowf * lgb_lane)
    xi_f = jnp.exp((rowf + 1.0) * lgf_lane)
    xi_b = jnp.exp((c - rowf) * lgb_lane)
    gch_f = jnp.where(same_head, jnp.exp(c * lgf_row), 0.0)
    gch_b = jnp.where(same_head, jnp.exp(c * lgb_row), 0.0)

    diff = (row - lane).astype(F32)

    def dmat(lf, lb):
        return jnp.exp(jnp.where(diff >= 0, diff * lf, -diff * lb))

    d_stack = jnp.concatenate([dmat(lgf0, lgb0), dmat(lgf1, lgb1)], axis=0)

    def kv_step(n, carry):
        sl = pl.ds(pl.multiple_of(n * c, c), c)
        k = k_ref[0, sl, :].astype(F32)
        v = v_ref[0, sl, :]
        kvf = lax.dot_general((k * zeta_f).astype(BF16), v, _TN, preferred_element_type=F32)
        kvb = lax.dot_general((k * zeta_b).astype(BF16), v, _TN, preferred_element_type=F32)
        kvf_ref[n] = jnp.where(same_head, kvf, 0.0)
        kvb_ref[n] = jnp.where(same_head, kvb, 0.0)
        return carry

    lax.fori_loop(0, n_chunks, kv_step, 0, unroll=True)

    def scan_step(i, states):
        sf, sb = states
        nb = n_chunks - 1 - i
        rf_ref[i] = sf.astype(BF16)
        rb_ref[nb] = sb.astype(BF16)
        return gch_f * sf + kvf_ref[i], gch_b * sb + kvb_ref[nb]

    zero_state = jnp.zeros((LANES, LANES), F32)
    lax.fori_loop(0, n_chunks, scan_step, (zero_state, zero_state), unroll=True)

    gn_gain = gn_ref[...]

    def score_stage(g, p_ref):
        for jc in range(RET_GROUP):
            sl = pl.ds(pl.multiple_of((g * RET_GROUP + jc) * c, c), c)
            q = q_ref[0, sl, :]
            zero = jnp.zeros_like(q)
            q_stack = jnp.concatenate([jnp.where(head0, q, zero), jnp.where(head0, zero, q)], axis=0)
            scores = lax.dot_general(q_stack, k_ref[0, sl, :], _NT, preferred_element_type=F32) * d_stack
            sb = scores.astype(BF16)
            p_ref[jc] = jnp.concatenate([sb[:c], sb[c:]], axis=1)

    def half_mean(t):
        s0 = jnp.sum(jnp.where(head0, t, 0.0), axis=-1, keepdims=True)
        s1 = jnp.sum(jnp.where(head0, 0.0, t), axis=-1, keepdims=True)
        return jnp.where(head0, s0, s1) * (1.0 / HEAD_DIM)

    def value_stage(g, p_ref):
        for jc in range(RET_GROUP):
            n = g * RET_GROUP + jc
            sl = pl.ds(pl.multiple_of(n * c, c), c)
            q = q_ref[0, sl, :]
            v = v_ref[0, sl, :]
            zero = jnp.zeros_like(v)
            v_stack = jnp.concatenate([jnp.where(head0, v, zero), jnp.where(head0, zero, v)], axis=0)
            inner = jnp.dot(p_ref[jc], v_stack, preferred_element_type=F32)
            cross_f = jnp.dot(q, rf_ref[n], preferred_element_type=F32) * xi_f
            cross_b = jnp.dot(q, rb_ref[n], preferred_element_type=F32) * xi_b
            r = inner + cross_f + cross_b
            dlt = r - half_mean(r)
            var = half_mean(dlt * dlt)
            gate = g_ref[0, sl, :].astype(F32)
            y = dlt * lax.rsqrt(var + EPS) * gn_gain * (gate * jax.nn.sigmoid(gate))
            out_ref[0, sl, :] = y.astype(out_ref.dtype)

    n_groups = n_chunks // RET_GROUP
    assert n_groups % 2 == 0 and n_groups >= 2
    score_stage(0, p0_ref)

    def group_pair(i, carry):
        g = 2 * i
        score_stage(g + 1, p1_ref)
        value_stage(g, p0_ref)
        score_stage(g + 2, p0_ref)
        value_stage(g + 1, p1_ref)
        return carry

    lax.fori_loop(0, n_groups // 2 - 1, group_pair, 0, unroll=True)
    score_stage(n_groups - 1, p1_ref)
    value_stage(n_groups - 2, p0_ref)
    value_stage(n_groups - 1, p1_ref)


def _retention(ret, decay, gn_gain, n_pairs, b):
    s = ret.shape[1] // b
    blk = lambda off: pl.BlockSpec((1, s, LANES), lambda bi, pi: (off + pi, bi, 0))
    return pl.pallas_call(
        functools.partial(_retention_kernel, seq=s),
        grid=(b, n_pairs),
        in_specs=[
            pl.BlockSpec(memory_space=pltpu.SMEM),
            blk(0), blk(n_pairs), blk(2 * n_pairs), blk(3 * n_pairs),
            pl.BlockSpec((1, LANES), lambda bi, pi: (0, pi)),
        ],
        out_specs=pl.BlockSpec((1, s, LANES), lambda bi, pi: (pi, bi, 0)),
        out_shape=jax.ShapeDtypeStruct((n_pairs, b * s, LANES), BF16),
        scratch_shapes=[
            pltpu.VMEM((s // RET_CHUNK, LANES, LANES), BF16),
            pltpu.VMEM((s // RET_CHUNK, LANES, LANES), BF16),
            pltpu.VMEM((s // RET_CHUNK, LANES, LANES), F32),
            pltpu.VMEM((s // RET_CHUNK, LANES, LANES), F32),
            pltpu.VMEM((RET_GROUP, RET_CHUNK, 2 * RET_CHUNK), BF16),
            pltpu.VMEM((RET_GROUP, RET_CHUNK, 2 * RET_CHUNK), BF16),
        ],
        compiler_params=_params("parallel", "parallel"),
        name="retention",
    )(decay, ret, ret, ret, ret, gn_gain)


QBLK = 2 * SIDE
KWIN = 4 * SIDE
BASE_W = 512
ATT_GROUP = 4


def _t5_bucket_np(rel):
    half = N_BUCKETS // 2
    max_exact = half // 2
    bucket = np.where(rel > 0, half, 0)
    n = np.abs(rel)
    nf = np.maximum(n, 1).astype(np.float32)
    large = max_exact + (np.log(nf / np.float32(max_exact)) / np.float32(math.log(MAX_DISTANCE / max_exact))
                         * np.float32(half - max_exact)).astype(np.int32)
    large = np.minimum(large, half - 1)
    return (bucket + np.where(n < max_exact, n, large)).astype(np.int32)


def _bucket_rows():
    k = np.arange(BASE_W)
    off = k - SIDE
    rows = []
    for _, dilation in DILATED_PATTERNS:
        rows.append(np.where(k <= 2 * SIDE, _t5_bucket_np(off * dilation), -1))
    return np.stack(rows).astype(np.int32)


def _dilated_kernel(bias_tab_ref, bucket_ref, qkv_ref, out_ref,
                    bias_ref, perm_ref, stage_ref, o_ref, lse_ref, e0_ref, e1_ref, m0_ref, m1_ref, *, seq):
    order = list(range(len(DILATED_PATTERNS) - 1, -1, -1))
    dil = [DILATED_PATTERNS[pi][1] for pi in order]
    n_slots = len(order)
    assert n_slots == 3 and dil[0] % dil[1] == 0 and dil[1] > 1 and dil[2] == 1
    p = pl.program_id(1)
    lane = lax.broadcasted_iota(I32, (QBLK, LANES), 1)
    head0 = lane < HEAD_DIM
    head0_k = lax.broadcasted_iota(I32, (KWIN, LANES), 1) < HEAD_DIM
    n_groups = seq // QBLK // ATT_GROUP
    n_loop = n_groups // 2 - 1
    assert n_groups % 2 == 0 and n_loop >= 0

    for pi in range(len(DILATED_PATTERNS)):
        bucket = jnp.broadcast_to(bucket_ref[pi:pi + 1, :], (QBLK, BASE_W))
        for hh in range(2):
            base = jnp.full((QBLK, BASE_W), NEG, F32)
            for bk in range(N_BUCKETS):
                base = jnp.where(bucket == bk, bias_tab_ref[bk, 2 * p + hh], base)
            for var, shift in enumerate((BASE_W - SIDE, 0, SIDE)):
                tile = pltpu.roll(base, shift, 1, stride=1, stride_axis=0)
                bias_ref[pi * 3 + var, hh * QBLK:(hh + 1) * QBLK, :] = tile[:, :KWIN]

    def relayout(slot, i):
        d = dil[slot]
        seg_len = seq // d
        scale = jnp.where(i == 0, HEAD_DIM ** -0.5, 1.0).astype(F32)
        if d == 1:
            perm_ref[slot, i] = (qkv_ref[i, 0] * scale).astype(BF16)
        elif slot == 1:
            for r in range(d):
                val = qkv_ref[i, 0, pl.ds(r, seg_len, stride=d), :]
                stage_ref[i, r * seg_len:(r + 1) * seg_len, :] = val
                perm_ref[slot, i, r * seg_len:(r + 1) * seg_len, :] = (val * scale).astype(BF16)
        else:
            prev_d = dil[1]
            step, prev_len = d // prev_d, seq // prev_d
            for r_prev in range(prev_d):
                for r_step in range(step):
                    val = stage_ref[i, pl.ds(r_prev * prev_len + r_step, seg_len, stride=step), :]
                    r = r_prev + prev_d * r_step
                    perm_ref[slot, i, r * seg_len:(r + 1) * seg_len, :] = (val * scale).astype(BF16)

    def placement(slot, t):
        seg_len = seq // dil[slot]
        blocks_per_seg = seg_len // QBLK
        seg = t // blocks_per_seg
        u = t % blocks_per_seg
        q0 = pl.multiple_of(t * QBLK, QBLK)
        seg0 = seg * seg_len
        k0 = pl.multiple_of(jnp.clip(q0 - SIDE, seg0, seg0 + seg_len - KWIN), SIDE)
        var = jnp.where(u == 0, 0, jnp.where(u == blocks_per_seg - 1, 2, 1))
        return seg, u, q0, k0, var

    def logits_stage(slot, g, e_ref, m_ref):
        for jb in range(ATT_GROUP):
            _, _, q0, k0, var = placement(slot, g * ATT_GROUP + jb)
            q = perm_ref[slot, 0, pl.ds(q0, QBLK), :]
            kw = perm_ref[slot, 1, pl.ds(k0, KWIN), :]
            zero = jnp.zeros_like(q)
            q_stack = jnp.concatenate([jnp.where(head0, q, zero), jnp.where(head0, zero, q)], axis=0)
            s = lax.dot_general(q_stack, kw, _NT, preferred_element_type=F32) + bias_ref[order[slot] * 3 + var]
            m = jnp.max(s, axis=-1, keepdims=True)
            e_ref[jb] = jnp.exp(s - m).astype(BF16)
            m_ref[jb] = jnp.where(head0, m[:QBLK], m[QBLK:])

    def value_stage(slot, g, e_ref, m_ref):
        d = dil[slot]
        for jb in range(ATT_GROUP):
            seg, u, q0, k0, _ = placement(slot, g * ATT_GROUP + jb)
            vw = perm_ref[slot, 2, pl.ds(k0, KWIN), :]
            one = jnp.ones_like(vw)
            pv0 = jnp.dot(e_ref[jb, :QBLK, :], jnp.where(head0_k, vw, one), preferred_element_type=F32)
            pv1 = jnp.dot(e_ref[jb, QBLK:, :], jnp.where(head0_k, one, vw), preferred_element_type=F32)
            num = jnp.where(head0, pv0, pv1)
            den = pltpu.roll(jnp.where(head0, pv1, pv0), HEAD_DIM, 1)
            o = num * (1.0 / den)
            lse = m_ref[jb] + jnp.log(den)
            if slot > 0:
                prev_o, prev_lse = o_ref[slot - 1, pl.ds(q0, QBLK), :], lse_ref[slot - 1, pl.ds(q0, QBLK), :]
                mx = jnp.maximum(lse, prev_lse)
                w_new, w_prev = jnp.exp(lse - mx), jnp.exp(prev_lse - mx)
                total = w_new + w_prev
                inv = 1.0 / total
                o = (w_new * inv) * o + (w_prev * inv) * prev_o
                lse = mx + jnp.log(total)
            if slot == n_slots - 1:
                out_ref[0, pl.ds(q0, QBLK), :] = o.astype(out_ref.dtype)
            else:
                d_next = dil[slot + 1]
                step = d // d_next
                start = (seg % d_next) * (seq // d_next) + seg // d_next + step * (u * QBLK)
                o_ref[slot, pl.ds(start, QBLK, stride=step), :] = o
                lse_ref[slot, pl.ds(start, QBLK, stride=step), :] = lse

    for slot in (1, 0, 2):
        for i in range(3):
            relayout(slot, i)
    logits_stage(0, 0, e0_ref, m0_ref)
    for slot in range(n_slots):
        nxt = slot + 1 if slot + 1 < n_slots else None

        def group_pair(i, carry, slot=slot):
            g = 2 * i
            logits_stage(slot, g + 1, e1_ref, m1_ref)
            value_stage(slot, g, e0_ref, m0_ref)
            logits_stage(slot, g + 2, e0_ref, m0_ref)
            value_stage(slot, g + 1, e1_ref, m1_ref)
            return carry

        lax.fori_loop(0, n_loop, group_pair, 0)
        logits_stage(slot, n_groups - 1, e1_ref, m1_ref)
        value_stage(slot, n_groups - 2, e0_ref, m0_ref)
        if nxt is not None:
            logits_stage(nxt, 0, e0_ref, m0_ref)
        value_stage(slot, n_groups - 1, e1_ref, m1_ref)


def _dilated(att, rel_bias, n_pairs, b):
    s = att.shape[1] // b
    n_pat = len(DILATED_PATTERNS)
    bucket_rows = jnp.asarray(_bucket_rows())
    return pl.pallas_call(
        functools.partial(_dilated_kernel, seq=s),
        grid=(b, n_pairs),
        in_specs=[
            pl.BlockSpec(memory_space=pltpu.SMEM),
            pl.BlockSpec((n_pat, BASE_W), lambda bi, pi: (0, 0)),
            pl.BlockSpec((3, 1, s, LANES), lambda bi, pi: (0, pi, bi, 0)),
        ],
        out_specs=pl.BlockSpec((1, s, LANES), lambda bi, pi: (pi, bi, 0)),
        out_shape=jax.ShapeDtypeStruct((n_pairs, b * s, LANES), BF16),
        scratch_shapes=[
            pltpu.VMEM((n_pat * 3, 2 * QBLK, KWIN), F32),
            pltpu.VMEM((n_pat, 3, s, LANES), BF16),
            pltpu.VMEM((3, s, LANES), F32),
            pltpu.VMEM((n_pat - 1, s, LANES), F32),
            pltpu.VMEM((n_pat - 1, s, LANES), F32),
            pltpu.VMEM((ATT_GROUP, 2 * QBLK, KWIN), BF16),
            pltpu.VMEM((ATT_GROUP, 2 * QBLK, KWIN), BF16),
            pltpu.VMEM((ATT_GROUP, QBLK, LANES), F32),
            pltpu.VMEM((ATT_GROUP, QBLK, LANES), F32),
        ],
        compiler_params=_params("parallel", "parallel"),
        name="dilated",
    )(rel_bias, bucket_rows, att.reshape(3, n_pairs, b * s, LANES))


def _outproj_kernel(r_ref, a_ref, x_ref, w_ref, g_ref, wr_ref, x1_ref, h2_ref, aff_ref):
    n_e = aff_ref.shape[1]
    wr = wr_ref[...]
    w_hi = wr.astype(BF16)
    w_lo = (wr - w_hi.astype(F32)).astype(BF16)
    w_parts = jnp.where(lax.broadcasted_iota(I32, wr.shape, 1) < n_e, w_hi, w_lo)
    tm = x_ref.shape[0]
    mixed = jnp.concatenate([r_ref[i] for i in range(r_ref.shape[0])]
                            + [a_ref[i] for i in range(a_ref.shape[0])], axis=1)
    x1 = x_ref[...] + jnp.dot(mixed, w_ref[...], preferred_element_type=F32)
    x1_ref[...] = x1
    ms = jnp.mean(x1 * x1, axis=-1, keepdims=True)
    h2 = x1 * lax.rsqrt(ms + EPS) * g_ref[...]
    h_hi = h2.astype(BF16)
    h2_ref[...] = h_hi
    h_lo = (h2 - h_hi.astype(F32)).astype(BF16)
    prod = jnp.dot(jnp.concatenate([h_hi, h_lo], axis=0), w_parts, preferred_element_type=F32)
    logits = prod[:tm] + (pltpu.roll(prod[:tm], LANES - n_e, 1) + prod[tm:])
    expert_lane = lax.broadcasted_iota(I32, logits.shape, 1) < n_e
    logits = jnp.where(expert_lane, logits, NEG)
    m = jnp.max(logits, axis=1, keepdims=True)
    e = jnp.exp(logits - m)
    aff = e / jnp.sum(e, axis=1, keepdims=True)
    aff_ref[0] = aff.T[:n_e, :]


def _outproj(r, a, x, w_bf16, gain, w_router, tm=1024):
    b, s, d = x.shape
    n_r, n_a = r.shape[0], a.shape[0]
    n_e = w_router.shape[1]
    assert 2 * n_e <= LANES
    w_router = jnp.pad(jnp.concatenate([w_router, w_router], axis=1), ((0, 0), (0, LANES - 2 * n_e)))
    spt = s // tm
    tok = lambda width: pl.BlockSpec((tm, width), lambda i: (i, 0))
    slab = lambda blocks: pl.BlockSpec((blocks, tm, LANES), lambda i: (0, i, 0))
    const = lambda shape: pl.BlockSpec(shape, lambda i: (0, 0))
    return pl.pallas_call(
        _outproj_kernel,
        grid=(b * spt,),
        in_specs=[slab(n_r), slab(n_a), tok(d), const(((n_r + n_a) * LANES, d)), const((1, d)), const((d, LANES))],
        out_specs=[tok(d), tok(d), pl.BlockSpec((1, n_e, tm), lambda i: (i // spt, 0, i % spt))],
        out_shape=[
            jax.ShapeDtypeStruct((b * s, d), F32),
            jax.ShapeDtypeStruct((b * s, d), BF16),
            jax.ShapeDtypeStruct((b, n_e, s), F32),
        ],
        compiler_params=_params("parallel"),
        name="outproj",
    )(r, a, x.reshape(b * s, d), w_bf16, gain, w_router)


def _select_kernel(aff_ref, posm_ref, cnt_ref, *, cap):
    rows, s = aff_ref.shape
    bits = pltpu.bitcast(aff_ref[...], I32)

    def search(i, thr):
        cand = thr | jnp.left_shift(jnp.int32(1), 30 - i)
        cnt = jnp.sum((bits >= cand).astype(I32), axis=1, keepdims=True)
        return jnp.where(cnt >= cap, cand, thr)

    thr = lax.fori_loop(0, 31, search, jnp.zeros((rows, 1), I32))
    gt = bits > thr
    eq = bits == thr
    need = cap - jnp.sum(gt.astype(I32), axis=1, keepdims=True)

    w = LANES
    tri = (lax.broadcasted_iota(I32, (w, w), 0) <= lax.broadcasted_iota(I32, (w, w), 1)).astype(BF16)

    def excl_prefix(flags):
        carry = jnp.zeros((rows, 1), F32)
        out = []
        for c0 in range(0, s, w):
            f = flags[:, c0:c0 + w].astype(BF16)
            inc = jnp.dot(f, tri, preferred_element_type=F32)
            out.append(inc - f.astype(F32) + carry)
            carry = carry + inc[:, w - 1:w]
        return jnp.concatenate(out, axis=1).astype(I32)

    sel = gt | (eq & (excl_prefix(eq) < need))
    posm_ref[...] = jnp.where(sel, excl_prefix(sel), -1)

    tok = lax.broadcasted_iota(I32, (w, w), 0)
    edge = lax.broadcasted_iota(I32, (w, w), 1) * MOE_TILE
    cnt = jnp.zeros((rows, w), F32)
    for c0 in range(0, s, w):
        before = ((tok + c0) < edge).astype(BF16)
        cnt = cnt + jnp.dot(sel[:, c0:c0 + w].astype(BF16), before, preferred_element_type=F32)
    cnt_ref[...] = cnt.astype(I32)


def _select(aff_rows, cap):
    rows, s = aff_rows.shape
    return pl.pallas_call(
        functools.partial(_select_kernel, cap=cap),
        out_shape=[jax.ShapeDtypeStruct((rows, s), I32), jax.ShapeDtypeStruct((rows, LANES), I32)],
        compiler_params=pltpu.CompilerParams(vmem_limit_bytes=VMEM_LIMIT),
        name="select",
    )(aff_rows)


def _slot_window(cnt_ref, row, j, cap):
    lo, hi = cnt_ref[row * CNT_STRIDE + j], cnt_ref[row * CNT_STRIDE + j + 1]
    w0 = pl.multiple_of(jnp.minimum((lo // SLOT_ALIGN) * SLOT_ALIGN, cap - SLOT_WIN), SLOT_ALIGN)
    return hi, w0


def _any_overflow(windows):
    return functools.reduce(jnp.logical_or, [hi > w0 + SLOT_WIN for hi, w0 in windows])


def _gather_kernel(cnt_ref, posm_ref, h2_ref, xg_ref):
    bi, j = pl.program_id(0), pl.program_id(1)
    n_e, cap = xg_ref.shape[1:3]
    tt = MOE_TILE

    @pl.when(j == 0)
    def _():
        xg_ref[...] = jnp.zeros_like(xg_ref)

    slot0 = lax.broadcasted_iota(I32, (SLOT_WIN, tt), 0)
    for sub in range(h2_ref.shape[1] // tt):
        tile = j * (h2_ref.shape[1] // tt) + sub
        tok = slice(sub * tt, (sub + 1) * tt)
        windows = [_slot_window(cnt_ref, bi * n_e + e, tile, cap) for e in range(n_e)]
        for e in range(0, n_e, WIN_GROUP):
            onehot = jnp.concatenate(
                [(posm_ref[0, e + i:e + i + 1, tok] == slot0 + windows[e + i][1]).astype(BF16)
                 for i in range(WIN_GROUP)], axis=0)
            rows = jnp.dot(onehot, h2_ref[0, tok, :], preferred_element_type=F32).astype(BF16)
            for i in range(WIN_GROUP):
                xg_ref[0, e + i, pl.ds(windows[e + i][1], SLOT_WIN), :] += rows[i * SLOT_WIN:(i + 1) * SLOT_WIN]

        @pl.when(_any_overflow(windows))
        def _(tile=tile, tok=tok):
            wide = lax.broadcasted_iota(I32, (MXU_DEPTH, tt), 0)

            def one_expert(e, carry):
                hi, w0 = _slot_window(cnt_ref, bi * n_e + e, tile, cap)

                @pl.when(hi > w0 + SLOT_WIN)
                def _():
                    pos = posm_ref[0, pl.ds(e, 1), tok]
                    for ws in range(0, cap, MXU_DEPTH):
                        slot = wide + ws
                        onehot = ((pos == slot) & (slot >= w0 + SLOT_WIN)).astype(BF16)
                        extra = jnp.dot(onehot, h2_ref[0, tok, :], preferred_element_type=F32)
                        xg_ref[0, e, ws:ws + MXU_DEPTH, :] += extra.astype(BF16)

                return carry

            lax.fori_loop(0, n_e, one_expert, 0)


def _gather(cnt, posm, h2, cap):
    b, n_e, s = posm.shape
    d = h2.shape[2]
    step = MOE_TILE * GATHER_TILES
    return pl.pallas_call(
        _gather_kernel,
        grid_spec=pltpu.PrefetchScalarGridSpec(
            num_scalar_prefetch=1,
            grid=(b, s // step),
            in_specs=[
                pl.BlockSpec((1, n_e, step), lambda bi, j, cnt: (bi, 0, j)),
                pl.BlockSpec((1, step, d), lambda bi, j, cnt: (bi, j, 0)),
            ],
            out_specs=pl.BlockSpec((1, n_e, cap, d), lambda bi, j, cnt: (bi, 0, 0, 0)),
        ),
        out_shape=jax.ShapeDtypeStruct((b, n_e, cap, d), BF16),
        compiler_params=_params("parallel", "arbitrary"),
        name="gather",
    )(cnt, posm, h2)


def _ffn_kernel(xg_ref, wg_hbm, wu_hbm, wd_hbm, y_ref, wg_buf, wu_buf, wd_buf, wg_stage, wu_stage, wd_stage,
                sem, acc_ref, *, tf):
    e, bi = pl.program_id(0), pl.program_id(1)
    n_e, n_steps = pl.num_programs(0), pl.num_programs(1)
    rows_in, rows_dn = wg_stage.shape[1], wd_stage.shape[1]
    f_total = wg_buf.shape[2]
    n_chunks = wg_buf.shape[1] // rows_in

    def chunk_copies(expert, c):
        st = c % STEP_CHUNKS
        return (
            pltpu.make_async_copy(wg_hbm.at[expert, pl.ds(c * rows_in, rows_in), :], wg_stage.at[st], sem.at[0, st]),
            pltpu.make_async_copy(wu_hbm.at[expert, pl.ds(c * rows_in, rows_in), :], wu_stage.at[st], sem.at[1, st]),
            pltpu.make_async_copy(wd_hbm.at[expert, pl.ds(c * rows_dn, rows_dn), :], wd_stage.at[st], sem.at[2, st]),
        )

    def finish_chunk(expert, c):
        for cp in chunk_copies(expert, c):
            cp.wait()
        slot, st = expert % 2, c % STEP_CHUNKS
        wg_buf[slot, pl.ds(c * rows_in, rows_in), :] = wg_stage[st].astype(BF16)
        wu_buf[slot, pl.ds(c * rows_in, rows_in), :] = wu_stage[st].astype(BF16)
        wd_buf[slot, pl.ds(c * rows_dn, rows_dn), :] = wd_stage[st].astype(BF16)

    @pl.when((e == 0) & (bi == 0))
    def _():
        for cp in chunk_copies(0, 0):
            cp.start()

        def load_first(c, carry):
            @pl.when(c + 1 < n_chunks)
            def _():
                for cp in chunk_copies(0, c + 1):
                    cp.start()

            finish_chunk(0, c)
            return carry

        lax.fori_loop(0, n_chunks, load_first, 0)

    pending = jnp.where(bi > 0, e + 1 < n_e, e > 0)

    @pl.when(pending)
    def _():
        for k in range(STEP_CHUNKS):
            finish_chunk(jnp.where(bi > 0, e + 1, e),
                         jnp.where(bi > 0, bi - 1, n_steps - 1) * STEP_CHUNKS + k)

    @pl.when(e + 1 < n_e)
    def _():
        for k in range(STEP_CHUNKS):
            for cp in chunk_copies(e + 1, bi * STEP_CHUNKS + k):
                cp.start()

    def swiglu(slot):
        n_rows, cap, d = xg_ref.shape[0], xg_ref.shape[2], xg_ref.shape[3]
        xg = xg_ref[:, 0].reshape(n_rows * cap, d)
        for fi in range(f_total // tf):
            fs = slice(fi * tf, (fi + 1) * tf)
            gate = jnp.dot(xg, wg_buf[slot, :, fs], preferred_element_type=F32)
            up = jnp.dot(xg, wu_buf[slot, :, fs], preferred_element_type=F32)
            hid = ((gate * jax.nn.sigmoid(gate)) * up).astype(BF16)
            part = jnp.dot(hid, wd_buf[slot, fs, :], preferred_element_type=F32)
            if fi == 0:
                acc_ref[...] = part
            else:
                acc_ref[...] += part
        y_ref[:, 0] = acc_ref[...].reshape(n_rows, cap, d).astype(y_ref.dtype)

    for slot in range(2):
        pl.when(e % 2 == slot)(functools.partial(swiglu, slot))


def _ffn(xg, wg, wu, wd, tf=MXU_DEPTH):
    b, n_e, cap, d = xg.shape
    f = wg.shape[2]
    assert b % FFN_BATCH == 0
    n_steps = b // FFN_BATCH
    n_chunks = n_steps * STEP_CHUNKS
    assert d % n_chunks == 0 and f % n_chunks == 0 and (d // n_chunks) % SLOT_ALIGN == 0
    tok = pl.BlockSpec((FFN_BATCH, 1, cap, d), lambda e, bi: (bi, e, 0, 0))
    hbm = pl.BlockSpec(memory_space=pl.ANY)
    return pl.pallas_call(
        functools.partial(_ffn_kernel, tf=tf),
        grid=(n_e, n_steps),
        in_specs=[tok, hbm, hbm, hbm],
        out_specs=tok,
        out_shape=jax.ShapeDtypeStruct((b, n_e, cap, d), BF16),
        scratch_shapes=[
            pltpu.VMEM((2, d, f), BF16), pltpu.VMEM((2, d, f), BF16), pltpu.VMEM((2, f, d), BF16),
            pltpu.VMEM((STEP_CHUNKS, d // n_chunks, f), F32), pltpu.VMEM((STEP_CHUNKS, d // n_chunks, f), F32),
            pltpu.VMEM((STEP_CHUNKS, f // n_chunks, d), F32),
            pltpu.SemaphoreType.DMA((3, STEP_CHUNKS)),
            pltpu.VMEM((FFN_BATCH * cap, d), F32),
        ],
        compiler_params=_params("arbitrary", "arbitrary"),
        name="ffn",
    )(xg, wg, wu, wd)


def _combine_kernel(cnt_ref, pos_ref, gate_ref, y_ref, x1_ref, g_ref, out_ref, acc_ref):
    bi, j = pl.program_id(0), pl.program_id(1)
    tt = MOE_TILE
    n_e, cap = y_ref.shape[1:3]
    slot0 = lax.broadcasted_iota(I32, (SLOT_WIN, tt), 0)
    for sub in range(x1_ref.shape[1] // tt):
        tile = j * (x1_ref.shape[1] // tt) + sub
        tok = slice(sub * tt, (sub + 1) * tt)
        windows = [_slot_window(cnt_ref, bi * n_e + e, tile, cap) for e in range(n_e)]
        acc = x1_ref[0, tok, :]
        for e in range(0, n_e, WIN_GROUP):
            scatter_t = jnp.concatenate(
                [jnp.where(pos_ref[0, e + i:e + i + 1, tok] == slot0 + windows[e + i][1],
                           gate_ref[0, e + i:e + i + 1, tok], 0.0).astype(BF16) for i in range(WIN_GROUP)], axis=0)
            y_group = jnp.concatenate(
                [y_ref[0, e + i, pl.ds(windows[e + i][1], SLOT_WIN), :] for i in range(WIN_GROUP)], axis=0)
            acc = acc + lax.dot_general(scatter_t, y_group, _TN, preferred_element_type=F32)
        acc_ref[...] = acc

        @pl.when(_any_overflow(windows))
        def _(tile=tile, tok=tok):
            wide = lax.broadcasted_iota(I32, (MXU_DEPTH, tt), 0)

            def one_expert(e, carry):
                hi, w0 = _slot_window(cnt_ref, bi * n_e + e, tile, cap)

                @pl.when(hi > w0 + SLOT_WIN)
                def _():
                    pos = pos_ref[0, pl.ds(e, 1), tok]
                    gate = gate_ref[0, pl.ds(e, 1), tok]
                    for ws in range(0, cap, MXU_DEPTH):
                        slot = wide + ws
                        extra = jnp.where((pos == slot) & (slot >= w0 + SLOT_WIN), gate, 0.0).astype(BF16)
                        acc_ref[...] += lax.dot_general(extra, y_ref[0, e, ws:ws + MXU_DEPTH, :], _TN,
                                                        preferred_element_type=F32)

                return carry

            lax.fori_loop(0, n_e, one_expert, 0)

        acc = acc_ref[...]
        ms = jnp.mean(acc * acc, axis=-1, keepdims=True)
        out_ref[0, tok, :] = acc * lax.rsqrt(ms + EPS) * g_ref[...]


def _combine(cnt, posm, aff, y, x1, gain):
    b, s, d = x1.shape
    n_e, cap = y.shape[1:3]
    step = MOE_TILE * COMBINE_TILES
    return pl.pallas_call(
        _combine_kernel,
        grid_spec=pltpu.PrefetchScalarGridSpec(
            num_scalar_prefetch=1,
            grid=(b, s // step),
            in_specs=[
                pl.BlockSpec((1, n_e, step), lambda bi, j, cnt: (bi, 0, j)),
                pl.BlockSpec((1, n_e, step), lambda bi, j, cnt: (bi, 0, j)),
                pl.BlockSpec((1, n_e, cap, d), lambda bi, j, cnt: (bi, 0, 0, 0)),
                pl.BlockSpec((1, step, d), lambda bi, j, cnt: (bi, j, 0)),
                pl.BlockSpec((1, d), lambda bi, j, cnt: (0, 0)),
            ],
            out_specs=pl.BlockSpec((1, step, d), lambda bi, j, cnt: (bi, j, 0)),
            scratch_shapes=[pltpu.VMEM((MOE_TILE, d), F32)],
        ),
        out_shape=jax.ShapeDtypeStruct((b, s, d), F32),
        compiler_params=_params("parallel", "parallel"),
        name="combine",
    )(cnt, posm, aff, y, x1, gain)


def _rope_tables(seq):
    half = HEAD_DIM // 2
    pos = jnp.arange(seq, dtype=F32)
    inv = ROPE_BASE ** (-jnp.arange(0, HEAD_DIM, 2, dtype=F32) / HEAD_DIM)
    ang = pos[:, None] * inv[None, :]
    cos, sin = jnp.cos(ang), jnp.sin(ang)
    reps = LANES // HEAD_DIM
    cos_t = jnp.tile(jnp.concatenate([cos, cos], axis=1), (1, reps))
    sin_t = jnp.tile(jnp.concatenate([-sin, sin], axis=1), (1, reps))
    return cos_t, sin_t


def kernel(x, norm1_gain, w_in, ret_log_decay, ret_gn_gain, rel_bias, w_out, norm2_gain, w_router,
           w_gate, w_up, w_down, final_gain):
    b, s, d = x.shape
    depth = w_in.shape[0]
    ret_width = ret_gn_gain.shape[1]
    n_ret_pairs = ret_width // LANES
    att_width = (w_in.shape[2] - 4 * ret_width) // 3
    n_att_pairs = att_width // LANES
    cap = CAPACITY_FACTOR * s // N_EXPERTS
    cos_t, sin_t = _rope_tables(s)

    assert depth == 1, "single-layer block: the final norm is fused into the combine kernel"
    layer = 0
    ret, att = _proj(x.reshape(b * s, d), norm1_gain[layer][None, :], w_in[layer].astype(BF16),
                     cos_t, sin_t, n_ret=4 * ret_width, ret_width=ret_width)
    r = _retention(ret, ret_log_decay[layer], ret_gn_gain[layer][None, :], n_ret_pairs, b)
    a = _dilated(att, rel_bias, n_att_pairs, b)
    x1, h2, aff = _outproj(r, a, x, w_out[layer].astype(BF16), norm2_gain[layer][None, :],
                           w_router[layer])
    assert s % (MOE_TILE * max(GATHER_TILES, COMBINE_TILES)) == 0 and s // MOE_TILE < CNT_STRIDE and cap % MXU_DEPTH == 0 and N_EXPERTS % WIN_GROUP == 0
    posm, cnt = _select(aff.reshape(b * N_EXPERTS, s), cap)
    posm = posm.reshape(b, N_EXPERTS, s)
    cnt = cnt[:, :CNT_STRIDE].reshape(-1)
    xg = _gather(cnt, posm, h2.reshape(b, s, d), cap)
    y = _ffn(xg, w_gate[layer], w_up[layer], w_down[layer])
    return _combine(cnt, posm, aff, y, x1.reshape(b, s, d),
                    final_gain[None, :])
```

```python
import functools
import math

import numpy as np
import jax
import jax.numpy as jnp
from jax import lax
from jax.experimental import pallas as pl
from jax.experimental.pallas import tpu as pltpu

F32 = jnp.float32
BF16 = jnp.bfloat16
I32 = jnp.int32

HEAD_DIM = 64
LANES = 128
RET_CHUNK = 128
RET_GROUP = 4
ROPE_BASE = 10000.0
DILATED_PATTERNS = ((128, 1), (512, 4), (2048, 16))
SIDE = 64
N_BUCKETS = 32
MAX_DISTANCE = 1024
N_EXPERTS = 16
CAPACITY_FACTOR = 2
MOE_TILE = 256
GATHER_TILES = 8
COMBINE_TILES = 4
MXU_DEPTH = 256
WIN_GROUP = 4
SLOT_WIN = MXU_DEPTH // WIN_GROUP
SLOT_ALIGN = 16
CNT_STRIDE = 32
FFN_BATCH = 2
STEP_CHUNKS = 2
EPS = 1e-6
NEG = -1e30
VMEM_LIMIT = 56 * 1024 * 1024

_NT = (((1,), (1,)), ((), ()))
_TN = (((0,), (0,)), ((), ()))


def _params(*sem):
    return pltpu.CompilerParams(dimension_semantics=sem, vmem_limit_bytes=VMEM_LIMIT)


def _proj_kernel(x_ref, g_ref, w_ref, cos_ref, sin_ref, ret_ref, att_ref, *, n_ret, ret_width, chunk):
    x = x_ref[...]
    ms = jnp.mean(x * x, axis=-1, keepdims=True)
    h = (x * lax.rsqrt(ms + EPS) * g_ref[...]).astype(BF16)
    n_cols = w_ref.shape[1]
    cos, sin = cos_ref[...], sin_ref[...]
    first_half = (lax.broadcasted_iota(I32, cos.shape, 1) % HEAD_DIM) < (HEAD_DIM // 2)
    for c0 in range(0, n_cols, chunk):
        o = jnp.dot(h, w_ref[:, c0:c0 + chunk], preferred_element_type=F32)
        for l0 in range(0, chunk, LANES):
            t = o[:, l0:l0 + LANES]
            col = c0 + l0
            if col >= n_ret:
                att_ref[(col - n_ret) // LANES] = t
            elif col >= 2 * ret_width:
                ret_ref[col // LANES] = t.astype(BF16)
            else:
                scale = 1.0 if col < ret_width else HEAD_DIM ** -0.5
                swapped = jnp.where(first_half, pltpu.roll(t, LANES - HEAD_DIM // 2, 1),
                                    pltpu.roll(t, HEAD_DIM // 2, 1))
                ret_ref[col // LANES] = ((t * cos + swapped * sin) * scale).astype(BF16)


def _proj(x2d, gain, w_bf16, cos_t, sin_t, n_ret, ret_width, tm=1024):
    t, d = x2d.shape
    n_cols = w_bf16.shape[1]
    n_att = n_cols - n_ret
    seq_tiles = cos_t.shape[0] // tm
    assert ret_width % 512 == 0 and n_ret % 512 == 0 and n_cols % 512 == 0
    return pl.pallas_call(
        functools.partial(_proj_kernel, n_ret=n_ret, ret_width=ret_width, chunk=512),
        grid=(t // tm,),
        in_specs=[
            pl.BlockSpec((tm, d), lambda i: (i, 0)),
            pl.BlockSpec((1, d), lambda i: (0, 0)),
            pl.BlockSpec((d, n_cols), lambda i: (0, 0)),
            pl.BlockSpec((tm, LANES), lambda i: (i % seq_tiles, 0)),
            pl.BlockSpec((tm, LANES), lambda i: (i % seq_tiles, 0)),
        ],
        out_specs=[
            pl.BlockSpec((n_ret // LANES, tm, LANES), lambda i: (0, i, 0)),
            pl.BlockSpec((n_att // LANES, tm, LANES), lambda i: (0, i, 0)),
        ],
        out_shape=[
            jax.ShapeDtypeStruct((n_ret // LANES, t, LANES), BF16),
            jax.ShapeDtypeStruct((n_att // LANES, t, LANES), F32),
        ],
        compiler_params=_params("parallel"),
        name="proj",
    )(x2d, gain, w_bf16, cos_t, sin_t)


def _retention_kernel(decay_ref, q_ref, k_ref, v_ref, g_ref, gn_ref, out_ref,
                      rf_ref, rb_ref, kvf_ref, kvb_ref, p0_ref, p1_ref, *, seq):
    c = RET_CHUNK
    n_chunks = seq // c
    p = pl.program_id(1)
    lane = lax.broadcasted_iota(I32, (c, LANES), 1)
    row = lax.broadcasted_iota(I32, (c, LANES), 0)
    head0 = lane < HEAD_DIM
    rowf = row.astype(F32)

    lgf0, lgf1 = decay_ref[0, 2 * p], decay_ref[0, 2 * p + 1]
    lgb0, lgb1 = decay_ref[1, 2 * p], decay_ref[1, 2 * p + 1]
    lgf_lane = jnp.where(head0, lgf0, lgf1)
    lgb_lane = jnp.where(head0, lgb0, lgb1)
    lgf_row = jnp.where(row < HEAD_DIM, lgf0, lgf1)
    lgb_row = jnp.where(row < HEAD_DIM, lgb0, lgb1)
    same_head = (row < HEAD_DIM) == head0

    zeta_f = jnp.exp((c - 1 - rowf) * lgf_lane)
    zeta_b = jnp.exp(rowf * lgb_lane)
    xi_f = jnp.exp((rowf + 1.0) * lgf_lane)
    xi_b = jnp.exp((c - rowf) * lgb_lane)
    gch_f = jnp.where(same_head, jnp.exp(c * lgf_row), 0.0)
    gch_b = jnp.where(same_head, jnp.exp(c * lgb_row), 0.0)

    diff = (row - lane).astype(F32)

    def dmat(lf, lb):
        return jnp.exp(jnp.where(diff >= 0, diff * lf, -diff * lb))

    d_stack = jnp.concatenate([dmat(lgf0, lgb0), dmat(lgf1, lgb1)], axis=0)

    def kv_step(n, carry):
        sl = pl.ds(pl.multiple_of(n * c, c), c)
        k = k_ref[0, sl, :].astype(F32)
        v = v_ref[0, sl, :]
        kvf = lax.dot_general((k * zeta_f).astype(BF16), v, _TN, preferred_element_type=F32)
        kvb = lax.dot_general((k * zeta_b).astype(BF16), v, _TN, preferred_element_type=F32)
        kvf_ref[n] = jnp.where(same_head, kvf, 0.0)
        kvb_ref[n] = jnp.where(same_head, kvb, 0.0)
        return carry

    lax.fori_loop(0, n_chunks, kv_step, 0, unroll=True)

    def scan_step(i, states):
        sf, sb = states
        nb = n_chunks - 1 - i
        rf_ref[i] = sf.astype(BF16)
        rb_ref[nb] = sb.astype(BF16)
        return gch_f * sf + kvf_ref[i], gch_b * sb + kvb_ref[nb]

    zero_state = jnp.zeros((LANES, LANES), F32)
    lax.fori_loop(0, n_chunks, scan_step, (zero_state, zero_state), unroll=True)

    gn_gain = gn_ref[...]

    def score_stage(g, p_ref):
        for jc in range(RET_GROUP):
            sl = pl.ds(pl.multiple_of((g * RET_GROUP + jc) * c, c), c)
            q = q_ref[0, sl, :]
            zero = jnp.zeros_like(q)
            q_stack = jnp.concatenate([jnp.where(head0, q, zero), jnp.where(head0, zero, q)], axis=0)
            scores = lax.dot_general(q_stack, k_ref[0, sl, :], _NT, preferred_element_type=F32) * d_stack
            sb = scores.astype(BF16)
            p_ref[jc] = jnp.concatenate([sb[:c], sb[c:]], axis=1)

    def half_mean(t):
        s0 = jnp.sum(jnp.where(head0, t, 0.0), axis=-1, keepdims=True)
        s1 = jnp.sum(jnp.where(head0, 0.0, t), axis=-1, keepdims=True)
        return jnp.where(head0, s0, s1) * (1.0 / HEAD_DIM)

    def value_stage(g, p_ref):
        for jc in range(RET_GROUP):
            n = g * RET_GROUP + jc
            sl = pl.ds(pl.multiple_of(n * c, c), c)
            q = q_ref[0, sl, :]
            v = v_ref[0, sl, :]
            zero = jnp.zeros_like(v)
            v_stack = jnp.concatenate([jnp.where(head0, v, zero), jnp.where(head0, zero, v)], axis=0)
            inner = jnp.dot(p_ref[jc], v_stack, preferred_element_type=F32)
            cross_f = jnp.dot(q, rf_ref[n], preferred_element_type=F32) * xi_f
            cross_b = jnp.dot(q, rb_ref[n], preferred_element_type=F32) * xi_b
            r = inner + cross_f + cross_b
            dlt = r - half_mean(r)
            var = half_mean(dlt * dlt)
            gate = g_ref[0, sl, :].astype(F32)
            y = dlt * lax.rsqrt(var + EPS) * gn_gain * (gate * jax.nn.sigmoid(gate))
            out_ref[0, sl, :] = y.astype(out_ref.dtype)

    n_groups = n_chunks // RET_GROUP
    assert n_groups % 2 == 0 and n_groups >= 2
    score_stage(0, p0_ref)

    def group_pair(i, carry):
        g = 2 * i
        score_stage(g + 1, p1_ref)
        value_stage(g, p0_ref)
        score_stage(g + 2, p0_ref)
        value_stage(g + 1, p1_ref)
        return carry

    lax.fori_loop(0, n_groups // 2 - 1, group_pair, 0, unroll=True)
    score_stage(n_groups - 1, p1_ref)
    value_stage(n_groups - 2, p0_ref)
    value_stage(n_groups - 1, p1_ref)


def _retention(ret, decay, gn_gain, n_pairs, b):
    s = ret.shape[1] // b
    blk = lambda off: pl.BlockSpec((1, s, LANES), lambda bi, pi: (off + pi, bi, 0))
    return pl.pallas_call(
        functools.partial(_retention_kernel, seq=s),
        grid=(b, n_pairs),
        in_specs=[
            pl.BlockSpec(memory_space=pltpu.SMEM),
            blk(0), blk(n_pairs), blk(2 * n_pairs), blk(3 * n_pairs),
            pl.BlockSpec((1, LANES), lambda bi, pi: (0, pi)),
        ],
        out_specs=pl.BlockSpec((1, s, LANES), lambda bi, pi: (pi, bi, 0)),
        out_shape=jax.ShapeDtypeStruct((n_pairs, b * s, LANES), BF16),
        scratch_shapes=[
            pltpu.VMEM((s // RET_CHUNK, LANES, LANES), BF16),
            pltpu.VMEM((s // RET_CHUNK, LANES, LANES), BF16),
            pltpu.VMEM((s // RET_CHUNK, LANES, LANES), F32),
            pltpu.VMEM((s // RET_CHUNK, LANES, LANES), F32),
            pltpu.VMEM((RET_GROUP, RET_CHUNK, 2 * RET_CHUNK), BF16),
            pltpu.VMEM((RET_GROUP, RET_CHUNK, 2 * RET_CHUNK), BF16),
        ],
        compiler_params=_params("parallel", "parallel"),
        name="retention",
    )(decay, ret, ret, ret, ret, gn_gain)


QBLK = 2 * SIDE
KWIN = 4 * SIDE
BASE_W = 512
ATT_GROUP = 4


def _t5_bucket_np(rel):
    half = N_BUCKETS // 2
    max_exact = half // 2
    bucket = np.where(rel > 0, half, 0)
    n = np.abs(rel)
    nf = np.maximum(n, 1).astype(np.float32)
    large = max_exact + (np.log(nf / np.float32(max_exact)) / np.float32(math.log(MAX_DISTANCE / max_exact))
                         * np.float32(half - max_exact)).astype(np.int32)
    large = np.minimum(large, half - 1)
    return (bucket + np.where(n < max_exact, n, large)).astype(np.int32)


def _bucket_rows():
    k = np.arange(BASE_W)
    off = k - SIDE
    rows = []
    for _, dilation in DILATED_PATTERNS:
        rows.append(np.where(k <= 2 * SIDE, _t5_bucket_np(off * dilation), -1))
    return np.stack(rows).astype(np.int32)


def _dilated_kernel(bias_tab_ref, bucket_ref, qkv_ref, out_ref,
                    bias_ref, perm_ref, stage_ref, o_ref, lse_ref, e0_ref, e1_ref, m0_ref, m1_ref, *, seq):
    order = list(range(len(DILATED_PATTERNS) - 1, -1, -1))
    dil = [DILATED_PATTERNS[pi][1] for pi in order]
    n_slots = len(order)
    assert n_slots == 3 and dil[0] % dil[1] == 0 and dil[1] > 1 and dil[2] == 1
    p = pl.program_id(1)
    lane = lax.broadcasted_iota(I32, (QBLK, LANES), 1)
    head0 = lane < HEAD_DIM
    head0_k = lax.broadcasted_iota(I32, (KWIN, LANES), 1) < HEAD_DIM
    n_groups = seq // QBLK // ATT_GROUP
    n_loop = n_groups // 2 - 1
    assert n_groups % 2 == 0 and n_loop >= 0

    for pi in range(len(DILATED_PATTERNS)):
        bucket = jnp.broadcast_to(bucket_ref[pi:pi + 1, :], (QBLK, BASE_W))
        for hh in range(2):
            base = jnp.full((QBLK, BASE_W), NEG, F32)
            for bk in range(N_BUCKETS):
                base = jnp.where(bucket == bk, bias_tab_ref[bk, 2 * p + hh], base)
            for var, shift in enumerate((BASE_W - SIDE, 0, SIDE)):
                tile = pltpu.roll(base, shift, 1, stride=1, stride_axis=0)
                bias_ref[pi * 3 + var, hh * QBLK:(hh + 1) * QBLK, :] = tile[:, :KWIN]

    def relayout(slot, i):
        d = dil[slot]
        seg_len = seq // d
        scale = jnp.where(i == 0, HEAD_DIM ** -0.5, 1.0).astype(F32)
        if d == 1:
            perm_ref[slot, i] = (qkv_ref[i, 0] * scale).astype(BF16)
        elif slot == 1:
            for r in range(d):
                val = qkv_ref[i, 0, pl.ds(r, seg_len, stride=d), :]
                stage_ref[i, r * seg_len:(r + 1) * seg_len, :] = val
                perm_ref[slot, i, r * seg_len:(r + 1) * seg_len, :] = (val * scale).astype(BF16)
        else:
            prev_d = dil[1]
            step, prev_len = d // prev_d, seq // prev_d
            for r_prev in range(prev_d):
                for r_step in range(step):
                    val = stage_ref[i, pl.ds(r_prev * prev_len + r_step, seg_len, stride=step), :]
                    r = r_prev + prev_d * r_step
                    perm_ref[slot, i, r * seg_len:(r + 1) * seg_len, :] = (val * scale).astype(BF16)

    def placement(slot, t):
        seg_len = seq // dil[slot]
        blocks_per_seg = seg_len // QBLK
        seg = t // blocks_per_seg
        u = t % blocks_per_seg
        q0 = pl.multiple_of(t * QBLK, QBLK)
        seg0 = seg * seg_len
        k0 = pl.multiple_of(jnp.clip(q0 - SIDE, seg0, seg0 + seg_len - KWIN), SIDE)
        var = jnp.where(u == 0, 0, jnp.where(u == blocks_per_seg - 1, 2, 1))
        return seg, u, q0, k0, var

    def logits_stage(slot, g, e_ref, m_ref):
        for jb in range(ATT_GROUP):
            _, _, q0, k0, var = placement(slot, g * ATT_GROUP + jb)
            q = perm_ref[slot, 0, pl.ds(q0, QBLK), :]
            kw = perm_ref[slot, 1, pl.ds(k0, KWIN), :]
            zero = jnp.zeros_like(q)
            q_stack = jnp.concatenate([jnp.where(head0, q, zero), jnp.where(head0, zero, q)], axis=0)
            s = lax.dot_general(q_stack, kw, _NT, preferred_element_type=F32) + bias_ref[order[slot] * 3 + var]
            m = jnp.max(s, axis=-1, keepdims=True)
            e_ref[jb] = jnp.exp(s - m).astype(BF16)
            m_ref[jb] = jnp.where(head0, m[:QBLK], m[QBLK:])

    def value_stage(slot, g, e_ref, m_ref):
        d = dil[slot]
        for jb in range(ATT_GROUP):
            seg, u, q0, k0, _ = placement(slot, g * ATT_GROUP + jb)
            vw = perm_ref[slot, 2, pl.ds(k0, KWIN), :]
            one = jnp.ones_like(vw)
            pv0 = jnp.dot(e_ref[jb, :QBLK, :], jnp.where(head0_k, vw, one), preferred_element_type=F32)
            pv1 = jnp.dot(e_ref[jb, QBLK:, :], jnp.where(head0_k, one, vw), preferred_element_type=F32)
            num = jnp.where(head0, pv0, pv1)
            den = pltpu.roll(jnp.where(head0, pv1, pv0), HEAD_DIM, 1)
            o = num * (1.0 / den)
            lse = m_ref[jb] + jnp.log(den)
            if slot > 0:
                prev_o, prev_lse = o_ref[slot - 1, pl.ds(q0, QBLK), :], lse_ref[slot - 1, pl.ds(q0, QBLK), :]
                mx = jnp.maximum(lse, prev_lse)
                w_new, w_prev = jnp.exp(lse - mx), jnp.exp(prev_lse - mx)
                total = w_new + w_prev
                inv = 1.0 / total
                o = (w_new * inv) * o + (w_prev * inv) * prev_o
                lse = mx + jnp.log(total)
            if slot == n_slots - 1:
                out_ref[0, pl.ds(q0, QBLK), :] = o.astype(out_ref.dtype)
            else:
                d_next = dil[slot + 1]
                step = d // d_next
                start = (seg % d_next) * (seq // d_next) + seg // d_next + step * (u * QBLK)
                o_ref[slot, pl.ds(start, QBLK, stride=step), :] = o
                lse_ref[slot, pl.ds(start, QBLK, stride=step), :] = lse

    for slot in (1, 0, 2):
        for i in range(3):
            relayout(slot, i)
    logits_stage(0, 0, e0_ref, m0_ref)
    for slot in range(n_slots):
        nxt = slot + 1 if slot + 1 < n_slots else None

        def group_pair(i, carry, slot=slot):
            g = 2 * i
            logits_stage(slot, g + 1, e1_ref, m1_ref)
            value_stage(slot, g, e0_ref, m0_ref)
            logits_stage(slot, g + 2, e0_ref, m0_ref)
            value_stage(slot, g + 1, e1_ref, m1_ref)
            return carry

        lax.fori_loop(0, n_loop, group_pair, 0)
        logits_stage(slot, n_groups - 1, e1_ref, m1_ref)
        value_stage(slot, n_groups - 2, e0_ref, m0_ref)
        if nxt is not None:
            logits_stage(nxt, 0, e0_ref, m0_ref)
        value_stage(slot, n_groups - 1, e1_ref, m1_ref)


def _dilated(att, rel_bias, n_pairs, b):
    s = att.shape[1] // b
    n_pat = len(DILATED_PATTERNS)
    bucket_rows = jnp.asarray(_bucket_rows())
    return pl.pallas_call(
        functools.partial(_dilated_kernel, seq=s),
        grid=(b, n_pairs),
        in_specs=[
            pl.BlockSpec(memory_space=pltpu.SMEM),
            pl.BlockSpec((n_pat, BASE_W), lambda bi, pi: (0, 0)),
            pl.BlockSpec((3, 1, s, LANES), lambda bi, pi: (0, pi, bi, 0)),
        ],
        out_specs=pl.BlockSpec((1, s, LANES), lambda bi, pi: (pi, bi, 0)),
        out_shape=jax.ShapeDtypeStruct((n_pairs, b * s, LANES), BF16),
        scratch_shapes=[
            pltpu.VMEM((n_pat * 3, 2 * QBLK, KWIN), F32),
            pltpu.VMEM((n_pat, 3, s, LANES), BF16),
            pltpu.VMEM((3, s, LANES), F32),
            pltpu.VMEM((n_pat - 1, s, LANES), F32),
            pltpu.VMEM((n_pat - 1, s, LANES), F32),
            pltpu.VMEM((ATT_GROUP, 2 * QBLK, KWIN), BF16),
            pltpu.VMEM((ATT_GROUP, 2 * QBLK, KWIN), BF16),
            pltpu.VMEM((ATT_GROUP, QBLK, LANES), F32),
            pltpu.VMEM((ATT_GROUP, QBLK, LANES), F32),
        ],
        compiler_params=_params("parallel", "parallel"),
        name="dilated",
    )(rel_bias, bucket_rows, att.reshape(3, n_pairs, b * s, LANES))


def _outproj_kernel(r_ref, a_ref, x_ref, w_ref, g_ref, wr_ref, x1_ref, h2_ref, aff_ref):
    n_e = aff_ref.shape[1]
    wr = wr_ref[...]
    w_hi = wr.astype(BF16)
    w_lo = (wr - w_hi.astype(F32)).astype(BF16)
    w_parts = jnp.where(lax.broadcasted_iota(I32, wr.shape, 1) < n_e, w_hi, w_lo)
    tm = x_ref.shape[0]
    mixed = jnp.concatenate([r_ref[i] for i in range(r_ref.shape[0])]
                            + [a_ref[i] for i in range(a_ref.shape[0])], axis=1)
    x1 = x_ref[...] + jnp.dot(mixed, w_ref[...], preferred_element_type=F32)
    x1_ref[...] = x1
    ms = jnp.mean(x1 * x1, axis=-1, keepdims=True)
    h2 = x1 * lax.rsqrt(ms + EPS) * g_ref[...]
    h_hi = h2.astype(BF16)
    h2_ref[...] = h_hi
    h_lo = (h2 - h_hi.astype(F32)).astype(BF16)
    prod = jnp.dot(jnp.concatenate([h_hi, h_lo], axis=0), w_parts, preferred_element_type=F32)
    logits = prod[:tm] + (pltpu.roll(prod[:tm], LANES - n_e, 1) + prod[tm:])
    expert_lane = lax.broadcasted_iota(I32, logits.shape, 1) < n_e
    logits = jnp.where(expert_lane, logits, NEG)
    m = jnp.max(logits, axis=1, keepdims=True)
    e = jnp.exp(logits - m)
    aff = e / jnp.sum(e, axis=1, keepdims=True)
    aff_ref[0] = aff.T[:n_e, :]


def _outproj(r, a, x, w_bf16, gain, w_router, tm=1024):
    b, s, d = x.shape
    n_r, n_a = r.shape[0], a.shape[0]
    n_e = w_router.shape[1]
    assert 2 * n_e <= LANES
    w_router = jnp.pad(jnp.concatenate([w_router, w_router], axis=1), ((0, 0), (0, LANES - 2 * n_e)))
    spt = s // tm
    tok = lambda width: pl.BlockSpec((tm, width), lambda i: (i, 0))
    slab = lambda blocks: pl.BlockSpec((blocks, tm, LANES), lambda i: (0, i, 0))
    const = lambda shape: pl.BlockSpec(shape, lambda i: (0, 0))
    return pl.pallas_call(
        _outproj_kernel,
        grid=(b * spt,),
        in_specs=[slab(n_r), slab(n_a), tok(d), const(((n_r + n_a) * LANES, d)), const((1, d)), const((d, LANES))],
        out_specs=[tok(d), tok(d), pl.BlockSpec((1, n_e, tm), lambda i: (i // spt, 0, i % spt))],
        out_shape=[
            jax.ShapeDtypeStruct((b * s, d), F32),
            jax.ShapeDtypeStruct((b * s, d), BF16),
            jax.ShapeDtypeStruct((b, n_e, s), F32),
        ],
        compiler_params=_params("parallel"),
        name="outproj",
    )(r, a, x.reshape(b * s, d), w_bf16, gain, w_router)


def _select_kernel(aff_ref, posm_ref, cnt_ref, *, cap):
    rows, s = aff_ref.shape
    bits = pltpu.bitcast(aff_ref[...], I32)

    def search(i, thr):
        cand = thr | jnp.left_shift(jnp.int32(1), 30 - i)
        cnt = jnp.sum((bits >= cand).astype(I32), axis=1, keepdims=True)
        return jnp.where(cnt >= cap, cand, thr)

    thr = lax.fori_loop(0, 31, search, jnp.zeros((rows, 1), I32))
    gt = bits > thr
    eq = bits == thr
    need = cap - jnp.sum(gt.astype(I32), axis=1, keepdims=True)

    w = LANES
    tri = (lax.broadcasted_iota(I32, (w, w), 0) <= lax.broadcasted_iota(I32, (w, w), 1)).astype(BF16)

    def excl_prefix(flags):
        carry = jnp.zeros((rows, 1), F32)
        out = []
        for c0 in range(0, s, w):
            f = flags[:, c0:c0 + w].astype(BF16)
            inc = jnp.dot(f, tri, preferred_element_type=F32)
            out.append(inc - f.astype(F32) + carry)
            carry = carry + inc[:, w - 1:w]
        return jnp.concatenate(out, axis=1).astype(I32)

    sel = gt | (eq & (excl_prefix(eq) < need))
    posm_ref[...] = jnp.where(sel, excl_prefix(sel), -1)

    tok = lax.broadcasted_iota(I32, (w, w), 0)
    edge = lax.broadcasted_iota(I32, (w, w), 1) * MOE_TILE
    cnt = jnp.zeros((rows, w), F32)
    for c0 in range(0, s, w):
        before = ((tok + c0) < edge).astype(BF16)
        cnt = cnt + jnp.dot(sel[:, c0:c0 + w].astype(BF16), before, preferred_element_type=F32)
    cnt_ref[...] = cnt.astype(I32)


def _select(aff_rows, cap):
    rows, s = aff_rows.shape
    return pl.pallas_call(
        functools.partial(_select_kernel, cap=cap),
        out_shape=[jax.ShapeDtypeStruct((rows, s), I32), jax.ShapeDtypeStruct((rows, LANES), I32)],
        compiler_params=pltpu.CompilerParams(vmem_limit_bytes=VMEM_LIMIT),
        name="select",
    )(aff_rows)


def _slot_window(cnt_ref, row, j, cap):
    lo, hi = cnt_ref[row * CNT_STRIDE + j], cnt_ref[row * CNT_STRIDE + j + 1]
    w0 = pl.multiple_of(jnp.minimum((lo // SLOT_ALIGN) * SLOT_ALIGN, cap - SLOT_WIN), SLOT_ALIGN)
    return hi, w0


def _any_overflow(windows):
    return functools.reduce(jnp.logical_or, [hi > w0 + SLOT_WIN for hi, w0 in windows])


def _gather_kernel(cnt_ref, posm_ref, h2_ref, xg_ref):
    bi, j = pl.program_id(0), pl.program_id(1)
    n_e, cap = xg_ref.shape[1:3]
    tt = MOE_TILE

    @pl.when(j == 0)
    def _():
        xg_ref[...] = jnp.zeros_like(xg_ref)

    slot0 = lax.broadcasted_iota(I32, (SLOT_WIN, tt), 0)
    for sub in range(h2_ref.shape[1] // tt):
        tile = j * (h2_ref.shape[1] // tt) + sub
        tok = slice(sub * tt, (sub + 1) * tt)
        windows = [_slot_window(cnt_ref, bi * n_e + e, tile, cap) for e in range(n_e)]
        for e in range(0, n_e, WIN_GROUP):
            onehot = jnp.concatenate(
                [(posm_ref[0, e + i:e + i + 1, tok] == slot0 + windows[e + i][1]).astype(BF16)
                 for i in range(WIN_GROUP)], axis=0)
            rows = jnp.dot(onehot, h2_ref[0, tok, :], preferred_element_type=F32).astype(BF16)
            for i in range(WIN_GROUP):
                xg_ref[0, e + i, pl.ds(windows[e + i][1], SLOT_WIN), :] += rows[i * SLOT_WIN:(i + 1) * SLOT_WIN]

        @pl.when(_any_overflow(windows))
        def _(tile=tile, tok=tok):
            wide = lax.broadcasted_iota(I32, (MXU_DEPTH, tt), 0)

            def one_expert(e, carry):
                hi, w0 = _slot_window(cnt_ref, bi * n_e + e, tile, cap)

                @pl.when(hi > w0 + SLOT_WIN)
                def _():
                    pos = posm_ref[0, pl.ds(e, 1), tok]
                    for ws in range(0, cap, MXU_DEPTH):
                        slot = wide + ws
                        onehot = ((pos == slot) & (slot >= w0 + SLOT_WIN)).astype(BF16)
                        extra = jnp.dot(onehot, h2_ref[0, tok, :], preferred_element_type=F32)
                        xg_ref[0, e, ws:ws + MXU_DEPTH, :] += extra.astype(BF16)

                return carry

            lax.fori_loop(0, n_e, one_expert, 0)


def _gather(cnt, posm, h2, cap):
    b, n_e, s = posm.shape
    d = h2.shape[2]
    step = MOE_TILE * GATHER_TILES
    return pl.pallas_call(
        _gather_kernel,
        grid_spec=pltpu.PrefetchScalarGridSpec(
            num_scalar_prefetch=1,
            grid=(b, s // step),
            in_specs=[
                pl.BlockSpec((1, n_e, step), lambda bi, j, cnt: (bi, 0, j)),
                pl.BlockSpec((1, step, d), lambda bi, j, cnt: (bi, j, 0)),
            ],
            out_specs=pl.BlockSpec((1, n_e, cap, d), lambda bi, j, cnt: (bi, 0, 0, 0)),
        ),
        out_shape=jax.ShapeDtypeStruct((b, n_e, cap, d), BF16),
        compiler_params=_params("parallel", "arbitrary"),
        name="gather",
    )(cnt, posm, h2)


def _ffn_kernel(xg_ref, wg_hbm, wu_hbm, wd_hbm, y_ref, wg_buf, wu_buf, wd_buf, wg_stage, wu_stage, wd_stage,
                sem, hid_ref, *, tf):
    e, bi = pl.program_id(0), pl.program_id(1)
    n_e, n_steps = pl.num_programs(0), pl.num_programs(1)
    rows_in, rows_dn = wg_stage.shape[1], wd_stage.shape[1]
    f_total = wg_buf.shape[2]
    n_chunks = wg_buf.shape[1] // rows_in

    def chunk_copies(expert, c):
        st = c % STEP_CHUNKS
        return (
            pltpu.make_async_copy(wg_hbm.at[expert, pl.ds(c * rows_in, rows_in), :], wg_stage.at[st], sem.at[0, st]),
            pltpu.make_async_copy(wu_hbm.at[expert, pl.ds(c * rows_in, rows_in), :], wu_stage.at[st], sem.at[1, st]),
            pltpu.make_async_copy(wd_hbm.at[expert, pl.ds(c * rows_dn, rows_dn), :], wd_stage.at[st], sem.at[2, st]),
        )

    def finish_chunk(expert, c):
        for cp in chunk_copies(expert, c):
            cp.wait()
        slot, st = expert % 2, c % STEP_CHUNKS
        wg_buf[slot, pl.ds(c * rows_in, rows_in), :] = wg_stage[st].astype(BF16)
        wu_buf[slot, pl.ds(c * rows_in, rows_in), :] = wu_stage[st].astype(BF16)
        wd_buf[slot, pl.ds(c * rows_dn, rows_dn), :] = wd_stage[st].astype(BF16)

    @pl.when((e == 0) & (bi == 0))
    def _():
        for cp in chunk_copies(0, 0):
            cp.start()

        def load_first(c, carry):
            @pl.when(c + 1 < n_chunks)
            def _():
                for cp in chunk_copies(0, c + 1):
                    cp.start()

            finish_chunk(0, c)
            return carry

        lax.fori_loop(0, n_chunks, load_first, 0)

    pending = jnp.where(bi > 0, e + 1 < n_e, e > 0)

    @pl.when(pending)
    def _():
        for k in range(STEP_CHUNKS):
            finish_chunk(jnp.where(bi > 0, e + 1, e),
                         jnp.where(bi > 0, bi - 1, n_steps - 1) * STEP_CHUNKS + k)

    @pl.when(e + 1 < n_e)
    def _():
        for k in range(STEP_CHUNKS):
            for cp in chunk_copies(e + 1, bi * STEP_CHUNKS + k):
                cp.start()

    def swiglu(slot):
        n_rows, cap, d = xg_ref.shape[0], xg_ref.shape[2], xg_ref.shape[3]
        xg = xg_ref[:, 0].reshape(n_rows * cap, d)
        for fi in range(f_total // tf):
            fs = slice(fi * tf, (fi + 1) * tf)
            gate = jnp.dot(xg, wg_buf[slot, :, fs], preferred_element_type=F32)
            up = jnp.dot(xg, wu_buf[slot, :, fs], preferred_element_type=F32)
            hid_ref[:, fs] = ((gate * jax.nn.sigmoid(gate)) * up).astype(BF16)
        y = jnp.dot(hid_ref[...], wd_buf[slot], preferred_element_type=F32)
        y_ref[:, 0] = y.reshape(n_rows, cap, d).astype(y_ref.dtype)

    for slot in range(2):
        pl.when(e % 2 == slot)(functools.partial(swiglu, slot))


def _ffn(xg, wg, wu, wd, tf=MXU_DEPTH):
    b, n_e, cap, d = xg.shape
    f = wg.shape[2]
    assert b % FFN_BATCH == 0
    n_steps = b // FFN_BATCH
    n_chunks = n_steps * STEP_CHUNKS
    assert d % n_chunks == 0 and f % n_chunks == 0 and (d // n_chunks) % SLOT_ALIGN == 0
    tok = pl.BlockSpec((FFN_BATCH, 1, cap, d), lambda e, bi: (bi, e, 0, 0))
    hbm = pl.BlockSpec(memory_space=pl.ANY)
    return pl.pallas_call(
        functools.partial(_ffn_kernel, tf=tf),
        grid=(n_e, n_steps),
        in_specs=[tok, hbm, hbm, hbm],
        out_specs=tok,
        out_shape=jax.ShapeDtypeStruct((b, n_e, cap, d), BF16),
        scratch_shapes=[
            pltpu.VMEM((2, d, f), BF16), pltpu.VMEM((2, d, f), BF16), pltpu.VMEM((2, f, d), BF16),
            pltpu.VMEM((STEP_CHUNKS, d // n_chunks, f), F32), pltpu.VMEM((STEP_CHUNKS, d // n_chunks, f), F32),
            pltpu.VMEM((STEP_CHUNKS, f // n_chunks, d), F32),
            pltpu.SemaphoreType.DMA((3, STEP_CHUNKS)),
            pltpu.VMEM((FFN_BATCH * cap, f), BF16),
        ],
        compiler_params=_params("arbitrary", "arbitrary"),
        name="ffn",
    )(xg, wg, wu, wd)


def _combine_kernel(cnt_ref, pos_ref, gate_ref, y_ref, x1_ref, g_ref, out_ref, acc_ref):
    bi, j = pl.program_id(0), pl.program_id(1)
    tt = MOE_TILE
    n_e, cap = y_ref.shape[1:3]
    slot0 = lax.broadcasted_iota(I32, (SLOT_WIN, tt), 0)
    for sub in range(x1_ref.shape[1] // tt):
        tile = j * (x1_ref.shape[1] // tt) + sub
        tok = slice(sub * tt, (sub + 1) * tt)
        windows = [_slot_window(cnt_ref, bi * n_e + e, tile, cap) for e in range(n_e)]
        acc = x1_ref[0, tok, :]
        for e in range(0, n_e, WIN_GROUP):
            scatter_t = jnp.concatenate(
                [jnp.where(pos_ref[0, e + i:e + i + 1, tok] == slot0 + windows[e + i][1],
                           gate_ref[0, e + i:e + i + 1, tok], 0.0).astype(BF16) for i in range(WIN_GROUP)], axis=0)
            y_group = jnp.concatenate(
                [y_ref[0, e + i, pl.ds(windows[e + i][1], SLOT_WIN), :] for i in range(WIN_GROUP)], axis=0)
            acc = acc + lax.dot_general(scatter_t, y_group, _TN, preferred_element_type=F32)
        acc_ref[...] = acc

        @pl.when(_any_overflow(windows))
        def _(tile=tile, tok=tok):
            wide = lax.broadcasted_iota(I32, (MXU_DEPTH, tt), 0)

            def one_expert(e, carry):
                hi, w0 = _slot_window(cnt_ref, bi * n_e + e, tile, cap)

                @pl.when(hi > w0 + SLOT_WIN)
                def _():
                    pos = pos_ref[0, pl.ds(e, 1), tok]
                    gate = gate_ref[0, pl.ds(e, 1), tok]
                    for ws in range(0, cap, MXU_DEPTH):
                        slot = wide + ws
                        extra = jnp.where((pos == slot) & (slot >= w0 + SLOT_WIN), gate, 0.0).astype(BF16)
                        acc_ref[...] += lax.dot_general(extra, y_ref[0, e, ws:ws + MXU_DEPTH, :], _TN,
                                                        preferred_element_type=F32)

                return carry

            lax.fori_loop(0, n_e, one_expert, 0)

        acc = acc_ref[...]
        ms = jnp.mean(acc * acc, axis=-1, keepdims=True)
        out_ref[0, tok, :] = acc * lax.rsqrt(ms + EPS) * g_ref[...]


def _combine(cnt, posm, aff, y, x1, gain):
    b, s, d = x1.shape
    n_e, cap = y.shape[1:3]
    step = MOE_TILE * COMBINE_TILES
    return pl.pallas_call(
        _combine_kernel,
        grid_spec=pltpu.PrefetchScalarGridSpec(
            num_scalar_prefetch=1,
            grid=(b, s // step),
            in_specs=[
                pl.BlockSpec((1, n_e, step), lambda bi, j, cnt: (bi, 0, j)),
                pl.BlockSpec((1, n_e, step), lambda bi, j, cnt: (bi, 0, j)),
                pl.BlockSpec((1, n_e, cap, d), lambda bi, j, cnt: (bi, 0, 0, 0)),
                pl.BlockSpec((1, step, d), lambda bi, j, cnt: (bi, j, 0)),
                pl.BlockSpec((1, d), lambda bi, j, cnt: (0, 0)),
            ],
            out_specs=pl.BlockSpec((1, step, d), lambda bi, j, cnt: (bi, j, 0)),
            scratch_shapes=[pltpu.VMEM((MOE_TILE, d), F32)],
        ),
        out_shape=jax.ShapeDtypeStruct((b, s, d), F32),
        compiler_params=_params("parallel", "parallel"),
        name="combine",
    )(cnt, posm, aff, y, x1, gain)


def _rope_tables(seq):
    half = HEAD_DIM // 2
    pos = jnp.arange(seq, dtype=F32)
    inv = ROPE_BASE ** (-jnp.arange(0, HEAD_DIM, 2, dtype=F32) / HEAD_DIM)
    ang = pos[:, None] * inv[None, :]
    cos, sin = jnp.cos(ang), jnp.sin(ang)
    reps = LANES // HEAD_DIM
    cos_t = jnp.tile(jnp.concatenate([cos, cos], axis=1), (1, reps))
    sin_t = jnp.tile(jnp.concatenate([-sin, sin], axis=1), (1, reps))
    return cos_t, sin_t


def kernel(x, norm1_gain, w_in, ret_log_decay, ret_gn_gain, rel_bias, w_out, norm2_gain, w_router,
           w_gate, w_up, w_down, final_gain):
    b, s, d = x.shape
    depth = w_in.shape[0]
    ret_width = ret_gn_gain.shape[1]
    n_ret_pairs = ret_width // LANES
    att_width = (w_in.shape[2] - 4 * ret_width) // 3
    n_att_pairs = att_width // LANES
    cap = CAPACITY_FACTOR * s // N_EXPERTS
    cos_t, sin_t = _rope_tables(s)

    assert depth == 1, "single-layer block: the final norm is fused into the combine kernel"
    layer = 0
    ret, att = _proj(x.reshape(b * s, d), norm1_gain[layer][None, :], w_in[layer].astype(BF16),
                     cos_t, sin_t, n_ret=4 * ret_width, ret_width=ret_width)
    r = _retention(ret, ret_log_decay[layer], ret_gn_gain[layer][None, :], n_ret_pairs, b)
    a = _dilated(att, rel_bias, n_att_pairs, b)
    x1, h2, aff = _outproj(r, a, x, w_out[layer].astype(BF16), norm2_gain[layer][None, :],
                           w_router[layer])
    assert s % (MOE_TILE * max(GATHER_TILES, COMBINE_TILES)) == 0 and s // MOE_TILE < CNT_STRIDE and cap % MXU_DEPTH == 0 and N_EXPERTS % WIN_GROUP == 0
    posm, cnt = _select(aff.reshape(b * N_EXPERTS, s), cap)
    posm = posm.reshape(b, N_EXPERTS, s)
    cnt = cnt[:, :CNT_STRIDE].reshape(-1)
    xg = _gather(cnt, posm, h2.reshape(b, s, d), cap)
    y = _ffn(xg, w_gate[layer], w_up[layer], w_down[layer])
    return _combine(cnt, posm, aff, y, x1.reshape(b, s, d),
                    final_gain[None, :])
```
